```python
import math
import jax, jax.numpy as jnp
from jax import lax
import numpy as np

D_MODEL = 1024
BATCH = 2
SEQ = 8192
DEPTH = 2

HEAD_DIM = 64
BLOCK = 128
A_Q_HEADS = 6
A_KV_HEADS = 2
WINDOW = 128
A_WIDTH = A_Q_HEADS * HEAD_DIM
A_KV_WIDTH = A_KV_HEADS * HEAD_DIM
B_HEADS = 4
B_WIDTH = B_HEADS * HEAD_DIM
DECAY_RANK = 64
ICLR_RANK = 64
GATE_RANK = 128
GN_EPS = 64e-5
C_HEADS = 6
C_WIDTH = C_HEADS * HEAD_DIM
MIX_WIDTH = A_WIDTH + B_WIDTH + C_WIDTH
A_COLS = A_WIDTH + 2 * A_KV_WIDTH
B_COLS = 3 * B_WIDTH + DECAY_RANK + ICLR_RANK + GATE_RANK
C_COLS = 3 * C_WIDTH
IN_COLS = A_COLS + B_COLS + C_COLS
NUM_BUCKETS = 32
MAX_EXACT = NUM_BUCKETS // 2
MAX_DISTANCE = 128
N_EXPERTS = 16
N_GROUPS = 4
EXPERTS_PER_GROUP = N_EXPERTS // N_GROUPS
TOP_K = 2
D_FF_EXPERT = 256
LN_EPS = 1e-5
ALPHA = (2 * DEPTH) ** 0.25
BETA_INIT = (8 * DEPTH) ** -0.25

kernel_name = "hybrid_swa_rwkv7_stickbreak_grouped_moe"


def layer_norm(x, g, b):
    xf = x.astype(jnp.float32)
    mu = xf.mean(-1, keepdims=True)
    var = jnp.square(xf - mu).mean(-1, keepdims=True)
    return ((xf - mu) * lax.rsqrt(var + LN_EPS) * g + b).astype(x.dtype)


def t5_causal_bucket(dist):
    dist = jnp.maximum(dist, 0)
    d = jnp.maximum(dist, 1).astype(jnp.float32)
    large = MAX_EXACT + (jnp.log(d / MAX_EXACT) / math.log(MAX_DISTANCE / MAX_EXACT)
                         * (NUM_BUCKETS - MAX_EXACT)).astype(jnp.int32)
    large = jnp.minimum(large, NUM_BUCKETS - 1)
    return jnp.where(dist < MAX_EXACT, dist, large)


def swa_sink_attention(q, k, v, sinks, rel_bias):
    B_, S, Hq, Dh = q.shape
    nb = S // BLOCK
    grp = Hq // A_KV_HEADS
    f32 = jnp.float32
    qb = q.astype(f32).reshape(B_, nb, BLOCK, A_KV_HEADS, grp, Dh)
    pad = ((0, 0), (BLOCK, 0), (0, 0), (0, 0))
    kb = jnp.pad(k.astype(f32), pad).reshape(B_, nb + 1, BLOCK, A_KV_HEADS, Dh)
    vb = jnp.pad(v.astype(f32), pad).reshape(B_, nb + 1, BLOCK, A_KV_HEADS, Dh)
    kband = jnp.concatenate([kb[:, :-1], kb[:, 1:]], axis=2)
    vband = jnp.concatenate([vb[:, :-1], vb[:, 1:]], axis=2)
    logits = jnp.einsum('bnqhgd,bnkhd->bnhgqk', qb, kband) * (Dh ** -0.5)
    qi = jnp.arange(BLOCK)[:, None]
    ki = jnp.arange(2 * BLOCK)[None, :]
    dist = qi + BLOCK - ki
    in_window = (dist >= 0) & (dist < WINDOW)
    key_pos = jnp.arange(nb)[:, None, None] * BLOCK - BLOCK + ki[None]
    valid = in_window[None] & (key_pos >= 0)
    bias = rel_bias.astype(f32)[t5_causal_bucket(dist)]
    bias = bias.transpose(2, 0, 1).reshape(A_KV_HEADS, grp, BLOCK, 2 * BLOCK)
    logits = jnp.where(valid[None, :, None, None], logits + bias[None, None], -jnp.inf)
    sink = sinks.astype(f32).reshape(1, 1, A_KV_HEADS, grp, 1, 1)
    m = jnp.maximum(logits.max(-1, keepdims=True), sink)
    p = jnp.exp(logits - m)
    probs = p / (p.sum(-1, keepdims=True) + jnp.exp(sink - m))
    out = jnp.einsum('bnhgqk,bnkhd->bnqhgd', probs, vband)
    return out.reshape(B_, S, Hq * Dh)


def stick_breaking_attention(q, k, v):
    B_, S, H, Dh = q.shape
    nb = S // BLOCK
    f32 = jnp.float32
    qb = q.astype(f32).reshape(B_, nb, BLOCK, H, Dh).transpose(1, 0, 3, 2, 4)
    kt = k.astype(f32).transpose(0, 2, 1, 3)
    vt = v.astype(f32).transpose(0, 2, 1, 3)
    key_pos = jnp.arange(S)
    scale = Dh ** -0.5

    def one_block(args):
        qblk, start = args
        z = jnp.einsum('bhqd,bhkd->bhqk', qblk, kt) * scale
        q_pos = start + jnp.arange(BLOCK)
        before = key_pos[None, :] < q_pos[:, None]
        log_beta = jnp.where(before, jax.nn.log_sigmoid(z), -jnp.inf)
        log_keep = jnp.where(before, jax.nn.log_sigmoid(-z), 0.0)
        suffix = lax.cumsum(log_keep, axis=3, reverse=True) - log_keep
        w = jnp.exp(log_beta + suffix)
        return jnp.einsum('bhqk,bhkd->bhqd', w, vt)

    out = lax.map(one_block, (qb, jnp.arange(nb) * BLOCK))
    return out.transpose(1, 0, 3, 2, 4).reshape(B_, S, H * Dh)


def token_shift_mix(p, mu):
    prev = jnp.pad(p, ((0, 0), (1, 0), (0, 0)))[:, :-1]
    return p + (prev - p) * mu


def rwkv7_scan(r, w, k, v, a, b):
    B_, S, H, N = r.shape

    def step(state, inp):
        r_t, w_t, k_t, v_t, a_t, b_t = inp
        sa = jnp.einsum('bhvk,bhk->bhv', state, a_t)
        state = (state * w_t[:, :, None, :] + sa[..., None] * b_t[:, :, None, :]
                 + v_t[..., None] * k_t[:, :, None, :])
        return state, jnp.einsum('bhvk,bhk->bhv', state, r_t)

    xs = tuple(t.transpose(1, 0, 2, 3) for t in (r, w, k, v, a, b))
    _, ys = lax.scan(step, jnp.zeros((B_, H, N, N), jnp.float32), xs)
    return ys.transpose(1, 0, 2, 3)


def rwkv7_time_mix(p, shift_mu, decay_w0, decay_up, iclr_a0, iclr_up, gate_up,
                   k_k, k_a, r_k, lnx_g, lnx_b):
    f32 = jnp.float32
    B_, S, _ = p.shape
    p = token_shift_mix(p, shift_mu)
    cuts = [B_WIDTH, 2 * B_WIDTH, 3 * B_WIDTH, 3 * B_WIDTH + DECAY_RANK,
            3 * B_WIDTH + DECAY_RANK + ICLR_RANK]
    r, k, v, xw, xa, xg = jnp.split(p, cuts, axis=-1)
    log_decay = -jax.nn.softplus(-(decay_w0 + jnp.tanh(xw) @ decay_up).astype(f32)) - 0.5
    decay = jnp.exp(-jnp.exp(log_decay))
    a = jax.nn.sigmoid((iclr_a0 + xa @ iclr_up).astype(f32))
    g = (jax.nn.sigmoid(xg) @ gate_up).astype(f32)
    heads = lambda t: t.astype(f32).reshape(B_, S, B_HEADS, HEAD_DIM)
    r, k, v, a_h, decay_h = heads(r), heads(k), heads(v), heads(a), heads(decay)
    kk = k * k_k.astype(f32).reshape(B_HEADS, HEAD_DIM)
    kk = kk * lax.rsqrt(jnp.maximum(jnp.sum(kk * kk, -1, keepdims=True), 1e-24))
    k = k * (1.0 + (a_h - 1.0) * k_a.astype(f32).reshape(B_HEADS, HEAD_DIM))
    y = rwkv7_scan(r, decay_h, k, v, -kk, kk * a_h)
    mu = y.mean(-1, keepdims=True)
    var = jnp.square(y - mu).mean(-1, keepdims=True)
    y = ((y - mu) * lax.rsqrt(var + GN_EPS)).reshape(B_, S, B_WIDTH) * lnx_g + lnx_b
    bonus = (jnp.sum(r * k * r_k.astype(f32), -1, keepdims=True) * v).reshape(B_, S, B_WIDTH)
    return (y + bonus) * g


def hybrid_mixer(h, w_in, w_out, sinks, rel_bias, shift_mu, decay_w0, decay_up,
                 iclr_a0, iclr_up, gate_up, k_k, k_a, r_k, lnx_g, lnx_b):
    B_, S, _ = h.shape
    proj = h @ w_in
    pa = proj[..., :A_COLS]
    pb = proj[..., A_COLS:A_COLS + B_COLS]
    pc = proj[..., A_COLS + B_COLS:]
    qa = pa[..., :A_WIDTH].reshape(B_, S, A_Q_HEADS, HEAD_DIM)
    ka = pa[..., A_WIDTH:A_WIDTH + A_KV_WIDTH].reshape(B_, S, A_KV_HEADS, HEAD_DIM)
    va = pa[..., A_WIDTH + A_KV_WIDTH:].reshape(B_, S, A_KV_HEADS, HEAD_DIM)
    out_a = swa_sink_attention(qa, ka, va, sinks, rel_bias).astype(h.dtype)
    out_b = rwkv7_time_mix(pb, shift_mu, decay_w0, decay_up, iclr_a0, iclr_up, gate_up,
                           k_k, k_a, r_k, lnx_g, lnx_b).astype(h.dtype)
    qc, kc, vc = [t.reshape(B_, S, C_HEADS, HEAD_DIM) for t in jnp.split(pc, 3, axis=-1)]
    out_c = stick_breaking_attention(qc, kc, vc).astype(h.dtype)
    return jnp.concatenate([out_a, out_b, out_c], axis=-1) @ w_out


def grouped_moe(h, router_w, router_bias, w_gate, w_up, w_down):
    B_, S, D = h.shape
    xt = h.reshape(-1, D)
    scores = jax.nn.sigmoid((xt @ router_w).astype(jnp.float32))
    sel = scores + router_bias.astype(jnp.float32)
    group_score = lax.top_k(sel.reshape(-1, N_GROUPS, EXPERTS_PER_GROUP), TOP_K)[0].sum(-1)
    best_group = jnp.argmax(group_score, axis=-1)
    in_group = jnp.repeat(jax.nn.one_hot(best_group, N_GROUPS, dtype=jnp.bool_),
                          EXPERTS_PER_GROUP, axis=1)
    _, top_idx = lax.top_k(jnp.where(in_group, sel, -jnp.inf), TOP_K)
    top_w = jnp.take_along_axis(scores, top_idx, axis=-1)
    top_w = top_w / top_w.sum(-1, keepdims=True)
    gates = (jax.nn.one_hot(top_idx, N_EXPERTS, dtype=jnp.float32) * top_w[..., None]).sum(1)
    hg = jnp.einsum('nd,edf->enf', xt, w_gate)
    hu = jnp.einsum('nd,edf->enf', xt, w_up)
    act = jax.nn.silu(hg) * hu * gates.T[:, :, None].astype(hg.dtype)
    y = jnp.einsum('enf,efd->nd', act, w_down)
    return y.reshape(B_, S, D).astype(h.dtype)


def setup_inputs(seed: int = 0) -> dict:
    key = jax.random.key(seed)
    ks = iter(jax.random.split(key, 40))
    f32 = jnp.float32
    L = DEPTH
    nrm = lambda shape, s: jax.random.normal(next(ks), shape, f32) * s
    uni = lambda shape, lo, hi: jax.random.uniform(next(ks), shape, f32, lo, hi)
    return {
        "x": nrm((BATCH, SEQ, D_MODEL), 1.0),
        "ln0_g": 1.0 + nrm((D_MODEL,), 0.02),
        "ln0_b": nrm((D_MODEL,), 0.02),
        "w_in": nrm((L, D_MODEL, IN_COLS), D_MODEL ** -0.5),
        "w_out": nrm((L, MIX_WIDTH, D_MODEL), MIX_WIDTH ** -0.5 * BETA_INIT),
        "sinks": nrm((L, A_Q_HEADS), 0.5),
        "rel_bias": nrm((NUM_BUCKETS, A_Q_HEADS), 0.5),
        "shift_mu": uni((L, B_COLS), 0.0, 1.0),
        "decay_w0": uni((L, B_WIDTH), -6.0, -1.0),
        "decay_up": nrm((L, DECAY_RANK, B_WIDTH), 0.5 * DECAY_RANK ** -0.5),
        "iclr_a0": nrm((L, B_WIDTH), 0.1),
        "iclr_up": nrm((L, ICLR_RANK, B_WIDTH), ICLR_RANK ** -0.5),
        "gate_up": nrm((L, GATE_RANK, B_WIDTH), GATE_RANK ** -0.5),
        "k_k": 0.85 + nrm((L, B_WIDTH), 0.05),
        "k_a": 1.0 + nrm((L, B_WIDTH), 0.05),
        "r_k": nrm((L, B_HEADS, HEAD_DIM), 0.1),
        "lnx_g": 1.0 + nrm((L, B_WIDTH), 0.02),
        "lnx_b": nrm((L, B_WIDTH), 0.02),
        "ln1_g": 1.0 + nrm((L, D_MODEL), 0.02),
        "ln1_b": nrm((L, D_MODEL), 0.02),
        "router_w": nrm((D_MODEL, N_EXPERTS), D_MODEL ** -0.5),
        "router_bias": nrm((N_EXPERTS,), 0.01),
        "w_gate": nrm((L, N_EXPERTS, D_MODEL, D_FF_EXPERT), D_MODEL ** -0.5),
        "w_up": nrm((L, N_EXPERTS, D_MODEL, D_FF_EXPERT), D_MODEL ** -0.5),
        "w_down": nrm((L, N_EXPERTS, D_FF_EXPERT, D_MODEL), D_FF_EXPERT ** -0.5 * BETA_INIT),
        "ln2_g": 1.0 + nrm((L, D_MODEL), 0.02),
        "ln2_b": nrm((L, D_MODEL), 0.02),
    }


def reference(x, ln0_g, ln0_b, w_in, w_out, sinks, rel_bias, shift_mu, decay_w0, decay_up,
              iclr_a0, iclr_up, gate_up, k_k, k_a, r_k, lnx_g, lnx_b, ln1_g, ln1_b,
              router_w, router_bias, w_gate, w_up, w_down, ln2_g, ln2_b):
    h = layer_norm(x, ln0_g, ln0_b)
    for l in range(DEPTH):
        m = hybrid_mixer(h, w_in[l], w_out[l], sinks[l], rel_bias, shift_mu[l], decay_w0[l],
                         decay_up[l], iclr_a0[l], iclr_up[l], gate_up[l], k_k[l], k_a[l],
                         r_k[l], lnx_g[l], lnx_b[l])
        h = layer_norm(ALPHA * h + m, ln1_g[l], ln1_b[l])
        f = grouped_moe(h, router_w, router_bias, w_gate[l], w_up[l], w_down[l])
        h = layer_norm(ALPHA * h + f, ln2_g[l], ln2_b[l])
    return h
```

```python
import functools
import math

import jax
import jax.numpy as jnp
from jax import lax
from jax.experimental import pallas as pl
from jax.experimental.pallas import tpu as pltpu

F32 = jnp.float32
BF16 = jnp.bfloat16
HI = lax.Precision.HIGHEST

DEPTH = 2
HEAD_DIM = 64
BLOCK = 128
LANES = 128
A_Q_HEADS = 6
A_KV_HEADS = 2
WINDOW = 128
A_WIDTH = A_Q_HEADS * HEAD_DIM
A_KV_WIDTH = A_KV_HEADS * HEAD_DIM
B_HEADS = 4
B_WIDTH = B_HEADS * HEAD_DIM
DECAY_RANK = 64
ICLR_RANK = 64
GATE_RANK = 128
GN_EPS = 64e-5
C_HEADS = 6
C_WIDTH = C_HEADS * HEAD_DIM
A_COLS = A_WIDTH + 2 * A_KV_WIDTH
B_COLS = 3 * B_WIDTH + DECAY_RANK + ICLR_RANK + GATE_RANK
NUM_BUCKETS = 32
MAX_EXACT = NUM_BUCKETS // 2
MAX_DISTANCE = 128
N_EXPERTS = 16
N_GROUPS = 4
EXPERTS_PER_GROUP = N_EXPERTS // N_GROUPS
D_FF_EXPERT = 256
LN_EPS = 1e-5
ALPHA = (2 * DEPTH) ** 0.25
SCALE = HEAD_DIM ** -0.5
MASKED = -1e30
CHUNK = 64
ROUTER_ROWS = 8

VMEM_LIMIT = 48 * 1024 * 1024


def _dot(a, b, prec=None):
    return jnp.dot(a, b, preferred_element_type=F32, precision=prec)


def _dot_nt(a, b, prec=None):
    return lax.dot_general(a, b, (((1,), (1,)), ((), ())),
                           preferred_element_type=F32, precision=prec)


def _sigmoid(x):
    return 1.0 / (1.0 + jnp.exp(-x))


def _softplus(x):
    return jnp.maximum(x, 0.0) + jnp.log(1.0 + jnp.exp(-jnp.abs(x)))


def _layer_norm(x, g, b):
    mu = jnp.mean(x, axis=-1, keepdims=True)
    xc = x - mu
    var = jnp.mean(xc * xc, axis=-1, keepdims=True)
    return xc * lax.rsqrt(var + LN_EPS) * g + b


def _params(*sem):
    return pltpu.CompilerParams(dimension_semantics=sem, vmem_limit_bytes=VMEM_LIMIT)


def _ln_kernel(x_ref, g_ref, b_ref, o_ref):
    o_ref[...] = _layer_norm(x_ref[...], g_ref[...], b_ref[...])


def _ln_call(x, g, b, tm=512):
    n, d = x.shape
    return pl.pallas_call(
        _ln_kernel,
        grid=(n // tm,),
        in_specs=[pl.BlockSpec((tm, d), lambda i: (i, 0)),
                  pl.BlockSpec((1, d), lambda i: (0, 0)),
                  pl.BlockSpec((1, d), lambda i: (0, 0))],
        out_specs=pl.BlockSpec((tm, d), lambda i: (i, 0)),
        out_shape=jax.ShapeDtypeStruct((n, d), F32),
        compiler_params=_params("parallel"),
        name="embed_ln",
    )(x, g.reshape(1, d), b.reshape(1, d))


def _proj_kernel(h_ref, wa_ref, wb_ref, wq_ref, wk_ref, wvt_ref,
                 pa_ref, pb_ref, qc_ref, kc_ref, vt_ref):
    hb = h_ref[...].astype(BF16)
    pa_ref[...] = _dot(hb, wa_ref[...]).astype(BF16)
    pb_ref[...] = _dot(hb, wb_ref[...])
    qc_ref[...] = (_dot(hb, wq_ref[...]) * SCALE).astype(BF16)
    kc_ref[...] = _dot(hb, wk_ref[...]).astype(BF16)
    vt_ref[...] = _dot_nt(wvt_ref[...], hb).astype(BF16)


def _proj_call(h, wa, wb, wq, wk, wvt, tm=512):
    bsz, s, d = h.shape
    full = lambda w: pl.BlockSpec(w.shape, lambda b, t: (0, 0))
    row = lambda c: pl.BlockSpec((None, tm, c), lambda b, t: (b, t, 0))
    return pl.pallas_call(
        _proj_kernel,
        grid=(bsz, s // tm),
        in_specs=[row(d), full(wa), full(wb), full(wq), full(wk), full(wvt)],
        out_specs=[row(A_COLS), row(B_COLS), row(C_WIDTH), row(C_WIDTH),
                   pl.BlockSpec((None, C_WIDTH, tm), lambda b, t: (b, 0, t))],
        out_shape=[jax.ShapeDtypeStruct((bsz, s, A_COLS), BF16),
                   jax.ShapeDtypeStruct((bsz, s, B_COLS), F32),
                   jax.ShapeDtypeStruct((bsz, s, C_WIDTH), BF16),
                   jax.ShapeDtypeStruct((bsz, s, C_WIDTH), BF16),
                   jax.ShapeDtypeStruct((bsz, C_WIDTH, s), BF16)],
        compiler_params=_params("parallel", "parallel"),
        name="in_proj",
    )(h, wa, wb, wq, wk, wvt)


def _swa_kernel(sink_ref, q_ref, kp_ref, kc_ref, vp_ref, vc_ref, bias_ref, o_ref):
    n = pl.program_id(1)
    kband = jnp.concatenate([kp_ref[...], kc_ref[...]], axis=0)
    vband = jnp.concatenate([vp_ref[...], vc_ref[...]], axis=0)
    lane = lax.broadcasted_iota(jnp.int32, (BLOCK, LANES), 1)
    row2 = lax.broadcasted_iota(jnp.int32, (2 * BLOCK, 1), 0)
    col2 = lax.broadcasted_iota(jnp.int32, (1, 2 * BLOCK), 1)
    pad = jnp.where(jnp.logical_and(n == 0, col2 < BLOCK), MASKED, 0.0)
    for c in range(A_Q_HEADS // 2):
        q2 = q_ref[:, c * LANES:(c + 1) * LANES]
        zero = jnp.zeros_like(q2)
        qs = jnp.concatenate([jnp.where(lane < HEAD_DIM, q2, zero),
                              jnp.where(lane >= HEAD_DIM, q2, zero)], axis=0)
        logits = _dot_nt(qs, kband) * SCALE + bias_ref[c] + pad
        sink = jnp.where(row2 < BLOCK, sink_ref[c], sink_ref[c + 3])
        m = jnp.maximum(jnp.max(logits, axis=-1, keepdims=True), sink)
        p = jnp.exp(logits - m)
        denom = jnp.sum(p, axis=-1, keepdims=True) + jnp.exp(sink - m)
        o = _dot(p.astype(BF16), vband) / denom
        o_ref[:, c * LANES:(c + 1) * LANES] = jnp.where(
            lane < HEAD_DIM, o[:BLOCK], o[BLOCK:]).astype(BF16)


def _swa_call(pa, sinks, bias_pairs):
    bsz, s, _ = pa.shape
    nb = s // BLOCK
    kcol = A_WIDTH // LANES
    vcol = kcol + 1
    prev = lambda n: jnp.maximum(n - 1, 0)
    return pl.pallas_call(
        _swa_kernel,
        grid=(bsz, nb),
        in_specs=[pl.BlockSpec(memory_space=pltpu.SMEM),
                  pl.BlockSpec((None, BLOCK, A_WIDTH), lambda b, n: (b, n, 0)),
                  pl.BlockSpec((None, BLOCK, LANES), lambda b, n: (b, prev(n), kcol)),
                  pl.BlockSpec((None, BLOCK, LANES), lambda b, n: (b, n, kcol)),
                  pl.BlockSpec((None, BLOCK, LANES), lambda b, n: (b, prev(n), vcol)),
                  pl.BlockSpec((None, BLOCK, LANES), lambda b, n: (b, n, vcol)),
                  pl.BlockSpec(bias_pairs.shape, lambda b, n: (0, 0, 0))],
        out_specs=pl.BlockSpec((None, BLOCK, A_WIDTH), lambda b, n: (b, n, 0)),
        out_shape=jax.ShapeDtypeStruct((bsz, s, A_WIDTH), BF16),
        compiler_params=_params("parallel", "parallel"),
        name="swa_attn",
    )(sinks, pa, pa, pa, pa, pa, bias_pairs)


def _t5_causal_bucket(dist):
    dist = jnp.maximum(dist, 0)
    d = jnp.maximum(dist, 1).astype(F32)
    large = MAX_EXACT + (jnp.log(d / MAX_EXACT) / math.log(MAX_DISTANCE / MAX_EXACT)
                         * (NUM_BUCKETS - MAX_EXACT)).astype(jnp.int32)
    large = jnp.minimum(large, NUM_BUCKETS - 1)
    return jnp.where(dist < MAX_EXACT, dist, large)


def _swa_bias_pairs(rel_bias):
    qi = jnp.arange(BLOCK)[:, None]
    ki = jnp.arange(2 * BLOCK)[None, :]
    dist = qi + BLOCK - ki
    in_window = (dist >= 0) & (dist < WINDOW)
    bias = rel_bias.astype(F32)[_t5_causal_bucket(dist)]
    bias = jnp.where(in_window[..., None], bias, MASKED).transpose(2, 0, 1)
    return jnp.stack([jnp.concatenate([bias[c], bias[c + 3]], axis=0)
                      for c in range(A_Q_HEADS // 2)])


def _rwkv_kernel(pb_ref, mu_ref, w0_ref, wd_ref, a0_ref, wa_ref, wg_ref, kk_ref, ka_ref,
                 rk_ref, lng_ref, lnb_ref, o_ref, prev_ref, h_ref):
    t = pl.program_id(1)

    @pl.when(t == 0)
    def _():
        prev_ref[...] = jnp.zeros_like(prev_ref)
        h_ref[...] = jnp.zeros_like(h_ref)

    L = CHUNK
    W = B_WIDTH
    p = pb_ref[...]
    rows = lax.broadcasted_iota(jnp.int32, (L, 1), 0)
    shifted = jnp.where(rows == 0, prev_ref[...], pltpu.roll(p, 1, axis=0))
    prev_ref[...] = p[L - 1:L, :]
    pm = p + (shifted - p) * mu_ref[...]
    r = pm[:, 0:W]
    k = pm[:, W:2 * W]
    v = pm[:, 2 * W:3 * W]
    xwa = pm[:, 3 * W:3 * W + DECAY_RANK + ICLR_RANK]
    xg = pm[:, 3 * W + DECAY_RANK + ICLR_RANK:]

    dw = w0_ref[...] + _dot(jnp.tanh(xwa), wd_ref[...], HI)
    lw = -jnp.exp(-_softplus(-dw) - 0.5)
    a = _sigmoid(a0_ref[...] + _dot(xwa, wa_ref[...], HI))
    g = _dot(_sigmoid(xg), wg_ref[...], HI)

    hr = lax.broadcasted_iota(jnp.int32, (W, W), 0) // HEAD_DIM
    hc = lax.broadcasted_iota(jnp.int32, (W, W), 1) // HEAD_DIM
    head_ones = jnp.where(hr == hc, 1.0, 0.0)
    kk = k * kk_ref[...]
    kk = kk * lax.rsqrt(jnp.maximum(_dot(kk * kk, head_ones, HI), 1e-24))
    k2 = k * (1.0 + (a - 1.0) * ka_ref[...])
    bonus = _dot(r * k2 * rk_ref[...], head_ones, HI) * v
    aa = -kk
    bb = kk * a

    ti = lax.broadcasted_iota(jnp.int32, (L, L), 0)
    tj = lax.broadcasted_iota(jnp.int32, (L, L), 1)
    cum = _dot(jnp.where(ti >= tj, 1.0, 0.0), lw, HI)
    cum_l = cum[L - 1:L, :]
    at = aa * jnp.exp(cum - lw)
    rt = r * jnp.exp(cum)
    inv = jnp.exp(-cum)
    bt = bb * inv
    kt = k2 * inv
    tail = jnp.exp(cum_l - cum)
    bh = bb * tail
    kh = k2 * tail

    lane_head = lax.broadcasted_iota(jnp.int32, (L, W), 1) // HEAD_DIM
    eye = jnp.where(ti == tj, 1.0, 0.0)
    h0 = h_ref[...]

    def only(x, h):
        return jnp.where(lane_head == h, x, 0.0)

    at_s = jnp.concatenate([only(at, h) for h in range(B_HEADS)], axis=0)
    rt_s = jnp.concatenate([only(rt, h) for h in range(B_HEADS)], axis=0)
    ab = _dot_nt(at_s, bt, HI)
    ak = _dot_nt(at_s, kt, HI)
    rb = _dot_nt(rt_s, bt, HI)
    rk = _dot_nt(rt_s, kt, HI)
    at_h0 = _dot(at, h0, HI)
    y = _dot(rt, h0, HI)
    u_parts, bu_rows, bu_cols = [], [], []
    for h in range(B_HEADS):
        sl = slice(h * L, (h + 1) * L)
        a_m = jnp.where(ti > tj, ab[sl], 0.0)
        tinv = eye + a_m
        pw = a_m
        for _ in range(int(math.log2(L)) - 1):
            pw = _dot(pw, pw, HI)
            tinv = tinv + _dot(tinv, pw, HI)
        rhs = at_h0 + _dot(jnp.where(ti > tj, ak[sl], 0.0), v, HI)
        u = only(_dot(tinv, rhs, HI), h)
        y = y + only(_dot(jnp.where(ti >= tj, rb[sl], 0.0), u, HI)
                     + _dot(jnp.where(ti >= tj, rk[sl], 0.0), v, HI), h)
        u_parts.append(u)
    u_all = u_parts[0] + u_parts[1] + u_parts[2] + u_parts[3]
    lhs_t = jnp.concatenate([bh, kh], axis=0).T
    blk_r = lax.broadcasted_iota(jnp.int32, (W, W), 0) // HEAD_DIM
    blk_c = lax.broadcasted_iota(jnp.int32, (W, W), 1) // HEAD_DIM
    upd = _dot(lhs_t, jnp.concatenate([u_all, v], axis=0), HI)
    pl_col = jnp.exp(cum_l).T
    h_ref[...] = pl_col * h0 + jnp.where(blk_r == blk_c, upd, 0.0)

    mean = _dot(y, head_ones, HI) * (1.0 / HEAD_DIM)
    yc = y - mean
    var = _dot(yc * yc, head_ones, HI) * (1.0 / HEAD_DIM)
    yn = yc * lax.rsqrt(var + GN_EPS) * lng_ref[...] + lnb_ref[...]
    o_ref[...] = ((yn + bonus) * g).astype(o_ref.dtype)


def _rwkv_call(pb, mu, w0, wd_pad, a0, wa_pad, wg, k_k, k_a, r_k, lnx_g, lnx_b):
    bsz, s, _ = pb.shape
    vec = lambda x: x.reshape(1, -1).astype(F32)
    small = [vec(mu), vec(w0), wd_pad, vec(a0), wa_pad, wg, vec(k_k), vec(k_a), vec(r_k),
             vec(lnx_g), vec(lnx_b)]
    return pl.pallas_call(
        _rwkv_kernel,
        grid=(bsz, s // CHUNK),
        in_specs=[pl.BlockSpec((None, CHUNK, B_COLS), lambda b, t: (b, t, 0))]
        + [pl.BlockSpec(x.shape, lambda b, t: (0, 0)) for x in small],
        out_specs=pl.BlockSpec((None, CHUNK, B_WIDTH), lambda b, t: (b, t, 0)),
        out_shape=jax.ShapeDtypeStruct((bsz, s, B_WIDTH), BF16),
        scratch_shapes=[pltpu.VMEM((1, B_COLS), F32), pltpu.VMEM((B_WIDTH, B_WIDTH), F32)],
        compiler_params=_params("parallel", "arbitrary"),
        name="rwkv7",
    )(pb, *small)


def _sb_block(kblk, vt, qh, ucat, acc, car, diag):
    zt = _dot_nt(kblk, qh)
    sp = _softplus(zt)
    if diag:
        kr = lax.broadcasted_iota(jnp.int32, (BLOCK, BLOCK), 0)
        qc = lax.broadcasted_iota(jnp.int32, (BLOCK, BLOCK), 1)
        before = kr < qc
        sp = jnp.where(before, sp, 0.0)
    sp_hi = sp.astype(BF16)
    sp_lo = (sp - sp_hi.astype(F32)).astype(BF16)
    excl = _dot(ucat, jnp.concatenate([sp_hi, sp_lo], axis=0))
    w = jnp.exp(zt - sp - excl - car)
    if diag:
        w = jnp.where(before, w, 0.0)
    acc = acc + _dot(vt, w.astype(BF16))
    car = car + jnp.sum(sp, axis=0, keepdims=True)
    return acc, car


def _sb_kernel(q_ref, k_ref, vt_ref, o_ref):
    i = pl.program_id(2)
    q2 = q_ref[...]
    lane = lax.broadcasted_iota(jnp.int32, (BLOCK, LANES), 1)
    zero = jnp.zeros_like(q2)
    qh = (jnp.where(lane < HEAD_DIM, q2, zero), jnp.where(lane >= HEAD_DIM, q2, zero))
    ur = lax.broadcasted_iota(jnp.int32, (BLOCK, 2 * BLOCK), 0)
    uc = lax.broadcasted_iota(jnp.int32, (BLOCK, 2 * BLOCK), 1) % BLOCK
    ucat = jnp.where(uc > ur, 1.0, 0.0).astype(BF16)

    def tile(j, state, diag):
        off = pl.multiple_of(j * BLOCK, BLOCK)
        kblk = k_ref[pl.ds(off, BLOCK), :]
        vt2 = vt_ref[:, pl.ds(off, BLOCK)]
        out = []
        for h in range(2):
            acc, car = state[h]
            out.append(_sb_block(kblk, vt2[h * HEAD_DIM:(h + 1) * HEAD_DIM], qh[h], ucat,
                                 acc, car, diag))
        return tuple(out)

    init = tuple((jnp.zeros((HEAD_DIM, BLOCK), F32), jnp.zeros((1, BLOCK), F32))
                 for _ in range(2))
    state = tile(i, init, True)
    state = lax.fori_loop(1, i + 1, lambda jj, st: tile(i - jj, st, False), state)
    out_t = jnp.concatenate([state[0][0], state[1][0]], axis=0)
    o_ref[...] = out_t.T.astype(o_ref.dtype)


def _sb_call(qc, kc, vt):
    bsz, s, _ = qc.shape
    return pl.pallas_call(
        _sb_kernel,
        grid=(bsz, C_WIDTH // LANES, s // BLOCK),
        in_specs=[pl.BlockSpec((None, BLOCK, LANES), lambda b, hp, i: (b, i, hp)),
                  pl.BlockSpec((None, s, LANES), lambda b, hp, i: (b, 0, hp)),
                  pl.BlockSpec((None, LANES, s), lambda b, hp, i: (b, hp, 0))],
        out_specs=pl.BlockSpec((None, BLOCK, LANES), lambda b, hp, i: (b, i, hp)),
        out_shape=jax.ShapeDtypeStruct((bsz, s, C_WIDTH), BF16),
        compiler_params=_params("parallel", "parallel", "parallel"),
        name="stickbreak_attn",
    )(qc, kc, vt)


def _outproj_kernel(h_ref, oa_ref, ob_ref, oc_ref, wa_ref, wb_ref, wc_ref, g_ref, b_ref, o_ref):
    m = (_dot(oa_ref[...], wa_ref[...]) + _dot(ob_ref[...], wb_ref[...])
         + _dot(oc_ref[...], wc_ref[...]))
    o_ref[...] = _layer_norm(ALPHA * h_ref[...] + m, g_ref[...], b_ref[...])


def _outproj_call(h, oa, ob, oc, wa, wb, wc, g, b, tm=512):
    bsz, s, d = h.shape
    row = lambda c: pl.BlockSpec((None, tm, c), lambda bb, t: (bb, t, 0))
    full = lambda w: pl.BlockSpec(w.shape, lambda bb, t: (0, 0))
    g2, b2 = g.reshape(1, d), b.reshape(1, d)
    return pl.pallas_call(
        _outproj_kernel,
        grid=(bsz, s // tm),
        in_specs=[row(d), row(A_WIDTH), row(B_WIDTH), row(C_WIDTH),
                  full(wa), full(wb), full(wc), full(g2), full(b2)],
        out_specs=row(d),
        out_shape=jax.ShapeDtypeStruct((bsz, s, d), F32),
        compiler_params=_params("parallel", "parallel"),
        name="out_proj_ln",
    )(h, oa, ob, oc, wa, wb, wc, g2, b2)


def _router_kernel(x_ref, rw_ref, rb_ref, o_ref):
    logits = _dot_nt(rw_ref[...], x_ref[...], HI)
    scores = _sigmoid(logits)
    sel = scores + rb_ref[...]
    R = ROUTER_ROWS
    s = [sel[m * R:(m + 1) * R] for m in range(EXPERTS_PER_GROUP)]
    sc = [scores[m * R:(m + 1) * R] for m in range(EXPERTS_PER_GROUP)]
    hi01, lo01 = jnp.maximum(s[0], s[1]), jnp.minimum(s[0], s[1])
    hi23, lo23 = jnp.maximum(s[2], s[3]), jnp.minimum(s[2], s[3])
    top1 = jnp.maximum(hi01, hi23)
    top2 = jnp.maximum(jnp.minimum(hi01, hi23), jnp.maximum(lo01, lo23))
    gscore = top1 + top2
    gi = lax.broadcasted_iota(jnp.int32, gscore.shape, 0)
    gmax = jnp.max(gscore, axis=0, keepdims=True)
    best = jnp.min(jnp.where(gscore == gmax, gi, R), axis=0, keepdims=True)
    in_group = gi == best
    picked = []
    for m in range(EXPERTS_PER_GROUP):
        rank = jnp.zeros(gscore.shape, jnp.int32)
        for j in range(EXPERTS_PER_GROUP):
            if j == m:
                continue
            ahead = (s[j] >= s[m]) if j < m else (s[j] > s[m])
            rank = rank + jnp.where(ahead, 1, 0)
        picked.append(jnp.where(jnp.logical_and(in_group, rank < 2), sc[m], 0.0))
    denom = jnp.sum(picked[0] + picked[1] + picked[2] + picked[3], axis=0, keepdims=True)
    for m in range(EXPERTS_PER_GROUP):
        o_ref[m * R:(m + 1) * R, :] = picked[m] / denom


def _router_call(x, rw_t, rb_col, tm=1024):
    n, d = x.shape
    rows = rw_t.shape[0]
    return pl.pallas_call(
        _router_kernel,
        grid=(n // tm,),
        in_specs=[pl.BlockSpec((tm, d), lambda i: (i, 0)),
                  pl.BlockSpec((rows, d), lambda i: (0, 0)),
                  pl.BlockSpec((rows, 1), lambda i: (0, 0))],
        out_specs=pl.BlockSpec((rows, tm), lambda i: (0, i)),
        out_shape=jax.ShapeDtypeStruct((rows, n), F32),
        compiler_params=_params("parallel"),
        name="moe_router",
    )(x, rw_t, rb_col)


def _moe_kernel(x_ref, gate_ref, wg_ref, wu_ref, wd_ref, g_ref, b_ref, o_ref, xb_ref, acc_ref):
    e = pl.program_id(1)

    @pl.when(e == 0)
    def _():
        xb_ref[...] = x_ref[...].astype(BF16)
        acc_ref[...] = jnp.zeros_like(acc_ref)

    xb = xb_ref[...]
    hg = _dot(xb, wg_ref[...])
    hu = _dot(xb, wu_ref[...])
    gates = gate_ref[...]
    lane = lax.broadcasted_iota(jnp.int32, gates.shape, 1)
    gcol = jnp.sum(jnp.where(lane == e, gates, 0.0), axis=-1, keepdims=True)
    act = (hg * _sigmoid(hg)) * hu * gcol
    acc_ref[...] += _dot(act.astype(BF16), wd_ref[...])

    @pl.when(e == N_EXPERTS - 1)
    def _():
        o_ref[...] = _layer_norm(ALPHA * x_ref[...] + acc_ref[...], g_ref[...], b_ref[...])


def _moe_call(x, gates, wg, wu, wd, g, b, tm=1024):
    n, d = x.shape
    f = wg.shape[-1]
    g2, b2 = g.reshape(1, d), b.reshape(1, d)
    return pl.pallas_call(
        _moe_kernel,
        grid=(n // tm, N_EXPERTS),
        in_specs=[pl.BlockSpec((tm, d), lambda i, e: (i, 0)),
                  pl.BlockSpec((tm, N_EXPERTS), lambda i, e: (i, 0)),
                  pl.BlockSpec((None, d, f), lambda i, e: (e, 0, 0)),
                  pl.BlockSpec((None, d, f), lambda i, e: (e, 0, 0)),
                  pl.BlockSpec((None, f, d), lambda i, e: (e, 0, 0)),
                  pl.BlockSpec((1, d), lambda i, e: (0, 0)),
                  pl.BlockSpec((1, d), lambda i, e: (0, 0))],
        out_specs=pl.BlockSpec((tm, d), lambda i, e: (i, 0)),
        out_shape=jax.ShapeDtypeStruct((n, d), F32),
        scratch_shapes=[pltpu.VMEM((tm, d), BF16), pltpu.VMEM((tm, d), F32)],
        compiler_params=_params("parallel", "arbitrary"),
        name="moe_experts_ln",
    )(x, gates, wg, wu, wd, g2, b2)


def _pair_heads(x, axis):
    shape = x.shape
    x = x.reshape(shape[:axis] + (A_KV_HEADS, A_Q_HEADS // A_KV_HEADS, HEAD_DIM) + shape[axis + 1:])
    x = jnp.swapaxes(x, axis, axis + 1)
    return x.reshape(shape)


def _router_layout(router_w, router_bias):
    d = router_w.shape[0]
    w = router_w.astype(F32).T.reshape(N_GROUPS, EXPERTS_PER_GROUP, d).transpose(1, 0, 2)
    w = jnp.pad(w, ((0, 0), (0, ROUTER_ROWS - N_GROUPS), (0, 0)))
    b = router_bias.astype(F32).reshape(N_GROUPS, EXPERTS_PER_GROUP).T
    b = jnp.pad(b, ((0, 0), (0, ROUTER_ROWS - N_GROUPS)), constant_values=MASKED)
    rows = EXPERTS_PER_GROUP * ROUTER_ROWS
    return w.reshape(rows, d), b.reshape(rows, 1)


def _gates_from_router(gates_t):
    n = gates_t.shape[1]
    g = gates_t.reshape(EXPERTS_PER_GROUP, ROUTER_ROWS, n)[:, :N_GROUPS]
    return g.transpose(2, 1, 0).reshape(n, N_EXPERTS)


def kernel(x, ln0_g, ln0_b, w_in, w_out, sinks, rel_bias, shift_mu, decay_w0, decay_up, iclr_a0,
           iclr_up, gate_up, k_k, k_a, r_k, lnx_g, lnx_b, ln1_g, ln1_b, router_w, router_bias,
           w_gate, w_up, w_down, ln2_g, ln2_b):
    bsz, s, d = x.shape
    n = bsz * s
    bias_pairs = _swa_bias_pairs(rel_bias)
    rw_t, rb_col = _router_layout(router_w, router_bias)
    zeros_lora = jnp.zeros((ICLR_RANK, B_WIDTH), F32)

    h = _ln_call(x.reshape(n, d), ln0_g, ln0_b).reshape(bsz, s, d)
    for l in range(DEPTH):
        wl = w_in[l]
        wa = jnp.concatenate([_pair_heads(wl[:, :A_WIDTH], 1), wl[:, A_WIDTH:A_COLS]], axis=1)
        wb = wl[:, A_COLS:A_COLS + B_COLS]
        c0 = A_COLS + B_COLS
        wq, wk, wv = (wl[:, c0 + j * C_WIDTH:c0 + (j + 1) * C_WIDTH] for j in range(3))
        pa, pb, qc, kc, vt = _proj_call(h, wa.astype(BF16), wb.astype(BF16), wq.astype(BF16),
                                        wk.astype(BF16), wv.T.astype(BF16))

        out_a = _swa_call(pa, sinks[l].astype(F32), bias_pairs)
        wd_pad = jnp.concatenate([decay_up[l].astype(F32), zeros_lora], axis=0)
        wa_pad = jnp.concatenate([zeros_lora, iclr_up[l].astype(F32)], axis=0)
        out_b = _rwkv_call(pb, shift_mu[l], decay_w0[l], wd_pad, iclr_a0[l], wa_pad,
                           gate_up[l].astype(F32), k_k[l], k_a[l], r_k[l], lnx_g[l], lnx_b[l])
        out_c = _sb_call(qc, kc, vt)

        wo = w_out[l]
        h = _outproj_call(h, out_a, out_b, out_c,
                          _pair_heads(wo[:A_WIDTH], 0).astype(BF16),
                          wo[A_WIDTH:A_WIDTH + B_WIDTH].astype(BF16),
                          wo[A_WIDTH + B_WIDTH:].astype(BF16), ln1_g[l], ln1_b[l])

        hf = h.reshape(n, d)
        gates = _gates_from_router(_router_call(hf, rw_t, rb_col))
        hf = _moe_call(hf, gates, w_gate[l].astype(BF16), w_up[l].astype(BF16),
                       w_down[l].astype(BF16), ln2_g[l], ln2_b[l])
        h = hf.reshape(bsz, s, d)
    return h
```

```python
import functools
import math

import jax
import jax.numpy as jnp
from jax import lax
from jax.experimental import pallas as pl
from jax.experimental.pallas import tpu as pltpu

F32 = jnp.float32
BF16 = jnp.bfloat16
HI = lax.Precision.HIGHEST

DEPTH = 2
HEAD_DIM = 64
BLOCK = 128
LANES = 128
A_Q_HEADS = 6
A_KV_HEADS = 2
WINDOW = 128
A_WIDTH = A_Q_HEADS * HEAD_DIM
A_KV_WIDTH = A_KV_HEADS * HEAD_DIM
B_HEADS = 4
B_WIDTH = B_HEADS * HEAD_DIM
DECAY_RANK = 64
ICLR_RANK = 64
GATE_RANK = 128
GN_EPS = 64e-5
C_HEADS = 6
C_WIDTH = C_HEADS * HEAD_DIM
A_COLS = A_WIDTH + 2 * A_KV_WIDTH
B_COLS = 3 * B_WIDTH + DECAY_RANK + ICLR_RANK + GATE_RANK
NUM_BUCKETS = 32
MAX_EXACT = NUM_BUCKETS // 2
MAX_DISTANCE = 128
N_EXPERTS = 16
N_GROUPS = 4
EXPERTS_PER_GROUP = N_EXPERTS // N_GROUPS
D_FF_EXPERT = 256
LN_EPS = 1e-5
ALPHA = (2 * DEPTH) ** 0.25
SCALE = HEAD_DIM ** -0.5
MASKED = -1e30
CHUNK = 64
SB_TQ = 512
SB_SUB = 256
LOG2E = 1.4426950408889634
ROUTER_ROWS = 8

VMEM_LIMIT = 48 * 1024 * 1024


def _dot(a, b, prec=None):
    return jnp.dot(a, b, preferred_element_type=F32, precision=prec)


def _dot_nt(a, b, prec=None):
    return lax.dot_general(a, b, (((1,), (1,)), ((), ())),
                           preferred_element_type=F32, precision=prec)


def _sigmoid(x):
    return 1.0 / (1.0 + jnp.exp(-x))


def _softplus(x):
    return jnp.maximum(x, 0.0) + jnp.log(1.0 + jnp.exp(-jnp.abs(x)))


def _layer_norm(x, g, b):
    mu = jnp.mean(x, axis=-1, keepdims=True)
    xc = x - mu
    var = jnp.mean(xc * xc, axis=-1, keepdims=True)
    return xc * lax.rsqrt(var + LN_EPS) * g + b


def _params(*sem):
    return pltpu.CompilerParams(dimension_semantics=sem, vmem_limit_bytes=VMEM_LIMIT)


def _ln_kernel(x_ref, g_ref, b_ref, o_ref):
    o_ref[...] = _layer_norm(x_ref[...], g_ref[...], b_ref[...])


def _ln_call(x, g, b, tm=512):
    n, d = x.shape
    return pl.pallas_call(
        _ln_kernel,
        grid=(n // tm,),
        in_specs=[pl.BlockSpec((tm, d), lambda i: (i, 0)),
                  pl.BlockSpec((1, d), lambda i: (0, 0)),
                  pl.BlockSpec((1, d), lambda i: (0, 0))],
        out_specs=pl.BlockSpec((tm, d), lambda i: (i, 0)),
        out_shape=jax.ShapeDtypeStruct((n, d), F32),
        compiler_params=_params("parallel"),
        name="embed_ln",
    )(x, g.reshape(1, d), b.reshape(1, d))


def _proj_kernel(h_ref, wa_ref, wb_ref, wq_ref, wk_ref, wvt_ref,
                 pa_ref, pb_ref, qc_ref, kc_ref, vt_ref):
    hb = h_ref[...].astype(BF16)
    pa_ref[...] = _dot(hb, wa_ref[...]).astype(BF16)
    pb_ref[...] = _dot(hb, wb_ref[...])
    qc_ref[...] = (_dot(hb, wq_ref[...]) * (SCALE * LOG2E)).astype(BF16)
    kc_ref[...] = _dot(hb, wk_ref[...]).astype(BF16)
    vt_ref[...] = _dot_nt(wvt_ref[...], hb).astype(BF16)


def _proj_call(h, wa, wb, wq, wk, wvt, tm=512):
    bsz, s, d = h.shape
    full = lambda w: pl.BlockSpec(w.shape, lambda b, t: (0, 0))
    row = lambda c: pl.BlockSpec((None, tm, c), lambda b, t: (b, t, 0))
    return pl.pallas_call(
        _proj_kernel,
        grid=(bsz, s // tm),
        in_specs=[row(d), full(wa), full(wb), full(wq), full(wk), full(wvt)],
        out_specs=[row(A_COLS), row(B_COLS), row(C_WIDTH), row(C_WIDTH),
                   pl.BlockSpec((None, C_WIDTH, tm), lambda b, t: (b, 0, t))],
        out_shape=[jax.ShapeDtypeStruct((bsz, s, A_COLS), BF16),
                   jax.ShapeDtypeStruct((bsz, s, B_COLS), F32),
                   jax.ShapeDtypeStruct((bsz, s, C_WIDTH), BF16),
                   jax.ShapeDtypeStruct((bsz, s, C_WIDTH), BF16),
                   jax.ShapeDtypeStruct((bsz, C_WIDTH, s), BF16)],
        compiler_params=_params("parallel", "parallel"),
        name="in_proj",
    )(h, wa, wb, wq, wk, wvt)


def _swa_kernel(sink_ref, q_ref, kp_ref, kc_ref, vp_ref, vc_ref, bias_ref, o_ref):
    n = pl.program_id(1)
    kband = jnp.concatenate([kp_ref[...], kc_ref[...]], axis=0)
    vband = jnp.concatenate([vp_ref[...], vc_ref[...]], axis=0)
    lane = lax.broadcasted_iota(jnp.int32, (BLOCK, LANES), 1)
    row2 = lax.broadcasted_iota(jnp.int32, (2 * BLOCK, 1), 0)
    col2 = lax.broadcasted_iota(jnp.int32, (1, 2 * BLOCK), 1)
    pad = jnp.where(jnp.logical_and(n == 0, col2 < BLOCK), MASKED, 0.0)
    for c in range(A_Q_HEADS // 2):
        q2 = q_ref[:, c * LANES:(c + 1) * LANES]
        zero = jnp.zeros_like(q2)
        qs = jnp.concatenate([jnp.where(lane < HEAD_DIM, q2, zero),
                              jnp.where(lane >= HEAD_DIM, q2, zero)], axis=0)
        logits = _dot_nt(qs, kband) * SCALE + bias_ref[c] + pad
        sink = jnp.where(row2 < BLOCK, sink_ref[c], sink_ref[c + 3])
        m = jnp.maximum(jnp.max(logits, axis=-1, keepdims=True), sink)
        p = jnp.exp(logits - m)
        denom = jnp.sum(p, axis=-1, keepdims=True) + jnp.exp(sink - m)
        o = _dot(p.astype(BF16), vband) / denom
        o_ref[:, c * LANES:(c + 1) * LANES] = jnp.where(
            lane < HEAD_DIM, o[:BLOCK], o[BLOCK:]).astype(BF16)


def _swa_call(pa, sinks, bias_pairs):
    bsz, s, _ = pa.shape
    nb = s // BLOCK
    kcol = A_WIDTH // LANES
    vcol = kcol + 1
    prev = lambda n: jnp.maximum(n - 1, 0)
    return pl.pallas_call(
        _swa_kernel,
        grid=(bsz, nb),
        in_specs=[pl.BlockSpec(memory_space=pltpu.SMEM),
                  pl.BlockSpec((None, BLOCK, A_WIDTH), lambda b, n: (b, n, 0)),
                  pl.BlockSpec((None, BLOCK, LANES), lambda b, n: (b, prev(n), kcol)),
                  pl.BlockSpec((None, BLOCK, LANES), lambda b, n: (b, n, kcol)),
                  pl.BlockSpec((None, BLOCK, LANES), lambda b, n: (b, prev(n), vcol)),
                  pl.BlockSpec((None, BLOCK, LANES), lambda b, n: (b, n, vcol)),
                  pl.BlockSpec(bias_pairs.shape, lambda b, n: (0, 0, 0))],
        out_specs=pl.BlockSpec((None, BLOCK, A_WIDTH), lambda b, n: (b, n, 0)),
        out_shape=jax.ShapeDtypeStruct((bsz, s, A_WIDTH), BF16),
        compiler_params=_params("parallel", "parallel"),
        name="swa_attn",
    )(sinks, pa, pa, pa, pa, pa, bias_pairs)


def _t5_causal_bucket(dist):
    dist = jnp.maximum(dist, 0)
    d = jnp.maximum(dist, 1).astype(F32)
    large = MAX_EXACT + (jnp.log(d / MAX_EXACT) / math.log(MAX_DISTANCE / MAX_EXACT)
                         * (NUM_BUCKETS - MAX_EXACT)).astype(jnp.int32)
    large = jnp.minimum(large, NUM_BUCKETS - 1)
    return jnp.where(dist < MAX_EXACT, dist, large)


def _swa_bias_pairs(rel_bias):
    qi = jnp.arange(BLOCK)[:, None]
    ki = jnp.arange(2 * BLOCK)[None, :]
    dist = qi + BLOCK - ki
    in_window = (dist >= 0) & (dist < WINDOW)
    bias = rel_bias.astype(F32)[_t5_causal_bucket(dist)]
    bias = jnp.where(in_window[..., None], bias, MASKED).transpose(2, 0, 1)
    return jnp.stack([jnp.concatenate([bias[c], bias[c + 3]], axis=0)
                      for c in range(A_Q_HEADS // 2)])


def _rwkv_kernel(pb_ref, mu_ref, w0_ref, wd_ref, a0_ref, wa_ref, wg_ref, kk_ref, ka_ref,
                 rk_ref, lng_ref, lnb_ref, o_ref, prev_ref, h_ref):
    t = pl.program_id(1)

    @pl.when(t == 0)
    def _():
        prev_ref[...] = jnp.zeros_like(prev_ref)
        h_ref[...] = jnp.zeros_like(h_ref)

    L = CHUNK
    W = B_WIDTH
    p = pb_ref[...]
    rows = lax.broadcasted_iota(jnp.int32, (L, 1), 0)
    shifted = jnp.where(rows == 0, prev_ref[...], pltpu.roll(p, 1, axis=0))
    prev_ref[...] = p[L - 1:L, :]
    pm = p + (shifted - p) * mu_ref[...]
    r = pm[:, 0:W]
    k = pm[:, W:2 * W]
    v = pm[:, 2 * W:3 * W]
    xwa = pm[:, 3 * W:3 * W + DECAY_RANK + ICLR_RANK]
    xg = pm[:, 3 * W + DECAY_RANK + ICLR_RANK:]

    dw = w0_ref[...] + _dot(jnp.tanh(xwa), wd_ref[...], HI)
    lw = -jnp.exp(-_softplus(-dw) - 0.5)
    a = _sigmoid(a0_ref[...] + _dot(xwa, wa_ref[...], HI))
    g = _dot(_sigmoid(xg), wg_ref[...], HI)

    hr = lax.broadcasted_iota(jnp.int32, (W, W), 0) // HEAD_DIM
    hc = lax.broadcasted_iota(jnp.int32, (W, W), 1) // HEAD_DIM
    head_ones = jnp.where(hr == hc, 1.0, 0.0)
    kk = k * kk_ref[...]
    kk = kk * lax.rsqrt(jnp.maximum(_dot(kk * kk, head_ones, HI), 1e-24))
    k2 = k * (1.0 + (a - 1.0) * ka_ref[...])
    bonus = _dot(r * k2 * rk_ref[...], head_ones, HI) * v
    aa = -kk
    bb = kk * a

    ti = lax.broadcasted_iota(jnp.int32, (L, L), 0)
    tj = lax.broadcasted_iota(jnp.int32, (L, L), 1)
    cum = _dot(jnp.where(ti >= tj, 1.0, 0.0), lw, HI)
    cum_l = cum[L - 1:L, :]
    at = aa * jnp.exp(cum - lw)
    rt = r * jnp.exp(cum)
    inv = jnp.exp(-cum)
    bt = bb * inv
    kt = k2 * inv
    tail = jnp.exp(cum_l - cum)
    bh = bb * tail
    kh = k2 * tail

    lane_head = lax.broadcasted_iota(jnp.int32, (L, W), 1) // HEAD_DIM
    eye = jnp.where(ti == tj, 1.0, 0.0)
    h0 = h_ref[...]

    def only(x, h):
        return jnp.where(lane_head == h, x, 0.0)

    at_s = jnp.concatenate([only(at, h) for h in range(B_HEADS)], axis=0)
    rt_s = jnp.concatenate([only(rt, h) for h in range(B_HEADS)], axis=0)
    ab = _dot_nt(at_s, bt, HI)
    ak = _dot_nt(at_s, kt, HI)
    rb = _dot_nt(rt_s, bt, HI)
    rk = _dot_nt(rt_s, kt, HI)
    at_h0 = _dot(at, h0, HI)
    y = _dot(rt, h0, HI)
    u_parts, bu_rows, bu_cols = [], [], []
    for h in range(B_HEADS):
        sl = slice(h * L, (h + 1) * L)
        a_m = jnp.where(ti > tj, ab[sl], 0.0)
        tinv = eye + a_m
        pw = a_m
        for _ in range(int(math.log2(L)) - 1):
            pw = _dot(pw, pw, HI)
            tinv = tinv + _dot(tinv, pw, HI)
        rhs = at_h0 + _dot(jnp.where(ti > tj, ak[sl], 0.0), v, HI)
        u = only(_dot(tinv, rhs, HI), h)
        y = y + only(_dot(jnp.where(ti >= tj, rb[sl], 0.0), u, HI)
                     + _dot(jnp.where(ti >= tj, rk[sl], 0.0), v, HI), h)
        u_parts.append(u)
    u_all = u_parts[0] + u_parts[1] + u_parts[2] + u_parts[3]
    lhs_t = jnp.concatenate([bh, kh], axis=0).T
    blk_r = lax.broadcasted_iota(jnp.int32, (W, W), 0) // HEAD_DIM
    blk_c = lax.broadcasted_iota(jnp.int32, (W, W), 1) // HEAD_DIM
    upd = _dot(lhs_t, jnp.concatenate([u_all, v], axis=0), HI)
    pl_col = jnp.exp(cum_l).T
    h_ref[...] = pl_col * h0 + jnp.where(blk_r == blk_c, upd, 0.0)

    mean = _dot(y, head_ones, HI) * (1.0 / HEAD_DIM)
    yc = y - mean
    var = _dot(yc * yc, head_ones, HI) * (1.0 / HEAD_DIM)
    yn = yc * lax.rsqrt(var + GN_EPS) * lng_ref[...] + lnb_ref[...]
    o_ref[...] = ((yn + bonus) * g).astype(o_ref.dtype)


def _rwkv_call(pb, mu, w0, wd_pad, a0, wa_pad, wg, k_k, k_a, r_k, lnx_g, lnx_b):
    bsz, s, _ = pb.shape
    vec = lambda x: x.reshape(1, -1).astype(F32)
    small = [vec(mu), vec(w0), wd_pad, vec(a0), wa_pad, wg, vec(k_k), vec(k_a), vec(r_k),
             vec(lnx_g), vec(lnx_b)]
    return pl.pallas_call(
        _rwkv_kernel,
        grid=(bsz, s // CHUNK),
        in_specs=[pl.BlockSpec((None, CHUNK, B_COLS), lambda b, t: (b, t, 0))]
        + [pl.BlockSpec(x.shape, lambda b, t: (0, 0)) for x in small],
        out_specs=pl.BlockSpec((None, CHUNK, B_WIDTH), lambda b, t: (b, t, 0)),
        out_shape=jax.ShapeDtypeStruct((bsz, s, B_WIDTH), BF16),
        scratch_shapes=[pltpu.VMEM((1, B_COLS), F32), pltpu.VMEM((B_WIDTH, B_WIDTH), F32)],
        compiler_params=_params("parallel", "arbitrary"),
        name="rwkv7",
    )(pb, *small)


def _softplus2(z):
    return jnp.maximum(z, 0.0) + jnp.log2(1.0 + jnp.exp2(-jnp.abs(z)))


def _sb_kernel(q_ref, k_ref, vt_ref, o_ref):
    it = pl.program_id(2)
    q2 = q_ref[...]
    lane = lax.broadcasted_iota(jnp.int32, (SB_TQ, LANES), 1)
    zero = jnp.zeros_like(q2)
    qh = (jnp.where(lane < HEAD_DIM, q2, zero), jnp.where(lane >= HEAD_DIM, q2, zero))
    ur = lax.broadcasted_iota(jnp.int32, (BLOCK, 2 * BLOCK), 0)
    uc = lax.broadcasted_iota(jnp.int32, (BLOCK, 2 * BLOCK), 1) % BLOCK
    ucat = jnp.where(uc > ur, 1.0, 0.0).astype(BF16)
    nsub = SB_TQ // BLOCK

    def tile(jb, state, masked):
        off = pl.multiple_of(jb * BLOCK, BLOCK)
        kblk = k_ref[pl.ds(off, BLOCK), :]
        vt2 = vt_ref[:, pl.ds(off, BLOCK)]
        if masked:
            kpos = jb * BLOCK + lax.broadcasted_iota(jnp.int32, (BLOCK, SB_TQ), 0)
            qpos = it * SB_TQ + lax.broadcasted_iota(jnp.int32, (BLOCK, SB_TQ), 1)
            before = kpos < qpos
        zts = [_dot_nt(kblk, qh[h][c * SB_SUB:(c + 1) * SB_SUB]) for h, c in units]
        sps, stacked = [], []
        for u, (h, c) in enumerate(units):
            sp = _softplus2(zts[u])
            if masked:
                sp = jnp.where(before[:, c * SB_SUB:(c + 1) * SB_SUB], sp, 0.0)
            sp_hi = sp.astype(BF16)
            sp_lo = (sp - sp_hi.astype(F32)).astype(BF16)
            sps.append(sp)
            stacked.append(jnp.concatenate([sp_hi, sp_lo], axis=0))
        excls = [_dot(ucat, x) for x in stacked]
        ws, cars = [], []
        for u, (h, c) in enumerate(units):
            acc, car = state[u]
            w = jnp.exp2(zts[u] - sps[u] - excls[u] - car)
            if masked:
                w = jnp.where(before[:, c * SB_SUB:(c + 1) * SB_SUB], w, 0.0)
            ws.append(w.astype(BF16))
            cars.append(car + jnp.sum(sps[u], axis=0, keepdims=True))
        return tuple((state[u][0] + _dot(vt2[h * HEAD_DIM:(h + 1) * HEAD_DIM], ws[u]), cars[u])
                     for u, (h, c) in enumerate(units))

    units = [(h, c) for h in range(2) for c in range(SB_TQ // SB_SUB)]
    init = tuple((jnp.zeros((HEAD_DIM, SB_SUB), F32), jnp.zeros((1, SB_SUB), F32))
                 for _ in units)
    first = it * nsub + nsub - 1
    state = lax.fori_loop(0, nsub, lambda d, st: tile(first - d, st, True), init)
    state = lax.fori_loop(nsub, first + 1, lambda d, st: tile(first - d, st, False), state)
    accs = [s[0] for s in state]
    ncol = SB_TQ // SB_SUB
    out_t = jnp.concatenate([jnp.concatenate(accs[h * ncol:(h + 1) * ncol], axis=1)
                             for h in range(2)], axis=0)
    o_ref[...] = out_t.T.astype(o_ref.dtype)


def _sb_call(qc, kc, vt):
    bsz, s, _ = qc.shape
    return pl.pallas_call(
        _sb_kernel,
        grid=(bsz, C_WIDTH // LANES, s // SB_TQ),
        in_specs=[pl.BlockSpec((None, SB_TQ, LANES), lambda b, hp, i: (b, i, hp)),
                  pl.BlockSpec((None, s, LANES), lambda b, hp, i: (b, 0, hp)),
                  pl.BlockSpec((None, LANES, s), lambda b, hp, i: (b, hp, 0))],
        out_specs=pl.BlockSpec((None, SB_TQ, LANES), lambda b, hp, i: (b, i, hp)),
        out_shape=jax.ShapeDtypeStruct((bsz, s, C_WIDTH), BF16),
        compiler_params=_params("parallel", "parallel", "parallel"),
        name="stickbreak_attn",
    )(qc, kc, vt)


def _outproj_kernel(h_ref, oa_ref, ob_ref, oc_ref, wa_ref, wb_ref, wc_ref, g_ref, b_ref, o_ref):
    m = (_dot(oa_ref[...], wa_ref[...]) + _dot(ob_ref[...], wb_ref[...])
         + _dot(oc_ref[...], wc_ref[...]))
    o_ref[...] = _layer_norm(ALPHA * h_ref[...] + m, g_ref[...], b_ref[...])


def _outproj_call(h, oa, ob, oc, wa, wb, wc, g, b, tm=512):
    bsz, s, d = h.shape
    row = lambda c: pl.BlockSpec((None, tm, c), lambda bb, t: (bb, t, 0))
    full = lambda w: pl.BlockSpec(w.shape, lambda bb, t: (0, 0))
    g2, b2 = g.reshape(1, d), b.reshape(1, d)
    return pl.pallas_call(
        _outproj_kernel,
        grid=(bsz, s // tm),
        in_specs=[row(d), row(A_WIDTH), row(B_WIDTH), row(C_WIDTH),
                  full(wa), full(wb), full(wc), full(g2), full(b2)],
        out_specs=row(d),
        out_shape=jax.ShapeDtypeStruct((bsz, s, d), F32),
        compiler_params=_params("parallel", "parallel"),
        name="out_proj_ln",
    )(h, oa, ob, oc, wa, wb, wc, g2, b2)


def _router_kernel(x_ref, rw_ref, rb_ref, o_ref):
    logits = _dot_nt(rw_ref[...], x_ref[...], HI)
    scores = _sigmoid(logits)
    sel = scores + rb_ref[...]
    R = ROUTER_ROWS
    s = [sel[m * R:(m + 1) * R] for m in range(EXPERTS_PER_GROUP)]
    sc = [scores[m * R:(m + 1) * R] for m in range(EXPERTS_PER_GROUP)]
    hi01, lo01 = jnp.maximum(s[0], s[1]), jnp.minimum(s[0], s[1])
    hi23, lo23 = jnp.maximum(s[2], s[3]), jnp.minimum(s[2], s[3])
    top1 = jnp.maximum(hi01, hi23)
    top2 = jnp.maximum(jnp.minimum(hi01, hi23), jnp.maximum(lo01, lo23))
    gscore = top1 + top2
    gi = lax.broadcasted_iota(jnp.int32, gscore.shape, 0)
    gmax = jnp.max(gscore, axis=0, keepdims=True)
    best = jnp.min(jnp.where(gscore == gmax, gi, R), axis=0, keepdims=True)
    in_group = gi == best
    picked = []
    for m in range(EXPERTS_PER_GROUP):
        rank = jnp.zeros(gscore.shape, jnp.int32)
        for j in range(EXPERTS_PER_GROUP):
            if j == m:
                continue
            ahead = (s[j] >= s[m]) if j < m else (s[j] > s[m])
            rank = rank + jnp.where(ahead, 1, 0)
        picked.append(jnp.where(jnp.logical_and(in_group, rank < 2), sc[m], 0.0))
    denom = jnp.sum(picked[0] + picked[1] + picked[2] + picked[3], axis=0, keepdims=True)
    for m in range(EXPERTS_PER_GROUP):
        o_ref[m * R:(m + 1) * R, :] = picked[m] / denom


def _router_call(x, rw_t, rb_col, tm=1024):
    n, d = x.shape
    rows = rw_t.shape[0]
    return pl.pallas_call(
        _router_kernel,
        grid=(n // tm,),
        in_specs=[pl.BlockSpec((tm, d), lambda i: (i, 0)),
                  pl.BlockSpec((rows, d), lambda i: (0, 0)),
                  pl.BlockSpec((rows, 1), lambda i: (0, 0))],
        out_specs=pl.BlockSpec((rows, tm), lambda i: (0, i)),
        out_shape=jax.ShapeDtypeStruct((rows, n), F32),
        compiler_params=_params("parallel"),
        name="moe_router",
    )(x, rw_t, rb_col)


def _moe_kernel(x_ref, gate_ref, wg_ref, wu_ref, wd_ref, g_ref, b_ref, o_ref, xb_ref, acc_ref):
    e = pl.program_id(1)

    @pl.when(e == 0)
    def _():
        xb_ref[...] = x_ref[...].astype(BF16)
        acc_ref[...] = jnp.zeros_like(acc_ref)

    xb = xb_ref[...]
    hg = _dot(xb, wg_ref[...])
    hu = _dot(xb, wu_ref[...])
    gates = gate_ref[...]
    lane = lax.broadcasted_iota(jnp.int32, gates.shape, 1)
    gcol = jnp.sum(jnp.where(lane == e, gates, 0.0), axis=-1, keepdims=True)
    act = (hg * _sigmoid(hg)) * hu * gcol
    acc_ref[...] += _dot(act.astype(BF16), wd_ref[...])

    @pl.when(e == N_EXPERTS - 1)
    def _():
        o_ref[...] = _layer_norm(ALPHA * x_ref[...] + acc_ref[...], g_ref[...], b_ref[...])


def _moe_call(x, gates, wg, wu, wd, g, b, tm=1024):
    n, d = x.shape
    f = wg.shape[-1]
    g2, b2 = g.reshape(1, d), b.reshape(1, d)
    return pl.pallas_call(
        _moe_kernel,
        grid=(n // tm, N_EXPERTS),
        in_specs=[pl.BlockSpec((tm, d), lambda i, e: (i, 0)),
                  pl.BlockSpec((tm, N_EXPERTS), lambda i, e: (i, 0)),
                  pl.BlockSpec((None, d, f), lambda i, e: (e, 0, 0)),
                  pl.BlockSpec((None, d, f), lambda i, e: (e, 0, 0)),
                  pl.BlockSpec((None, f, d), lambda i, e: (e, 0, 0)),
                  pl.BlockSpec((1, d), lambda i, e: (0, 0)),
                  pl.BlockSpec((1, d), lambda i, e: (0, 0))],
        out_specs=pl.BlockSpec((tm, d), lambda i, e: (i, 0)),
        out_shape=jax.ShapeDtypeStruct((n, d), F32),
        scratch_shapes=[pltpu.VMEM((tm, d), BF16), pltpu.VMEM((tm, d), F32)],
        compiler_params=_params("parallel", "arbitrary"),
        name="moe_experts_ln",
    )(x, gates, wg, wu, wd, g2, b2)


def _pair_heads(x, axis):
    shape = x.shape
    x = x.reshape(shape[:axis] + (A_KV_HEADS, A_Q_HEADS // A_KV_HEADS, HEAD_DIM) + shape[axis + 1:])
    x = jnp.swapaxes(x, axis, axis + 1)
    return x.reshape(shape)


def _router_layout(router_w, router_bias):
    d = router_w.shape[0]
    w = router_w.astype(F32).T.reshape(N_GROUPS, EXPERTS_PER_GROUP, d).transpose(1, 0, 2)
    w = jnp.pad(w, ((0, 0), (0, ROUTER_ROWS - N_GROUPS), (0, 0)))
    b = router_bias.astype(F32).reshape(N_GROUPS, EXPERTS_PER_GROUP).T
    b = jnp.pad(b, ((0, 0), (0, ROUTER_ROWS - N_GROUPS)), constant_values=MASKED)
    rows = EXPERTS_PER_GROUP * ROUTER_ROWS
    return w.reshape(rows, d), b.reshape(rows, 1)


def _gates_from_router(gates_t):
    n = gates_t.shape[1]
    g = gates_t.reshape(EXPERTS_PER_GROUP, ROUTER_ROWS, n)[:, :N_GROUPS]
    return g.transpose(2, 1, 0).reshape(n, N_EXPERTS)


def kernel(x, ln0_g, ln0_b, w_in, w_out, sinks, rel_bias, shift_mu, decay_w0, decay_up, iclr_a0,
           iclr_up, gate_up, k_k, k_a, r_k, lnx_g, lnx_b, ln1_g, ln1_b, router_w, router_bias,
           w_gate, w_up, w_down, ln2_g, ln2_b):
    bsz, s, d = x.shape
    n = bsz * s
    bias_pairs = _swa_bias_pairs(rel_bias)
    rw_t, rb_col = _router_layout(router_w, router_bias)
    zeros_lora = jnp.zeros((ICLR_RANK, B_WIDTH), F32)

    h = _ln_call(x.reshape(n, d), ln0_g, ln0_b).reshape(bsz, s, d)
    for l in range(DEPTH):
        wl = w_in[l]
        wa = jnp.concatenate([_pair_heads(wl[:, :A_WIDTH], 1), wl[:, A_WIDTH:A_COLS]], axis=1)
        wb = wl[:, A_COLS:A_COLS + B_COLS]
        c0 = A_COLS + B_COLS
        wq, wk, wv = (wl[:, c0 + j * C_WIDTH:c0 + (j + 1) * C_WIDTH] for j in range(3))
        pa, pb, qc, kc, vt = _proj_call(h, wa.astype(BF16), wb.astype(BF16), wq.astype(BF16),
                                        wk.astype(BF16), wv.T.astype(BF16))

        out_a = _swa_call(pa, sinks[l].astype(F32), bias_pairs)
        wd_pad = jnp.concatenate([decay_up[l].astype(F32), zeros_lora], axis=0)
        wa_pad = jnp.concatenate([zeros_lora, iclr_up[l].astype(F32)], axis=0)
        out_b = _rwkv_call(pb, shift_mu[l], decay_w0[l], wd_pad, iclr_a0[l], wa_pad,
                           gate_up[l].astype(F32), k_k[l], k_a[l], r_k[l], lnx_g[l], lnx_b[l])
        out_c = _sb_call(qc, kc, vt)

        wo = w_out[l]
        h = _outproj_call(h, out_a, out_b, out_c,
                          _pair_heads(wo[:A_WIDTH], 0).astype(BF16),
                          wo[A_WIDTH:A_WIDTH + B_WIDTH].astype(BF16),
                          wo[A_WIDTH + B_WIDTH:].astype(BF16), ln1_g[l], ln1_b[l])

        hf = h.reshape(n, d)
        gates = _gates_from_router(_router_call(hf, rw_t, rb_col))
        hf = _moe_call(hf, gates, w_gate[l].astype(BF16), w_up[l].astype(BF16),
                       w_down[l].astype(BF16), ln2_g[l], ln2_b[l])
        h = hf.reshape(bsz, s, d)
    return h
```

```python
import functools
import math

import jax
import jax.numpy as jnp
from jax import lax
from jax.experimental import pallas as pl
from jax.experimental.pallas import tpu as pltpu

F32 = jnp.float32
BF16 = jnp.bfloat16
HI = lax.Precision.HIGHEST

DEPTH = 2
HEAD_DIM = 64
BLOCK = 128
LANES = 128
A_Q_HEADS = 6
A_KV_HEADS = 2
WINDOW = 128
A_WIDTH = A_Q_HEADS * HEAD_DIM
A_KV_WIDTH = A_KV_HEADS * HEAD_DIM
B_HEADS = 4
B_WIDTH = B_HEADS * HEAD_DIM
DECAY_RANK = 64
ICLR_RANK = 64
GATE_RANK = 128
GN_EPS = 64e-5
C_HEADS = 6
C_WIDTH = C_HEADS * HEAD_DIM
A_COLS = A_WIDTH + 2 * A_KV_WIDTH
B_COLS = 3 * B_WIDTH + DECAY_RANK + ICLR_RANK + GATE_RANK
NUM_BUCKETS = 32
MAX_EXACT = NUM_BUCKETS // 2
MAX_DISTANCE = 128
N_EXPERTS = 16
N_GROUPS = 4
EXPERTS_PER_GROUP = N_EXPERTS // N_GROUPS
D_FF_EXPERT = 256
LN_EPS = 1e-5
ALPHA = (2 * DEPTH) ** 0.25
SCALE = HEAD_DIM ** -0.5
MASKED = -1e30
CHUNK = 64
RWKV_TS = 256
SB_TQ = 512
LOG2E = 1.4426950408889634
ROUTER_ROWS = 8

VMEM_LIMIT = 48 * 1024 * 1024


def _dot(a, b, prec=None):
    return jnp.dot(a, b, preferred_element_type=F32, precision=prec)


def _dot_nt(a, b, prec=None):
    return lax.dot_general(a, b, (((1,), (1,)), ((), ())),
                           preferred_element_type=F32, precision=prec)


def _sigmoid(x):
    return 1.0 / (1.0 + jnp.exp(-x))


def _softplus(x):
    return jnp.maximum(x, 0.0) + jnp.log(1.0 + jnp.exp(-jnp.abs(x)))


def _layer_norm(x, g, b):
    mu = jnp.mean(x, axis=-1, keepdims=True)
    xc = x - mu
    var = jnp.mean(xc * xc, axis=-1, keepdims=True)
    return xc * lax.rsqrt(var + LN_EPS) * g + b


def _params(*sem):
    return pltpu.CompilerParams(dimension_semantics=sem, vmem_limit_bytes=VMEM_LIMIT)


def _ln_kernel(x_ref, g_ref, b_ref, o_ref):
    o_ref[...] = _layer_norm(x_ref[...], g_ref[...], b_ref[...])


def _ln_call(x, g, b, tm=512):
    n, d = x.shape
    return pl.pallas_call(
        _ln_kernel,
        grid=(n // tm,),
        in_specs=[pl.BlockSpec((tm, d), lambda i: (i, 0)),
                  pl.BlockSpec((1, d), lambda i: (0, 0)),
                  pl.BlockSpec((1, d), lambda i: (0, 0))],
        out_specs=pl.BlockSpec((tm, d), lambda i: (i, 0)),
        out_shape=jax.ShapeDtypeStruct((n, d), F32),
        compiler_params=_params("parallel"),
        name="embed_ln",
    )(x, g.reshape(1, d), b.reshape(1, d))


def _proj_kernel(h_ref, wa_ref, wb_ref, wq_ref, wk_ref, wvt_ref,
                 pa_ref, pb_ref, qc_ref, kc_ref, vt_ref):
    hb = h_ref[...].astype(BF16)
    pa_ref[...] = _dot(hb, wa_ref[...]).astype(BF16)
    pb_ref[...] = _dot(hb, wb_ref[...])
    qc_ref[...] = (_dot(hb, wq_ref[...]) * (SCALE * LOG2E)).astype(BF16)
    kc_ref[...] = _dot(hb, wk_ref[...]).astype(BF16)
    vt_ref[...] = _dot_nt(wvt_ref[...], hb).astype(BF16)


def _proj_call(h, wa, wb, wq, wk, wvt, tm=512):
    bsz, s, d = h.shape
    full = lambda w: pl.BlockSpec(w.shape, lambda b, t: (0, 0))
    row = lambda c: pl.BlockSpec((None, tm, c), lambda b, t: (b, t, 0))
    return pl.pallas_call(
        _proj_kernel,
        grid=(bsz, s // tm),
        in_specs=[row(d), full(wa), full(wb), full(wq), full(wk), full(wvt)],
        out_specs=[row(A_COLS), row(B_COLS), row(C_WIDTH), row(C_WIDTH),
                   pl.BlockSpec((None, C_WIDTH, tm), lambda b, t: (b, 0, t))],
        out_shape=[jax.ShapeDtypeStruct((bsz, s, A_COLS), BF16),
                   jax.ShapeDtypeStruct((bsz, s, B_COLS), F32),
                   jax.ShapeDtypeStruct((bsz, s, C_WIDTH), BF16),
                   jax.ShapeDtypeStruct((bsz, s, C_WIDTH), BF16),
                   jax.ShapeDtypeStruct((bsz, C_WIDTH, s), BF16)],
        compiler_params=_params("parallel", "parallel"),
        name="in_proj",
    )(h, wa, wb, wq, wk, wvt)


def _swa_kernel(sink_ref, q_ref, kp_ref, kc_ref, vp_ref, vc_ref, bias_ref, o_ref):
    n = pl.program_id(1)
    kband = jnp.concatenate([kp_ref[...], kc_ref[...]], axis=0)
    vband = jnp.concatenate([vp_ref[...], vc_ref[...]], axis=0)
    lane = lax.broadcasted_iota(jnp.int32, (BLOCK, LANES), 1)
    row2 = lax.broadcasted_iota(jnp.int32, (2 * BLOCK, 1), 0)
    col2 = lax.broadcasted_iota(jnp.int32, (1, 2 * BLOCK), 1)
    pad = jnp.where(jnp.logical_and(n == 0, col2 < BLOCK), MASKED, 0.0)
    for c in range(A_Q_HEADS // 2):
        q2 = q_ref[:, c * LANES:(c + 1) * LANES]
        zero = jnp.zeros_like(q2)
        qs = jnp.concatenate([jnp.where(lane < HEAD_DIM, q2, zero),
                              jnp.where(lane >= HEAD_DIM, q2, zero)], axis=0)
        logits = _dot_nt(qs, kband) * SCALE + bias_ref[c] + pad
        sink = jnp.where(row2 < BLOCK, sink_ref[c], sink_ref[c + 3])
        m = jnp.maximum(jnp.max(logits, axis=-1, keepdims=True), sink)
        p = jnp.exp(logits - m)
        denom = jnp.sum(p, axis=-1, keepdims=True) + jnp.exp(sink - m)
        o = _dot(p.astype(BF16), vband) / denom
        o_ref[:, c * LANES:(c + 1) * LANES] = jnp.where(
            lane < HEAD_DIM, o[:BLOCK], o[BLOCK:]).astype(BF16)


def _swa_call(pa, sinks, bias_pairs):
    bsz, s, _ = pa.shape
    nb = s // BLOCK
    kcol = A_WIDTH // LANES
    vcol = kcol + 1
    prev = lambda n: jnp.maximum(n - 1, 0)
    return pl.pallas_call(
        _swa_kernel,
        grid=(bsz, nb),
        in_specs=[pl.BlockSpec(memory_space=pltpu.SMEM),
                  pl.BlockSpec((None, BLOCK, A_WIDTH), lambda b, n: (b, n, 0)),
                  pl.BlockSpec((None, BLOCK, LANES), lambda b, n: (b, prev(n), kcol)),
                  pl.BlockSpec((None, BLOCK, LANES), lambda b, n: (b, n, kcol)),
                  pl.BlockSpec((None, BLOCK, LANES), lambda b, n: (b, prev(n), vcol)),
                  pl.BlockSpec((None, BLOCK, LANES), lambda b, n: (b, n, vcol)),
                  pl.BlockSpec(bias_pairs.shape, lambda b, n: (0, 0, 0))],
        out_specs=pl.BlockSpec((None, BLOCK, A_WIDTH), lambda b, n: (b, n, 0)),
        out_shape=jax.ShapeDtypeStruct((bsz, s, A_WIDTH), BF16),
        compiler_params=_params("parallel", "parallel"),
        name="swa_attn",
    )(sinks, pa, pa, pa, pa, pa, bias_pairs)


def _t5_causal_bucket(dist):
    dist = jnp.maximum(dist, 0)
    d = jnp.maximum(dist, 1).astype(F32)
    large = MAX_EXACT + (jnp.log(d / MAX_EXACT) / math.log(MAX_DISTANCE / MAX_EXACT)
                         * (NUM_BUCKETS - MAX_EXACT)).astype(jnp.int32)
    large = jnp.minimum(large, NUM_BUCKETS - 1)
    return jnp.where(dist < MAX_EXACT, dist, large)


def _swa_bias_pairs(rel_bias):
    qi = jnp.arange(BLOCK)[:, None]
    ki = jnp.arange(2 * BLOCK)[None, :]
    dist = qi + BLOCK - ki
    in_window = (dist >= 0) & (dist < WINDOW)
    bias = rel_bias.astype(F32)[_t5_causal_bucket(dist)]
    bias = jnp.where(in_window[..., None], bias, MASKED).transpose(2, 0, 1)
    return jnp.stack([jnp.concatenate([bias[c], bias[c + 3]], axis=0)
                      for c in range(A_Q_HEADS // 2)])


def _split2(x):
    hi = x.astype(BF16)
    return hi, (x - hi.astype(F32)).astype(BF16)


def _dot3(a, b, nt=False):
    ah, al = _split2(a)
    bh, bl = _split2(b)
    d = _dot_nt if nt else _dot
    return d(ah, bh) + d(ah, bl) + d(al, bh)


def _dot3_many(pairs, nt=False):
    parts = [(_split2(a), _split2(b)) for a, b in pairs]
    d = _dot_nt if nt else _dot
    return [d(ah, bh) + d(ah, bl) + d(al, bh) for (ah, al), (bh, bl) in parts]


def _dot_x2(a, b_exact):
    ah, al = _split2(a)
    return _dot(ah, b_exact) + _dot(al, b_exact)


def _dot_2x(a_exact, b):
    bh, bl = _split2(b)
    return _dot(a_exact, bh) + _dot(a_exact, bl)


def _rwkv_kernel(pb_ref, mu_ref, w0_ref, wd_ref, a0_ref, wa_ref, wg_ref, kk_ref, ka_ref,
                 rk_ref, lng_ref, lnb_ref, o_ref, prev_ref, h_ref):
    t = pl.program_id(1)

    @pl.when(t == 0)
    def _():
        prev_ref[...] = jnp.zeros_like(prev_ref)
        h_ref[...] = jnp.zeros_like(h_ref)

    L = CHUNK
    W = B_WIDTH
    TS = RWKV_TS
    p = pb_ref[...]
    rows = lax.broadcasted_iota(jnp.int32, (TS, 1), 0)
    shifted = jnp.where(rows == 0, prev_ref[...], pltpu.roll(p, 1, axis=0))
    prev_ref[...] = p[TS - 1:TS, :]
    pm = p + (shifted - p) * mu_ref[...]
    r = pm[:, 0:W]
    k = pm[:, W:2 * W]
    v = pm[:, 2 * W:3 * W]
    xwa = pm[:, 3 * W:3 * W + DECAY_RANK + ICLR_RANK]
    xg = pm[:, 3 * W + DECAY_RANK + ICLR_RANK:]

    dw = w0_ref[...] + _dot(jnp.tanh(xwa).astype(BF16), wd_ref[...])
    lw = -jnp.exp(-_softplus(-dw) - 0.5)
    a = _sigmoid(a0_ref[...] + _dot(xwa.astype(BF16), wa_ref[...]))
    g = _dot(_sigmoid(xg).astype(BF16), wg_ref[...])

    hr = lax.broadcasted_iota(jnp.int32, (W, W), 0) // HEAD_DIM
    hc = lax.broadcasted_iota(jnp.int32, (W, W), 1) // HEAD_DIM
    same_head = hr == hc
    diag_w = (lax.broadcasted_iota(jnp.int32, (W, W), 0)
              == lax.broadcasted_iota(jnp.int32, (W, W), 1))
    head_ones = jnp.where(same_head, 1.0, 0.0).astype(BF16)
    kk = k * kk_ref[...]
    kk = kk * lax.rsqrt(jnp.maximum(_dot_x2(kk * kk, head_ones), 1e-24))
    k2 = k * (1.0 + (a - 1.0) * ka_ref[...])
    bonus = _dot_x2(r * k2 * rk_ref[...], head_ones) * v
    aa = -kk
    bb = kk * a

    ti = lax.broadcasted_iota(jnp.int32, (L, L), 0)
    tj = lax.broadcasted_iota(jnp.int32, (L, L), 1)
    lower = jnp.where(ti >= tj, 1.0, 0.0).astype(BF16)
    eye = jnp.where(ti == tj, 1.0, 0.0)
    lane_head = lax.broadcasted_iota(jnp.int32, (L, W), 1) // HEAD_DIM

    def only(x, h):
        return jnp.where(lane_head == h, x, 0.0)

    nch = TS // L
    chunks = range(nch)
    units = [(c, h) for c in chunks for h in range(B_HEADS)]
    hsl = [slice(h * L, (h + 1) * L) for h in range(B_HEADS)]
    csl = [slice(c * L, (c + 1) * L) for c in chunks]
    v_c = [v[s] for s in csl]
    cum = [_dot_2x(lower, lw[s]) for s in csl]
    cum_l = [x[L - 1:L, :] for x in cum]
    at = [aa[csl[c]] * jnp.exp(cum[c] - lw[csl[c]]) for c in chunks]
    rt = [r[csl[c]] * jnp.exp(cum[c]) for c in chunks]
    inv = [jnp.exp(-x) for x in cum]
    bt = [bb[csl[c]] * inv[c] for c in chunks]
    kt = [k2[csl[c]] * inv[c] for c in chunks]
    tail = [jnp.exp(cum_l[c] - cum[c]) for c in chunks]
    bh = [bb[csl[c]] * tail[c] for c in chunks]
    kh = [k2[csl[c]] * tail[c] for c in chunks]

    at_s = [jnp.concatenate([only(x, h) for h in range(B_HEADS)], axis=0) for x in at]
    rt_s = [jnp.concatenate([only(x, h) for h in range(B_HEADS)], axis=0) for x in rt]
    ab = _dot3_many([(at_s[c], bt[c]) for c in chunks], nt=True)
    ak = _dot3_many([(at_s[c], kt[c]) for c in chunks], nt=True)
    rb = _dot3_many([(rt_s[c], bt[c]) for c in chunks], nt=True)
    rk = _dot3_many([(rt_s[c], kt[c]) for c in chunks], nt=True)

    pw = [jnp.where(ti > tj, ab[c][hsl[h]], 0.0) for c, h in units]
    tinv = [eye + x for x in pw]
    for _ in range(int(math.log2(L)) - 1):
        pw = _dot3_many([(x, x) for x in pw])
        tinv = [t + d for t, d in zip(tinv, _dot3_many(list(zip(tinv, pw))))]
    ak_v = _dot3_many([(jnp.where(ti > tj, ak[c][hsl[h]], 0.0), v_c[c]) for c, h in units])
    w_u = _dot3_many([(tinv[i], only(at[c], h)) for i, (c, h) in enumerate(units)])
    u0_u = [only(x, h) for x, (c, h) in zip(_dot3_many(list(zip(tinv, ak_v))), units)]
    rb_l = [jnp.where(ti >= tj, rb[c][hsl[h]], 0.0) for c, h in units]
    rk_l = [jnp.where(ti >= tj, rk[c][hsl[h]], 0.0) for c, h in units]
    qm_u = _dot3_many(list(zip(rb_l, w_u)))
    y0_u = [only(p + q, h) for p, q, (c, h) in zip(
        _dot3_many(list(zip(rb_l, u0_u))),
        _dot3_many([(rk_l[i], v_c[c]) for i, (c, h) in enumerate(units)]), units)]

    def chunk_sum(xs, c):
        return functools.reduce(lambda p, q: p + q, xs[c * B_HEADS:(c + 1) * B_HEADS])

    w_sum = [chunk_sum(w_u, c) for c in chunks]
    u0 = [chunk_sum(u0_u, c) for c in chunks]
    qm = [rt[c] + chunk_sum(qm_u, c) for c in chunks]
    y0 = [chunk_sum(y0_u, c) for c in chunks]
    bw = _dot3_many([(bh[c].T, w_sum[c]) for c in chunks])
    g_mat = [jnp.where(same_head, bw[c], 0.0) + jnp.where(diag_w, jnp.exp(cum_l[c]).T, 0.0)
             for c in chunks]
    c_mat = [jnp.where(same_head, x, 0.0) for x in _dot3_many(
        [(jnp.concatenate([bh[c], kh[c]], axis=0).T, jnp.concatenate([u0[c], v_c[c]], axis=0))
         for c in chunks])]

    hst = h_ref[...]
    ys = []
    for c in chunks:
        ys.append(_dot3(qm[c], hst) + y0[c])
        hst = _dot3(g_mat[c], hst) + c_mat[c]
    h_ref[...] = hst
    y = jnp.concatenate(ys, axis=0)

    mean = _dot_x2(y, head_ones) * (1.0 / HEAD_DIM)
    yc = y - mean
    var = _dot_x2(yc * yc, head_ones) * (1.0 / HEAD_DIM)
    yn = yc * lax.rsqrt(var + GN_EPS) * lng_ref[...] + lnb_ref[...]
    o_ref[...] = ((yn + bonus) * g).astype(o_ref.dtype)


def _rwkv_call(pb, mu, w0, wd_pad, a0, wa_pad, wg, k_k, k_a, r_k, lnx_g, lnx_b):
    bsz, s, _ = pb.shape
    vec = lambda x: x.reshape(1, -1).astype(F32)
    small = [vec(mu), vec(w0), wd_pad, vec(a0), wa_pad, wg, vec(k_k), vec(k_a), vec(r_k),
             vec(lnx_g), vec(lnx_b)]
    return pl.pallas_call(
        _rwkv_kernel,
        grid=(bsz, s // RWKV_TS),
        in_specs=[pl.BlockSpec((None, RWKV_TS, B_COLS), lambda b, t: (b, t, 0))]
        + [pl.BlockSpec(x.shape, lambda b, t: (0, 0)) for x in small],
        out_specs=pl.BlockSpec((None, RWKV_TS, B_WIDTH), lambda b, t: (b, t, 0)),
        out_shape=jax.ShapeDtypeStruct((bsz, s, B_WIDTH), BF16),
        scratch_shapes=[pltpu.VMEM((1, B_COLS), F32), pltpu.VMEM((B_WIDTH, B_WIDTH), F32)],
        compiler_params=_params("parallel", "arbitrary"),
        name="rwkv7",
    )(pb, *small)


def _softplus2(z):
    return jnp.maximum(z, 0.0) + jnp.log2(1.0 + jnp.exp2(-jnp.abs(z)))


def _sb_kernel(q_ref, k_ref, vt_ref, o_ref, z_sc, lb_sc, e_sc, w_sc, acc_sc):
    it = pl.program_id(2)
    q2 = q_ref[...]
    lane = lax.broadcasted_iota(jnp.int32, (SB_TQ, LANES), 1)
    zero = jnp.zeros_like(q2)
    qh = (jnp.where(lane < HEAD_DIM, q2, zero), jnp.where(lane >= HEAD_DIM, q2, zero))
    ur = lax.broadcasted_iota(jnp.int32, (BLOCK, 2 * BLOCK), 0)
    uc = lax.broadcasted_iota(jnp.int32, (BLOCK, 2 * BLOCK), 1) % BLOCK
    ucat = jnp.where(uc > ur, 1.0, 0.0).astype(BF16)
    nsub = SB_TQ // BLOCK
    first = it * nsub + nsub - 1
    nblk = first + 1
    wide = 2 * SB_TQ

    def key_off(n):
        return pl.multiple_of(jnp.clip(first - n, 0, first) * BLOCK, BLOCK)

    def logits(n):
        kblk = k_ref[pl.ds(key_off(n), BLOCK), :]
        return jnp.concatenate([_dot_nt(kblk, qh[0]), _dot_nt(kblk, qh[1])], axis=1)

    def softplus_stage(zt, n, masked):
        sp = _softplus2(zt)
        lb = zt - sp
        if masked:
            kpos = key_off(n) + lax.broadcasted_iota(jnp.int32, (BLOCK, wide), 0)
            qpos = it * SB_TQ + lax.broadcasted_iota(jnp.int32, (BLOCK, wide), 1) % SB_TQ
            before = kpos < qpos
            sp = jnp.where(before, sp, 0.0)
            lb = jnp.where(before, lb, MASKED)
        sp_hi = sp.astype(BF16)
        sp_lo = (sp - sp_hi.astype(F32)).astype(BF16)
        stacked = jnp.concatenate([sp_hi, sp_lo], axis=0)
        return lb, stacked, jnp.sum(sp, axis=0, keepdims=True)

    def value_stage(n):
        vt2 = vt_ref[:, pl.ds(key_off(n), BLOCK)]
        w = w_sc[...]
        for h in range(2):
            cols = slice(h * SB_TQ, (h + 1) * SB_TQ)
            acc_sc[:, cols] += _dot(vt2[h * HEAD_DIM:(h + 1) * HEAD_DIM], w[:, cols])

    def step(n, carry, masked):
        car, colsum = carry
        value_stage(n - 1)
        z_new = logits(n + 2)
        lb_new, stacked, colsum_new = softplus_stage(z_sc[...], n + 1, masked)
        e_new = _dot(ucat, stacked)
        w_sc[...] = jnp.exp2(lb_sc[...] - e_sc[...] - car).astype(BF16)
        z_sc[...] = z_new
        lb_sc[...] = lb_new
        e_sc[...] = e_new
        return car + colsum, colsum_new

    lb0, stacked0, colsum0 = softplus_stage(logits(0), 0, True)
    lb_sc[...] = lb0
    e_sc[...] = _dot(ucat, stacked0)
    z_sc[...] = logits(1)
    w_sc[...] = jnp.zeros_like(w_sc)
    acc_sc[...] = jnp.zeros_like(acc_sc)
    carry = (jnp.zeros((1, wide), F32), colsum0)
    carry = lax.fori_loop(0, nsub, lambda n, c: step(n, c, True), carry)
    lax.fori_loop(nsub, nblk, lambda n, c: step(n, c, False), carry)
    value_stage(nblk - 1)
    acc = acc_sc[...]
    out_t = jnp.concatenate([acc[:, :SB_TQ], acc[:, SB_TQ:]], axis=0)
    o_ref[...] = out_t.T.astype(o_ref.dtype)


def _sb_call(qc, kc, vt):
    bsz, s, _ = qc.shape
    return pl.pallas_call(
        _sb_kernel,
        grid=(bsz, C_WIDTH // LANES, s // SB_TQ),
        in_specs=[pl.BlockSpec((None, SB_TQ, LANES), lambda b, hp, i: (b, i, hp)),
                  pl.BlockSpec((None, s, LANES), lambda b, hp, i: (b, 0, hp)),
                  pl.BlockSpec((None, LANES, s), lambda b, hp, i: (b, hp, 0))],
        out_specs=pl.BlockSpec((None, SB_TQ, LANES), lambda b, hp, i: (b, i, hp)),
        out_shape=jax.ShapeDtypeStruct((bsz, s, C_WIDTH), BF16),
        scratch_shapes=[pltpu.VMEM((BLOCK, 2 * SB_TQ), F32),
                        pltpu.VMEM((BLOCK, 2 * SB_TQ), F32),
                        pltpu.VMEM((BLOCK, 2 * SB_TQ), F32),
                        pltpu.VMEM((BLOCK, 2 * SB_TQ), BF16),
                        pltpu.VMEM((HEAD_DIM, 2 * SB_TQ), F32)],
        compiler_params=_params("parallel", "parallel", "parallel"),
        name="stickbreak_attn",
    )(qc, kc, vt)


def _outproj_kernel(h_ref, oa_ref, ob_ref, oc_ref, wa_ref, wb_ref, wc_ref, g_ref, b_ref, o_ref):
    m = (_dot(oa_ref[...], wa_ref[...]) + _dot(ob_ref[...], wb_ref[...])
         + _dot(oc_ref[...], wc_ref[...]))
    o_ref[...] = _layer_norm(ALPHA * h_ref[...] + m, g_ref[...], b_ref[...])


def _outproj_call(h, oa, ob, oc, wa, wb, wc, g, b, tm=512):
    bsz, s, d = h.shape
    row = lambda c: pl.BlockSpec((None, tm, c), lambda bb, t: (bb, t, 0))
    full = lambda w: pl.BlockSpec(w.shape, lambda bb, t: (0, 0))
    g2, b2 = g.reshape(1, d), b.reshape(1, d)
    return pl.pallas_call(
        _outproj_kernel,
        grid=(bsz, s // tm),
        in_specs=[row(d), row(A_WIDTH), row(B_WIDTH), row(C_WIDTH),
                  full(wa), full(wb), full(wc), full(g2), full(b2)],
        out_specs=row(d),
        out_shape=jax.ShapeDtypeStruct((bsz, s, d), F32),
        compiler_params=_params("parallel", "parallel"),
        name="out_proj_ln",
    )(h, oa, ob, oc, wa, wb, wc, g2, b2)


def _router_kernel(x_ref, rw_ref, rb_ref, o_ref):
    logits = _dot_nt(rw_ref[...], x_ref[...], HI)
    scores = _sigmoid(logits)
    sel = scores + rb_ref[...]
    R = ROUTER_ROWS
    s = [sel[m * R:(m + 1) * R] for m in range(EXPERTS_PER_GROUP)]
    sc = [scores[m * R:(m + 1) * R] for m in range(EXPERTS_PER_GROUP)]
    hi01, lo01 = jnp.maximum(s[0], s[1]), jnp.minimum(s[0], s[1])
    hi23, lo23 = jnp.maximum(s[2], s[3]), jnp.minimum(s[2], s[3])
    top1 = jnp.maximum(hi01, hi23)
    top2 = jnp.maximum(jnp.minimum(hi01, hi23), jnp.maximum(lo01, lo23))
    gscore = top1 + top2
    gi = lax.broadcasted_iota(jnp.int32, gscore.shape, 0)
    gmax = jnp.max(gscore, axis=0, keepdims=True)
    best = jnp.min(jnp.where(gscore == gmax, gi, R), axis=0, keepdims=True)
    in_group = gi == best
    picked = []
    for m in range(EXPERTS_PER_GROUP):
        rank = jnp.zeros(gscore.shape, jnp.int32)
        for j in range(EXPERTS_PER_GROUP):
            if j == m:
                continue
            ahead = (s[j] >= s[m]) if j < m else (s[j] > s[m])
            rank = rank + jnp.where(ahead, 1, 0)
        picked.append(jnp.where(jnp.logical_and(in_group, rank < 2), sc[m], 0.0))
    denom = jnp.sum(picked[0] + picked[1] + picked[2] + picked[3], axis=0, keepdims=True)
    for m in range(EXPERTS_PER_GROUP):
        o_ref[m * R:(m + 1) * R, :] = picked[m] / denom


def _router_call(x, rw_t, rb_col, tm=1024):
    n, d = x.shape
    rows = rw_t.shape[0]
    return pl.pallas_call(
        _router_kernel,
        grid=(n // tm,),
        in_specs=[pl.BlockSpec((tm, d), lambda i: (i, 0)),
                  pl.BlockSpec((rows, d), lambda i: (0, 0)),
                  pl.BlockSpec((rows, 1), lambda i: (0, 0))],
        out_specs=pl.BlockSpec((rows, tm), lambda i: (0, i)),
        out_shape=jax.ShapeDtypeStruct((rows, n), F32),
        compiler_params=_params("parallel"),
        name="moe_router",
    )(x, rw_t, rb_col)


def _moe_kernel(x_ref, gate_ref, wg_ref, wu_ref, wd_ref, g_ref, b_ref, o_ref, xb_ref, acc_ref):
    e = pl.program_id(1)

    @pl.when(e == 0)
    def _():
        xb_ref[...] = x_ref[...].astype(BF16)
        acc_ref[...] = jnp.zeros_like(acc_ref)

    xb = xb_ref[...]
    hg = _dot(xb, wg_ref[...])
    hu = _dot(xb, wu_ref[...])
    gates = gate_ref[...]
    lane = lax.broadcasted_iota(jnp.int32, gates.shape, 1)
    gcol = jnp.sum(jnp.where(lane == e, gates, 0.0), axis=-1, keepdims=True)
    act = (hg * _sigmoid(hg)) * hu * gcol
    acc_ref[...] += _dot(act.astype(BF16), wd_ref[...])

    @pl.when(e == N_EXPERTS - 1)
    def _():
        o_ref[...] = _layer_norm(ALPHA * x_ref[...] + acc_ref[...], g_ref[...], b_ref[...])


def _moe_call(x, gates, wg, wu, wd, g, b, tm=1024):
    n, d = x.shape
    f = wg.shape[-1]
    g2, b2 = g.reshape(1, d), b.reshape(1, d)
    return pl.pallas_call(
        _moe_kernel,
        grid=(n // tm, N_EXPERTS),
        in_specs=[pl.BlockSpec((tm, d), lambda i, e: (i, 0)),
                  pl.BlockSpec((tm, N_EXPERTS), lambda i, e: (i, 0)),
                  pl.BlockSpec((None, d, f), lambda i, e: (e, 0, 0)),
                  pl.BlockSpec((None, d, f), lambda i, e: (e, 0, 0)),
                  pl.BlockSpec((None, f, d), lambda i, e: (e, 0, 0)),
                  pl.BlockSpec((1, d), lambda i, e: (0, 0)),
                  pl.BlockSpec((1, d), lambda i, e: (0, 0))],
        out_specs=pl.BlockSpec((tm, d), lambda i, e: (i, 0)),
        out_shape=jax.ShapeDtypeStruct((n, d), F32),
        scratch_shapes=[pltpu.VMEM((tm, d), BF16), pltpu.VMEM((tm, d), F32)],
        compiler_params=_params("parallel", "arbitrary"),
        name="moe_experts_ln",
    )(x, gates, wg, wu, wd, g2, b2)


def _pair_heads(x, axis):
    shape = x.shape
    x = x.reshape(shape[:axis] + (A_KV_HEADS, A_Q_HEADS // A_KV_HEADS, HEAD_DIM) + shape[axis + 1:])
    x = jnp.swapaxes(x, axis, axis + 1)
    return x.reshape(shape)


def _router_layout(router_w, router_bias):
    d = router_w.shape[0]
    w = router_w.astype(F32).T.reshape(N_GROUPS, EXPERTS_PER_GROUP, d).transpose(1, 0, 2)
    w = jnp.pad(w, ((0, 0), (0, ROUTER_ROWS - N_GROUPS), (0, 0)))
    b = router_bias.astype(F32).reshape(N_GROUPS, EXPERTS_PER_GROUP).T
    b = jnp.pad(b, ((0, 0), (0, ROUTER_ROWS - N_GROUPS)), constant_values=MASKED)
    rows = EXPERTS_PER_GROUP * ROUTER_ROWS
    return w.reshape(rows, d), b.reshape(rows, 1)


def _gates_from_router(gates_t):
    n = gates_t.shape[1]
    g = gates_t.reshape(EXPERTS_PER_GROUP, ROUTER_ROWS, n)[:, :N_GROUPS]
    return g.transpose(2, 1, 0).reshape(n, N_EXPERTS)


def kernel(x, ln0_g, ln0_b, w_in, w_out, sinks, rel_bias, shift_mu, decay_w0, decay_up, iclr_a0,
           iclr_up, gate_up, k_k, k_a, r_k, lnx_g, lnx_b, ln1_g, ln1_b, router_w, router_bias,
           w_gate, w_up, w_down, ln2_g, ln2_b):
    bsz, s, d = x.shape
    n = bsz * s
    bias_pairs = _swa_bias_pairs(rel_bias)
    rw_t, rb_col = _router_layout(router_w, router_bias)
    zeros_lora = jnp.zeros((ICLR_RANK, B_WIDTH), F32)

    h = _ln_call(x.reshape(n, d), ln0_g, ln0_b).reshape(bsz, s, d)
    for l in range(DEPTH):
        wl = w_in[l]
        wa = jnp.concatenate([_pair_heads(wl[:, :A_WIDTH], 1), wl[:, A_WIDTH:A_COLS]], axis=1)
        wb = wl[:, A_COLS:A_COLS + B_COLS]
        c0 = A_COLS + B_COLS
        wq, wk, wv = (wl[:, c0 + j * C_WIDTH:c0 + (j + 1) * C_WIDTH] for j in range(3))
        pa, pb, qc, kc, vt = _proj_call(h, wa.astype(BF16), wb.astype(BF16), wq.astype(BF16),
                                        wk.astype(BF16), wv.T.astype(BF16))

        out_a = _swa_call(pa, sinks[l].astype(F32), bias_pairs)
        wd_pad = jnp.concatenate([decay_up[l].astype(F32), zeros_lora], axis=0).astype(BF16)
        wa_pad = jnp.concatenate([zeros_lora, iclr_up[l].astype(F32)], axis=0).astype(BF16)
        out_b = _rwkv_call(pb, shift_mu[l], decay_w0[l], wd_pad, iclr_a0[l], wa_pad,
                           gate_up[l].astype(BF16), k_k[l], k_a[l], r_k[l], lnx_g[l], lnx_b[l])
        out_c = _sb_call(qc, kc, vt)

        wo = w_out[l]
        h = _outproj_call(h, out_a, out_b, out_c,
                          _pair_heads(wo[:A_WIDTH], 0).astype(BF16),
                          wo[A_WIDTH:A_WIDTH + B_WIDTH].astype(BF16),
                          wo[A_WIDTH + B_WIDTH:].astype(BF16), ln1_g[l], ln1_b[l])

        hf = h.reshape(n, d)
        gates = _gates_from_router(_router_call(hf, rw_t, rb_col))
        hf = _moe_call(hf, gates, w_gate[l].astype(BF16), w_up[l].astype(BF16),
                       w_down[l].astype(BF16), ln2_g[l], ln2_b[l])
        h = hf.reshape(bsz, s, d)
    return h
```

```python
import functools
import math

import jax
import jax.numpy as jnp
from jax import lax
from jax.experimental import pallas as pl
from jax.experimental.pallas import tpu as pltpu

F32 = jnp.float32
BF16 = jnp.bfloat16
HI = lax.Precision.HIGHEST

DEPTH = 2
HEAD_DIM = 64
BLOCK = 128
LANES = 128
A_Q_HEADS = 6
A_KV_HEADS = 2
WINDOW = 128
A_WIDTH = A_Q_HEADS * HEAD_DIM
A_KV_WIDTH = A_KV_HEADS * HEAD_DIM
B_HEADS = 4
B_WIDTH = B_HEADS * HEAD_DIM
DECAY_RANK = 64
ICLR_RANK = 64
GATE_RANK = 128
GN_EPS = 64e-5
C_HEADS = 6
C_WIDTH = C_HEADS * HEAD_DIM
A_COLS = A_WIDTH + 2 * A_KV_WIDTH
B_COLS = 3 * B_WIDTH + DECAY_RANK + ICLR_RANK + GATE_RANK
NUM_BUCKETS = 32
MAX_EXACT = NUM_BUCKETS // 2
MAX_DISTANCE = 128
N_EXPERTS = 16
N_GROUPS = 4
EXPERTS_PER_GROUP = N_EXPERTS // N_GROUPS
D_FF_EXPERT = 256
LN_EPS = 1e-5
ALPHA = (2 * DEPTH) ** 0.25
SCALE = HEAD_DIM ** -0.5
MASKED = -1e30
CHUNK = 64
RWKV_TS = 256
SB_TQ = 512
SB_DEAD = 151.0
LOG2E = 1.4426950408889634
ROUTER_ROWS = 8

VMEM_LIMIT = 48 * 1024 * 1024


def _dot(a, b, prec=None):
    return jnp.dot(a, b, preferred_element_type=F32, precision=prec)


def _dot_nt(a, b, prec=None):
    return lax.dot_general(a, b, (((1,), (1,)), ((), ())),
                           preferred_element_type=F32, precision=prec)


def _sigmoid(x):
    return 1.0 / (1.0 + jnp.exp(-x))


def _softplus(x):
    return jnp.maximum(x, 0.0) + jnp.log(1.0 + jnp.exp(-jnp.abs(x)))


def _layer_norm(x, g, b):
    mu = jnp.mean(x, axis=-1, keepdims=True)
    xc = x - mu
    var = jnp.mean(xc * xc, axis=-1, keepdims=True)
    return xc * lax.rsqrt(var + LN_EPS) * g + b


def _params(*sem):
    return pltpu.CompilerParams(dimension_semantics=sem, vmem_limit_bytes=VMEM_LIMIT)


def _ln_kernel(x_ref, g_ref, b_ref, o_ref):
    o_ref[...] = _layer_norm(x_ref[...], g_ref[...], b_ref[...])


def _ln_call(x, g, b, tm=512):
    n, d = x.shape
    return pl.pallas_call(
        _ln_kernel,
        grid=(n // tm,),
        in_specs=[pl.BlockSpec((tm, d), lambda i: (i, 0)),
                  pl.BlockSpec((1, d), lambda i: (0, 0)),
                  pl.BlockSpec((1, d), lambda i: (0, 0))],
        out_specs=pl.BlockSpec((tm, d), lambda i: (i, 0)),
        out_shape=jax.ShapeDtypeStruct((n, d), F32),
        compiler_params=_params("parallel"),
        name="embed_ln",
    )(x, g.reshape(1, d), b.reshape(1, d))


def _proj_kernel(h_ref, wa_ref, wb_ref, wq_ref, wk_ref, wvt_ref,
                 pa_ref, pb_ref, qc_ref, kc_ref, vt_ref):
    hb = h_ref[...].astype(BF16)
    pa_ref[...] = _dot(hb, wa_ref[...]).astype(BF16)
    pb_ref[...] = _dot(hb, wb_ref[...])
    qc_ref[...] = (_dot(hb, wq_ref[...]) * (SCALE * LOG2E)).astype(BF16)
    kc_ref[...] = _dot(hb, wk_ref[...]).astype(BF16)
    vt_ref[...] = _dot_nt(wvt_ref[...], hb).astype(BF16)


def _proj_call(h, wa, wb, wq, wk, wvt, tm=512):
    bsz, s, d = h.shape
    full = lambda w: pl.BlockSpec(w.shape, lambda b, t: (0, 0))
    row = lambda c: pl.BlockSpec((None, tm, c), lambda b, t: (b, t, 0))
    return pl.pallas_call(
        _proj_kernel,
        grid=(bsz, s // tm),
        in_specs=[row(d), full(wa), full(wb), full(wq), full(wk), full(wvt)],
        out_specs=[row(A_COLS), row(B_COLS), row(C_WIDTH), row(C_WIDTH),
                   pl.BlockSpec((None, C_WIDTH, tm), lambda b, t: (b, 0, t))],
        out_shape=[jax.ShapeDtypeStruct((bsz, s, A_COLS), BF16),
                   jax.ShapeDtypeStruct((bsz, s, B_COLS), F32),
                   jax.ShapeDtypeStruct((bsz, s, C_WIDTH), BF16),
                   jax.ShapeDtypeStruct((bsz, s, C_WIDTH), BF16),
                   jax.ShapeDtypeStruct((bsz, C_WIDTH, s), BF16)],
        compiler_params=_params("parallel", "parallel"),
        name="in_proj",
    )(h, wa, wb, wq, wk, wvt)


def _swa_kernel(sink_ref, q_ref, kp_ref, kc_ref, vp_ref, vc_ref, bias_ref, o_ref):
    n = pl.program_id(1)
    kband = jnp.concatenate([kp_ref[...], kc_ref[...]], axis=0)
    vband = jnp.concatenate([vp_ref[...], vc_ref[...]], axis=0)
    lane = lax.broadcasted_iota(jnp.int32, (BLOCK, LANES), 1)
    row2 = lax.broadcasted_iota(jnp.int32, (2 * BLOCK, 1), 0)
    col2 = lax.broadcasted_iota(jnp.int32, (1, 2 * BLOCK), 1)
    pad = jnp.where(jnp.logical_and(n == 0, col2 < BLOCK), MASKED, 0.0)
    for c in range(A_Q_HEADS // 2):
        q2 = q_ref[:, c * LANES:(c + 1) * LANES]
        zero = jnp.zeros_like(q2)
        qs = jnp.concatenate([jnp.where(lane < HEAD_DIM, q2, zero),
                              jnp.where(lane >= HEAD_DIM, q2, zero)], axis=0)
        logits = _dot_nt(qs, kband) * SCALE + bias_ref[c] + pad
        sink = jnp.where(row2 < BLOCK, sink_ref[c], sink_ref[c + 3])
        m = jnp.maximum(jnp.max(logits, axis=-1, keepdims=True), sink)
        p = jnp.exp(logits - m)
        denom = jnp.sum(p, axis=-1, keepdims=True) + jnp.exp(sink - m)
        o = _dot(p.astype(BF16), vband) / denom
        o_ref[:, c * LANES:(c + 1) * LANES] = jnp.where(
            lane < HEAD_DIM, o[:BLOCK], o[BLOCK:]).astype(BF16)


def _swa_call(pa, sinks, bias_pairs):
    bsz, s, _ = pa.shape
    nb = s // BLOCK
    kcol = A_WIDTH // LANES
    vcol = kcol + 1
    prev = lambda n: jnp.maximum(n - 1, 0)
    return pl.pallas_call(
        _swa_kernel,
        grid=(bsz, nb),
        in_specs=[pl.BlockSpec(memory_space=pltpu.SMEM),
                  pl.BlockSpec((None, BLOCK, A_WIDTH), lambda b, n: (b, n, 0)),
                  pl.BlockSpec((None, BLOCK, LANES), lambda b, n: (b, prev(n), kcol)),
                  pl.BlockSpec((None, BLOCK, LANES), lambda b, n: (b, n, kcol)),
                  pl.BlockSpec((None, BLOCK, LANES), lambda b, n: (b, prev(n), vcol)),
                  pl.BlockSpec((None, BLOCK, LANES), lambda b, n: (b, n, vcol)),
                  pl.BlockSpec(bias_pairs.shape, lambda b, n: (0, 0, 0))],
        out_specs=pl.BlockSpec((None, BLOCK, A_WIDTH), lambda b, n: (b, n, 0)),
        out_shape=jax.ShapeDtypeStruct((bsz, s, A_WIDTH), BF16),
        compiler_params=_params("parallel", "parallel"),
        name="swa_attn",
    )(sinks, pa, pa, pa, pa, pa, bias_pairs)


def _t5_causal_bucket(dist):
    dist = jnp.maximum(dist, 0)
    d = jnp.maximum(dist, 1).astype(F32)
    large = MAX_EXACT + (jnp.log(d / MAX_EXACT) / math.log(MAX_DISTANCE / MAX_EXACT)
                         * (NUM_BUCKETS - MAX_EXACT)).astype(jnp.int32)
    large = jnp.minimum(large, NUM_BUCKETS - 1)
    return jnp.where(dist < MAX_EXACT, dist, large)


def _swa_bias_pairs(rel_bias):
    qi = jnp.arange(BLOCK)[:, None]
    ki = jnp.arange(2 * BLOCK)[None, :]
    dist = qi + BLOCK - ki
    in_window = (dist >= 0) & (dist < WINDOW)
    onehot = jax.nn.one_hot(_t5_causal_bucket(dist), NUM_BUCKETS, dtype=F32)
    bias = jnp.einsum("qkb,bh->qkh", onehot, rel_bias.astype(F32), precision=HI)
    bias = jnp.where(in_window[..., None], bias, MASKED).transpose(2, 0, 1)
    return jnp.stack([jnp.concatenate([bias[c], bias[c + 3]], axis=0)
                      for c in range(A_Q_HEADS // 2)])


def _split2(x):
    hi = x.astype(BF16)
    return hi, (x - hi.astype(F32)).astype(BF16)


def _dot3(a, b, nt=False):
    ah, al = _split2(a)
    bh, bl = _split2(b)
    d = _dot_nt if nt else _dot
    return d(ah, bh) + d(ah, bl) + d(al, bh)


def _dot3_many(pairs, nt=False):
    parts = [(_split2(a), _split2(b)) for a, b in pairs]
    d = _dot_nt if nt else _dot
    return [d(ah, bh) + d(ah, bl) + d(al, bh) for (ah, al), (bh, bl) in parts]


def _dot_x2(a, b_exact):
    ah, al = _split2(a)
    return _dot(ah, b_exact) + _dot(al, b_exact)


def _dot_2x(a_exact, b):
    bh, bl = _split2(b)
    return _dot(a_exact, bh) + _dot(a_exact, bl)


def _rwkv_kernel(pb_ref, mu_ref, w0_ref, wd_ref, a0_ref, wa_ref, wg_ref, kk_ref, ka_ref,
                 rk_ref, lng_ref, lnb_ref, o_ref, prev_ref, h_ref):
    t = pl.program_id(1)

    @pl.when(t == 0)
    def _():
        prev_ref[...] = jnp.zeros_like(prev_ref)
        h_ref[...] = jnp.zeros_like(h_ref)

    L = CHUNK
    W = B_WIDTH
    TS = RWKV_TS
    p = pb_ref[...]
    rows = lax.broadcasted_iota(jnp.int32, (TS, 1), 0)
    shifted = jnp.where(rows == 0, prev_ref[...], pltpu.roll(p, 1, axis=0))
    prev_ref[...] = p[TS - 1:TS, :]
    pm = p + (shifted - p) * mu_ref[...]
    r = pm[:, 0:W]
    k = pm[:, W:2 * W]
    v = pm[:, 2 * W:3 * W]
    xwa = pm[:, 3 * W:3 * W + DECAY_RANK + ICLR_RANK]
    xg = pm[:, 3 * W + DECAY_RANK + ICLR_RANK:]

    dw = w0_ref[...] + _dot(jnp.tanh(xwa).astype(BF16), wd_ref[...])
    lw = -jnp.exp(-_softplus(-dw) - 0.5)
    a = _sigmoid(a0_ref[...] + _dot(xwa.astype(BF16), wa_ref[...]))
    g = _dot(_sigmoid(xg).astype(BF16), wg_ref[...])

    hr = lax.broadcasted_iota(jnp.int32, (W, W), 0) // HEAD_DIM
    hc = lax.broadcasted_iota(jnp.int32, (W, W), 1) // HEAD_DIM
    same_head = hr == hc
    diag_w = (lax.broadcasted_iota(jnp.int32, (W, W), 0)
              == lax.broadcasted_iota(jnp.int32, (W, W), 1))
    head_ones = jnp.where(same_head, 1.0, 0.0).astype(BF16)
    kk = k * kk_ref[...]
    kk = kk * lax.rsqrt(jnp.maximum(_dot_x2(kk * kk, head_ones), 1e-24))
    k2 = k * (1.0 + (a - 1.0) * ka_ref[...])
    bonus = _dot_x2(r * k2 * rk_ref[...], head_ones) * v
    aa = -kk
    bb = kk * a

    ti = lax.broadcasted_iota(jnp.int32, (L, L), 0)
    tj = lax.broadcasted_iota(jnp.int32, (L, L), 1)
    lower = jnp.where(ti >= tj, 1.0, 0.0).astype(BF16)
    eye = jnp.where(ti == tj, 1.0, 0.0)
    lane_head = lax.broadcasted_iota(jnp.int32, (L, W), 1) // HEAD_DIM

    def only(x, h):
        return jnp.where(lane_head == h, x, 0.0)

    nch = TS // L
    chunks = range(nch)
    units = [(c, h) for c in chunks for h in range(B_HEADS)]
    hsl = [slice(h * L, (h + 1) * L) for h in range(B_HEADS)]
    csl = [slice(c * L, (c + 1) * L) for c in chunks]
    v_c = [v[s] for s in csl]
    cum = [_dot_2x(lower, lw[s]) for s in csl]
    cum_l = [x[L - 1:L, :] for x in cum]
    at = [aa[csl[c]] * jnp.exp(cum[c] - lw[csl[c]]) for c in chunks]
    rt = [r[csl[c]] * jnp.exp(cum[c]) for c in chunks]
    inv = [jnp.exp(-x) for x in cum]
    bt = [bb[csl[c]] * inv[c] for c in chunks]
    kt = [k2[csl[c]] * inv[c] for c in chunks]
    tail = [jnp.exp(cum_l[c] - cum[c]) for c in chunks]
    bh = [bb[csl[c]] * tail[c] for c in chunks]
    kh = [k2[csl[c]] * tail[c] for c in chunks]

    at_s = [jnp.concatenate([only(x, h) for h in range(B_HEADS)], axis=0) for x in at]
    rt_s = [jnp.concatenate([only(x, h) for h in range(B_HEADS)], axis=0) for x in rt]
    ab = _dot3_many([(at_s[c], bt[c]) for c in chunks], nt=True)
    ak = _dot3_many([(at_s[c], kt[c]) for c in chunks], nt=True)
    rb = _dot3_many([(rt_s[c], bt[c]) for c in chunks], nt=True)
    rk = _dot3_many([(rt_s[c], kt[c]) for c in chunks], nt=True)

    pw = [jnp.where(ti > tj, ab[c][hsl[h]], 0.0) for c, h in units]
    tinv = [eye + x for x in pw]
    for _ in range(int(math.log2(L)) - 1):
        pw = _dot3_many([(x, x) for x in pw])
        tinv = [t + d for t, d in zip(tinv, _dot3_many(list(zip(tinv, pw))))]
    ak_v = _dot3_many([(jnp.where(ti > tj, ak[c][hsl[h]], 0.0), v_c[c]) for c, h in units])
    w_u = _dot3_many([(tinv[i], only(at[c], h)) for i, (c, h) in enumerate(units)])
    u0_u = [only(x, h) for x, (c, h) in zip(_dot3_many(list(zip(tinv, ak_v))), units)]
    rb_l = [jnp.where(ti >= tj, rb[c][hsl[h]], 0.0) for c, h in units]
    rk_l = [jnp.where(ti >= tj, rk[c][hsl[h]], 0.0) for c, h in units]
    qm_u = _dot3_many(list(zip(rb_l, w_u)))
    y0_u = [only(p + q, h) for p, q, (c, h) in zip(
        _dot3_many(list(zip(rb_l, u0_u))),
        _dot3_many([(rk_l[i], v_c[c]) for i, (c, h) in enumerate(units)]), units)]

    def chunk_sum(xs, c):
        return functools.reduce(lambda p, q: p + q, xs[c * B_HEADS:(c + 1) * B_HEADS])

    w_sum = [chunk_sum(w_u, c) for c in chunks]
    u0 = [chunk_sum(u0_u, c) for c in chunks]
    qm = [rt[c] + chunk_sum(qm_u, c) for c in chunks]
    y0 = [chunk_sum(y0_u, c) for c in chunks]
    bw = _dot3_many([(bh[c].T, w_sum[c]) for c in chunks])
    g_mat = [jnp.where(same_head, bw[c], 0.0) + jnp.where(diag_w, jnp.exp(cum_l[c]).T, 0.0)
             for c in chunks]
    c_mat = [jnp.where(same_head, x, 0.0) for x in _dot3_many(
        [(jnp.concatenate([bh[c], kh[c]], axis=0).T, jnp.concatenate([u0[c], v_c[c]], axis=0))
         for c in chunks])]

    hst = h_ref[...]
    ys = []
    for c in chunks:
        ys.append(_dot3(qm[c], hst) + y0[c])
        hst = _dot3(g_mat[c], hst) + c_mat[c]
    h_ref[...] = hst
    y = jnp.concatenate(ys, axis=0)

    mean = _dot_x2(y, head_ones) * (1.0 / HEAD_DIM)
    yc = y - mean
    var = _dot_x2(yc * yc, head_ones) * (1.0 / HEAD_DIM)
    yn = yc * lax.rsqrt(var + GN_EPS) * lng_ref[...] + lnb_ref[...]
    o_ref[...] = ((yn + bonus) * g).astype(o_ref.dtype)


def _rwkv_call(pb, mu, w0, wd_pad, a0, wa_pad, wg, k_k, k_a, r_k, lnx_g, lnx_b):
    bsz, s, _ = pb.shape
    vec = lambda x: x.reshape(1, -1).astype(F32)
    small = [vec(mu), vec(w0), wd_pad, vec(a0), wa_pad, wg, vec(k_k), vec(k_a), vec(r_k),
             vec(lnx_g), vec(lnx_b)]
    return pl.pallas_call(
        _rwkv_kernel,
        grid=(bsz, s // RWKV_TS),
        in_specs=[pl.BlockSpec((None, RWKV_TS, B_COLS), lambda b, t: (b, t, 0))]
        + [pl.BlockSpec(x.shape, lambda b, t: (0, 0)) for x in small],
        out_specs=pl.BlockSpec((None, RWKV_TS, B_WIDTH), lambda b, t: (b, t, 0)),
        out_shape=jax.ShapeDtypeStruct((bsz, s, B_WIDTH), BF16),
        scratch_shapes=[pltpu.VMEM((1, B_COLS), F32), pltpu.VMEM((B_WIDTH, B_WIDTH), F32)],
        compiler_params=_params("parallel", "arbitrary"),
        name="rwkv7",
    )(pb, *small)


def _softplus2(z):
    return jnp.maximum(z, 0.0) + jnp.log2(1.0 + jnp.exp2(-jnp.abs(z)))


def _sb_kernel(q_ref, k_ref, vt_ref, o_ref, z_sc, lb_sc, e_sc, w_sc, acc_sc):
    it = pl.program_id(2)
    q2 = q_ref[...]
    lane = lax.broadcasted_iota(jnp.int32, (SB_TQ, LANES), 1)
    zero = jnp.zeros_like(q2)
    qh = (jnp.where(lane < HEAD_DIM, q2, zero), jnp.where(lane >= HEAD_DIM, q2, zero))
    ur = lax.broadcasted_iota(jnp.int32, (BLOCK, 2 * BLOCK), 0)
    uc = lax.broadcasted_iota(jnp.int32, (BLOCK, 2 * BLOCK), 1) % BLOCK
    ucat = jnp.where(uc > ur, 1.0, 0.0).astype(BF16)
    nsub = SB_TQ // BLOCK
    first = it * nsub + nsub - 1
    nblk = first + 1
    wide = 2 * SB_TQ

    def key_off(n):
        return pl.multiple_of(jnp.clip(first - n, 0, first) * BLOCK, BLOCK)

    def logits(n):
        kblk = k_ref[pl.ds(key_off(n), BLOCK), :]
        return jnp.concatenate([_dot_nt(kblk, qh[0]), _dot_nt(kblk, qh[1])], axis=1)

    def softplus_stage(zt, n, masked):
        sp = _softplus2(zt)
        lb = zt - sp
        if masked:
            kpos = key_off(n) + lax.broadcasted_iota(jnp.int32, (BLOCK, wide), 0)
            qpos = it * SB_TQ + lax.broadcasted_iota(jnp.int32, (BLOCK, wide), 1) % SB_TQ
            before = kpos < qpos
            sp = jnp.where(before, sp, 0.0)
            lb = jnp.where(before, lb, MASKED)
        sp_hi = sp.astype(BF16)
        sp_lo = (sp - sp_hi.astype(F32)).astype(BF16)
        stacked = jnp.concatenate([sp_hi, sp_lo], axis=0)
        return lb, stacked, jnp.sum(sp, axis=0, keepdims=True)

    def value_stage(n):
        vt2 = vt_ref[:, pl.ds(key_off(n), BLOCK)]
        w = w_sc[...]
        for h in range(2):
            cols = slice(h * SB_TQ, (h + 1) * SB_TQ)
            acc_sc[:, cols] += _dot(vt2[h * HEAD_DIM:(h + 1) * HEAD_DIM], w[:, cols])

    def step(n, carry, masked):
        car, colsum = carry
        value_stage(n - 1)
        z_new = logits(n + 2)
        lb_new, stacked, colsum_new = softplus_stage(z_sc[...], n + 1, masked)
        e_new = _dot(ucat, stacked)
        w_sc[...] = jnp.exp2(lb_sc[...] - e_sc[...] - car).astype(BF16)
        z_sc[...] = z_new
        lb_sc[...] = lb_new
        e_sc[...] = e_new
        return car + colsum, colsum_new

    lb0, stacked0, colsum0 = softplus_stage(logits(0), 0, True)
    lb_sc[...] = lb0
    e_sc[...] = _dot(ucat, stacked0)
    z_sc[...] = logits(1)
    w_sc[...] = jnp.zeros_like(w_sc)
    acc_sc[...] = jnp.zeros_like(acc_sc)
    carry = (jnp.zeros((1, wide), F32), colsum0)
    car, colsum = lax.fori_loop(0, nsub, lambda n, c: step(n, c, True), carry)

    def live(c):
        n, _, _, alive = c
        return jnp.logical_and(n < nblk, alive > 0)

    def visit(c):
        n, car, colsum, _ = c
        alive = (jnp.min(car) < SB_DEAD).astype(jnp.int32)
        car, colsum = step(n, (car, colsum), False)
        return n + jnp.int32(1), car, colsum, alive

    n_stop = lax.while_loop(live, visit, (jnp.int32(nsub), car, colsum, jnp.int32(1)))[0]
    value_stage(n_stop - 1)
    acc = acc_sc[...]
    out_t = jnp.concatenate([acc[:, :SB_TQ], acc[:, SB_TQ:]], axis=0)
    o_ref[...] = out_t.T.astype(o_ref.dtype)


def _sb_call(qc, kc, vt):
    bsz, s, _ = qc.shape
    return pl.pallas_call(
        _sb_kernel,
        grid=(bsz, C_WIDTH // LANES, s // SB_TQ),
        in_specs=[pl.BlockSpec((None, SB_TQ, LANES), lambda b, hp, i: (b, i, hp)),
                  pl.BlockSpec((None, s, LANES), lambda b, hp, i: (b, 0, hp)),
                  pl.BlockSpec((None, LANES, s), lambda b, hp, i: (b, hp, 0))],
        out_specs=pl.BlockSpec((None, SB_TQ, LANES), lambda b, hp, i: (b, i, hp)),
        out_shape=jax.ShapeDtypeStruct((bsz, s, C_WIDTH), BF16),
        scratch_shapes=[pltpu.VMEM((BLOCK, 2 * SB_TQ), F32),
                        pltpu.VMEM((BLOCK, 2 * SB_TQ), F32),
                        pltpu.VMEM((BLOCK, 2 * SB_TQ), F32),
                        pltpu.VMEM((BLOCK, 2 * SB_TQ), BF16),
                        pltpu.VMEM((HEAD_DIM, 2 * SB_TQ), F32)],
        compiler_params=_params("parallel", "parallel", "parallel"),
        name="stickbreak_attn",
    )(qc, kc, vt)


def _outproj_kernel(h_ref, oa_ref, ob_ref, oc_ref, wa_ref, wb_ref, wc_ref, g_ref, b_ref, o_ref):
    m = (_dot(oa_ref[...], wa_ref[...]) + _dot(ob_ref[...], wb_ref[...])
         + _dot(oc_ref[...], wc_ref[...]))
    o_ref[...] = _layer_norm(ALPHA * h_ref[...] + m, g_ref[...], b_ref[...])


def _outproj_call(h, oa, ob, oc, wa, wb, wc, g, b, tm=512):
    bsz, s, d = h.shape
    row = lambda c: pl.BlockSpec((None, tm, c), lambda bb, t: (bb, t, 0))
    full = lambda w: pl.BlockSpec(w.shape, lambda bb, t: (0, 0))
    g2, b2 = g.reshape(1, d), b.reshape(1, d)
    return pl.pallas_call(
        _outproj_kernel,
        grid=(bsz, s // tm),
        in_specs=[row(d), row(A_WIDTH), row(B_WIDTH), row(C_WIDTH),
                  full(wa), full(wb), full(wc), full(g2), full(b2)],
        out_specs=row(d),
        out_shape=jax.ShapeDtypeStruct((bsz, s, d), F32),
        compiler_params=_params("parallel", "parallel"),
        name="out_proj_ln",
    )(h, oa, ob, oc, wa, wb, wc, g2, b2)


def _router_kernel(x_ref, rw_ref, rb_ref, o_ref):
    logits = _dot_nt(rw_ref[...], x_ref[...], HI)
    scores = _sigmoid(logits)
    sel = scores + rb_ref[...]
    R = ROUTER_ROWS
    s = [sel[m * R:(m + 1) * R] for m in range(EXPERTS_PER_GROUP)]
    sc = [scores[m * R:(m + 1) * R] for m in range(EXPERTS_PER_GROUP)]
    hi01, lo01 = jnp.maximum(s[0], s[1]), jnp.minimum(s[0], s[1])
    hi23, lo23 = jnp.maximum(s[2], s[3]), jnp.minimum(s[2], s[3])
    top1 = jnp.maximum(hi01, hi23)
    top2 = jnp.maximum(jnp.minimum(hi01, hi23), jnp.maximum(lo01, lo23))
    gscore = top1 + top2
    gi = lax.broadcasted_iota(jnp.int32, gscore.shape, 0)
    gmax = jnp.max(gscore, axis=0, keepdims=True)
    best = jnp.min(jnp.where(gscore == gmax, gi, R), axis=0, keepdims=True)
    in_group = gi == best
    picked = []
    for m in range(EXPERTS_PER_GROUP):
        rank = jnp.zeros(gscore.shape, jnp.int32)
        for j in range(EXPERTS_PER_GROUP):
            if j == m:
                continue
            ahead = (s[j] >= s[m]) if j < m else (s[j] > s[m])
            rank = rank + jnp.where(ahead, 1, 0)
        picked.append(jnp.where(jnp.logical_and(in_group, rank < 2), sc[m], 0.0))
    denom = jnp.sum(picked[0] + picked[1] + picked[2] + picked[3], axis=0, keepdims=True)
    for m in range(EXPERTS_PER_GROUP):
        o_ref[m * R:(m + 1) * R, :] = picked[m] / denom


def _router_call(x, rw_t, rb_col, tm=1024):
    n, d = x.shape
    rows = rw_t.shape[0]
    return pl.pallas_call(
        _router_kernel,
        grid=(n // tm,),
        in_specs=[pl.BlockSpec((tm, d), lambda i: (i, 0)),
                  pl.BlockSpec((rows, d), lambda i: (0, 0)),
                  pl.BlockSpec((rows, 1), lambda i: (0, 0))],
        out_specs=pl.BlockSpec((rows, tm), lambda i: (0, i)),
        out_shape=jax.ShapeDtypeStruct((rows, n), F32),
        compiler_params=_params("parallel"),
        name="moe_router",
    )(x, rw_t, rb_col)


def _moe_kernel(x_ref, gate_ref, wg_ref, wu_ref, wd_ref, g_ref, b_ref, o_ref, xb_ref, acc_ref):
    e = pl.program_id(1)

    @pl.when(e == 0)
    def _():
        xb_ref[...] = x_ref[...].astype(BF16)
        acc_ref[...] = jnp.zeros_like(acc_ref)

    xb = xb_ref[...]
    hg = _dot(xb, wg_ref[...])
    hu = _dot(xb, wu_ref[...])
    gates = gate_ref[...]
    lane = lax.broadcasted_iota(jnp.int32, gates.shape, 1)
    gcol = jnp.sum(jnp.where(lane == e, gates, 0.0), axis=-1, keepdims=True)
    act = (hg * _sigmoid(hg)) * hu * gcol
    acc_ref[...] += _dot(act.astype(BF16), wd_ref[...])

    @pl.when(e == N_EXPERTS - 1)
    def _():
        o_ref[...] = _layer_norm(ALPHA * x_ref[...] + acc_ref[...], g_ref[...], b_ref[...])


def _moe_call(x, gates, wg, wu, wd, g, b, tm=1024):
    n, d = x.shape
    f = wg.shape[-1]
    g2, b2 = g.reshape(1, d), b.reshape(1, d)
    return pl.pallas_call(
        _moe_kernel,
        grid=(n // tm, N_EXPERTS),
        in_specs=[pl.BlockSpec((tm, d), lambda i, e: (i, 0)),
                  pl.BlockSpec((tm, N_EXPERTS), lambda i, e: (i, 0)),
                  pl.BlockSpec((None, d, f), lambda i, e: (e, 0, 0)),
                  pl.BlockSpec((None, d, f), lambda i, e: (e, 0, 0)),
                  pl.BlockSpec((None, f, d), lambda i, e: (e, 0, 0)),
                  pl.BlockSpec((1, d), lambda i, e: (0, 0)),
                  pl.BlockSpec((1, d), lambda i, e: (0, 0))],
        out_specs=pl.BlockSpec((tm, d), lambda i, e: (i, 0)),
        out_shape=jax.ShapeDtypeStruct((n, d), F32),
        scratch_shapes=[pltpu.VMEM((tm, d), BF16), pltpu.VMEM((tm, d), F32)],
        compiler_params=_params("parallel", "arbitrary"),
        name="moe_experts_ln",
    )(x, gates, wg, wu, wd, g2, b2)


def _pair_heads(x, axis):
    shape = x.shape
    x = x.reshape(shape[:axis] + (A_KV_HEADS, A_Q_HEADS // A_KV_HEADS, HEAD_DIM) + shape[axis + 1:])
    x = jnp.swapaxes(x, axis, axis + 1)
    return x.reshape(shape)


def _router_layout(router_w, router_bias):
    d = router_w.shape[0]
    w = router_w.astype(F32).T.reshape(N_GROUPS, EXPERTS_PER_GROUP, d).transpose(1, 0, 2)
    w = jnp.pad(w, ((0, 0), (0, ROUTER_ROWS - N_GROUPS), (0, 0)))
    b = router_bias.astype(F32).reshape(N_GROUPS, EXPERTS_PER_GROUP).T
    b = jnp.pad(b, ((0, 0), (0, ROUTER_ROWS - N_GROUPS)), constant_values=MASKED)
    rows = EXPERTS_PER_GROUP * ROUTER_ROWS
    return w.reshape(rows, d), b.reshape(rows, 1)


def _gates_from_router(gates_t):
    n = gates_t.shape[1]
    g = gates_t.reshape(EXPERTS_PER_GROUP, ROUTER_ROWS, n)[:, :N_GROUPS]
    return g.transpose(2, 1, 0).reshape(n, N_EXPERTS)


def kernel(x, ln0_g, ln0_b, w_in, w_out, sinks, rel_bias, shift_mu, decay_w0, decay_up, iclr_a0,
           iclr_up, gate_up, k_k, k_a, r_k, lnx_g, lnx_b, ln1_g, ln1_b, router_w, router_bias,
           w_gate, w_up, w_down, ln2_g, ln2_b):
    bsz, s, d = x.shape
    n = bsz * s
    bias_pairs = _swa_bias_pairs(rel_bias)
    rw_t, rb_col = _router_layout(router_w, router_bias)
    zeros_lora = jnp.zeros((ICLR_RANK, B_WIDTH), F32)

    h = _ln_call(x.reshape(n, d), ln0_g, ln0_b).reshape(bsz, s, d)
    for l in range(DEPTH):
        wl = w_in[l]
        wa = jnp.concatenate([_pair_heads(wl[:, :A_WIDTH], 1), wl[:, A_WIDTH:A_COLS]], axis=1)
        wb = wl[:, A_COLS:A_COLS + B_COLS]
        c0 = A_COLS + B_COLS
        wq, wk, wv = (wl[:, c0 + j * C_WIDTH:c0 + (j + 1) * C_WIDTH] for j in range(3))
        pa, pb, qc, kc, vt = _proj_call(h, wa.astype(BF16), wb.astype(BF16), wq.astype(BF16),
                                        wk.astype(BF16), wv.T.astype(BF16))

        out_a = _swa_call(pa, sinks[l].astype(F32), bias_pairs)
        wd_pad = jnp.concatenate([decay_up[l].astype(F32), zeros_lora], axis=0).astype(BF16)
        wa_pad = jnp.concatenate([zeros_lora, iclr_up[l].astype(F32)], axis=0).astype(BF16)
        out_b = _rwkv_call(pb, shift_mu[l], decay_w0[l], wd_pad, iclr_a0[l], wa_pad,
                           gate_up[l].astype(BF16), k_k[l], k_a[l], r_k[l], lnx_g[l], lnx_b[l])
        out_c = _sb_call(qc, kc, vt)

        wo = w_out[l]
        h = _outproj_call(h, out_a, out_b, out_c,
                          _pair_heads(wo[:A_WIDTH], 0).astype(BF16),
                          wo[A_WIDTH:A_WIDTH + B_WIDTH].astype(BF16),
                          wo[A_WIDTH + B_WIDTH:].astype(BF16), ln1_g[l], ln1_b[l])

        hf = h.reshape(n, d)
        gates = _gates_from_router(_router_call(hf, rw_t, rb_col))
        hf = _moe_call(hf, gates, w_gate[l].astype(BF16), w_up[l].astype(BF16),
                       w_down[l].astype(BF16), ln2_g[l], ln2_b[l])
        h = hf.reshape(bsz, s, d)
    return h
```

```python
import functools
import math

import jax
import jax.numpy as jnp
from jax import lax
from jax.experimental import pallas as pl
from jax.experimental.pallas import tpu as pltpu

F32 = jnp.float32
BF16 = jnp.bfloat16
HI = lax.Precision.HIGHEST

DEPTH = 2
HEAD_DIM = 64
BLOCK = 128
LANES = 128
A_Q_HEADS = 6
A_KV_HEADS = 2
WINDOW = 128
A_WIDTH = A_Q_HEADS * HEAD_DIM
A_KV_WIDTH = A_KV_HEADS * HEAD_DIM
B_HEADS = 4
B_WIDTH = B_HEADS * HEAD_DIM
DECAY_RANK = 64
ICLR_RANK = 64
GATE_RANK = 128
GN_EPS = 64e-5
C_HEADS = 6
C_WIDTH = C_HEADS * HEAD_DIM
A_COLS = A_WIDTH + 2 * A_KV_WIDTH
B_COLS = 3 * B_WIDTH + DECAY_RANK + ICLR_RANK + GATE_RANK
NUM_BUCKETS = 32
MAX_EXACT = NUM_BUCKETS // 2
MAX_DISTANCE = 128
N_EXPERTS = 16
N_GROUPS = 4
EXPERTS_PER_GROUP = N_EXPERTS // N_GROUPS
D_FF_EXPERT = 256
LN_EPS = 1e-5
ALPHA = (2 * DEPTH) ** 0.25
SCALE = HEAD_DIM ** -0.5
MASKED = -1e30
CHUNK = 64
RWKV_TS = 256
SB_TQ = 512
SB_NEAR = 6
SB_DEAD = 151.0
LOG2E = 1.4426950408889634
ROUTER_ROWS = 8

VMEM_LIMIT = 48 * 1024 * 1024


def _dot(a, b, prec=None):
    return jnp.dot(a, b, preferred_element_type=F32, precision=prec)


def _dot_nt(a, b, prec=None):
    return lax.dot_general(a, b, (((1,), (1,)), ((), ())),
                           preferred_element_type=F32, precision=prec)


def _sigmoid(x):
    return 1.0 / (1.0 + jnp.exp(-x))


def _softplus(x):
    return jnp.maximum(x, 0.0) + jnp.log(1.0 + jnp.exp(-jnp.abs(x)))


def _layer_norm(x, g, b):
    mu = jnp.mean(x, axis=-1, keepdims=True)
    xc = x - mu
    var = jnp.mean(xc * xc, axis=-1, keepdims=True)
    return xc * lax.rsqrt(var + LN_EPS) * g + b


def _params(*sem):
    return pltpu.CompilerParams(dimension_semantics=sem, vmem_limit_bytes=VMEM_LIMIT)


def _ln_kernel(x_ref, g_ref, b_ref, o_ref):
    o_ref[...] = _layer_norm(x_ref[...], g_ref[...], b_ref[...])


def _ln_call(x, g, b, tm=512):
    n, d = x.shape
    return pl.pallas_call(
        _ln_kernel,
        grid=(n // tm,),
        in_specs=[pl.BlockSpec((tm, d), lambda i: (i, 0)),
                  pl.BlockSpec((1, d), lambda i: (0, 0)),
                  pl.BlockSpec((1, d), lambda i: (0, 0))],
        out_specs=pl.BlockSpec((tm, d), lambda i: (i, 0)),
        out_shape=jax.ShapeDtypeStruct((n, d), F32),
        compiler_params=_params("parallel"),
        name="embed_ln",
    )(x, g.reshape(1, d), b.reshape(1, d))


def _proj_kernel(h_ref, wa_ref, wb_ref, wq_ref, wk_ref, wvt_ref,
                 pa_ref, pb_ref, qc_ref, kc_ref, vt_ref):
    hb = h_ref[...].astype(BF16)
    pa_ref[...] = _dot(hb, wa_ref[...]).astype(BF16)
    pb_ref[...] = _dot(hb, wb_ref[...])
    qc_ref[...] = (_dot(hb, wq_ref[...]) * (SCALE * LOG2E)).astype(BF16)
    kc_ref[...] = _dot(hb, wk_ref[...]).astype(BF16)
    vt_ref[...] = _dot_nt(wvt_ref[...], hb).astype(BF16)


def _proj_call(h, wa, wb, wq, wk, wvt, tm=512):
    bsz, s, d = h.shape
    full = lambda w: pl.BlockSpec(w.shape, lambda b, t: (0, 0))
    row = lambda c: pl.BlockSpec((None, tm, c), lambda b, t: (b, t, 0))
    return pl.pallas_call(
        _proj_kernel,
        grid=(bsz, s // tm),
        in_specs=[row(d), full(wa), full(wb), full(wq), full(wk), full(wvt)],
        out_specs=[row(A_COLS), row(B_COLS), row(C_WIDTH), row(C_WIDTH),
                   pl.BlockSpec((None, C_WIDTH, tm), lambda b, t: (b, 0, t))],
        out_shape=[jax.ShapeDtypeStruct((bsz, s, A_COLS), BF16),
                   jax.ShapeDtypeStruct((bsz, s, B_COLS), F32),
                   jax.ShapeDtypeStruct((bsz, s, C_WIDTH), BF16),
                   jax.ShapeDtypeStruct((bsz, s, C_WIDTH), BF16),
                   jax.ShapeDtypeStruct((bsz, C_WIDTH, s), BF16)],
        compiler_params=_params("parallel", "parallel"),
        name="in_proj",
    )(h, wa, wb, wq, wk, wvt)


def _swa_kernel(sink_ref, q_ref, kp_ref, kc_ref, vp_ref, vc_ref, bias_ref, o_ref):
    n = pl.program_id(1)
    kband = jnp.concatenate([kp_ref[...], kc_ref[...]], axis=0)
    vband = jnp.concatenate([vp_ref[...], vc_ref[...]], axis=0)
    lane = lax.broadcasted_iota(jnp.int32, (BLOCK, LANES), 1)
    row2 = lax.broadcasted_iota(jnp.int32, (2 * BLOCK, 1), 0)
    col2 = lax.broadcasted_iota(jnp.int32, (1, 2 * BLOCK), 1)
    pad = jnp.where(jnp.logical_and(n == 0, col2 < BLOCK), MASKED, 0.0)
    for c in range(A_Q_HEADS // 2):
        q2 = q_ref[:, c * LANES:(c + 1) * LANES]
        zero = jnp.zeros_like(q2)
        qs = jnp.concatenate([jnp.where(lane < HEAD_DIM, q2, zero),
                              jnp.where(lane >= HEAD_DIM, q2, zero)], axis=0)
        logits = _dot_nt(qs, kband) * SCALE + bias_ref[c] + pad
        sink = jnp.where(row2 < BLOCK, sink_ref[c], sink_ref[c + 3])
        m = jnp.maximum(jnp.max(logits, axis=-1, keepdims=True), sink)
        p = jnp.exp(logits - m)
        denom = jnp.sum(p, axis=-1, keepdims=True) + jnp.exp(sink - m)
        o = _dot(p.astype(BF16), vband) / denom
        o_ref[:, c * LANES:(c + 1) * LANES] = jnp.where(
            lane < HEAD_DIM, o[:BLOCK], o[BLOCK:]).astype(BF16)


def _swa_call(pa, sinks, bias_pairs):
    bsz, s, _ = pa.shape
    nb = s // BLOCK
    kcol = A_WIDTH // LANES
    vcol = kcol + 1
    prev = lambda n: jnp.maximum(n - 1, 0)
    return pl.pallas_call(
        _swa_kernel,
        grid=(bsz, nb),
        in_specs=[pl.BlockSpec(memory_space=pltpu.SMEM),
                  pl.BlockSpec((None, BLOCK, A_WIDTH), lambda b, n: (b, n, 0)),
                  pl.BlockSpec((None, BLOCK, LANES), lambda b, n: (b, prev(n), kcol)),
                  pl.BlockSpec((None, BLOCK, LANES), lambda b, n: (b, n, kcol)),
                  pl.BlockSpec((None, BLOCK, LANES), lambda b, n: (b, prev(n), vcol)),
                  pl.BlockSpec((None, BLOCK, LANES), lambda b, n: (b, n, vcol)),
                  pl.BlockSpec(bias_pairs.shape, lambda b, n: (0, 0, 0))],
        out_specs=pl.BlockSpec((None, BLOCK, A_WIDTH), lambda b, n: (b, n, 0)),
        out_shape=jax.ShapeDtypeStruct((bsz, s, A_WIDTH), BF16),
        compiler_params=_params("parallel", "parallel"),
        name="swa_attn",
    )(sinks, pa, pa, pa, pa, pa, bias_pairs)


def _t5_causal_bucket(dist):
    dist = jnp.maximum(dist, 0)
    d = jnp.maximum(dist, 1).astype(F32)
    large = MAX_EXACT + (jnp.log(d / MAX_EXACT) / math.log(MAX_DISTANCE / MAX_EXACT)
                         * (NUM_BUCKETS - MAX_EXACT)).astype(jnp.int32)
    large = jnp.minimum(large, NUM_BUCKETS - 1)
    return jnp.where(dist < MAX_EXACT, dist, large)


def _swa_bias_pairs(rel_bias):
    qi = jnp.arange(BLOCK)[:, None]
    ki = jnp.arange(2 * BLOCK)[None, :]
    dist = qi + BLOCK - ki
    in_window = (dist >= 0) & (dist < WINDOW)
    onehot = jax.nn.one_hot(_t5_causal_bucket(dist), NUM_BUCKETS, dtype=F32)
    bias = jnp.einsum("qkb,bh->qkh", onehot, rel_bias.astype(F32), precision=HI)
    bias = jnp.where(in_window[..., None], bias, MASKED).transpose(2, 0, 1)
    return jnp.stack([jnp.concatenate([bias[c], bias[c + 3]], axis=0)
                      for c in range(A_Q_HEADS // 2)])


def _split2(x):
    hi = x.astype(BF16)
    return hi, (x - hi.astype(F32)).astype(BF16)


def _dot3(a, b, nt=False):
    ah, al = _split2(a)
    bh, bl = _split2(b)
    d = _dot_nt if nt else _dot
    return d(ah, bh) + d(ah, bl) + d(al, bh)


def _dot3_many(pairs, nt=False):
    parts = [(_split2(a), _split2(b)) for a, b in pairs]
    d = _dot_nt if nt else _dot
    return [d(ah, bh) + d(ah, bl) + d(al, bh) for (ah, al), (bh, bl) in parts]


def _dot_x2(a, b_exact):
    ah, al = _split2(a)
    return _dot(ah, b_exact) + _dot(al, b_exact)


def _dot_2x(a_exact, b):
    bh, bl = _split2(b)
    return _dot(a_exact, bh) + _dot(a_exact, bl)


def _rwkv_kernel(pb_ref, mu_ref, w0_ref, wd_ref, a0_ref, wa_ref, wg_ref, kk_ref, ka_ref,
                 rk_ref, lng_ref, lnb_ref, o_ref, prev_ref, h_ref):
    t = pl.program_id(1)

    @pl.when(t == 0)
    def _():
        prev_ref[...] = jnp.zeros_like(prev_ref)
        h_ref[...] = jnp.zeros_like(h_ref)

    L = CHUNK
    W = B_WIDTH
    TS = RWKV_TS
    p = pb_ref[...]
    rows = lax.broadcasted_iota(jnp.int32, (TS, 1), 0)
    shifted = jnp.where(rows == 0, prev_ref[...], pltpu.roll(p, 1, axis=0))
    prev_ref[...] = p[TS - 1:TS, :]
    pm = p + (shifted - p) * mu_ref[...]
    r = pm[:, 0:W]
    k = pm[:, W:2 * W]
    v = pm[:, 2 * W:3 * W]
    xwa = pm[:, 3 * W:3 * W + DECAY_RANK + ICLR_RANK]
    xg = pm[:, 3 * W + DECAY_RANK + ICLR_RANK:]

    dw = w0_ref[...] + _dot(jnp.tanh(xwa).astype(BF16), wd_ref[...])
    lw = -jnp.exp(-_softplus(-dw) - 0.5)
    a = _sigmoid(a0_ref[...] + _dot(xwa.astype(BF16), wa_ref[...]))
    g = _dot(_sigmoid(xg).astype(BF16), wg_ref[...])

    hr = lax.broadcasted_iota(jnp.int32, (W, W), 0) // HEAD_DIM
    hc = lax.broadcasted_iota(jnp.int32, (W, W), 1) // HEAD_DIM
    same_head = hr == hc
    diag_w = (lax.broadcasted_iota(jnp.int32, (W, W), 0)
              == lax.broadcasted_iota(jnp.int32, (W, W), 1))
    head_ones = jnp.where(same_head, 1.0, 0.0).astype(BF16)
    kk = k * kk_ref[...]
    kk = kk * lax.rsqrt(jnp.maximum(_dot_x2(kk * kk, head_ones), 1e-24))
    k2 = k * (1.0 + (a - 1.0) * ka_ref[...])
    bonus = _dot_x2(r * k2 * rk_ref[...], head_ones) * v
    aa = -kk
    bb = kk * a

    ti = lax.broadcasted_iota(jnp.int32, (L, L), 0)
    tj = lax.broadcasted_iota(jnp.int32, (L, L), 1)
    lower = jnp.where(ti >= tj, 1.0, 0.0).astype(BF16)
    eye = jnp.where(ti == tj, 1.0, 0.0)
    lane_head = lax.broadcasted_iota(jnp.int32, (L, W), 1) // HEAD_DIM

    def only(x, h):
        return jnp.where(lane_head == h, x, 0.0)

    nch = TS // L
    chunks = range(nch)
    units = [(c, h) for c in chunks for h in range(B_HEADS)]
    hsl = [slice(h * L, (h + 1) * L) for h in range(B_HEADS)]
    csl = [slice(c * L, (c + 1) * L) for c in chunks]
    v_c = [v[s] for s in csl]
    cum = [_dot_2x(lower, lw[s]) for s in csl]
    cum_l = [x[L - 1:L, :] for x in cum]
    at = [aa[csl[c]] * jnp.exp(cum[c] - lw[csl[c]]) for c in chunks]
    rt = [r[csl[c]] * jnp.exp(cum[c]) for c in chunks]
    inv = [jnp.exp(-x) for x in cum]
    bt = [bb[csl[c]] * inv[c] for c in chunks]
    kt = [k2[csl[c]] * inv[c] for c in chunks]
    tail = [jnp.exp(cum_l[c] - cum[c]) for c in chunks]
    bh = [bb[csl[c]] * tail[c] for c in chunks]
    kh = [k2[csl[c]] * tail[c] for c in chunks]

    at_s = [jnp.concatenate([only(x, h) for h in range(B_HEADS)], axis=0) for x in at]
    rt_s = [jnp.concatenate([only(x, h) for h in range(B_HEADS)], axis=0) for x in rt]
    ab = _dot3_many([(at_s[c], bt[c]) for c in chunks], nt=True)
    ak = _dot3_many([(at_s[c], kt[c]) for c in chunks], nt=True)
    rb = _dot3_many([(rt_s[c], bt[c]) for c in chunks], nt=True)
    rk = _dot3_many([(rt_s[c], kt[c]) for c in chunks], nt=True)

    pw = [jnp.where(ti > tj, ab[c][hsl[h]], 0.0) for c, h in units]
    tinv = [eye + x for x in pw]
    for _ in range(int(math.log2(L)) - 1):
        pw = _dot3_many([(x, x) for x in pw])
        tinv = [t + d for t, d in zip(tinv, _dot3_many(list(zip(tinv, pw))))]
    ak_v = _dot3_many([(jnp.where(ti > tj, ak[c][hsl[h]], 0.0), v_c[c]) for c, h in units])
    w_u = _dot3_many([(tinv[i], only(at[c], h)) for i, (c, h) in enumerate(units)])
    u0_u = [only(x, h) for x, (c, h) in zip(_dot3_many(list(zip(tinv, ak_v))), units)]
    rb_l = [jnp.where(ti >= tj, rb[c][hsl[h]], 0.0) for c, h in units]
    rk_l = [jnp.where(ti >= tj, rk[c][hsl[h]], 0.0) for c, h in units]
    qm_u = _dot3_many(list(zip(rb_l, w_u)))
    y0_u = [only(p + q, h) for p, q, (c, h) in zip(
        _dot3_many(list(zip(rb_l, u0_u))),
        _dot3_many([(rk_l[i], v_c[c]) for i, (c, h) in enumerate(units)]), units)]

    def chunk_sum(xs, c):
        return functools.reduce(lambda p, q: p + q, xs[c * B_HEADS:(c + 1) * B_HEADS])

    w_sum = [chunk_sum(w_u, c) for c in chunks]
    u0 = [chunk_sum(u0_u, c) for c in chunks]
    qm = [rt[c] + chunk_sum(qm_u, c) for c in chunks]
    y0 = [chunk_sum(y0_u, c) for c in chunks]
    bw = _dot3_many([(bh[c].T, w_sum[c]) for c in chunks])
    g_mat = [jnp.where(same_head, bw[c], 0.0) + jnp.where(diag_w, jnp.exp(cum_l[c]).T, 0.0)
             for c in chunks]
    c_mat = [jnp.where(same_head, x, 0.0) for x in _dot3_many(
        [(jnp.concatenate([bh[c], kh[c]], axis=0).T, jnp.concatenate([u0[c], v_c[c]], axis=0))
         for c in chunks])]

    hst = h_ref[...]
    ys = []
    for c in chunks:
        ys.append(_dot3(qm[c], hst) + y0[c])
        hst = _dot3(g_mat[c], hst) + c_mat[c]
    h_ref[...] = hst
    y = jnp.concatenate(ys, axis=0)

    mean = _dot_x2(y, head_ones) * (1.0 / HEAD_DIM)
    yc = y - mean
    var = _dot_x2(yc * yc, head_ones) * (1.0 / HEAD_DIM)
    yn = yc * lax.rsqrt(var + GN_EPS) * lng_ref[...] + lnb_ref[...]
    o_ref[...] = ((yn + bonus) * g).astype(o_ref.dtype)


def _rwkv_call(pb, mu, w0, wd_pad, a0, wa_pad, wg, k_k, k_a, r_k, lnx_g, lnx_b):
    bsz, s, _ = pb.shape
    vec = lambda x: x.reshape(1, -1).astype(F32)
    small = [vec(mu), vec(w0), wd_pad, vec(a0), wa_pad, wg, vec(k_k), vec(k_a), vec(r_k),
             vec(lnx_g), vec(lnx_b)]
    return pl.pallas_call(
        _rwkv_kernel,
        grid=(bsz, s // RWKV_TS),
        in_specs=[pl.BlockSpec((None, RWKV_TS, B_COLS), lambda b, t: (b, t, 0))]
        + [pl.BlockSpec(x.shape, lambda b, t: (0, 0)) for x in small],
        out_specs=pl.BlockSpec((None, RWKV_TS, B_WIDTH), lambda b, t: (b, t, 0)),
        out_shape=jax.ShapeDtypeStruct((bsz, s, B_WIDTH), BF16),
        scratch_shapes=[pltpu.VMEM((1, B_COLS), F32), pltpu.VMEM((B_WIDTH, B_WIDTH), F32)],
        compiler_params=_params("parallel", "arbitrary"),
        name="rwkv7",
    )(pb, *small)


def _softplus2(z):
    return jnp.maximum(z, 0.0) + jnp.log2(1.0 + jnp.exp2(-jnp.abs(z)))


def _sb_kernel(q_ref, k_ref, vt_ref, o_ref, z_sc, lb_sc, e_sc, w_sc, acc_sc):
    it = pl.program_id(2)
    q2 = q_ref[...]
    lane = lax.broadcasted_iota(jnp.int32, (SB_TQ, LANES), 1)
    zero = jnp.zeros_like(q2)
    qh = (jnp.where(lane < HEAD_DIM, q2, zero), jnp.where(lane >= HEAD_DIM, q2, zero))
    ur = lax.broadcasted_iota(jnp.int32, (BLOCK, 2 * BLOCK), 0)
    uc = lax.broadcasted_iota(jnp.int32, (BLOCK, 2 * BLOCK), 1) % BLOCK
    ucat = jnp.where(uc > ur, 1.0, 0.0).astype(BF16)
    nsub = SB_TQ // BLOCK
    first = it * nsub + nsub - 1
    nblk = first + 1

    def key_off(n):
        return pl.multiple_of(jnp.clip(first - n, 0, first) * BLOCK, BLOCK)

    def logits(n):
        kblk = k_ref[pl.ds(key_off(n), BLOCK), :]
        return jnp.concatenate([_dot_nt(kblk, qh[0]), _dot_nt(kblk, qh[1])], axis=1)

    def softplus_stage(zt):
        sp = _softplus2(zt)
        sp_hi = sp.astype(BF16)
        sp_lo = (sp - sp_hi.astype(F32)).astype(BF16)
        stacked = jnp.concatenate([sp_hi, sp_lo], axis=0)
        return zt - sp, stacked, jnp.sum(sp, axis=0, keepdims=True)

    def value_stage(n):
        vt2 = vt_ref[:, pl.ds(key_off(n), BLOCK)]
        w = w_sc[...]
        for h in range(2):
            cols = slice(h * SB_TQ, (h + 1) * SB_TQ)
            acc_sc[:, cols] += _dot(vt2[h * HEAD_DIM:(h + 1) * HEAD_DIM], w[:, cols])

    def step(n, carry):
        car, colsum = carry
        value_stage(n - 1)
        z_new = logits(n + 2)
        lb_new, stacked, colsum_new = softplus_stage(z_sc[...])
        e_new = _dot(ucat, stacked)
        w_sc[...] = jnp.exp2(lb_sc[...] - e_sc[...] - car).astype(BF16)
        z_sc[...] = z_new
        lb_sc[...] = lb_new
        e_sc[...] = e_new
        return car + colsum, colsum_new

    def lane_off(n):
        return max(0, nsub - 1 - n) * BLOCK

    def near_logits(n):
        kblk = k_ref[pl.ds(key_off(n), BLOCK), :]
        return [_dot_nt(kblk, qh[h][lane_off(n):]) for h in range(2)]

    def near_softplus(zt, n):
        width = SB_TQ - lane_off(n)
        sp = _softplus2(zt)
        lb = zt - sp
        if n < nsub:
            kr = lax.broadcasted_iota(jnp.int32, (BLOCK, width), 0)
            qc = lax.broadcasted_iota(jnp.int32, (BLOCK, width), 1)
            before = kr < qc
            sp = jnp.where(before, sp, 0.0)
            lb = jnp.where(before, lb, MASKED)
        sp_hi = sp.astype(BF16)
        sp_lo = (sp - sp_hi.astype(F32)).astype(BF16)
        colsum = jnp.sum(sp, axis=0, keepdims=True)
        if lane_off(n):
            colsum = jnp.concatenate([jnp.zeros((1, lane_off(n)), F32), colsum], axis=1)
        return lb, jnp.concatenate([sp_hi, sp_lo], axis=0), colsum

    def near_values(n, ws):
        vt2 = vt_ref[:, pl.ds(key_off(n), BLOCK)]
        if n >= nsub:
            vt2 = jnp.where(n < nblk, vt2, jnp.zeros_like(vt2))
        for h in range(2):
            cols = slice(h * SB_TQ + lane_off(n), (h + 1) * SB_TQ)
            acc_sc[:, cols] += _dot(vt2[h * HEAD_DIM:(h + 1) * HEAD_DIM], ws[h])

    acc_sc[...] = jnp.zeros_like(acc_sc)
    zs = {0: near_logits(0), 1: near_logits(1)}
    soft = {0: [near_softplus(zs[0][h], 0) for h in range(2)]}
    excl = {0: [_dot(ucat, soft[0][h][1]) for h in range(2)]}
    cars = [jnp.zeros((1, SB_TQ), F32) for _ in range(2)]
    ws = None
    for n in range(SB_NEAR):
        if n >= 1:
            near_values(n - 1, ws)
        if n + 2 < SB_NEAR:
            zs[n + 2] = near_logits(n + 2)
        if n + 1 < SB_NEAR:
            soft[n + 1] = [near_softplus(zs[n + 1][h], n + 1) for h in range(2)]
            excl[n + 1] = [_dot(ucat, soft[n + 1][h][1]) for h in range(2)]
        ws = [jnp.exp2(soft[n][h][0] - excl[n][h] - cars[h][:, lane_off(n):]).astype(BF16)
              for h in range(2)]
        cars = [cars[h] + soft[n][h][2] for h in range(2)]
    near_values(SB_NEAR - 1, ws)
    car = jnp.concatenate(cars, axis=1)

    @pl.when(jnp.logical_and(nblk > SB_NEAR, jnp.min(car) < SB_DEAD))
    def _():
        lb0, stacked0, colsum0 = softplus_stage(logits(SB_NEAR))
        lb_sc[...] = lb0
        e_sc[...] = _dot(ucat, stacked0)
        z_sc[...] = logits(SB_NEAR + 1)
        w_sc[...] = jnp.zeros_like(w_sc)

        def live(c):
            n, _, _, alive = c
            return jnp.logical_and(n < nblk, alive > 0)

        def visit(c):
            n, car, colsum, _ = c
            alive = (jnp.min(car) < SB_DEAD).astype(jnp.int32)
            car, colsum = step(n, (car, colsum))
            return n + jnp.int32(1), car, colsum, alive

        n_stop = lax.while_loop(live, visit, (jnp.int32(SB_NEAR), car, colsum0, jnp.int32(1)))[0]
        value_stage(n_stop - 1)

    acc = acc_sc[...]
    out_t = jnp.concatenate([acc[:, :SB_TQ], acc[:, SB_TQ:]], axis=0)
    o_ref[...] = out_t.T.astype(o_ref.dtype)


def _sb_call(qc, kc, vt):
    bsz, s, _ = qc.shape
    return pl.pallas_call(
        _sb_kernel,
        grid=(bsz, C_WIDTH // LANES, s // SB_TQ),
        in_specs=[pl.BlockSpec((None, SB_TQ, LANES), lambda b, hp, i: (b, i, hp)),
                  pl.BlockSpec((None, s, LANES), lambda b, hp, i: (b, 0, hp)),
                  pl.BlockSpec((None, LANES, s), lambda b, hp, i: (b, hp, 0))],
        out_specs=pl.BlockSpec((None, SB_TQ, LANES), lambda b, hp, i: (b, i, hp)),
        out_shape=jax.ShapeDtypeStruct((bsz, s, C_WIDTH), BF16),
        scratch_shapes=[pltpu.VMEM((BLOCK, 2 * SB_TQ), F32),
                        pltpu.VMEM((BLOCK, 2 * SB_TQ), F32),
                        pltpu.VMEM((BLOCK, 2 * SB_TQ), F32),
                        pltpu.VMEM((BLOCK, 2 * SB_TQ), BF16),
                        pltpu.VMEM((HEAD_DIM, 2 * SB_TQ), F32)],
        compiler_params=_params("parallel", "parallel", "parallel"),
        name="stickbreak_attn",
    )(qc, kc, vt)


def _outproj_kernel(h_ref, oa_ref, ob_ref, oc_ref, wa_ref, wb_ref, wc_ref, g_ref, b_ref,
                    rw_ref, rb_ref, o_ref, gate_ref):
    m = (_dot(oa_ref[...], wa_ref[...]) + _dot(ob_ref[...], wb_ref[...])
         + _dot(oc_ref[...], wc_ref[...]))
    h1 = _layer_norm(ALPHA * h_ref[...] + m, g_ref[...], b_ref[...])
    o_ref[...] = h1
    logits = _dot(h1, rw_ref[...], HI)
    gate_ref[...] = _route(logits.T[:rb_ref.shape[0]], rb_ref[...])


def _outproj_call(h, oa, ob, oc, wa, wb, wc, g, b, rw, rb_col, tm=512):
    bsz, s, d = h.shape
    nt = s // tm
    rows = rb_col.shape[0]
    row = lambda c: pl.BlockSpec((None, tm, c), lambda bb, t: (bb, t, 0))
    full = lambda w: pl.BlockSpec(w.shape, lambda bb, t: (0, 0))
    g2, b2 = g.reshape(1, d), b.reshape(1, d)
    return pl.pallas_call(
        _outproj_kernel,
        grid=(bsz, nt),
        in_specs=[row(d), row(A_WIDTH), row(B_WIDTH), row(C_WIDTH),
                  full(wa), full(wb), full(wc), full(g2), full(b2), full(rw), full(rb_col)],
        out_specs=[row(d), pl.BlockSpec((rows, tm), lambda bb, t: (0, bb * nt + t))],
        out_shape=[jax.ShapeDtypeStruct((bsz, s, d), F32),
                   jax.ShapeDtypeStruct((rows, bsz * s), F32)],
        compiler_params=_params("parallel", "parallel"),
        name="out_proj_ln_router",
    )(h, oa, ob, oc, wa, wb, wc, g2, b2, rw, rb_col)


def _route(logits, rb):
    scores = _sigmoid(logits)
    sel = scores + rb
    R = ROUTER_ROWS
    s = [sel[m * R:(m + 1) * R] for m in range(EXPERTS_PER_GROUP)]
    sc = [scores[m * R:(m + 1) * R] for m in range(EXPERTS_PER_GROUP)]
    hi01, lo01 = jnp.maximum(s[0], s[1]), jnp.minimum(s[0], s[1])
    hi23, lo23 = jnp.maximum(s[2], s[3]), jnp.minimum(s[2], s[3])
    top1 = jnp.maximum(hi01, hi23)
    top2 = jnp.maximum(jnp.minimum(hi01, hi23), jnp.maximum(lo01, lo23))
    gscore = top1 + top2
    gi = lax.broadcasted_iota(jnp.int32, gscore.shape, 0)
    gmax = jnp.max(gscore, axis=0, keepdims=True)
    best = jnp.min(jnp.where(gscore == gmax, gi, R), axis=0, keepdims=True)
    in_group = gi == best
    picked = []
    for m in range(EXPERTS_PER_GROUP):
        rank = jnp.zeros(gscore.shape, jnp.int32)
        for j in range(EXPERTS_PER_GROUP):
            if j == m:
                continue
            ahead = (s[j] >= s[m]) if j < m else (s[j] > s[m])
            rank = rank + jnp.where(ahead, 1, 0)
        picked.append(jnp.where(jnp.logical_and(in_group, rank < 2), sc[m], 0.0))
    denom = jnp.sum(picked[0] + picked[1] + picked[2] + picked[3], axis=0, keepdims=True)
    return jnp.concatenate([x / denom for x in picked], axis=0)


def _moe_kernel(x_ref, gate_ref, wg_ref, wu_ref, wd_ref, g_ref, b_ref, o_ref, xb_ref, acc_ref):
    e = pl.program_id(1)

    @pl.when(e == 0)
    def _():
        xb_ref[...] = x_ref[...].astype(BF16)
        acc_ref[...] = jnp.zeros_like(acc_ref)

    xb = xb_ref[...]
    hg = _dot(xb, wg_ref[...])
    hu = _dot(xb, wu_ref[...])
    gates = gate_ref[...]
    lane = lax.broadcasted_iota(jnp.int32, gates.shape, 1)
    gcol = jnp.sum(jnp.where(lane == e, gates, 0.0), axis=-1, keepdims=True)
    act = (hg * _sigmoid(hg)) * hu * gcol
    acc_ref[...] += _dot(act.astype(BF16), wd_ref[...])

    @pl.when(e == N_EXPERTS - 1)
    def _():
        o_ref[...] = _layer_norm(ALPHA * x_ref[...] + acc_ref[...], g_ref[...], b_ref[...])


def _moe_call(x, gates, wg, wu, wd, g, b, tm=1024):
    n, d = x.shape
    f = wg.shape[-1]
    g2, b2 = g.reshape(1, d), b.reshape(1, d)
    return pl.pallas_call(
        _moe_kernel,
        grid=(n // tm, N_EXPERTS),
        in_specs=[pl.BlockSpec((tm, d), lambda i, e: (i, 0)),
                  pl.BlockSpec((tm, N_EXPERTS), lambda i, e: (i, 0)),
                  pl.BlockSpec((None, d, f), lambda i, e: (e, 0, 0)),
                  pl.BlockSpec((None, d, f), lambda i, e: (e, 0, 0)),
                  pl.BlockSpec((None, f, d), lambda i, e: (e, 0, 0)),
                  pl.BlockSpec((1, d), lambda i, e: (0, 0)),
                  pl.BlockSpec((1, d), lambda i, e: (0, 0))],
        out_specs=pl.BlockSpec((tm, d), lambda i, e: (i, 0)),
        out_shape=jax.ShapeDtypeStruct((n, d), F32),
        scratch_shapes=[pltpu.VMEM((tm, d), BF16), pltpu.VMEM((tm, d), F32)],
        compiler_params=_params("parallel", "arbitrary"),
        name="moe_experts_ln",
    )(x, gates, wg, wu, wd, g2, b2)


def _pair_heads(x, axis):
    shape = x.shape
    x = x.reshape(shape[:axis] + (A_KV_HEADS, A_Q_HEADS // A_KV_HEADS, HEAD_DIM) + shape[axis + 1:])
    x = jnp.swapaxes(x, axis, axis + 1)
    return x.reshape(shape)


def _router_layout(router_w, router_bias):
    d = router_w.shape[0]
    w = router_w.astype(F32).T.reshape(N_GROUPS, EXPERTS_PER_GROUP, d).transpose(1, 0, 2)
    w = jnp.pad(w, ((0, 0), (0, ROUTER_ROWS - N_GROUPS), (0, 0)))
    b = router_bias.astype(F32).reshape(N_GROUPS, EXPERTS_PER_GROUP).T
    b = jnp.pad(b, ((0, 0), (0, ROUTER_ROWS - N_GROUPS)), constant_values=MASKED)
    rows = EXPERTS_PER_GROUP * ROUTER_ROWS
    w = jnp.pad(w.reshape(rows, d).T, ((0, 0), (0, LANES - rows)))
    return w, b.reshape(rows, 1)


def _gates_from_router(gates_t):
    n = gates_t.shape[1]
    g = gates_t.reshape(EXPERTS_PER_GROUP, ROUTER_ROWS, n)[:, :N_GROUPS]
    return g.transpose(2, 1, 0).reshape(n, N_EXPERTS)


def kernel(x, ln0_g, ln0_b, w_in, w_out, sinks, rel_bias, shift_mu, decay_w0, decay_up, iclr_a0,
           iclr_up, gate_up, k_k, k_a, r_k, lnx_g, lnx_b, ln1_g, ln1_b, router_w, router_bias,
           w_gate, w_up, w_down, ln2_g, ln2_b):
    bsz, s, d = x.shape
    n = bsz * s
    bias_pairs = _swa_bias_pairs(rel_bias)
    rw, rb_col = _router_layout(router_w, router_bias)
    zeros_lora = jnp.zeros((ICLR_RANK, B_WIDTH), F32)

    h = _ln_call(x.reshape(n, d), ln0_g, ln0_b).reshape(bsz, s, d)
    for l in range(DEPTH):
        wl = w_in[l]
        wa = jnp.concatenate([_pair_heads(wl[:, :A_WIDTH], 1), wl[:, A_WIDTH:A_COLS]], axis=1)
        wb = wl[:, A_COLS:A_COLS + B_COLS]
        c0 = A_COLS + B_COLS
        wq, wk, wv = (wl[:, c0 + j * C_WIDTH:c0 + (j + 1) * C_WIDTH] for j in range(3))
        pa, pb, qc, kc, vt = _proj_call(h, wa.astype(BF16), wb.astype(BF16), wq.astype(BF16),
                                        wk.astype(BF16), wv.T.astype(BF16))

        out_a = _swa_call(pa, sinks[l].astype(F32), bias_pairs)
        wd_pad = jnp.concatenate([decay_up[l].astype(F32), zeros_lora], axis=0).astype(BF16)
        wa_pad = jnp.concatenate([zeros_lora, iclr_up[l].astype(F32)], axis=0).astype(BF16)
        out_b = _rwkv_call(pb, shift_mu[l], decay_w0[l], wd_pad, iclr_a0[l], wa_pad,
                           gate_up[l].astype(BF16), k_k[l], k_a[l], r_k[l], lnx_g[l], lnx_b[l])
        out_c = _sb_call(qc, kc, vt)

        wo = w_out[l]
        h, gates_t = _outproj_call(h, out_a, out_b, out_c,
                                   _pair_heads(wo[:A_WIDTH], 0).astype(BF16),
                                   wo[A_WIDTH:A_WIDTH + B_WIDTH].astype(BF16),
                                   wo[A_WIDTH + B_WIDTH:].astype(BF16), ln1_g[l], ln1_b[l],
                                   rw, rb_col)

        hf = h.reshape(n, d)
        gates = _gates_from_router(gates_t)
        hf = _moe_call(hf, gates, w_gate[l].astype(BF16), w_up[l].astype(BF16),
                       w_down[l].astype(BF16), ln2_g[l], ln2_b[l])
        h = hf.reshape(bsz, s, d)
    return h
```

```python
import functools
import math

import jax
import jax.numpy as jnp
from jax import lax
from jax.experimental import pallas as pl
from jax.experimental.pallas import tpu as pltpu

F32 = jnp.float32
BF16 = jnp.bfloat16
HI = lax.Precision.HIGHEST

DEPTH = 2
HEAD_DIM = 64
BLOCK = 128
LANES = 128
A_Q_HEADS = 6
A_KV_HEADS = 2
WINDOW = 128
A_WIDTH = A_Q_HEADS * HEAD_DIM
A_KV_WIDTH = A_KV_HEADS * HEAD_DIM
B_HEADS = 4
B_WIDTH = B_HEADS * HEAD_DIM
DECAY_RANK = 64
ICLR_RANK = 64
GATE_RANK = 128
GN_EPS = 64e-5
C_HEADS = 6
C_WIDTH = C_HEADS * HEAD_DIM
A_COLS = A_WIDTH + 2 * A_KV_WIDTH
B_COLS = 3 * B_WIDTH + DECAY_RANK + ICLR_RANK + GATE_RANK
NUM_BUCKETS = 32
MAX_EXACT = NUM_BUCKETS // 2
MAX_DISTANCE = 128
N_EXPERTS = 16
N_GROUPS = 4
EXPERTS_PER_GROUP = N_EXPERTS // N_GROUPS
D_FF_EXPERT = 256
LN_EPS = 1e-5
ALPHA = (2 * DEPTH) ** 0.25
SCALE = HEAD_DIM ** -0.5
MASKED = -1e30
SWA_QB = 2
CHUNK = 64
RWKV_TS = 256
SB_TQ = 512
SB_NEAR = 6
SB_DEAD = 151.0
LOG2E = 1.4426950408889634
ROUTER_ROWS = 8

VMEM_LIMIT = 48 * 1024 * 1024


def _dot(a, b, prec=None):
    return jnp.dot(a, b, preferred_element_type=F32, precision=prec)


def _dot_nt(a, b, prec=None):
    return lax.dot_general(a, b, (((1,), (1,)), ((), ())),
                           preferred_element_type=F32, precision=prec)


def _sigmoid(x):
    return 1.0 / (1.0 + jnp.exp(-x))


def _softplus(x):
    return jnp.maximum(x, 0.0) + jnp.log(1.0 + jnp.exp(-jnp.abs(x)))


def _layer_norm(x, g, b):
    mu = jnp.mean(x, axis=-1, keepdims=True)
    xc = x - mu
    var = jnp.mean(xc * xc, axis=-1, keepdims=True)
    return xc * lax.rsqrt(var + LN_EPS) * g + b


def _params(*sem):
    return pltpu.CompilerParams(dimension_semantics=sem, vmem_limit_bytes=VMEM_LIMIT)


def _ln_kernel(x_ref, g_ref, b_ref, o_ref):
    o_ref[...] = _layer_norm(x_ref[...], g_ref[...], b_ref[...])


def _ln_call(x, g, b, tm=512):
    n, d = x.shape
    return pl.pallas_call(
        _ln_kernel,
        grid=(n // tm,),
        in_specs=[pl.BlockSpec((tm, d), lambda i: (i, 0)),
                  pl.BlockSpec((1, d), lambda i: (0, 0)),
                  pl.BlockSpec((1, d), lambda i: (0, 0))],
        out_specs=pl.BlockSpec((tm, d), lambda i: (i, 0)),
        out_shape=jax.ShapeDtypeStruct((n, d), F32),
        compiler_params=_params("parallel"),
        name="embed_ln",
    )(x, g.reshape(1, d), b.reshape(1, d))


def _proj_kernel(h_ref, wa_ref, wb_ref, wq_ref, wk_ref, wvt_ref,
                 pa_ref, pb_ref, qc_ref, kc_ref, vt_ref):
    hb = h_ref[...].astype(BF16)
    pa_ref[...] = _dot(hb, wa_ref[...]).astype(BF16)
    pb_ref[...] = _dot(hb, wb_ref[...])
    qc_ref[...] = (_dot(hb, wq_ref[...]) * (SCALE * LOG2E)).astype(BF16)
    kc_ref[...] = _dot(hb, wk_ref[...]).astype(BF16)
    vt_ref[...] = _dot_nt(wvt_ref[...], hb).astype(BF16)


def _proj_call(h, wa, wb, wq, wk, wvt, tm=512):
    bsz, s, d = h.shape
    full = lambda w: pl.BlockSpec(w.shape, lambda b, t: (0, 0))
    row = lambda c: pl.BlockSpec((None, tm, c), lambda b, t: (b, t, 0))
    return pl.pallas_call(
        _proj_kernel,
        grid=(bsz, s // tm),
        in_specs=[row(d), full(wa), full(wb), full(wq), full(wk), full(wvt)],
        out_specs=[row(A_COLS), row(B_COLS), row(C_WIDTH), row(C_WIDTH),
                   pl.BlockSpec((None, C_WIDTH, tm), lambda b, t: (b, 0, t))],
        out_shape=[jax.ShapeDtypeStruct((bsz, s, A_COLS), BF16),
                   jax.ShapeDtypeStruct((bsz, s, B_COLS), F32),
                   jax.ShapeDtypeStruct((bsz, s, C_WIDTH), BF16),
                   jax.ShapeDtypeStruct((bsz, s, C_WIDTH), BF16),
                   jax.ShapeDtypeStruct((bsz, C_WIDTH, s), BF16)],
        compiler_params=_params("parallel", "parallel"),
        name="in_proj",
    )(h, wa, wb, wq, wk, wvt)


def _swa_kernel(sink_ref, q_ref, kp_ref, kc_ref, vp_ref, vc_ref, bias_ref, o_ref):
    n = pl.program_id(1)
    kall = jnp.concatenate([kp_ref[...], kc_ref[...]], axis=0)
    vall = jnp.concatenate([vp_ref[...], vc_ref[...]], axis=0)
    lane = lax.broadcasted_iota(jnp.int32, (BLOCK, LANES), 1)
    row2 = lax.broadcasted_iota(jnp.int32, (2 * BLOCK, 1), 0)
    col2 = lax.broadcasted_iota(jnp.int32, (1, 2 * BLOCK), 1)
    pad = jnp.where(jnp.logical_and(n == 0, col2 < BLOCK), MASKED, 0.0)
    units = [(j, c) for j in range(SWA_QB) for c in range(A_Q_HEADS // 2)]
    logits = []
    for j, c in units:
        q2 = q_ref[j * BLOCK:(j + 1) * BLOCK, c * LANES:(c + 1) * LANES]
        zero = jnp.zeros_like(q2)
        qs = jnp.concatenate([jnp.where(lane < HEAD_DIM, q2, zero),
                              jnp.where(lane >= HEAD_DIM, q2, zero)], axis=0)
        x = _dot_nt(qs, kall[j * BLOCK:(j + 2) * BLOCK]) * SCALE + bias_ref[c]
        logits.append(x + pad if j == 0 else x)
    probs, denoms = [], []
    for (j, c), x in zip(units, logits):
        sink = jnp.where(row2 < BLOCK, sink_ref[c], sink_ref[c + 3])
        m = jnp.maximum(jnp.max(x, axis=-1, keepdims=True), sink)
        p = jnp.exp(x - m)
        denoms.append(jnp.sum(p, axis=-1, keepdims=True) + jnp.exp(sink - m))
        probs.append(p.astype(BF16))
    for (j, c), p, denom in zip(units, probs, denoms):
        o = _dot(p, vall[j * BLOCK:(j + 2) * BLOCK]) / denom
        o_ref[j * BLOCK:(j + 1) * BLOCK, c * LANES:(c + 1) * LANES] = jnp.where(
            lane < HEAD_DIM, o[:BLOCK], o[BLOCK:]).astype(BF16)


def _swa_call(pa, sinks, bias_pairs):
    bsz, s, _ = pa.shape
    tq = SWA_QB * BLOCK
    kcol = A_WIDTH // LANES
    vcol = kcol + 1
    prev = lambda n: jnp.maximum(n * SWA_QB - 1, 0)
    return pl.pallas_call(
        _swa_kernel,
        grid=(bsz, s // tq),
        in_specs=[pl.BlockSpec(memory_space=pltpu.SMEM),
                  pl.BlockSpec((None, tq, A_WIDTH), lambda b, n: (b, n, 0)),
                  pl.BlockSpec((None, BLOCK, LANES), lambda b, n: (b, prev(n), kcol)),
                  pl.BlockSpec((None, tq, LANES), lambda b, n: (b, n, kcol)),
                  pl.BlockSpec((None, BLOCK, LANES), lambda b, n: (b, prev(n), vcol)),
                  pl.BlockSpec((None, tq, LANES), lambda b, n: (b, n, vcol)),
                  pl.BlockSpec(bias_pairs.shape, lambda b, n: (0, 0, 0))],
        out_specs=pl.BlockSpec((None, tq, A_WIDTH), lambda b, n: (b, n, 0)),
        out_shape=jax.ShapeDtypeStruct((bsz, s, A_WIDTH), BF16),
        compiler_params=_params("parallel", "parallel"),
        name="swa_attn",
    )(sinks, pa, pa, pa, pa, pa, bias_pairs)


def _t5_causal_bucket(dist):
    dist = jnp.maximum(dist, 0)
    d = jnp.maximum(dist, 1).astype(F32)
    large = MAX_EXACT + (jnp.log(d / MAX_EXACT) / math.log(MAX_DISTANCE / MAX_EXACT)
                         * (NUM_BUCKETS - MAX_EXACT)).astype(jnp.int32)
    large = jnp.minimum(large, NUM_BUCKETS - 1)
    return jnp.where(dist < MAX_EXACT, dist, large)


def _swa_bias_pairs(rel_bias):
    qi = jnp.arange(BLOCK)[:, None]
    ki = jnp.arange(2 * BLOCK)[None, :]
    dist = qi + BLOCK - ki
    in_window = (dist >= 0) & (dist < WINDOW)
    onehot = jax.nn.one_hot(_t5_causal_bucket(dist), NUM_BUCKETS, dtype=F32)
    bias = jnp.einsum("qkb,bh->qkh", onehot, rel_bias.astype(F32), precision=HI)
    bias = jnp.where(in_window[..., None], bias, MASKED).transpose(2, 0, 1)
    return jnp.stack([jnp.concatenate([bias[c], bias[c + 3]], axis=0)
                      for c in range(A_Q_HEADS // 2)])


def _split2(x):
    hi = x.astype(BF16)
    return hi, (x - hi.astype(F32)).astype(BF16)


def _dot3(a, b, nt=False):
    ah, al = _split2(a)
    bh, bl = _split2(b)
    d = _dot_nt if nt else _dot
    return d(ah, bh) + d(ah, bl) + d(al, bh)


def _dot3_many(pairs, nt=False):
    parts = [(_split2(a), _split2(b)) for a, b in pairs]
    d = _dot_nt if nt else _dot
    return [d(ah, bh) + d(ah, bl) + d(al, bh) for (ah, al), (bh, bl) in parts]


def _dot_x2(a, b_exact):
    ah, al = _split2(a)
    return _dot(ah, b_exact) + _dot(al, b_exact)


def _dot_2x(a_exact, b):
    bh, bl = _split2(b)
    return _dot(a_exact, bh) + _dot(a_exact, bl)


def _rwkv_kernel(pb_ref, mu_ref, w0_ref, wd_ref, a0_ref, wa_ref, wg_ref, kk_ref, ka_ref,
                 rk_ref, lng_ref, lnb_ref, o_ref, prev_ref, h_ref):
    t = pl.program_id(1)

    @pl.when(t == 0)
    def _():
        prev_ref[...] = jnp.zeros_like(prev_ref)
        h_ref[...] = jnp.zeros_like(h_ref)

    L = CHUNK
    W = B_WIDTH
    TS = RWKV_TS
    p = pb_ref[...]
    rows = lax.broadcasted_iota(jnp.int32, (TS, 1), 0)
    shifted = jnp.where(rows == 0, prev_ref[...], pltpu.roll(p, 1, axis=0))
    prev_ref[...] = p[TS - 1:TS, :]
    pm = p + (shifted - p) * mu_ref[...]
    r = pm[:, 0:W]
    k = pm[:, W:2 * W]
    v = pm[:, 2 * W:3 * W]
    xwa = pm[:, 3 * W:3 * W + DECAY_RANK + ICLR_RANK]
    xg = pm[:, 3 * W + DECAY_RANK + ICLR_RANK:]

    dw = w0_ref[...] + _dot(jnp.tanh(xwa).astype(BF16), wd_ref[...])
    lw = -jnp.exp(-_softplus(-dw) - 0.5)
    a = _sigmoid(a0_ref[...] + _dot(xwa.astype(BF16), wa_ref[...]))
    g = _dot(_sigmoid(xg).astype(BF16), wg_ref[...])

    hr = lax.broadcasted_iota(jnp.int32, (W, W), 0) // HEAD_DIM
    hc = lax.broadcasted_iota(jnp.int32, (W, W), 1) // HEAD_DIM
    same_head = hr == hc
    diag_w = (lax.broadcasted_iota(jnp.int32, (W, W), 0)
              == lax.broadcasted_iota(jnp.int32, (W, W), 1))
    head_ones = jnp.where(same_head, 1.0, 0.0).astype(BF16)
    kk = k * kk_ref[...]
    kk = kk * lax.rsqrt(jnp.maximum(_dot_x2(kk * kk, head_ones), 1e-24))
    k2 = k * (1.0 + (a - 1.0) * ka_ref[...])
    bonus = _dot_x2(r * k2 * rk_ref[...], head_ones) * v
    aa = -kk
    bb = kk * a

    ti = lax.broadcasted_iota(jnp.int32, (L, L), 0)
    tj = lax.broadcasted_iota(jnp.int32, (L, L), 1)
    lower = jnp.where(ti >= tj, 1.0, 0.0).astype(BF16)
    eye = jnp.where(ti == tj, 1.0, 0.0)
    lane_head = lax.broadcasted_iota(jnp.int32, (L, W), 1) // HEAD_DIM

    def only(x, h):
        return jnp.where(lane_head == h, x, 0.0)

    nch = TS // L
    chunks = range(nch)
    units = [(c, h) for c in chunks for h in range(B_HEADS)]
    hsl = [slice(h * L, (h + 1) * L) for h in range(B_HEADS)]
    csl = [slice(c * L, (c + 1) * L) for c in chunks]
    v_c = [v[s] for s in csl]
    cum = [_dot_2x(lower, lw[s]) for s in csl]
    cum_l = [x[L - 1:L, :] for x in cum]
    at = [aa[csl[c]] * jnp.exp(cum[c] - lw[csl[c]]) for c in chunks]
    rt = [r[csl[c]] * jnp.exp(cum[c]) for c in chunks]
    inv = [jnp.exp(-x) for x in cum]
    bt = [bb[csl[c]] * inv[c] for c in chunks]
    kt = [k2[csl[c]] * inv[c] for c in chunks]
    tail = [jnp.exp(cum_l[c] - cum[c]) for c in chunks]
    bh = [bb[csl[c]] * tail[c] for c in chunks]
    kh = [k2[csl[c]] * tail[c] for c in chunks]

    at_s = [jnp.concatenate([only(x, h) for h in range(B_HEADS)], axis=0) for x in at]
    rt_s = [jnp.concatenate([only(x, h) for h in range(B_HEADS)], axis=0) for x in rt]
    ab = _dot3_many([(at_s[c], bt[c]) for c in chunks], nt=True)
    ak = _dot3_many([(at_s[c], kt[c]) for c in chunks], nt=True)
    rb = _dot3_many([(rt_s[c], bt[c]) for c in chunks], nt=True)
    rk = _dot3_many([(rt_s[c], kt[c]) for c in chunks], nt=True)

    pw = [jnp.where(ti > tj, ab[c][hsl[h]], 0.0) for c, h in units]
    tinv = [eye + x for x in pw]
    for _ in range(int(math.log2(L)) - 1):
        pw = _dot3_many([(x, x) for x in pw])
        tinv = [t + d for t, d in zip(tinv, _dot3_many(list(zip(tinv, pw))))]
    ak_v = _dot3_many([(jnp.where(ti > tj, ak[c][hsl[h]], 0.0), v_c[c]) for c, h in units])
    w_u = _dot3_many([(tinv[i], only(at[c], h)) for i, (c, h) in enumerate(units)])
    u0_u = [only(x, h) for x, (c, h) in zip(_dot3_many(list(zip(tinv, ak_v))), units)]
    rb_l = [jnp.where(ti >= tj, rb[c][hsl[h]], 0.0) for c, h in units]
    rk_l = [jnp.where(ti >= tj, rk[c][hsl[h]], 0.0) for c, h in units]
    qm_u = _dot3_many(list(zip(rb_l, w_u)))
    y0_u = [only(p + q, h) for p, q, (c, h) in zip(
        _dot3_many(list(zip(rb_l, u0_u))),
        _dot3_many([(rk_l[i], v_c[c]) for i, (c, h) in enumerate(units)]), units)]

    def chunk_sum(xs, c):
        return functools.reduce(lambda p, q: p + q, xs[c * B_HEADS:(c + 1) * B_HEADS])

    w_sum = [chunk_sum(w_u, c) for c in chunks]
    u0 = [chunk_sum(u0_u, c) for c in chunks]
    qm = [rt[c] + chunk_sum(qm_u, c) for c in chunks]
    y0 = [chunk_sum(y0_u, c) for c in chunks]
    bw = _dot3_many([(bh[c].T, w_sum[c]) for c in chunks])
    g_mat = [jnp.where(same_head, bw[c], 0.0) + jnp.where(diag_w, jnp.exp(cum_l[c]).T, 0.0)
             for c in chunks]
    c_mat = [jnp.where(same_head, x, 0.0) for x in _dot3_many(
        [(jnp.concatenate([bh[c], kh[c]], axis=0).T, jnp.concatenate([u0[c], v_c[c]], axis=0))
         for c in chunks])]

    hst = h_ref[...]
    ys = []
    for c in chunks:
        ys.append(_dot3(qm[c], hst) + y0[c])
        hst = _dot3(g_mat[c], hst) + c_mat[c]
    h_ref[...] = hst
    y = jnp.concatenate(ys, axis=0)

    mean = _dot_x2(y, head_ones) * (1.0 / HEAD_DIM)
    yc = y - mean
    var = _dot_x2(yc * yc, head_ones) * (1.0 / HEAD_DIM)
    yn = yc * lax.rsqrt(var + GN_EPS) * lng_ref[...] + lnb_ref[...]
    o_ref[...] = ((yn + bonus) * g).astype(o_ref.dtype)


def _rwkv_call(pb, mu, w0, wd_pad, a0, wa_pad, wg, k_k, k_a, r_k, lnx_g, lnx_b):
    bsz, s, _ = pb.shape
    vec = lambda x: x.reshape(1, -1).astype(F32)
    small = [vec(mu), vec(w0), wd_pad, vec(a0), wa_pad, wg, vec(k_k), vec(k_a), vec(r_k),
             vec(lnx_g), vec(lnx_b)]
    return pl.pallas_call(
        _rwkv_kernel,
        grid=(bsz, s // RWKV_TS),
        in_specs=[pl.BlockSpec((None, RWKV_TS, B_COLS), lambda b, t: (b, t, 0))]
        + [pl.BlockSpec(x.shape, lambda b, t: (0, 0)) for x in small],
        out_specs=pl.BlockSpec((None, RWKV_TS, B_WIDTH), lambda b, t: (b, t, 0)),
        out_shape=jax.ShapeDtypeStruct((bsz, s, B_WIDTH), BF16),
        scratch_shapes=[pltpu.VMEM((1, B_COLS), F32), pltpu.VMEM((B_WIDTH, B_WIDTH), F32)],
        compiler_params=_params("parallel", "arbitrary"),
        name="rwkv7",
    )(pb, *small)


def _softplus2(z):
    return jnp.maximum(z, 0.0) + jnp.log2(1.0 + jnp.exp2(-jnp.abs(z)))


def _sb_kernel(q_ref, k_ref, vt_ref, o_ref, z_sc, lb_sc, e_sc, w_sc, acc_sc):
    it = pl.program_id(2)
    q2 = q_ref[...]
    lane = lax.broadcasted_iota(jnp.int32, (SB_TQ, LANES), 1)
    zero = jnp.zeros_like(q2)
    qh = (jnp.where(lane < HEAD_DIM, q2, zero), jnp.where(lane >= HEAD_DIM, q2, zero))
    ur = lax.broadcasted_iota(jnp.int32, (BLOCK, 2 * BLOCK), 0)
    uc = lax.broadcasted_iota(jnp.int32, (BLOCK, 2 * BLOCK), 1) % BLOCK
    ucat = jnp.where(uc > ur, 1.0, 0.0).astype(BF16)
    nsub = SB_TQ // BLOCK
    first = it * nsub + nsub - 1
    nblk = first + 1

    def key_off(n):
        return pl.multiple_of(jnp.clip(first - n, 0, first) * BLOCK, BLOCK)

    def logits(n):
        kblk = k_ref[pl.ds(key_off(n), BLOCK), :]
        return jnp.concatenate([_dot_nt(kblk, qh[0]), _dot_nt(kblk, qh[1])], axis=1)

    def softplus_stage(zt):
        sp = _softplus2(zt)
        sp_hi = sp.astype(BF16)
        sp_lo = (sp - sp_hi.astype(F32)).astype(BF16)
        stacked = jnp.concatenate([sp_hi, sp_lo], axis=0)
        return zt - sp, stacked, jnp.sum(sp, axis=0, keepdims=True)

    def value_stage(n):
        vt2 = vt_ref[:, pl.ds(key_off(n), BLOCK)]
        w = w_sc[...]
        for h in range(2):
            cols = slice(h * SB_TQ, (h + 1) * SB_TQ)
            acc_sc[:, cols] += _dot(vt2[h * HEAD_DIM:(h + 1) * HEAD_DIM], w[:, cols])

    def step(n, carry):
        car, colsum = carry
        value_stage(n - 1)
        z_new = logits(n + 2)
        lb_new, stacked, colsum_new = softplus_stage(z_sc[...])
        e_new = _dot(ucat, stacked)
        w_sc[...] = jnp.exp2(lb_sc[...] - e_sc[...] - car).astype(BF16)
        z_sc[...] = z_new
        lb_sc[...] = lb_new
        e_sc[...] = e_new
        return car + colsum, colsum_new

    def lane_off(n):
        return max(0, nsub - 1 - n) * BLOCK

    def near_logits(n):
        kblk = k_ref[pl.ds(key_off(n), BLOCK), :]
        return [_dot_nt(kblk, qh[h][lane_off(n):]) for h in range(2)]

    def near_softplus(zt, n):
        width = SB_TQ - lane_off(n)
        sp = _softplus2(zt)
        lb = zt - sp
        if n < nsub:
            kr = lax.broadcasted_iota(jnp.int32, (BLOCK, width), 0)
            qc = lax.broadcasted_iota(jnp.int32, (BLOCK, width), 1)
            before = kr < qc
            sp = jnp.where(before, sp, 0.0)
            lb = jnp.where(before, lb, MASKED)
        sp_hi = sp.astype(BF16)
        sp_lo = (sp - sp_hi.astype(F32)).astype(BF16)
        colsum = jnp.sum(sp, axis=0, keepdims=True)
        if lane_off(n):
            colsum = jnp.concatenate([jnp.zeros((1, lane_off(n)), F32), colsum], axis=1)
        return lb, jnp.concatenate([sp_hi, sp_lo], axis=0), colsum

    def near_values(n, ws):
        vt2 = vt_ref[:, pl.ds(key_off(n), BLOCK)]
        if n >= nsub:
            vt2 = jnp.where(n < nblk, vt2, jnp.zeros_like(vt2))
        for h in range(2):
            cols = slice(h * SB_TQ + lane_off(n), (h + 1) * SB_TQ)
            acc_sc[:, cols] += _dot(vt2[h * HEAD_DIM:(h + 1) * HEAD_DIM], ws[h])

    acc_sc[...] = jnp.zeros_like(acc_sc)
    zs = {0: near_logits(0), 1: near_logits(1)}
    soft = {0: [near_softplus(zs[0][h], 0) for h in range(2)]}
    excl = {0: [_dot(ucat, soft[0][h][1]) for h in range(2)]}
    cars = [jnp.zeros((1, SB_TQ), F32) for _ in range(2)]
    ws = None
    for n in range(SB_NEAR):
        if n >= 1:
            near_values(n - 1, ws)
        if n + 2 < SB_NEAR:
            zs[n + 2] = near_logits(n + 2)
        if n + 1 < SB_NEAR:
            soft[n + 1] = [near_softplus(zs[n + 1][h], n + 1) for h in range(2)]
            excl[n + 1] = [_dot(ucat, soft[n + 1][h][1]) for h in range(2)]
        ws = [jnp.exp2(soft[n][h][0] - excl[n][h] - cars[h][:, lane_off(n):]).astype(BF16)
              for h in range(2)]
        cars = [cars[h] + soft[n][h][2] for h in range(2)]
    near_values(SB_NEAR - 1, ws)
    car = jnp.concatenate(cars, axis=1)

    @pl.when(jnp.logical_and(nblk > SB_NEAR, jnp.min(car) < SB_DEAD))
    def _():
        lb0, stacked0, colsum0 = softplus_stage(logits(SB_NEAR))
        lb_sc[...] = lb0
        e_sc[...] = _dot(ucat, stacked0)
        z_sc[...] = logits(SB_NEAR + 1)
        w_sc[...] = jnp.zeros_like(w_sc)

        def live(c):
            n, _, _, alive = c
            return jnp.logical_and(n < nblk, alive > 0)

        def visit(c):
            n, car, colsum, _ = c
            alive = (jnp.min(car) < SB_DEAD).astype(jnp.int32)
            car, colsum = step(n, (car, colsum))
            return n + jnp.int32(1), car, colsum, alive

        n_stop = lax.while_loop(live, visit, (jnp.int32(SB_NEAR), car, colsum0, jnp.int32(1)))[0]
        value_stage(n_stop - 1)

    acc = acc_sc[...]
    out_t = jnp.concatenate([acc[:, :SB_TQ], acc[:, SB_TQ:]], axis=0)
    o_ref[...] = out_t.T.astype(o_ref.dtype)


def _sb_call(qc, kc, vt):
    bsz, s, _ = qc.shape
    return pl.pallas_call(
        _sb_kernel,
        grid=(bsz, C_WIDTH // LANES, s // SB_TQ),
        in_specs=[pl.BlockSpec((None, SB_TQ, LANES), lambda b, hp, i: (b, i, hp)),
                  pl.BlockSpec((None, s, LANES), lambda b, hp, i: (b, 0, hp)),
                  pl.BlockSpec((None, LANES, s), lambda b, hp, i: (b, hp, 0))],
        out_specs=pl.BlockSpec((None, SB_TQ, LANES), lambda b, hp, i: (b, i, hp)),
        out_shape=jax.ShapeDtypeStruct((bsz, s, C_WIDTH), BF16),
        scratch_shapes=[pltpu.VMEM((BLOCK, 2 * SB_TQ), F32),
                        pltpu.VMEM((BLOCK, 2 * SB_TQ), F32),
                        pltpu.VMEM((BLOCK, 2 * SB_TQ), F32),
                        pltpu.VMEM((BLOCK, 2 * SB_TQ), BF16),
                        pltpu.VMEM((HEAD_DIM, 2 * SB_TQ), F32)],
        compiler_params=_params("parallel", "parallel", "parallel"),
        name="stickbreak_attn",
    )(qc, kc, vt)


def _outproj_kernel(h_ref, oa_ref, ob_ref, oc_ref, wa_ref, wb_ref, wc_ref, g_ref, b_ref,
                    rw_ref, rb_ref, o_ref, gate_ref):
    m = (_dot(oa_ref[...], wa_ref[...]) + _dot(ob_ref[...], wb_ref[...])
         + _dot(oc_ref[...], wc_ref[...]))
    h1 = _layer_norm(ALPHA * h_ref[...] + m, g_ref[...], b_ref[...])
    o_ref[...] = h1
    logits = _dot3(h1, rw_ref[...])
    gate_ref[...] = _route(logits.T[:rb_ref.shape[0]], rb_ref[...])


def _outproj_call(h, oa, ob, oc, wa, wb, wc, g, b, rw, rb_col, tm=512):
    bsz, s, d = h.shape
    nt = s // tm
    rows = rb_col.shape[0]
    row = lambda c: pl.BlockSpec((None, tm, c), lambda bb, t: (bb, t, 0))
    full = lambda w: pl.BlockSpec(w.shape, lambda bb, t: (0, 0))
    g2, b2 = g.reshape(1, d), b.reshape(1, d)
    return pl.pallas_call(
        _outproj_kernel,
        grid=(bsz, nt),
        in_specs=[row(d), row(A_WIDTH), row(B_WIDTH), row(C_WIDTH),
                  full(wa), full(wb), full(wc), full(g2), full(b2), full(rw), full(rb_col)],
        out_specs=[row(d), pl.BlockSpec((rows, tm), lambda bb, t: (0, bb * nt + t))],
        out_shape=[jax.ShapeDtypeStruct((bsz, s, d), F32),
                   jax.ShapeDtypeStruct((rows, bsz * s), F32)],
        compiler_params=_params("parallel", "parallel"),
        name="out_proj_ln_router",
    )(h, oa, ob, oc, wa, wb, wc, g2, b2, rw, rb_col)


def _route(logits, rb):
    scores = _sigmoid(logits)
    sel = scores + rb
    R = ROUTER_ROWS
    s = [sel[m * R:(m + 1) * R] for m in range(EXPERTS_PER_GROUP)]
    sc = [scores[m * R:(m + 1) * R] for m in range(EXPERTS_PER_GROUP)]
    hi01, lo01 = jnp.maximum(s[0], s[1]), jnp.minimum(s[0], s[1])
    hi23, lo23 = jnp.maximum(s[2], s[3]), jnp.minimum(s[2], s[3])
    top1 = jnp.maximum(hi01, hi23)
    top2 = jnp.maximum(jnp.minimum(hi01, hi23), jnp.maximum(lo01, lo23))
    gscore = top1 + top2
    gi = lax.broadcasted_iota(jnp.int32, gscore.shape, 0)
    gmax = jnp.max(gscore, axis=0, keepdims=True)
    best = jnp.min(jnp.where(gscore == gmax, gi, R), axis=0, keepdims=True)
    in_group = gi == best
    picked = []
    for m in range(EXPERTS_PER_GROUP):
        rank = jnp.zeros(gscore.shape, jnp.int32)
        for j in range(EXPERTS_PER_GROUP):
            if j == m:
                continue
            ahead = (s[j] >= s[m]) if j < m else (s[j] > s[m])
            rank = rank + jnp.where(ahead, 1, 0)
        picked.append(jnp.where(jnp.logical_and(in_group, rank < 2), sc[m], 0.0))
    denom = jnp.sum(picked[0] + picked[1] + picked[2] + picked[3], axis=0, keepdims=True)
    return jnp.concatenate([x / denom for x in picked], axis=0)


def _moe_kernel(x_ref, gate_ref, wg_ref, wu_ref, wd_ref, g_ref, b_ref, o_ref, xb_ref, acc_ref):
    e = pl.program_id(1)

    @pl.when(e == 0)
    def _():
        xb_ref[...] = x_ref[...].astype(BF16)
        acc_ref[...] = jnp.zeros_like(acc_ref)

    xb = xb_ref[...]
    hg = _dot(xb, wg_ref[...])
    hu = _dot(xb, wu_ref[...])
    gates = gate_ref[...]
    lane = lax.broadcasted_iota(jnp.int32, gates.shape, 1)
    gcol = jnp.sum(jnp.where(lane == e, gates, 0.0), axis=-1, keepdims=True)
    act = (hg * _sigmoid(hg)) * hu * gcol
    acc_ref[...] += _dot(act.astype(BF16), wd_ref[...])

    @pl.when(e == N_EXPERTS - 1)
    def _():
        o_ref[...] = _layer_norm(ALPHA * x_ref[...] + acc_ref[...], g_ref[...], b_ref[...])


def _moe_call(x, gates, wg, wu, wd, g, b, tm=1024):
    n, d = x.shape
    f = wg.shape[-1]
    g2, b2 = g.reshape(1, d), b.reshape(1, d)
    return pl.pallas_call(
        _moe_kernel,
        grid=(n // tm, N_EXPERTS),
        in_specs=[pl.BlockSpec((tm, d), lambda i, e: (i, 0)),
                  pl.BlockSpec((tm, N_EXPERTS), lambda i, e: (i, 0)),
                  pl.BlockSpec((None, d, f), lambda i, e: (e, 0, 0)),
                  pl.BlockSpec((None, d, f), lambda i, e: (e, 0, 0)),
                  pl.BlockSpec((None, f, d), lambda i, e: (e, 0, 0)),
                  pl.BlockSpec((1, d), lambda i, e: (0, 0)),
                  pl.BlockSpec((1, d), lambda i, e: (0, 0))],
        out_specs=pl.BlockSpec((tm, d), lambda i, e: (i, 0)),
        out_shape=jax.ShapeDtypeStruct((n, d), F32),
        scratch_shapes=[pltpu.VMEM((tm, d), BF16), pltpu.VMEM((tm, d), F32)],
        compiler_params=_params("parallel", "arbitrary"),
        name="moe_experts_ln",
    )(x, gates, wg, wu, wd, g2, b2)


def _pair_heads(x, axis):
    shape = x.shape
    x = x.reshape(shape[:axis] + (A_KV_HEADS, A_Q_HEADS // A_KV_HEADS, HEAD_DIM) + shape[axis + 1:])
    x = jnp.swapaxes(x, axis, axis + 1)
    return x.reshape(shape)


def _router_layout(router_w, router_bias):
    d = router_w.shape[0]
    w = router_w.astype(F32).T.reshape(N_GROUPS, EXPERTS_PER_GROUP, d).transpose(1, 0, 2)
    w = jnp.pad(w, ((0, 0), (0, ROUTER_ROWS - N_GROUPS), (0, 0)))
    b = router_bias.astype(F32).reshape(N_GROUPS, EXPERTS_PER_GROUP).T
    b = jnp.pad(b, ((0, 0), (0, ROUTER_ROWS - N_GROUPS)), constant_values=MASKED)
    rows = EXPERTS_PER_GROUP * ROUTER_ROWS
    w = jnp.pad(w.reshape(rows, d).T, ((0, 0), (0, LANES - rows)))
    return w, b.reshape(rows, 1)


def _gates_from_router(gates_t):
    n = gates_t.shape[1]
    g = gates_t.reshape(EXPERTS_PER_GROUP, ROUTER_ROWS, n)[:, :N_GROUPS]
    return g.transpose(2, 1, 0).reshape(n, N_EXPERTS)


def kernel(x, ln0_g, ln0_b, w_in, w_out, sinks, rel_bias, shift_mu, decay_w0, decay_up, iclr_a0,
           iclr_up, gate_up, k_k, k_a, r_k, lnx_g, lnx_b, ln1_g, ln1_b, router_w, router_bias,
           w_gate, w_up, w_down, ln2_g, ln2_b):
    bsz, s, d = x.shape
    n = bsz * s
    bias_pairs = _swa_bias_pairs(rel_bias)
    rw, rb_col = _router_layout(router_w, router_bias)
    zeros_lora = jnp.zeros((ICLR_RANK, B_WIDTH), F32)

    h = _ln_call(x.reshape(n, d), ln0_g, ln0_b).reshape(bsz, s, d)
    for l in range(DEPTH):
        wl = w_in[l]
        wa = jnp.concatenate([_pair_heads(wl[:, :A_WIDTH], 1), wl[:, A_WIDTH:A_COLS]], axis=1)
        wb = wl[:, A_COLS:A_COLS + B_COLS]
        c0 = A_COLS + B_COLS
        wq, wk, wv = (wl[:, c0 + j * C_WIDTH:c0 + (j + 1) * C_WIDTH] for j in range(3))
        pa, pb, qc, kc, vt = _proj_call(h, wa.astype(BF16), wb.astype(BF16), wq.astype(BF16),
                                        wk.astype(BF16), wv.T.astype(BF16))

        out_a = _swa_call(pa, sinks[l].astype(F32), bias_pairs)
        wd_pad = jnp.concatenate([decay_up[l].astype(F32), zeros_lora], axis=0).astype(BF16)
        wa_pad = jnp.concatenate([zeros_lora, iclr_up[l].astype(F32)], axis=0).astype(BF16)
        out_b = _rwkv_call(pb, shift_mu[l], decay_w0[l], wd_pad, iclr_a0[l], wa_pad,
                           gate_up[l].astype(BF16), k_k[l], k_a[l], r_k[l], lnx_g[l], lnx_b[l])
        out_c = _sb_call(qc, kc, vt)

        wo = w_out[l]
        h, gates_t = _outproj_call(h, out_a, out_b, out_c,
                                   _pair_heads(wo[:A_WIDTH], 0).astype(BF16),
                                   wo[A_WIDTH:A_WIDTH + B_WIDTH].astype(BF16),
                                   wo[A_WIDTH + B_WIDTH:].astype(BF16), ln1_g[l], ln1_b[l],
                                   rw, rb_col)

        hf = h.reshape(n, d)
        gates = _gates_from_router(gates_t)
        hf = _moe_call(hf, gates, w_gate[l].astype(BF16), w_up[l].astype(BF16),
                       w_down[l].astype(BF16), ln2_g[l], ln2_b[l])
        h = hf.reshape(bsz, s, d)
    return h
```

```python
import functools
import math

import jax
import jax.numpy as jnp
from jax import lax
from jax.experimental import pallas as pl
from jax.experimental.pallas import tpu as pltpu

F32 = jnp.float32
BF16 = jnp.bfloat16
HI = lax.Precision.HIGHEST

DEPTH = 2
HEAD_DIM = 64
BLOCK = 128
LANES = 128
A_Q_HEADS = 6
A_KV_HEADS = 2
WINDOW = 128
A_WIDTH = A_Q_HEADS * HEAD_DIM
A_KV_WIDTH = A_KV_HEADS * HEAD_DIM
B_HEADS = 4
B_WIDTH = B_HEADS * HEAD_DIM
DECAY_RANK = 64
ICLR_RANK = 64
GATE_RANK = 128
GN_EPS = 64e-5
C_HEADS = 6
C_WIDTH = C_HEADS * HEAD_DIM
A_COLS = A_WIDTH + 2 * A_KV_WIDTH
B_COLS = 3 * B_WIDTH + DECAY_RANK + ICLR_RANK + GATE_RANK
NUM_BUCKETS = 32
MAX_EXACT = NUM_BUCKETS // 2
MAX_DISTANCE = 128
N_EXPERTS = 16
N_GROUPS = 4
EXPERTS_PER_GROUP = N_EXPERTS // N_GROUPS
D_FF_EXPERT = 256
LN_EPS = 1e-5
ALPHA = (2 * DEPTH) ** 0.25
SCALE = HEAD_DIM ** -0.5
MASKED = -1e30
SWA_QB = 2
CHUNK = 64
RWKV_TS = 256
SB_TQ = 512
SB_NEAR = 6
SB_DEAD = 151.0
LOG2E = 1.4426950408889634
ROUTER_ROWS = 8
MOE_TG = 512
MOE_SPARE_ROWS = N_GROUPS * MOE_TG

VMEM_LIMIT = 48 * 1024 * 1024


def _dot(a, b, prec=None):
    return jnp.dot(a, b, preferred_element_type=F32, precision=prec)


def _dot_nt(a, b, prec=None):
    return lax.dot_general(a, b, (((1,), (1,)), ((), ())),
                           preferred_element_type=F32, precision=prec)


def _sigmoid(x):
    return 1.0 / (1.0 + jnp.exp(-x))


def _softplus(x):
    return jnp.maximum(x, 0.0) + jnp.log(1.0 + jnp.exp(-jnp.abs(x)))


def _layer_norm(x, g, b):
    mu = jnp.mean(x, axis=-1, keepdims=True)
    xc = x - mu
    var = jnp.mean(xc * xc, axis=-1, keepdims=True)
    return xc * lax.rsqrt(var + LN_EPS) * g + b


def _params(*sem):
    return pltpu.CompilerParams(dimension_semantics=sem, vmem_limit_bytes=VMEM_LIMIT)


def _ln_kernel(x_ref, g_ref, b_ref, o_ref):
    o_ref[...] = _layer_norm(x_ref[...], g_ref[...], b_ref[...])


def _ln_call(x, g, b, tm=512):
    n, d = x.shape
    return pl.pallas_call(
        _ln_kernel,
        grid=(n // tm,),
        in_specs=[pl.BlockSpec((tm, d), lambda i: (i, 0)),
                  pl.BlockSpec((1, d), lambda i: (0, 0)),
                  pl.BlockSpec((1, d), lambda i: (0, 0))],
        out_specs=pl.BlockSpec((tm, d), lambda i: (i, 0)),
        out_shape=jax.ShapeDtypeStruct((n, d), F32),
        compiler_params=_params("parallel"),
        name="embed_ln",
    )(x, g.reshape(1, d), b.reshape(1, d))


def _proj_kernel(h_ref, wa_ref, wb_ref, wq_ref, wk_ref, wvt_ref,
                 pa_ref, pb_ref, qc_ref, kc_ref, vt_ref):
    hb = h_ref[...].astype(BF16)
    pa_ref[...] = _dot(hb, wa_ref[...]).astype(BF16)
    pb_ref[...] = _dot(hb, wb_ref[...])
    qc_ref[...] = (_dot(hb, wq_ref[...]) * (SCALE * LOG2E)).astype(BF16)
    kc_ref[...] = _dot(hb, wk_ref[...]).astype(BF16)
    vt_ref[...] = _dot_nt(wvt_ref[...], hb).astype(BF16)


def _proj_call(h, wa, wb, wq, wk, wvt, tm=512):
    bsz, s, d = h.shape
    full = lambda w: pl.BlockSpec(w.shape, lambda b, t: (0, 0))
    row = lambda c: pl.BlockSpec((None, tm, c), lambda b, t: (b, t, 0))
    return pl.pallas_call(
        _proj_kernel,
        grid=(bsz, s // tm),
        in_specs=[row(d), full(wa), full(wb), full(wq), full(wk), full(wvt)],
        out_specs=[row(A_COLS), row(B_COLS), row(C_WIDTH), row(C_WIDTH),
                   pl.BlockSpec((None, C_WIDTH, tm), lambda b, t: (b, 0, t))],
        out_shape=[jax.ShapeDtypeStruct((bsz, s, A_COLS), BF16),
                   jax.ShapeDtypeStruct((bsz, s, B_COLS), F32),
                   jax.ShapeDtypeStruct((bsz, s, C_WIDTH), BF16),
                   jax.ShapeDtypeStruct((bsz, s, C_WIDTH), BF16),
                   jax.ShapeDtypeStruct((bsz, C_WIDTH, s), BF16)],
        compiler_params=_params("parallel", "parallel"),
        name="in_proj",
    )(h, wa, wb, wq, wk, wvt)


def _swa_kernel(sink_ref, q_ref, kp_ref, kc_ref, vp_ref, vc_ref, bias_ref, o_ref):
    n = pl.program_id(1)
    kall = jnp.concatenate([kp_ref[...], kc_ref[...]], axis=0)
    vall = jnp.concatenate([vp_ref[...], vc_ref[...]], axis=0)
    lane = lax.broadcasted_iota(jnp.int32, (BLOCK, LANES), 1)
    row2 = lax.broadcasted_iota(jnp.int32, (2 * BLOCK, 1), 0)
    col2 = lax.broadcasted_iota(jnp.int32, (1, 2 * BLOCK), 1)
    pad = jnp.where(jnp.logical_and(n == 0, col2 < BLOCK), MASKED, 0.0)
    units = [(j, c) for j in range(SWA_QB) for c in range(A_Q_HEADS // 2)]
    logits = []
    for j, c in units:
        q2 = q_ref[j * BLOCK:(j + 1) * BLOCK, c * LANES:(c + 1) * LANES]
        zero = jnp.zeros_like(q2)
        qs = jnp.concatenate([jnp.where(lane < HEAD_DIM, q2, zero),
                              jnp.where(lane >= HEAD_DIM, q2, zero)], axis=0)
        x = _dot_nt(qs, kall[j * BLOCK:(j + 2) * BLOCK]) * SCALE + bias_ref[c]
        logits.append(x + pad if j == 0 else x)
    probs, denoms = [], []
    for (j, c), x in zip(units, logits):
        sink = jnp.where(row2 < BLOCK, sink_ref[c], sink_ref[c + 3])
        m = jnp.maximum(jnp.max(x, axis=-1, keepdims=True), sink)
        p = jnp.exp(x - m)
        denoms.append(jnp.sum(p, axis=-1, keepdims=True) + jnp.exp(sink - m))
        probs.append(p.astype(BF16))
    for (j, c), p, denom in zip(units, probs, denoms):
        o = _dot(p, vall[j * BLOCK:(j + 2) * BLOCK]) / denom
        o_ref[j * BLOCK:(j + 1) * BLOCK, c * LANES:(c + 1) * LANES] = jnp.where(
            lane < HEAD_DIM, o[:BLOCK], o[BLOCK:]).astype(BF16)


def _swa_call(pa, sinks, bias_pairs):
    bsz, s, _ = pa.shape
    tq = SWA_QB * BLOCK
    kcol = A_WIDTH // LANES
    vcol = kcol + 1
    prev = lambda n: jnp.maximum(n * SWA_QB - 1, 0)
    return pl.pallas_call(
        _swa_kernel,
        grid=(bsz, s // tq),
        in_specs=[pl.BlockSpec(memory_space=pltpu.SMEM),
                  pl.BlockSpec((None, tq, A_WIDTH), lambda b, n: (b, n, 0)),
                  pl.BlockSpec((None, BLOCK, LANES), lambda b, n: (b, prev(n), kcol)),
                  pl.BlockSpec((None, tq, LANES), lambda b, n: (b, n, kcol)),
                  pl.BlockSpec((None, BLOCK, LANES), lambda b, n: (b, prev(n), vcol)),
                  pl.BlockSpec((None, tq, LANES), lambda b, n: (b, n, vcol)),
                  pl.BlockSpec(bias_pairs.shape, lambda b, n: (0, 0, 0))],
        out_specs=pl.BlockSpec((None, tq, A_WIDTH), lambda b, n: (b, n, 0)),
        out_shape=jax.ShapeDtypeStruct((bsz, s, A_WIDTH), BF16),
        compiler_params=_params("parallel", "parallel"),
        name="swa_attn",
    )(sinks, pa, pa, pa, pa, pa, bias_pairs)


def _t5_causal_bucket(dist):
    dist = jnp.maximum(dist, 0)
    d = jnp.maximum(dist, 1).astype(F32)
    large = MAX_EXACT + (jnp.log(d / MAX_EXACT) / math.log(MAX_DISTANCE / MAX_EXACT)
                         * (NUM_BUCKETS - MAX_EXACT)).astype(jnp.int32)
    large = jnp.minimum(large, NUM_BUCKETS - 1)
    return jnp.where(dist < MAX_EXACT, dist, large)


def _swa_bias_pairs(rel_bias):
    qi = jnp.arange(BLOCK)[:, None]
    ki = jnp.arange(2 * BLOCK)[None, :]
    dist = qi + BLOCK - ki
    in_window = (dist >= 0) & (dist < WINDOW)
    onehot = jax.nn.one_hot(_t5_causal_bucket(dist), NUM_BUCKETS, dtype=F32)
    bias = jnp.einsum("qkb,bh->qkh", onehot, rel_bias.astype(F32), precision=HI)
    bias = jnp.where(in_window[..., None], bias, MASKED).transpose(2, 0, 1)
    return jnp.stack([jnp.concatenate([bias[c], bias[c + 3]], axis=0)
                      for c in range(A_Q_HEADS // 2)])


def _split2(x):
    hi = x.astype(BF16)
    return hi, (x - hi.astype(F32)).astype(BF16)


def _dot3(a, b, nt=False):
    ah, al = _split2(a)
    bh, bl = _split2(b)
    d = _dot_nt if nt else _dot
    return d(ah, bh) + d(ah, bl) + d(al, bh)


def _dot3_many(pairs, nt=False):
    parts = [(_split2(a), _split2(b)) for a, b in pairs]
    d = _dot_nt if nt else _dot
    return [d(ah, bh) + d(ah, bl) + d(al, bh) for (ah, al), (bh, bl) in parts]


def _dot_x2(a, b_exact):
    ah, al = _split2(a)
    return _dot(ah, b_exact) + _dot(al, b_exact)


def _dot_2x(a_exact, b):
    bh, bl = _split2(b)
    return _dot(a_exact, bh) + _dot(a_exact, bl)


def _rwkv_kernel(pb_ref, mu_ref, w0_ref, wd_ref, a0_ref, wa_ref, wg_ref, kk_ref, ka_ref,
                 rk_ref, lng_ref, lnb_ref, o_ref, prev_ref, h_ref):
    t = pl.program_id(1)

    @pl.when(t == 0)
    def _():
        prev_ref[...] = jnp.zeros_like(prev_ref)
        h_ref[...] = jnp.zeros_like(h_ref)

    L = CHUNK
    W = B_WIDTH
    TS = RWKV_TS
    p = pb_ref[...]
    rows = lax.broadcasted_iota(jnp.int32, (TS, 1), 0)
    shifted = jnp.where(rows == 0, prev_ref[...], pltpu.roll(p, 1, axis=0))
    prev_ref[...] = p[TS - 1:TS, :]
    pm = p + (shifted - p) * mu_ref[...]
    r = pm[:, 0:W]
    k = pm[:, W:2 * W]
    v = pm[:, 2 * W:3 * W]
    xwa = pm[:, 3 * W:3 * W + DECAY_RANK + ICLR_RANK]
    xg = pm[:, 3 * W + DECAY_RANK + ICLR_RANK:]

    dw = w0_ref[...] + _dot(jnp.tanh(xwa).astype(BF16), wd_ref[...])
    lw = -jnp.exp(-_softplus(-dw) - 0.5)
    a = _sigmoid(a0_ref[...] + _dot(xwa.astype(BF16), wa_ref[...]))
    g = _dot(_sigmoid(xg).astype(BF16), wg_ref[...])

    hr = lax.broadcasted_iota(jnp.int32, (W, W), 0) // HEAD_DIM
    hc = lax.broadcasted_iota(jnp.int32, (W, W), 1) // HEAD_DIM
    same_head = hr == hc
    diag_w = (lax.broadcasted_iota(jnp.int32, (W, W), 0)
              == lax.broadcasted_iota(jnp.int32, (W, W), 1))
    head_ones = jnp.where(same_head, 1.0, 0.0).astype(BF16)
    kk = k * kk_ref[...]
    kk = kk * lax.rsqrt(jnp.maximum(_dot_x2(kk * kk, head_ones), 1e-24))
    k2 = k * (1.0 + (a - 1.0) * ka_ref[...])
    bonus = _dot_x2(r * k2 * rk_ref[...], head_ones) * v
    aa = -kk
    bb = kk * a

    ti = lax.broadcasted_iota(jnp.int32, (L, L), 0)
    tj = lax.broadcasted_iota(jnp.int32, (L, L), 1)
    lower = jnp.where(ti >= tj, 1.0, 0.0).astype(BF16)
    eye = jnp.where(ti == tj, 1.0, 0.0)
    lane_head = lax.broadcasted_iota(jnp.int32, (L, W), 1) // HEAD_DIM

    def only(x, h):
        return jnp.where(lane_head == h, x, 0.0)

    nch = TS // L
    chunks = range(nch)
    units = [(c, h) for c in chunks for h in range(B_HEADS)]
    hsl = [slice(h * L, (h + 1) * L) for h in range(B_HEADS)]
    csl = [slice(c * L, (c + 1) * L) for c in chunks]
    v_c = [v[s] for s in csl]
    cum = [_dot_2x(lower, lw[s]) for s in csl]
    cum_l = [x[L - 1:L, :] for x in cum]
    at = [aa[csl[c]] * jnp.exp(cum[c] - lw[csl[c]]) for c in chunks]
    rt = [r[csl[c]] * jnp.exp(cum[c]) for c in chunks]
    inv = [jnp.exp(-x) for x in cum]
    bt = [bb[csl[c]] * inv[c] for c in chunks]
    kt = [k2[csl[c]] * inv[c] for c in chunks]
    tail = [jnp.exp(cum_l[c] - cum[c]) for c in chunks]
    bh = [bb[csl[c]] * tail[c] for c in chunks]
    kh = [k2[csl[c]] * tail[c] for c in chunks]

    at_s = [jnp.concatenate([only(x, h) for h in range(B_HEADS)], axis=0) for x in at]
    rt_s = [jnp.concatenate([only(x, h) for h in range(B_HEADS)], axis=0) for x in rt]
    ab = _dot3_many([(at_s[c], bt[c]) for c in chunks], nt=True)
    ak = _dot3_many([(at_s[c], kt[c]) for c in chunks], nt=True)
    rb = _dot3_many([(rt_s[c], bt[c]) for c in chunks], nt=True)
    rk = _dot3_many([(rt_s[c], kt[c]) for c in chunks], nt=True)

    pw = [jnp.where(ti > tj, ab[c][hsl[h]], 0.0) for c, h in units]
    tinv = [eye + x for x in pw]
    for _ in range(int(math.log2(L)) - 1):
        pw = _dot3_many([(x, x) for x in pw])
        tinv = [t + d for t, d in zip(tinv, _dot3_many(list(zip(tinv, pw))))]
    ak_v = _dot3_many([(jnp.where(ti > tj, ak[c][hsl[h]], 0.0), v_c[c]) for c, h in units])
    w_u = _dot3_many([(tinv[i], only(at[c], h)) for i, (c, h) in enumerate(units)])
    u0_u = [only(x, h) for x, (c, h) in zip(_dot3_many(list(zip(tinv, ak_v))), units)]
    rb_l = [jnp.where(ti >= tj, rb[c][hsl[h]], 0.0) for c, h in units]
    rk_l = [jnp.where(ti >= tj, rk[c][hsl[h]], 0.0) for c, h in units]
    qm_u = _dot3_many(list(zip(rb_l, w_u)))
    y0_u = [only(p + q, h) for p, q, (c, h) in zip(
        _dot3_many(list(zip(rb_l, u0_u))),
        _dot3_many([(rk_l[i], v_c[c]) for i, (c, h) in enumerate(units)]), units)]

    def chunk_sum(xs, c):
        return functools.reduce(lambda p, q: p + q, xs[c * B_HEADS:(c + 1) * B_HEADS])

    w_sum = [chunk_sum(w_u, c) for c in chunks]
    u0 = [chunk_sum(u0_u, c) for c in chunks]
    qm = [rt[c] + chunk_sum(qm_u, c) for c in chunks]
    y0 = [chunk_sum(y0_u, c) for c in chunks]
    bw = _dot3_many([(bh[c].T, w_sum[c]) for c in chunks])
    g_mat = [jnp.where(same_head, bw[c], 0.0) + jnp.where(diag_w, jnp.exp(cum_l[c]).T, 0.0)
             for c in chunks]
    c_mat = [jnp.where(same_head, x, 0.0) for x in _dot3_many(
        [(jnp.concatenate([bh[c], kh[c]], axis=0).T, jnp.concatenate([u0[c], v_c[c]], axis=0))
         for c in chunks])]

    hst = h_ref[...]
    ys = []
    for c in chunks:
        ys.append(_dot3(qm[c], hst) + y0[c])
        hst = _dot3(g_mat[c], hst) + c_mat[c]
    h_ref[...] = hst
    y = jnp.concatenate(ys, axis=0)

    mean = _dot_x2(y, head_ones) * (1.0 / HEAD_DIM)
    yc = y - mean
    var = _dot_x2(yc * yc, head_ones) * (1.0 / HEAD_DIM)
    yn = yc * lax.rsqrt(var + GN_EPS) * lng_ref[...] + lnb_ref[...]
    o_ref[...] = ((yn + bonus) * g).astype(o_ref.dtype)


def _rwkv_call(pb, mu, w0, wd_pad, a0, wa_pad, wg, k_k, k_a, r_k, lnx_g, lnx_b):
    bsz, s, _ = pb.shape
    vec = lambda x: x.reshape(1, -1).astype(F32)
    small = [vec(mu), vec(w0), wd_pad, vec(a0), wa_pad, wg, vec(k_k), vec(k_a), vec(r_k),
             vec(lnx_g), vec(lnx_b)]
    return pl.pallas_call(
        _rwkv_kernel,
        grid=(bsz, s // RWKV_TS),
        in_specs=[pl.BlockSpec((None, RWKV_TS, B_COLS), lambda b, t: (b, t, 0))]
        + [pl.BlockSpec(x.shape, lambda b, t: (0, 0)) for x in small],
        out_specs=pl.BlockSpec((None, RWKV_TS, B_WIDTH), lambda b, t: (b, t, 0)),
        out_shape=jax.ShapeDtypeStruct((bsz, s, B_WIDTH), BF16),
        scratch_shapes=[pltpu.VMEM((1, B_COLS), F32), pltpu.VMEM((B_WIDTH, B_WIDTH), F32)],
        compiler_params=_params("parallel", "arbitrary"),
        name="rwkv7",
    )(pb, *small)


def _softplus2(z):
    return jnp.maximum(z, 0.0) + jnp.log2(1.0 + jnp.exp2(-jnp.abs(z)))


def _sb_kernel(q_ref, k_ref, vt_ref, o_ref, z_sc, lb_sc, e_sc, w_sc, acc_sc):
    it = pl.program_id(2)
    q2 = q_ref[...]
    lane = lax.broadcasted_iota(jnp.int32, (SB_TQ, LANES), 1)
    zero = jnp.zeros_like(q2)
    qh = (jnp.where(lane < HEAD_DIM, q2, zero), jnp.where(lane >= HEAD_DIM, q2, zero))
    ur = lax.broadcasted_iota(jnp.int32, (BLOCK, 2 * BLOCK), 0)
    uc = lax.broadcasted_iota(jnp.int32, (BLOCK, 2 * BLOCK), 1) % BLOCK
    ucat = jnp.where(uc > ur, 1.0, 0.0).astype(BF16)
    nsub = SB_TQ // BLOCK
    first = it * nsub + nsub - 1
    nblk = first + 1

    def key_off(n):
        return pl.multiple_of(jnp.clip(first - n, 0, first) * BLOCK, BLOCK)

    def logits(n):
        kblk = k_ref[pl.ds(key_off(n), BLOCK), :]
        return jnp.concatenate([_dot_nt(kblk, qh[0]), _dot_nt(kblk, qh[1])], axis=1)

    def softplus_stage(zt):
        sp = _softplus2(zt)
        sp_hi = sp.astype(BF16)
        sp_lo = (sp - sp_hi.astype(F32)).astype(BF16)
        stacked = jnp.concatenate([sp_hi, sp_lo], axis=0)
        return zt - sp, stacked, jnp.sum(sp, axis=0, keepdims=True)

    def value_stage(n):
        vt2 = vt_ref[:, pl.ds(key_off(n), BLOCK)]
        w = w_sc[...]
        for h in range(2):
            cols = slice(h * SB_TQ, (h + 1) * SB_TQ)
            acc_sc[:, cols] += _dot(vt2[h * HEAD_DIM:(h + 1) * HEAD_DIM], w[:, cols])

    def step(n, carry):
        car, colsum = carry
        value_stage(n - 1)
        z_new = logits(n + 2)
        lb_new, stacked, colsum_new = softplus_stage(z_sc[...])
        e_new = _dot(ucat, stacked)
        w_sc[...] = jnp.exp2(lb_sc[...] - e_sc[...] - car).astype(BF16)
        z_sc[...] = z_new
        lb_sc[...] = lb_new
        e_sc[...] = e_new
        return car + colsum, colsum_new

    def lane_off(n):
        return max(0, nsub - 1 - n) * BLOCK

    def near_logits(n):
        kblk = k_ref[pl.ds(key_off(n), BLOCK), :]
        return [_dot_nt(kblk, qh[h][lane_off(n):]) for h in range(2)]

    def near_softplus(zt, n):
        width = SB_TQ - lane_off(n)
        sp = _softplus2(zt)
        lb = zt - sp
        if n < nsub:
            kr = lax.broadcasted_iota(jnp.int32, (BLOCK, width), 0)
            qc = lax.broadcasted_iota(jnp.int32, (BLOCK, width), 1)
            before = kr < qc
            sp = jnp.where(before, sp, 0.0)
            lb = jnp.where(before, lb, MASKED)
        sp_hi = sp.astype(BF16)
        sp_lo = (sp - sp_hi.astype(F32)).astype(BF16)
        colsum = jnp.sum(sp, axis=0, keepdims=True)
        if lane_off(n):
            colsum = jnp.concatenate([jnp.zeros((1, lane_off(n)), F32), colsum], axis=1)
        return lb, jnp.concatenate([sp_hi, sp_lo], axis=0), colsum

    def near_values(n, ws):
        vt2 = vt_ref[:, pl.ds(key_off(n), BLOCK)]
        if n >= nsub:
            vt2 = jnp.where(n < nblk, vt2, jnp.zeros_like(vt2))
        for h in range(2):
            cols = slice(h * SB_TQ + lane_off(n), (h + 1) * SB_TQ)
            acc_sc[:, cols] += _dot(vt2[h * HEAD_DIM:(h + 1) * HEAD_DIM], ws[h])

    acc_sc[...] = jnp.zeros_like(acc_sc)
    zs = {0: near_logits(0), 1: near_logits(1)}
    soft = {0: [near_softplus(zs[0][h], 0) for h in range(2)]}
    excl = {0: [_dot(ucat, soft[0][h][1]) for h in range(2)]}
    cars = [jnp.zeros((1, SB_TQ), F32) for _ in range(2)]
    ws = None
    for n in range(SB_NEAR):
        if n >= 1:
            near_values(n - 1, ws)
        if n + 2 < SB_NEAR:
            zs[n + 2] = near_logits(n + 2)
        if n + 1 < SB_NEAR:
            soft[n + 1] = [near_softplus(zs[n + 1][h], n + 1) for h in range(2)]
            excl[n + 1] = [_dot(ucat, soft[n + 1][h][1]) for h in range(2)]
        ws = [jnp.exp2(soft[n][h][0] - excl[n][h] - cars[h][:, lane_off(n):]).astype(BF16)
              for h in range(2)]
        cars = [cars[h] + soft[n][h][2] for h in range(2)]
    near_values(SB_NEAR - 1, ws)
    car = jnp.concatenate(cars, axis=1)

    @pl.when(jnp.logical_and(nblk > SB_NEAR, jnp.min(car) < SB_DEAD))
    def _():
        lb0, stacked0, colsum0 = softplus_stage(logits(SB_NEAR))
        lb_sc[...] = lb0
        e_sc[...] = _dot(ucat, stacked0)
        z_sc[...] = logits(SB_NEAR + 1)
        w_sc[...] = jnp.zeros_like(w_sc)

        def live(c):
            n, _, _, alive = c
            return jnp.logical_and(n < nblk, alive > 0)

        def visit(c):
            n, car, colsum, _ = c
            alive = (jnp.min(car) < SB_DEAD).astype(jnp.int32)
            car, colsum = step(n, (car, colsum))
            return n + jnp.int32(1), car, colsum, alive

        n_stop = lax.while_loop(live, visit, (jnp.int32(SB_NEAR), car, colsum0, jnp.int32(1)))[0]
        value_stage(n_stop - 1)

    acc = acc_sc[...]
    out_t = jnp.concatenate([acc[:, :SB_TQ], acc[:, SB_TQ:]], axis=0)
    o_ref[...] = out_t.T.astype(o_ref.dtype)


def _sb_call(qc, kc, vt):
    bsz, s, _ = qc.shape
    return pl.pallas_call(
        _sb_kernel,
        grid=(bsz, C_WIDTH // LANES, s // SB_TQ),
        in_specs=[pl.BlockSpec((None, SB_TQ, LANES), lambda b, hp, i: (b, i, hp)),
                  pl.BlockSpec((None, s, LANES), lambda b, hp, i: (b, 0, hp)),
                  pl.BlockSpec((None, LANES, s), lambda b, hp, i: (b, hp, 0))],
        out_specs=pl.BlockSpec((None, SB_TQ, LANES), lambda b, hp, i: (b, i, hp)),
        out_shape=jax.ShapeDtypeStruct((bsz, s, C_WIDTH), BF16),
        scratch_shapes=[pltpu.VMEM((BLOCK, 2 * SB_TQ), F32),
                        pltpu.VMEM((BLOCK, 2 * SB_TQ), F32),
                        pltpu.VMEM((BLOCK, 2 * SB_TQ), F32),
                        pltpu.VMEM((BLOCK, 2 * SB_TQ), BF16),
                        pltpu.VMEM((HEAD_DIM, 2 * SB_TQ), F32)],
        compiler_params=_params("parallel", "parallel", "parallel"),
        name="stickbreak_attn",
    )(qc, kc, vt)


def _outproj_kernel(h_ref, oa_ref, ob_ref, oc_ref, wa_ref, wb_ref, wc_ref, g_ref, b_ref,
                    rw_ref, rb_ref, o_ref, gate_ref):
    m = (_dot(oa_ref[...], wa_ref[...]) + _dot(ob_ref[...], wb_ref[...])
         + _dot(oc_ref[...], wc_ref[...]))
    h1 = _layer_norm(ALPHA * h_ref[...] + m, g_ref[...], b_ref[...])
    logits = _dot3(h1, rw_ref[...])
    rows = rb_ref.shape[0]
    gates = _route(logits.T[:rows], rb_ref[...])
    gate_ref[...] = gates
    d = h1.shape[1]
    o_ref[:, :d] = h1
    o_ref[:, d:] = jnp.concatenate([gates, jnp.zeros((LANES - rows, gates.shape[1]), F32)],
                                   axis=0).T


def _outproj_call(h, oa, ob, oc, wa, wb, wc, g, b, rw, rb_col, tm=512):
    bsz, s, d = h.shape
    nt = s // tm
    rows = rb_col.shape[0]
    row = lambda c: pl.BlockSpec((None, tm, c), lambda bb, t: (bb, t, 0))
    full = lambda w: pl.BlockSpec(w.shape, lambda bb, t: (0, 0))
    g2, b2 = g.reshape(1, d), b.reshape(1, d)
    return pl.pallas_call(
        _outproj_kernel,
        grid=(bsz, nt),
        in_specs=[row(d), row(A_WIDTH), row(B_WIDTH), row(C_WIDTH),
                  full(wa), full(wb), full(wc), full(g2), full(b2), full(rw), full(rb_col)],
        out_specs=[row(d + LANES), pl.BlockSpec((rows, tm), lambda bb, t: (0, bb * nt + t))],
        out_shape=[jax.ShapeDtypeStruct((bsz, s, d + LANES), F32),
                   jax.ShapeDtypeStruct((rows, bsz * s), F32)],
        compiler_params=_params("parallel", "parallel"),
        name="out_proj_ln_router",
    )(h, oa, ob, oc, wa, wb, wc, g2, b2, rw, rb_col)


def _route(logits, rb):
    scores = _sigmoid(logits)
    sel = scores + rb
    R = ROUTER_ROWS
    s = [sel[m * R:(m + 1) * R] for m in range(EXPERTS_PER_GROUP)]
    sc = [scores[m * R:(m + 1) * R] for m in range(EXPERTS_PER_GROUP)]
    hi01, lo01 = jnp.maximum(s[0], s[1]), jnp.minimum(s[0], s[1])
    hi23, lo23 = jnp.maximum(s[2], s[3]), jnp.minimum(s[2], s[3])
    top1 = jnp.maximum(hi01, hi23)
    top2 = jnp.maximum(jnp.minimum(hi01, hi23), jnp.maximum(lo01, lo23))
    gscore = top1 + top2
    gi = lax.broadcasted_iota(jnp.int32, gscore.shape, 0)
    gmax = jnp.max(gscore, axis=0, keepdims=True)
    best = jnp.min(jnp.where(gscore == gmax, gi, R), axis=0, keepdims=True)
    in_group = gi == best
    picked = []
    for m in range(EXPERTS_PER_GROUP):
        rank = jnp.zeros(gscore.shape, jnp.int32)
        for j in range(EXPERTS_PER_GROUP):
            if j == m:
                continue
            ahead = (s[j] >= s[m]) if j < m else (s[j] > s[m])
            rank = rank + jnp.where(ahead, 1, 0)
        picked.append(jnp.where(jnp.logical_and(in_group, rank < 2), sc[m], 0.0))
    denom = jnp.sum(picked[0] + picked[1] + picked[2] + picked[3], axis=0, keepdims=True)
    return jnp.concatenate([x / denom for x in picked], axis=0)


def _moe_group_kernel(tg_ref, src_ref, dst_ref, x_hbm, wg_ref, wu_ref, wd_ref, y_hbm,
                      xbuf, ybuf, gsem, ssem):
    i = pl.program_id(0)
    last = pl.num_programs(0) - 1
    slot = i % 2
    other = 1 - slot
    d = ybuf.shape[2]

    def gather_row(tile, buf_slot, r):
        return pltpu.make_async_copy(x_hbm.at[pl.ds(src_ref[tile * MOE_TG + r], 1)],
                                     xbuf.at[buf_slot, pl.ds(r, 1)], gsem.at[buf_slot])

    def scatter_row(tile, buf_slot, r):
        return pltpu.make_async_copy(ybuf.at[buf_slot, pl.ds(r, 1)],
                                     y_hbm.at[pl.ds(dst_ref[tile * MOE_TG + r], 1)],
                                     ssem.at[buf_slot])

    def for_rows(fn):
        lax.fori_loop(0, MOE_TG, lambda r, c: (fn(r), c)[1], 0, unroll=8)

    @pl.when(i == 0)
    def _():
        ybuf[...] = jnp.zeros_like(ybuf)
        for_rows(lambda r: gather_row(0, 0, r).start())

    for_rows(lambda r: gather_row(i, slot, r).wait())

    @pl.when(i > 0)
    def _():
        for_rows(lambda r: scatter_row(0, slot, r).wait())

    prev_tile = jnp.maximum(i - 1, 0)
    next_tile = jnp.minimum(i + 1, last)
    pieces = 2 * EXPERTS_PER_GROUP
    per_piece = MOE_TG // pieces

    def issue(piece):
        for r in range(piece * per_piece, (piece + 1) * per_piece):
            scatter_row(prev_tile, other, r).start()
            gather_row(next_tile, other, r).start()

    x = xbuf[slot]
    xb = x[:, :d].astype(BF16)
    gates = x[:, d:]
    lane = lax.broadcasted_iota(jnp.int32, gates.shape, 1)
    grp = tg_ref[i]
    y = jnp.zeros((MOE_TG, d), F32)
    for m in range(EXPERTS_PER_GROUP):
        issue(2 * m)
        hg = _dot(xb, wg_ref[m])
        hu = _dot(xb, wu_ref[m])
        gcol = jnp.sum(jnp.where(lane == m * ROUTER_ROWS + grp, gates, 0.0),
                       axis=-1, keepdims=True)
        act = (hg * _sigmoid(hg)) * hu * gcol
        issue(2 * m + 1)
        y = y + _dot(act.astype(BF16), wd_ref[m])
    ybuf[slot] = y

    @pl.when(i == last)
    def _():
        for_rows(lambda r: scatter_row(0, other, r).wait())
        for_rows(lambda r: scatter_row(i, slot, r).start())
        for_rows(lambda r: scatter_row(0, slot, r).wait())
        for_rows(lambda r: gather_row(0, other, r).wait())


def _moe_group_call(h_ext, tile_group, src, dst, wg, wu, wd):
    n, dx = h_ext.shape
    d = dx - LANES
    f = wg.shape[-1]
    n_tiles = tile_group.shape[0]
    per_group = lambda w: w.reshape((N_GROUPS, EXPERTS_PER_GROUP) + w.shape[1:])
    wspec = lambda a, b: pl.BlockSpec((None, EXPERTS_PER_GROUP, a, b),
                                      lambda i, tg, s_, d_: (tg[i], 0, 0, 0))
    return pl.pallas_call(
        _moe_group_kernel,
        grid_spec=pltpu.PrefetchScalarGridSpec(
            num_scalar_prefetch=3,
            grid=(n_tiles,),
            in_specs=[pl.BlockSpec(memory_space=pl.ANY), wspec(d, f), wspec(d, f), wspec(f, d)],
            out_specs=pl.BlockSpec(memory_space=pl.ANY),
            scratch_shapes=[pltpu.VMEM((2, MOE_TG, dx), F32), pltpu.VMEM((2, MOE_TG, d), F32),
                            pltpu.SemaphoreType.DMA((2,)), pltpu.SemaphoreType.DMA((2,))]),
        out_shape=jax.ShapeDtypeStruct((n + MOE_SPARE_ROWS, d), F32),
        compiler_params=_params("arbitrary"),
        name="moe_grouped_experts",
    )(tile_group, src, dst, h_ext, per_group(wg), per_group(wu), per_group(wd))


def _resid_ln_kernel(x_ref, y_ref, g_ref, b_ref, o_ref):
    o_ref[...] = _layer_norm(ALPHA * x_ref[...] + y_ref[...], g_ref[...], b_ref[...])


def _resid_ln_call(h_ext, y, g, b, tm=512):
    n = h_ext.shape[0]
    d = y.shape[1]
    return pl.pallas_call(
        _resid_ln_kernel,
        grid=(n // tm,),
        in_specs=[pl.BlockSpec((tm, d), lambda i: (i, 0)),
                  pl.BlockSpec((tm, d), lambda i: (i, 0)),
                  pl.BlockSpec((1, d), lambda i: (0, 0)),
                  pl.BlockSpec((1, d), lambda i: (0, 0))],
        out_specs=pl.BlockSpec((tm, d), lambda i: (i, 0)),
        out_shape=jax.ShapeDtypeStruct((n, d), F32),
        compiler_params=_params("parallel"),
        name="moe_resid_ln",
    )(h_ext, y, g.reshape(1, d), b.reshape(1, d))


def _moe_plan(gates_t, n):
    g = gates_t.reshape(EXPERTS_PER_GROUP, ROUTER_ROWS, n)[:, :N_GROUPS]
    grp = jnp.argmax(jnp.sum(g, axis=0), axis=0).astype(jnp.int32)
    onehot = (grp[None, :] == jnp.arange(N_GROUPS, dtype=jnp.int32)[:, None]).astype(jnp.int32)
    count = jnp.sum(onehot, axis=1)
    padded = (count + MOE_TG - 1) // MOE_TG * MOE_TG
    ends = jnp.cumsum(padded)
    starts = ends - padded
    rank = jnp.sum((jnp.cumsum(onehot, axis=1) - onehot) * onehot, axis=0)
    slot = jnp.sum(starts[:, None] * onehot, axis=0) + rank
    n_slots = n + N_GROUPS * MOE_TG
    n_tiles = n_slots // MOE_TG
    dst = jnp.full((n_slots,), n, jnp.int32).at[slot].set(jnp.arange(n, dtype=jnp.int32))
    is_pad = dst == n
    src = jnp.where(is_pad, 0, dst)
    dst = jnp.where(is_pad, n - 1 + jnp.cumsum(is_pad.astype(jnp.int32)), dst)
    tile_start = jnp.arange(n_tiles, dtype=jnp.int32) * MOE_TG
    tile_group = jnp.minimum(jnp.sum((tile_start[:, None] >= ends[None, :]).astype(jnp.int32),
                                     axis=1), N_GROUPS - 1)
    return tile_group, src, dst


def _pair_heads(x, axis):
    shape = x.shape
    x = x.reshape(shape[:axis] + (A_KV_HEADS, A_Q_HEADS // A_KV_HEADS, HEAD_DIM) + shape[axis + 1:])
    x = jnp.swapaxes(x, axis, axis + 1)
    return x.reshape(shape)


def _router_layout(router_w, router_bias):
    d = router_w.shape[0]
    w = router_w.astype(F32).T.reshape(N_GROUPS, EXPERTS_PER_GROUP, d).transpose(1, 0, 2)
    w = jnp.pad(w, ((0, 0), (0, ROUTER_ROWS - N_GROUPS), (0, 0)))
    b = router_bias.astype(F32).reshape(N_GROUPS, EXPERTS_PER_GROUP).T
    b = jnp.pad(b, ((0, 0), (0, ROUTER_ROWS - N_GROUPS)), constant_values=MASKED)
    rows = EXPERTS_PER_GROUP * ROUTER_ROWS
    w = jnp.pad(w.reshape(rows, d).T, ((0, 0), (0, LANES - rows)))
    return w, b.reshape(rows, 1)


def kernel(x, ln0_g, ln0_b, w_in, w_out, sinks, rel_bias, shift_mu, decay_w0, decay_up, iclr_a0,
           iclr_up, gate_up, k_k, k_a, r_k, lnx_g, lnx_b, ln1_g, ln1_b, router_w, router_bias,
           w_gate, w_up, w_down, ln2_g, ln2_b):
    bsz, s, d = x.shape
    n = bsz * s
    bias_pairs = _swa_bias_pairs(rel_bias)
    rw, rb_col = _router_layout(router_w, router_bias)
    zeros_lora = jnp.zeros((ICLR_RANK, B_WIDTH), F32)

    h = _ln_call(x.reshape(n, d), ln0_g, ln0_b).reshape(bsz, s, d)
    for l in range(DEPTH):
        wl = w_in[l]
        wa = jnp.concatenate([_pair_heads(wl[:, :A_WIDTH], 1), wl[:, A_WIDTH:A_COLS]], axis=1)
        wb = wl[:, A_COLS:A_COLS + B_COLS]
        c0 = A_COLS + B_COLS
        wq, wk, wv = (wl[:, c0 + j * C_WIDTH:c0 + (j + 1) * C_WIDTH] for j in range(3))
        pa, pb, qc, kc, vt = _proj_call(h, wa.astype(BF16), wb.astype(BF16), wq.astype(BF16),
                                        wk.astype(BF16), wv.T.astype(BF16))

        out_a = _swa_call(pa, sinks[l].astype(F32), bias_pairs)
        wd_pad = jnp.concatenate([decay_up[l].astype(F32), zeros_lora], axis=0).astype(BF16)
        wa_pad = jnp.concatenate([zeros_lora, iclr_up[l].astype(F32)], axis=0).astype(BF16)
        out_b = _rwkv_call(pb, shift_mu[l], decay_w0[l], wd_pad, iclr_a0[l], wa_pad,
                           gate_up[l].astype(BF16), k_k[l], k_a[l], r_k[l], lnx_g[l], lnx_b[l])
        out_c = _sb_call(qc, kc, vt)

        wo = w_out[l]
        h_ext, gates_t = _outproj_call(h, out_a, out_b, out_c,
                                       _pair_heads(wo[:A_WIDTH], 0).astype(BF16),
                                       wo[A_WIDTH:A_WIDTH + B_WIDTH].astype(BF16),
                                       wo[A_WIDTH + B_WIDTH:].astype(BF16), ln1_g[l], ln1_b[l],
                                       rw, rb_col)

        h_ext = h_ext.reshape(n, d + LANES)
        y = _moe_group_call(h_ext, *_moe_plan(gates_t, n), w_gate[l].astype(BF16),
                            w_up[l].astype(BF16), w_down[l].astype(BF16))
        h = _resid_ln_call(h_ext, y, ln2_g[l], ln2_b[l]).reshape(bsz, s, d)
    return h
```

```python
import functools
import math

import jax
import jax.numpy as jnp
from jax import lax
from jax.experimental import pallas as pl
from jax.experimental.pallas import tpu as pltpu

F32 = jnp.float32
BF16 = jnp.bfloat16
HI = lax.Precision.HIGHEST

DEPTH = 2
HEAD_DIM = 64
BLOCK = 128
LANES = 128
A_Q_HEADS = 6
A_KV_HEADS = 2
WINDOW = 128
A_WIDTH = A_Q_HEADS * HEAD_DIM
A_KV_WIDTH = A_KV_HEADS * HEAD_DIM
B_HEADS = 4
B_WIDTH = B_HEADS * HEAD_DIM
DECAY_RANK = 64
ICLR_RANK = 64
GATE_RANK = 128
GN_EPS = 64e-5
C_HEADS = 6
C_WIDTH = C_HEADS * HEAD_DIM
A_COLS = A_WIDTH + 2 * A_KV_WIDTH
B_COLS = 3 * B_WIDTH + DECAY_RANK + ICLR_RANK + GATE_RANK
NUM_BUCKETS = 32
MAX_EXACT = NUM_BUCKETS // 2
MAX_DISTANCE = 128
N_EXPERTS = 16
N_GROUPS = 4
EXPERTS_PER_GROUP = N_EXPERTS // N_GROUPS
D_FF_EXPERT = 256
LN_EPS = 1e-5
ALPHA = (2 * DEPTH) ** 0.25
SCALE = HEAD_DIM ** -0.5
MASKED = -1e30
SWA_QB = 2
CHUNK = 64
RWKV_TS = 256
SB_TQ = 512
SB_NEAR = 6
SB_DEAD = 151.0
LOG2E = 1.4426950408889634
ROUTER_ROWS = 8
MOE_EPS = 4

VMEM_LIMIT = 48 * 1024 * 1024


def _dot(a, b, prec=None):
    return jnp.dot(a, b, preferred_element_type=F32, precision=prec)


def _dot_nt(a, b, prec=None):
    return lax.dot_general(a, b, (((1,), (1,)), ((), ())),
                           preferred_element_type=F32, precision=prec)


def _sigmoid(x):
    return 1.0 / (1.0 + jnp.exp(-x))


def _softplus(x):
    return jnp.maximum(x, 0.0) + jnp.log(1.0 + jnp.exp(-jnp.abs(x)))


def _layer_norm(x, g, b):
    mu = jnp.mean(x, axis=-1, keepdims=True)
    xc = x - mu
    var = jnp.mean(xc * xc, axis=-1, keepdims=True)
    return xc * lax.rsqrt(var + LN_EPS) * g + b


def _params(*sem):
    return pltpu.CompilerParams(dimension_semantics=sem, vmem_limit_bytes=VMEM_LIMIT)


def _ln_kernel(x_ref, g_ref, b_ref, o_ref):
    o_ref[...] = _layer_norm(x_ref[...], g_ref[...], b_ref[...])


def _ln_call(x, g, b, tm=512):
    n, d = x.shape
    return pl.pallas_call(
        _ln_kernel,
        grid=(n // tm,),
        in_specs=[pl.BlockSpec((tm, d), lambda i: (i, 0)),
                  pl.BlockSpec((1, d), lambda i: (0, 0)),
                  pl.BlockSpec((1, d), lambda i: (0, 0))],
        out_specs=pl.BlockSpec((tm, d), lambda i: (i, 0)),
        out_shape=jax.ShapeDtypeStruct((n, d), F32),
        compiler_params=_params("parallel"),
        name="embed_ln",
    )(x, g.reshape(1, d), b.reshape(1, d))


def _proj_kernel(h_ref, wa_ref, wb_ref, wq_ref, wk_ref, wvt_ref,
                 pa_ref, pb_ref, qc_ref, kc_ref, vt_ref):
    hb = h_ref[...].astype(BF16)
    pa_ref[...] = _dot(hb, wa_ref[...]).astype(BF16)
    pb_ref[...] = _dot(hb, wb_ref[...])
    qc_ref[...] = (_dot(hb, wq_ref[...]) * (SCALE * LOG2E)).astype(BF16)
    kc_ref[...] = _dot(hb, wk_ref[...]).astype(BF16)
    vt_ref[...] = _dot_nt(wvt_ref[...], hb).astype(BF16)


def _proj_call(h, wa, wb, wq, wk, wvt, tm=512):
    bsz, s, d = h.shape
    full = lambda w: pl.BlockSpec(w.shape, lambda b, t: (0, 0))
    row = lambda c: pl.BlockSpec((None, tm, c), lambda b, t: (b, t, 0))
    return pl.pallas_call(
        _proj_kernel,
        grid=(bsz, s // tm),
        in_specs=[row(d), full(wa), full(wb), full(wq), full(wk), full(wvt)],
        out_specs=[row(A_COLS), row(B_COLS), row(C_WIDTH), row(C_WIDTH),
                   pl.BlockSpec((None, C_WIDTH, tm), lambda b, t: (b, 0, t))],
        out_shape=[jax.ShapeDtypeStruct((bsz, s, A_COLS), BF16),
                   jax.ShapeDtypeStruct((bsz, s, B_COLS), F32),
                   jax.ShapeDtypeStruct((bsz, s, C_WIDTH), BF16),
                   jax.ShapeDtypeStruct((bsz, s, C_WIDTH), BF16),
                   jax.ShapeDtypeStruct((bsz, C_WIDTH, s), BF16)],
        compiler_params=_params("parallel", "parallel"),
        name="in_proj",
    )(h, wa, wb, wq, wk, wvt)


def _swa_kernel(sink_ref, q_ref, kp_ref, kc_ref, vp_ref, vc_ref, bias_ref, o_ref):
    n = pl.program_id(1)
    kall = jnp.concatenate([kp_ref[...], kc_ref[...]], axis=0)
    vall = jnp.concatenate([vp_ref[...], vc_ref[...]], axis=0)
    lane = lax.broadcasted_iota(jnp.int32, (BLOCK, LANES), 1)
    row2 = lax.broadcasted_iota(jnp.int32, (2 * BLOCK, 1), 0)
    col2 = lax.broadcasted_iota(jnp.int32, (1, 2 * BLOCK), 1)
    pad = jnp.where(jnp.logical_and(n == 0, col2 < BLOCK), MASKED, 0.0)
    units = [(j, c) for j in range(SWA_QB) for c in range(A_Q_HEADS // 2)]
    logits = []
    for j, c in units:
        q2 = q_ref[j * BLOCK:(j + 1) * BLOCK, c * LANES:(c + 1) * LANES]
        zero = jnp.zeros_like(q2)
        qs = jnp.concatenate([jnp.where(lane < HEAD_DIM, q2, zero),
                              jnp.where(lane >= HEAD_DIM, q2, zero)], axis=0)
        x = _dot_nt(qs, kall[j * BLOCK:(j + 2) * BLOCK]) * SCALE + bias_ref[c]
        logits.append(x + pad if j == 0 else x)
    probs, denoms = [], []
    for (j, c), x in zip(units, logits):
        sink = jnp.where(row2 < BLOCK, sink_ref[c], sink_ref[c + 3])
        m = jnp.maximum(jnp.max(x, axis=-1, keepdims=True), sink)
        p = jnp.exp(x - m)
        denoms.append(jnp.sum(p, axis=-1, keepdims=True) + jnp.exp(sink - m))
        probs.append(p.astype(BF16))
    for (j, c), p, denom in zip(units, probs, denoms):
        o = _dot(p, vall[j * BLOCK:(j + 2) * BLOCK]) / denom
        o_ref[j * BLOCK:(j + 1) * BLOCK, c * LANES:(c + 1) * LANES] = jnp.where(
            lane < HEAD_DIM, o[:BLOCK], o[BLOCK:]).astype(BF16)


def _swa_call(pa, sinks, bias_pairs):
    bsz, s, _ = pa.shape
    tq = SWA_QB * BLOCK
    kcol = A_WIDTH // LANES
    vcol = kcol + 1
    prev = lambda n: jnp.maximum(n * SWA_QB - 1, 0)
    return pl.pallas_call(
        _swa_kernel,
        grid=(bsz, s // tq),
        in_specs=[pl.BlockSpec(memory_space=pltpu.SMEM),
                  pl.BlockSpec((None, tq, A_WIDTH), lambda b, n: (b, n, 0)),
                  pl.BlockSpec((None, BLOCK, LANES), lambda b, n: (b, prev(n), kcol)),
                  pl.BlockSpec((None, tq, LANES), lambda b, n: (b, n, kcol)),
                  pl.BlockSpec((None, BLOCK, LANES), lambda b, n: (b, prev(n), vcol)),
                  pl.BlockSpec((None, tq, LANES), lambda b, n: (b, n, vcol)),
                  pl.BlockSpec(bias_pairs.shape, lambda b, n: (0, 0, 0))],
        out_specs=pl.BlockSpec((None, tq, A_WIDTH), lambda b, n: (b, n, 0)),
        out_shape=jax.ShapeDtypeStruct((bsz, s, A_WIDTH), BF16),
        compiler_params=_params("parallel", "parallel"),
        name="swa_attn",
    )(sinks, pa, pa, pa, pa, pa, bias_pairs)


def _t5_causal_bucket(dist):
    dist = jnp.maximum(dist, 0)
    d = jnp.maximum(dist, 1).astype(F32)
    large = MAX_EXACT + (jnp.log(d / MAX_EXACT) / math.log(MAX_DISTANCE / MAX_EXACT)
                         * (NUM_BUCKETS - MAX_EXACT)).astype(jnp.int32)
    large = jnp.minimum(large, NUM_BUCKETS - 1)
    return jnp.where(dist < MAX_EXACT, dist, large)


def _swa_bias_pairs(rel_bias):
    qi = jnp.arange(BLOCK)[:, None]
    ki = jnp.arange(2 * BLOCK)[None, :]
    dist = qi + BLOCK - ki
    in_window = (dist >= 0) & (dist < WINDOW)
    onehot = jax.nn.one_hot(_t5_causal_bucket(dist), NUM_BUCKETS, dtype=F32)
    bias = jnp.einsum("qkb,bh->qkh", onehot, rel_bias.astype(F32), precision=HI)
    bias = jnp.where(in_window[..., None], bias, MASKED).transpose(2, 0, 1)
    return jnp.stack([jnp.concatenate([bias[c], bias[c + 3]], axis=0)
                      for c in range(A_Q_HEADS // 2)])


def _split2(x):
    hi = x.astype(BF16)
    return hi, (x - hi.astype(F32)).astype(BF16)


def _dot3(a, b, nt=False):
    ah, al = _split2(a)
    bh, bl = _split2(b)
    d = _dot_nt if nt else _dot
    return d(ah, bh) + d(ah, bl) + d(al, bh)


def _dot3_many(pairs, nt=False):
    parts = [(_split2(a), _split2(b)) for a, b in pairs]
    d = _dot_nt if nt else _dot
    return [d(ah, bh) + d(ah, bl) + d(al, bh) for (ah, al), (bh, bl) in parts]


def _dot_x2(a, b_exact):
    ah, al = _split2(a)
    return _dot(ah, b_exact) + _dot(al, b_exact)


def _dot_2x(a_exact, b):
    bh, bl = _split2(b)
    return _dot(a_exact, bh) + _dot(a_exact, bl)


def _rwkv_kernel(pb_ref, mu_ref, w0_ref, wd_ref, a0_ref, wa_ref, wg_ref, kk_ref, ka_ref,
                 rk_ref, lng_ref, lnb_ref, o_ref, prev_ref, h_ref):
    t = pl.program_id(1)

    @pl.when(t == 0)
    def _():
        prev_ref[...] = jnp.zeros_like(prev_ref)
        h_ref[...] = jnp.zeros_like(h_ref)

    L = CHUNK
    W = B_WIDTH
    TS = RWKV_TS
    p = pb_ref[...]
    rows = lax.broadcasted_iota(jnp.int32, (TS, 1), 0)
    shifted = jnp.where(rows == 0, prev_ref[...], pltpu.roll(p, 1, axis=0))
    prev_ref[...] = p[TS - 1:TS, :]
    pm = p + (shifted - p) * mu_ref[...]
    r = pm[:, 0:W]
    k = pm[:, W:2 * W]
    v = pm[:, 2 * W:3 * W]
    xwa = pm[:, 3 * W:3 * W + DECAY_RANK + ICLR_RANK]
    xg = pm[:, 3 * W + DECAY_RANK + ICLR_RANK:]

    dw = w0_ref[...] + _dot(jnp.tanh(xwa).astype(BF16), wd_ref[...])
    lw = -jnp.exp(-_softplus(-dw) - 0.5)
    a = _sigmoid(a0_ref[...] + _dot(xwa.astype(BF16), wa_ref[...]))
    g = _dot(_sigmoid(xg).astype(BF16), wg_ref[...])

    hr = lax.broadcasted_iota(jnp.int32, (W, W), 0) // HEAD_DIM
    hc = lax.broadcasted_iota(jnp.int32, (W, W), 1) // HEAD_DIM
    same_head = hr == hc
    diag_w = (lax.broadcasted_iota(jnp.int32, (W, W), 0)
              == lax.broadcasted_iota(jnp.int32, (W, W), 1))
    head_ones = jnp.where(same_head, 1.0, 0.0).astype(BF16)
    kk = k * kk_ref[...]
    kk = kk * lax.rsqrt(jnp.maximum(_dot_x2(kk * kk, head_ones), 1e-24))
    k2 = k * (1.0 + (a - 1.0) * ka_ref[...])
    bonus = _dot_x2(r * k2 * rk_ref[...], head_ones) * v
    aa = -kk
    bb = kk * a

    ti = lax.broadcasted_iota(jnp.int32, (L, L), 0)
    tj = lax.broadcasted_iota(jnp.int32, (L, L), 1)
    lower = jnp.where(ti >= tj, 1.0, 0.0).astype(BF16)
    eye = jnp.where(ti == tj, 1.0, 0.0)
    lane_head = lax.broadcasted_iota(jnp.int32, (L, W), 1) // HEAD_DIM

    def only(x, h):
        return jnp.where(lane_head == h, x, 0.0)

    nch = TS // L
    chunks = range(nch)
    units = [(c, h) for c in chunks for h in range(B_HEADS)]
    hsl = [slice(h * L, (h + 1) * L) for h in range(B_HEADS)]
    csl = [slice(c * L, (c + 1) * L) for c in chunks]
    v_c = [v[s] for s in csl]
    cum = [_dot_2x(lower, lw[s]) for s in csl]
    cum_l = [x[L - 1:L, :] for x in cum]
    at = [aa[csl[c]] * jnp.exp(cum[c] - lw[csl[c]]) for c in chunks]
    rt = [r[csl[c]] * jnp.exp(cum[c]) for c in chunks]
    inv = [jnp.exp(-x) for x in cum]
    bt = [bb[csl[c]] * inv[c] for c in chunks]
    kt = [k2[csl[c]] * inv[c] for c in chunks]
    tail = [jnp.exp(cum_l[c] - cum[c]) for c in chunks]
    bh = [bb[csl[c]] * tail[c] for c in chunks]
    kh = [k2[csl[c]] * tail[c] for c in chunks]

    at_s = [jnp.concatenate([only(x, h) for h in range(B_HEADS)], axis=0) for x in at]
    rt_s = [jnp.concatenate([only(x, h) for h in range(B_HEADS)], axis=0) for x in rt]
    ab = _dot3_many([(at_s[c], bt[c]) for c in chunks], nt=True)
    ak = _dot3_many([(at_s[c], kt[c]) for c in chunks], nt=True)
    rb = _dot3_many([(rt_s[c], bt[c]) for c in chunks], nt=True)
    rk = _dot3_many([(rt_s[c], kt[c]) for c in chunks], nt=True)

    pw = [jnp.where(ti > tj, ab[c][hsl[h]], 0.0) for c, h in units]
    tinv = [eye + x for x in pw]
    for _ in range(int(math.log2(L)) - 1):
        pw = _dot3_many([(x, x) for x in pw])
        tinv = [t + d for t, d in zip(tinv, _dot3_many(list(zip(tinv, pw))))]
    ak_v = _dot3_many([(jnp.where(ti > tj, ak[c][hsl[h]], 0.0), v_c[c]) for c, h in units])
    w_u = _dot3_many([(tinv[i], only(at[c], h)) for i, (c, h) in enumerate(units)])
    u0_u = [only(x, h) for x, (c, h) in zip(_dot3_many(list(zip(tinv, ak_v))), units)]
    rb_l = [jnp.where(ti >= tj, rb[c][hsl[h]], 0.0) for c, h in units]
    rk_l = [jnp.where(ti >= tj, rk[c][hsl[h]], 0.0) for c, h in units]
    qm_u = _dot3_many(list(zip(rb_l, w_u)))
    y0_u = [only(p + q, h) for p, q, (c, h) in zip(
        _dot3_many(list(zip(rb_l, u0_u))),
        _dot3_many([(rk_l[i], v_c[c]) for i, (c, h) in enumerate(units)]), units)]

    def chunk_sum(xs, c):
        return functools.reduce(lambda p, q: p + q, xs[c * B_HEADS:(c + 1) * B_HEADS])

    w_sum = [chunk_sum(w_u, c) for c in chunks]
    u0 = [chunk_sum(u0_u, c) for c in chunks]
    qm = [rt[c] + chunk_sum(qm_u, c) for c in chunks]
    y0 = [chunk_sum(y0_u, c) for c in chunks]
    bw = _dot3_many([(bh[c].T, w_sum[c]) for c in chunks])
    g_mat = [jnp.where(same_head, bw[c], 0.0) + jnp.where(diag_w, jnp.exp(cum_l[c]).T, 0.0)
             for c in chunks]
    c_mat = [jnp.where(same_head, x, 0.0) for x in _dot3_many(
        [(jnp.concatenate([bh[c], kh[c]], axis=0).T, jnp.concatenate([u0[c], v_c[c]], axis=0))
         for c in chunks])]

    hst = h_ref[...]
    ys = []
    for c in chunks:
        ys.append(_dot3(qm[c], hst) + y0[c])
        hst = _dot3(g_mat[c], hst) + c_mat[c]
    h_ref[...] = hst
    y = jnp.concatenate(ys, axis=0)

    mean = _dot_x2(y, head_ones) * (1.0 / HEAD_DIM)
    yc = y - mean
    var = _dot_x2(yc * yc, head_ones) * (1.0 / HEAD_DIM)
    yn = yc * lax.rsqrt(var + GN_EPS) * lng_ref[...] + lnb_ref[...]
    o_ref[...] = ((yn + bonus) * g).astype(o_ref.dtype)


def _rwkv_call(pb, mu, w0, wd_pad, a0, wa_pad, wg, k_k, k_a, r_k, lnx_g, lnx_b):
    bsz, s, _ = pb.shape
    vec = lambda x: x.reshape(1, -1).astype(F32)
    small = [vec(mu), vec(w0), wd_pad, vec(a0), wa_pad, wg, vec(k_k), vec(k_a), vec(r_k),
             vec(lnx_g), vec(lnx_b)]
    return pl.pallas_call(
        _rwkv_kernel,
        grid=(bsz, s // RWKV_TS),
        in_specs=[pl.BlockSpec((None, RWKV_TS, B_COLS), lambda b, t: (b, t, 0))]
        + [pl.BlockSpec(x.shape, lambda b, t: (0, 0)) for x in small],
        out_specs=pl.BlockSpec((None, RWKV_TS, B_WIDTH), lambda b, t: (b, t, 0)),
        out_shape=jax.ShapeDtypeStruct((bsz, s, B_WIDTH), BF16),
        scratch_shapes=[pltpu.VMEM((1, B_COLS), F32), pltpu.VMEM((B_WIDTH, B_WIDTH), F32)],
        compiler_params=_params("parallel", "arbitrary"),
        name="rwkv7",
    )(pb, *small)


def _softplus2(z):
    return jnp.maximum(z, 0.0) + jnp.log2(1.0 + jnp.exp2(-jnp.abs(z)))


def _sb_kernel(q_ref, k_ref, vt_ref, o_ref, z_sc, lb_sc, e_sc, w_sc, acc_sc):
    it = pl.program_id(2)
    q2 = q_ref[...]
    lane = lax.broadcasted_iota(jnp.int32, (SB_TQ, LANES), 1)
    zero = jnp.zeros_like(q2)
    qh = (jnp.where(lane < HEAD_DIM, q2, zero), jnp.where(lane >= HEAD_DIM, q2, zero))
    ur = lax.broadcasted_iota(jnp.int32, (BLOCK, 2 * BLOCK), 0)
    uc = lax.broadcasted_iota(jnp.int32, (BLOCK, 2 * BLOCK), 1) % BLOCK
    ucat = jnp.where(uc > ur, 1.0, 0.0).astype(BF16)
    nsub = SB_TQ // BLOCK
    first = it * nsub + nsub - 1
    nblk = first + 1

    def key_off(n):
        return pl.multiple_of(jnp.clip(first - n, 0, first) * BLOCK, BLOCK)

    def logits(n):
        kblk = k_ref[pl.ds(key_off(n), BLOCK), :]
        return jnp.concatenate([_dot_nt(kblk, qh[0]), _dot_nt(kblk, qh[1])], axis=1)

    def softplus_stage(zt):
        sp = _softplus2(zt)
        sp_hi = sp.astype(BF16)
        sp_lo = (sp - sp_hi.astype(F32)).astype(BF16)
        stacked = jnp.concatenate([sp_hi, sp_lo], axis=0)
        return zt - sp, stacked, jnp.sum(sp, axis=0, keepdims=True)

    def value_stage(n):
        vt2 = vt_ref[:, pl.ds(key_off(n), BLOCK)]
        w = w_sc[...]
        for h in range(2):
            cols = slice(h * SB_TQ, (h + 1) * SB_TQ)
            acc_sc[:, cols] += _dot(vt2[h * HEAD_DIM:(h + 1) * HEAD_DIM], w[:, cols])

    def step(n, carry):
        car, colsum = carry
        value_stage(n - 1)
        z_new = logits(n + 2)
        lb_new, stacked, colsum_new = softplus_stage(z_sc[...])
        e_new = _dot(ucat, stacked)
        w_sc[...] = jnp.exp2(lb_sc[...] - e_sc[...] - car).astype(BF16)
        z_sc[...] = z_new
        lb_sc[...] = lb_new
        e_sc[...] = e_new
        return car + colsum, colsum_new

    def lane_off(n):
        return max(0, nsub - 1 - n) * BLOCK

    def near_logits(n):
        kblk = k_ref[pl.ds(key_off(n), BLOCK), :]
        return [_dot_nt(kblk, qh[h][lane_off(n):]) for h in range(2)]

    def near_softplus(zt, n):
        width = SB_TQ - lane_off(n)
        sp = _softplus2(zt)
        lb = zt - sp
        if n < nsub:
            kr = lax.broadcasted_iota(jnp.int32, (BLOCK, width), 0)
            qc = lax.broadcasted_iota(jnp.int32, (BLOCK, width), 1)
            before = kr < qc
            sp = jnp.where(before, sp, 0.0)
            lb = jnp.where(before, lb, MASKED)
        sp_hi = sp.astype(BF16)
        sp_lo = (sp - sp_hi.astype(F32)).astype(BF16)
        colsum = jnp.sum(sp, axis=0, keepdims=True)
        if lane_off(n):
            colsum = jnp.concatenate([jnp.zeros((1, lane_off(n)), F32), colsum], axis=1)
        return lb, jnp.concatenate([sp_hi, sp_lo], axis=0), colsum

    def near_values(n, ws):
        vt2 = vt_ref[:, pl.ds(key_off(n), BLOCK)]
        if n >= nsub:
            vt2 = jnp.where(n < nblk, vt2, jnp.zeros_like(vt2))
        for h in range(2):
            cols = slice(h * SB_TQ + lane_off(n), (h + 1) * SB_TQ)
            acc_sc[:, cols] += _dot(vt2[h * HEAD_DIM:(h + 1) * HEAD_DIM], ws[h])

    acc_sc[...] = jnp.zeros_like(acc_sc)
    zs = {0: near_logits(0), 1: near_logits(1)}
    soft = {0: [near_softplus(zs[0][h], 0) for h in range(2)]}
    excl = {0: [_dot(ucat, soft[0][h][1]) for h in range(2)]}
    cars = [jnp.zeros((1, SB_TQ), F32) for _ in range(2)]
    ws = None
    for n in range(SB_NEAR):
        if n >= 1:
            near_values(n - 1, ws)
        if n + 2 < SB_NEAR:
            zs[n + 2] = near_logits(n + 2)
        if n + 1 < SB_NEAR:
            soft[n + 1] = [near_softplus(zs[n + 1][h], n + 1) for h in range(2)]
            excl[n + 1] = [_dot(ucat, soft[n + 1][h][1]) for h in range(2)]
        ws = [jnp.exp2(soft[n][h][0] - excl[n][h] - cars[h][:, lane_off(n):]).astype(BF16)
              for h in range(2)]
        cars = [cars[h] + soft[n][h][2] for h in range(2)]
    near_values(SB_NEAR - 1, ws)
    car = jnp.concatenate(cars, axis=1)

    @pl.when(jnp.logical_and(nblk > SB_NEAR, jnp.min(car) < SB_DEAD))
    def _():
        lb0, stacked0, colsum0 = softplus_stage(logits(SB_NEAR))
        lb_sc[...] = lb0
        e_sc[...] = _dot(ucat, stacked0)
        z_sc[...] = logits(SB_NEAR + 1)
        w_sc[...] = jnp.zeros_like(w_sc)

        def live(c):
            n, _, _, alive = c
            return jnp.logical_and(n < nblk, alive > 0)

        def visit(c):
            n, car, colsum, _ = c
            alive = (jnp.min(car) < SB_DEAD).astype(jnp.int32)
            car, colsum = step(n, (car, colsum))
            return n + jnp.int32(1), car, colsum, alive

        n_stop = lax.while_loop(live, visit, (jnp.int32(SB_NEAR), car, colsum0, jnp.int32(1)))[0]
        value_stage(n_stop - 1)

    acc = acc_sc[...]
    out_t = jnp.concatenate([acc[:, :SB_TQ], acc[:, SB_TQ:]], axis=0)
    o_ref[...] = out_t.T.astype(o_ref.dtype)


def _sb_call(qc, kc, vt):
    bsz, s, _ = qc.shape
    return pl.pallas_call(
        _sb_kernel,
        grid=(bsz, C_WIDTH // LANES, s // SB_TQ),
        in_specs=[pl.BlockSpec((None, SB_TQ, LANES), lambda b, hp, i: (b, i, hp)),
                  pl.BlockSpec((None, s, LANES), lambda b, hp, i: (b, 0, hp)),
                  pl.BlockSpec((None, LANES, s), lambda b, hp, i: (b, hp, 0))],
        out_specs=pl.BlockSpec((None, SB_TQ, LANES), lambda b, hp, i: (b, i, hp)),
        out_shape=jax.ShapeDtypeStruct((bsz, s, C_WIDTH), BF16),
        scratch_shapes=[pltpu.VMEM((BLOCK, 2 * SB_TQ), F32),
                        pltpu.VMEM((BLOCK, 2 * SB_TQ), F32),
                        pltpu.VMEM((BLOCK, 2 * SB_TQ), F32),
                        pltpu.VMEM((BLOCK, 2 * SB_TQ), BF16),
                        pltpu.VMEM((HEAD_DIM, 2 * SB_TQ), F32)],
        compiler_params=_params("parallel", "parallel", "parallel"),
        name="stickbreak_attn",
    )(qc, kc, vt)


def _outproj_kernel(h_ref, oa_ref, ob_ref, oc_ref, wa_ref, wb_ref, wc_ref, g_ref, b_ref,
                    rw_ref, rb_ref, o_ref, gate_ref):
    m = (_dot(oa_ref[...], wa_ref[...]) + _dot(ob_ref[...], wb_ref[...])
         + _dot(oc_ref[...], wc_ref[...]))
    h1 = _layer_norm(ALPHA * h_ref[...] + m, g_ref[...], b_ref[...])
    o_ref[...] = h1
    logits = _dot3(h1, rw_ref[...])
    gate_ref[...] = _route(logits.T[:rb_ref.shape[0]], rb_ref[...])


def _outproj_call(h, oa, ob, oc, wa, wb, wc, g, b, rw, rb_col, tm=512):
    bsz, s, d = h.shape
    nt = s // tm
    rows = rb_col.shape[0]
    row = lambda c: pl.BlockSpec((None, tm, c), lambda bb, t: (bb, t, 0))
    full = lambda w: pl.BlockSpec(w.shape, lambda bb, t: (0, 0))
    g2, b2 = g.reshape(1, d), b.reshape(1, d)
    return pl.pallas_call(
        _outproj_kernel,
        grid=(bsz, nt),
        in_specs=[row(d), row(A_WIDTH), row(B_WIDTH), row(C_WIDTH),
                  full(wa), full(wb), full(wc), full(g2), full(b2), full(rw), full(rb_col)],
        out_specs=[row(d), pl.BlockSpec((rows, tm), lambda bb, t: (0, bb * nt + t))],
        out_shape=[jax.ShapeDtypeStruct((bsz, s, d), F32),
                   jax.ShapeDtypeStruct((rows, bsz * s), F32)],
        compiler_params=_params("parallel", "parallel"),
        name="out_proj_ln_router",
    )(h, oa, ob, oc, wa, wb, wc, g2, b2, rw, rb_col)


def _route(logits, rb):
    scores = _sigmoid(logits)
    sel = scores + rb
    R = ROUTER_ROWS
    s = [sel[m * R:(m + 1) * R] for m in range(EXPERTS_PER_GROUP)]
    sc = [scores[m * R:(m + 1) * R] for m in range(EXPERTS_PER_GROUP)]
    hi01, lo01 = jnp.maximum(s[0], s[1]), jnp.minimum(s[0], s[1])
    hi23, lo23 = jnp.maximum(s[2], s[3]), jnp.minimum(s[2], s[3])
    top1 = jnp.maximum(hi01, hi23)
    top2 = jnp.maximum(jnp.minimum(hi01, hi23), jnp.maximum(lo01, lo23))
    gscore = top1 + top2
    gi = lax.broadcasted_iota(jnp.int32, gscore.shape, 0)
    gmax = jnp.max(gscore, axis=0, keepdims=True)
    best = jnp.min(jnp.where(gscore == gmax, gi, R), axis=0, keepdims=True)
    in_group = gi == best
    picked = []
    for m in range(EXPERTS_PER_GROUP):
        rank = jnp.zeros(gscore.shape, jnp.int32)
        for j in range(EXPERTS_PER_GROUP):
            if j == m:
                continue
            ahead = (s[j] >= s[m]) if j < m else (s[j] > s[m])
            rank = rank + jnp.where(ahead, 1, 0)
        picked.append(jnp.where(jnp.logical_and(in_group, rank < 2), sc[m], 0.0))
    denom = jnp.sum(picked[0] + picked[1] + picked[2] + picked[3], axis=0, keepdims=True)
    return jnp.concatenate([x / denom for x in picked], axis=0)


def _moe_kernel(x_ref, gate_ref, wgu_ref, wd_ref, g_ref, b_ref, o_ref, xb_ref, acc_ref):
    j = pl.program_id(1)

    @pl.when(j == 0)
    def _():
        xb_ref[...] = x_ref[...].astype(BF16)
        acc_ref[...] = jnp.zeros_like(acc_ref)

    xb = xb_ref[...]
    gates = gate_ref[...]
    lane = lax.broadcasted_iota(jnp.int32, gates.shape, 1)
    f = wd_ref.shape[1]
    y = None
    for m in range(MOE_EPS):
        h2 = _dot(xb, wgu_ref[m])
        hg, hu = h2[:, :f], h2[:, f:]
        gcol = jnp.sum(jnp.where(lane == j * MOE_EPS + m, gates, 0.0), axis=-1, keepdims=True)
        act = (hg * _sigmoid(hg)) * hu * gcol
        part = _dot(act.astype(BF16), wd_ref[m])
        y = part if y is None else y + part
    acc_ref[...] += y

    @pl.when(j == pl.num_programs(1) - 1)
    def _():
        o_ref[...] = _layer_norm(ALPHA * x_ref[...] + acc_ref[...], g_ref[...], b_ref[...])


def _moe_call(x, gates, wgu, wd, g, b, tm=1024):
    n, d = x.shape
    f2 = wgu.shape[-1]
    g2, b2 = g.reshape(1, d), b.reshape(1, d)
    return pl.pallas_call(
        _moe_kernel,
        grid=(n // tm, N_EXPERTS // MOE_EPS),
        in_specs=[pl.BlockSpec((tm, d), lambda i, j: (i, 0)),
                  pl.BlockSpec((tm, N_EXPERTS), lambda i, j: (i, 0)),
                  pl.BlockSpec((MOE_EPS, d, f2), lambda i, j: (j, 0, 0)),
                  pl.BlockSpec((MOE_EPS, f2 // 2, d), lambda i, j: (j, 0, 0)),
                  pl.BlockSpec((1, d), lambda i, j: (0, 0)),
                  pl.BlockSpec((1, d), lambda i, j: (0, 0))],
        out_specs=pl.BlockSpec((tm, d), lambda i, j: (i, 0)),
        out_shape=jax.ShapeDtypeStruct((n, d), F32),
        scratch_shapes=[pltpu.VMEM((tm, d), BF16), pltpu.VMEM((tm, d), F32)],
        compiler_params=_params("parallel", "arbitrary"),
        name="moe_experts_ln",
    )(x, gates, wgu, wd, g2, b2)


def _pair_heads(x, axis):
    shape = x.shape
    x = x.reshape(shape[:axis] + (A_KV_HEADS, A_Q_HEADS // A_KV_HEADS, HEAD_DIM) + shape[axis + 1:])
    x = jnp.swapaxes(x, axis, axis + 1)
    return x.reshape(shape)


def _router_layout(router_w, router_bias):
    d = router_w.shape[0]
    w = router_w.astype(F32).T.reshape(N_GROUPS, EXPERTS_PER_GROUP, d).transpose(1, 0, 2)
    w = jnp.pad(w, ((0, 0), (0, ROUTER_ROWS - N_GROUPS), (0, 0)))
    b = router_bias.astype(F32).reshape(N_GROUPS, EXPERTS_PER_GROUP).T
    b = jnp.pad(b, ((0, 0), (0, ROUTER_ROWS - N_GROUPS)), constant_values=MASKED)
    rows = EXPERTS_PER_GROUP * ROUTER_ROWS
    w = jnp.pad(w.reshape(rows, d).T, ((0, 0), (0, LANES - rows)))
    return w, b.reshape(rows, 1)


def _gates_from_router(gates_t):
    n = gates_t.shape[1]
    g = gates_t.reshape(EXPERTS_PER_GROUP, ROUTER_ROWS, n)[:, :N_GROUPS]
    return g.transpose(2, 1, 0).reshape(n, N_EXPERTS)


def kernel(x, ln0_g, ln0_b, w_in, w_out, sinks, rel_bias, shift_mu, decay_w0, decay_up, iclr_a0,
           iclr_up, gate_up, k_k, k_a, r_k, lnx_g, lnx_b, ln1_g, ln1_b, router_w, router_bias,
           w_gate, w_up, w_down, ln2_g, ln2_b):
    bsz, s, d = x.shape
    n = bsz * s
    bias_pairs = _swa_bias_pairs(rel_bias)
    rw, rb_col = _router_layout(router_w, router_bias)
    zeros_lora = jnp.zeros((ICLR_RANK, B_WIDTH), F32)

    h = _ln_call(x.reshape(n, d), ln0_g, ln0_b).reshape(bsz, s, d)
    for l in range(DEPTH):
        wl = w_in[l]
        wa = jnp.concatenate([_pair_heads(wl[:, :A_WIDTH], 1), wl[:, A_WIDTH:A_COLS]], axis=1)
        wb = wl[:, A_COLS:A_COLS + B_COLS]
        c0 = A_COLS + B_COLS
        wq, wk, wv = (wl[:, c0 + j * C_WIDTH:c0 + (j + 1) * C_WIDTH] for j in range(3))
        pa, pb, qc, kc, vt = _proj_call(h, wa.astype(BF16), wb.astype(BF16), wq.astype(BF16),
                                        wk.astype(BF16), wv.T.astype(BF16))

        out_a = _swa_call(pa, sinks[l].astype(F32), bias_pairs)
        wd_pad = jnp.concatenate([decay_up[l].astype(F32), zeros_lora], axis=0).astype(BF16)
        wa_pad = jnp.concatenate([zeros_lora, iclr_up[l].astype(F32)], axis=0).astype(BF16)
        out_b = _rwkv_call(pb, shift_mu[l], decay_w0[l], wd_pad, iclr_a0[l], wa_pad,
                           gate_up[l].astype(BF16), k_k[l], k_a[l], r_k[l], lnx_g[l], lnx_b[l])
        out_c = _sb_call(qc, kc, vt)

        wo = w_out[l]
        h, gates_t = _outproj_call(h, out_a, out_b, out_c,
                                   _pair_heads(wo[:A_WIDTH], 0).astype(BF16),
                                   wo[A_WIDTH:A_WIDTH + B_WIDTH].astype(BF16),
                                   wo[A_WIDTH + B_WIDTH:].astype(BF16), ln1_g[l], ln1_b[l],
                                   rw, rb_col)

        wgu = jnp.concatenate([w_gate[l], w_up[l]], axis=-1).astype(BF16)
        hf = _moe_call(h.reshape(n, d), _gates_from_router(gates_t), wgu,
                       w_down[l].astype(BF16), ln2_g[l], ln2_b[l])
        h = hf.reshape(bsz, s, d)
    return h
```

```python
import functools
import math

import jax
import jax.numpy as jnp
from jax import lax
from jax.experimental import pallas as pl
from jax.experimental.pallas import tpu as pltpu

F32 = jnp.float32
BF16 = jnp.bfloat16
HI = lax.Precision.HIGHEST

DEPTH = 2
HEAD_DIM = 64
BLOCK = 128
LANES = 128
A_Q_HEADS = 6
A_KV_HEADS = 2
WINDOW = 128
A_WIDTH = A_Q_HEADS * HEAD_DIM
A_KV_WIDTH = A_KV_HEADS * HEAD_DIM
B_HEADS = 4
B_WIDTH = B_HEADS * HEAD_DIM
DECAY_RANK = 64
ICLR_RANK = 64
GATE_RANK = 128
GN_EPS = 64e-5
C_HEADS = 6
C_WIDTH = C_HEADS * HEAD_DIM
A_COLS = A_WIDTH + 2 * A_KV_WIDTH
B_COLS = 3 * B_WIDTH + DECAY_RANK + ICLR_RANK + GATE_RANK
NUM_BUCKETS = 32
MAX_EXACT = NUM_BUCKETS // 2
MAX_DISTANCE = 128
N_EXPERTS = 16
N_GROUPS = 4
EXPERTS_PER_GROUP = N_EXPERTS // N_GROUPS
D_FF_EXPERT = 256
LN_EPS = 1e-5
ALPHA = (2 * DEPTH) ** 0.25
SCALE = HEAD_DIM ** -0.5
MASKED = -1e30
SWA_QB = 2
CHUNK = 64
RWKV_TS = 256
SB_TQ = 512
SB_NEAR = 6
SB_DEAD = 151.0
LOG2E = 1.4426950408889634
ROUTER_ROWS = 8
MOE_EPS = 4

VMEM_LIMIT = 48 * 1024 * 1024


def _dot(a, b, prec=None):
    return jnp.dot(a, b, preferred_element_type=F32, precision=prec)


def _dot_nt(a, b, prec=None):
    return lax.dot_general(a, b, (((1,), (1,)), ((), ())),
                           preferred_element_type=F32, precision=prec)


def _sigmoid(x):
    return 1.0 / (1.0 + jnp.exp(-x))


def _softplus(x):
    return jnp.maximum(x, 0.0) + jnp.log(1.0 + jnp.exp(-jnp.abs(x)))


def _layer_norm(x, g, b):
    mu = jnp.mean(x, axis=-1, keepdims=True)
    xc = x - mu
    var = jnp.mean(xc * xc, axis=-1, keepdims=True)
    return xc * lax.rsqrt(var + LN_EPS) * g + b


def _params(*sem):
    return pltpu.CompilerParams(dimension_semantics=sem, vmem_limit_bytes=VMEM_LIMIT)


def _proj_body(h, wa_ref, wb_ref, wq_ref, wk_ref, wvt_ref, pa_ref, pb_ref, qc_ref, kc_ref, vt_ref):
    hb = h.astype(BF16)
    pa_ref[...] = _dot(hb, wa_ref[...]).astype(BF16)
    pb_ref[...] = _dot(hb, wb_ref[...])
    qc_ref[...] = (_dot(hb, wq_ref[...]) * (SCALE * LOG2E)).astype(BF16)
    kc_ref[...] = _dot(hb, wk_ref[...]).astype(BF16)
    vt_ref[...] = _dot_nt(wvt_ref[...], hb).astype(BF16)


def _proj_kernel(h_ref, *refs):
    _proj_body(h_ref[...], *refs)


def _embed_proj_kernel(x_ref, g_ref, b_ref, *refs):
    h = _layer_norm(x_ref[...], g_ref[...], b_ref[...])
    refs[-1][...] = h
    _proj_body(h, *refs[:-1])


def _proj_call(h, weights, embed_ln=None, tm=512):
    bsz, s, d = h.shape
    full = lambda w: pl.BlockSpec(w.shape, lambda b, t: (0, 0))
    row = lambda c: pl.BlockSpec((None, tm, c), lambda b, t: (b, t, 0))
    in_specs = [row(d)] + [full(w) for w in weights]
    out_specs = [row(A_COLS), row(B_COLS), row(C_WIDTH), row(C_WIDTH),
                 pl.BlockSpec((None, C_WIDTH, tm), lambda b, t: (b, 0, t))]
    out_shape = [jax.ShapeDtypeStruct((bsz, s, A_COLS), BF16),
                 jax.ShapeDtypeStruct((bsz, s, B_COLS), F32),
                 jax.ShapeDtypeStruct((bsz, s, C_WIDTH), BF16),
                 jax.ShapeDtypeStruct((bsz, s, C_WIDTH), BF16),
                 jax.ShapeDtypeStruct((bsz, C_WIDTH, s), BF16)]
    args = [h] + list(weights)
    if embed_ln is not None:
        vecs = [v.reshape(1, d) for v in embed_ln]
        in_specs[1:1] = [full(v) for v in vecs]
        args[1:1] = vecs
        out_specs.append(row(d))
        out_shape.append(jax.ShapeDtypeStruct((bsz, s, d), F32))
    return pl.pallas_call(
        _proj_kernel if embed_ln is None else _embed_proj_kernel,
        grid=(bsz, s // tm),
        in_specs=in_specs,
        out_specs=out_specs,
        out_shape=out_shape,
        compiler_params=_params("parallel", "parallel"),
        name="in_proj" if embed_ln is None else "embed_ln_in_proj",
    )(*args)


def _swa_kernel(sink_ref, q_ref, kp_ref, kc_ref, vp_ref, vc_ref, bias_ref, o_ref):
    n = pl.program_id(1)
    kall = jnp.concatenate([kp_ref[...], kc_ref[...]], axis=0)
    vall = jnp.concatenate([vp_ref[...], vc_ref[...]], axis=0)
    lane = lax.broadcasted_iota(jnp.int32, (BLOCK, LANES), 1)
    row2 = lax.broadcasted_iota(jnp.int32, (2 * BLOCK, 1), 0)
    col2 = lax.broadcasted_iota(jnp.int32, (1, 2 * BLOCK), 1)
    pad = jnp.where(jnp.logical_and(n == 0, col2 < BLOCK), MASKED, 0.0)
    units = [(j, c) for j in range(SWA_QB) for c in range(A_Q_HEADS // 2)]
    logits = []
    for j, c in units:
        q2 = q_ref[j * BLOCK:(j + 1) * BLOCK, c * LANES:(c + 1) * LANES]
        zero = jnp.zeros_like(q2)
        qs = jnp.concatenate([jnp.where(lane < HEAD_DIM, q2, zero),
                              jnp.where(lane >= HEAD_DIM, q2, zero)], axis=0)
        x = _dot_nt(qs, kall[j * BLOCK:(j + 2) * BLOCK]) * SCALE + bias_ref[c]
        logits.append(x + pad if j == 0 else x)
    probs, denoms = [], []
    for (j, c), x in zip(units, logits):
        sink = jnp.where(row2 < BLOCK, sink_ref[c], sink_ref[c + 3])
        m = jnp.maximum(jnp.max(x, axis=-1, keepdims=True), sink)
        p = jnp.exp(x - m)
        denoms.append(jnp.sum(p, axis=-1, keepdims=True) + jnp.exp(sink - m))
        probs.append(p.astype(BF16))
    for (j, c), p, denom in zip(units, probs, denoms):
        o = _dot(p, vall[j * BLOCK:(j + 2) * BLOCK]) / denom
        o_ref[j * BLOCK:(j + 1) * BLOCK, c * LANES:(c + 1) * LANES] = jnp.where(
            lane < HEAD_DIM, o[:BLOCK], o[BLOCK:]).astype(BF16)


def _swa_call(pa, sinks, bias_pairs):
    bsz, s, _ = pa.shape
    tq = SWA_QB * BLOCK
    kcol = A_WIDTH // LANES
    vcol = kcol + 1
    prev = lambda n: jnp.maximum(n * SWA_QB - 1, 0)
    return pl.pallas_call(
        _swa_kernel,
        grid=(bsz, s // tq),
        in_specs=[pl.BlockSpec(memory_space=pltpu.SMEM),
                  pl.BlockSpec((None, tq, A_WIDTH), lambda b, n: (b, n, 0)),
                  pl.BlockSpec((None, BLOCK, LANES), lambda b, n: (b, prev(n), kcol)),
                  pl.BlockSpec((None, tq, LANES), lambda b, n: (b, n, kcol)),
                  pl.BlockSpec((None, BLOCK, LANES), lambda b, n: (b, prev(n), vcol)),
                  pl.BlockSpec((None, tq, LANES), lambda b, n: (b, n, vcol)),
                  pl.BlockSpec(bias_pairs.shape, lambda b, n: (0, 0, 0))],
        out_specs=pl.BlockSpec((None, tq, A_WIDTH), lambda b, n: (b, n, 0)),
        out_shape=jax.ShapeDtypeStruct((bsz, s, A_WIDTH), BF16),
        compiler_params=_params("parallel", "parallel"),
        name="swa_attn",
    )(sinks, pa, pa, pa, pa, pa, bias_pairs)


def _t5_causal_bucket(dist):
    dist = jnp.maximum(dist, 0)
    d = jnp.maximum(dist, 1).astype(F32)
    large = MAX_EXACT + (jnp.log(d / MAX_EXACT) / math.log(MAX_DISTANCE / MAX_EXACT)
                         * (NUM_BUCKETS - MAX_EXACT)).astype(jnp.int32)
    large = jnp.minimum(large, NUM_BUCKETS - 1)
    return jnp.where(dist < MAX_EXACT, dist, large)


def _swa_bias_pairs(rel_bias):
    qi = jnp.arange(BLOCK)[:, None]
    ki = jnp.arange(2 * BLOCK)[None, :]
    dist = qi + BLOCK - ki
    in_window = (dist >= 0) & (dist < WINDOW)
    onehot = jax.nn.one_hot(_t5_causal_bucket(dist), NUM_BUCKETS, dtype=F32)
    bias = jnp.einsum("qkb,bh->qkh", onehot, rel_bias.astype(F32), precision=HI)
    bias = jnp.where(in_window[..., None], bias, MASKED).transpose(2, 0, 1)
    return jnp.stack([jnp.concatenate([bias[c], bias[c + 3]], axis=0)
                      for c in range(A_Q_HEADS // 2)])


def _split2(x):
    hi = x.astype(BF16)
    return hi, (x - hi.astype(F32)).astype(BF16)


def _dot3(a, b, nt=False):
    ah, al = _split2(a)
    bh, bl = _split2(b)
    d = _dot_nt if nt else _dot
    return d(ah, bh) + d(ah, bl) + d(al, bh)


def _dot3_many(pairs, nt=False):
    parts = [(_split2(a), _split2(b)) for a, b in pairs]
    d = _dot_nt if nt else _dot
    return [d(ah, bh) + d(ah, bl) + d(al, bh) for (ah, al), (bh, bl) in parts]


def _dot_x2(a, b_exact):
    ah, al = _split2(a)
    return _dot(ah, b_exact) + _dot(al, b_exact)


def _dot_2x(a_exact, b):
    bh, bl = _split2(b)
    return _dot(a_exact, bh) + _dot(a_exact, bl)


def _rwkv_kernel(pb_ref, mu_ref, w0_ref, wd_ref, a0_ref, wa_ref, wg_ref, kk_ref, ka_ref,
                 rk_ref, lng_ref, lnb_ref, o_ref, prev_ref, h_ref):
    t = pl.program_id(1)

    @pl.when(t == 0)
    def _():
        prev_ref[...] = jnp.zeros_like(prev_ref)
        h_ref[...] = jnp.zeros_like(h_ref)

    L = CHUNK
    W = B_WIDTH
    TS = RWKV_TS
    p = pb_ref[...]
    rows = lax.broadcasted_iota(jnp.int32, (TS, 1), 0)
    shifted = jnp.where(rows == 0, prev_ref[...], pltpu.roll(p, 1, axis=0))
    prev_ref[...] = p[TS - 1:TS, :]
    pm = p + (shifted - p) * mu_ref[...]
    r = pm[:, 0:W]
    k = pm[:, W:2 * W]
    v = pm[:, 2 * W:3 * W]
    xwa = pm[:, 3 * W:3 * W + DECAY_RANK + ICLR_RANK]
    xg = pm[:, 3 * W + DECAY_RANK + ICLR_RANK:]

    dw = w0_ref[...] + _dot(jnp.tanh(xwa).astype(BF16), wd_ref[...])
    lw = -jnp.exp(-_softplus(-dw) - 0.5)
    a = _sigmoid(a0_ref[...] + _dot(xwa.astype(BF16), wa_ref[...]))
    g = _dot(_sigmoid(xg).astype(BF16), wg_ref[...])

    hr = lax.broadcasted_iota(jnp.int32, (W, W), 0) // HEAD_DIM
    hc = lax.broadcasted_iota(jnp.int32, (W, W), 1) // HEAD_DIM
    same_head = hr == hc
    diag_w = (lax.broadcasted_iota(jnp.int32, (W, W), 0)
              == lax.broadcasted_iota(jnp.int32, (W, W), 1))
    head_ones = jnp.where(same_head, 1.0, 0.0).astype(BF16)
    kk = k * kk_ref[...]
    kk = kk * lax.rsqrt(jnp.maximum(_dot_x2(kk * kk, head_ones), 1e-24))
    k2 = k * (1.0 + (a - 1.0) * ka_ref[...])
    bonus = _dot_x2(r * k2 * rk_ref[...], head_ones) * v
    aa = -kk
    bb = kk * a

    ti = lax.broadcasted_iota(jnp.int32, (L, L), 0)
    tj = lax.broadcasted_iota(jnp.int32, (L, L), 1)
    lower = jnp.where(ti >= tj, 1.0, 0.0).astype(BF16)
    eye = jnp.where(ti == tj, 1.0, 0.0)
    lane_head = lax.broadcasted_iota(jnp.int32, (L, W), 1) // HEAD_DIM

    def only(x, h):
        return jnp.where(lane_head == h, x, 0.0)

    nch = TS // L
    chunks = range(nch)
    units = [(c, h) for c in chunks for h in range(B_HEADS)]
    hsl = [slice(h * L, (h + 1) * L) for h in range(B_HEADS)]
    csl = [slice(c * L, (c + 1) * L) for c in chunks]
    v_c = [v[s] for s in csl]
    cum = [_dot_2x(lower, lw[s]) for s in csl]
    cum_l = [x[L - 1:L, :] for x in cum]
    at = [aa[csl[c]] * jnp.exp(cum[c] - lw[csl[c]]) for c in chunks]
    rt = [r[csl[c]] * jnp.exp(cum[c]) for c in chunks]
    inv = [jnp.exp(-x) for x in cum]
    bt = [bb[csl[c]] * inv[c] for c in chunks]
    kt = [k2[csl[c]] * inv[c] for c in chunks]
    tail = [jnp.exp(cum_l[c] - cum[c]) for c in chunks]
    bh = [bb[csl[c]] * tail[c] for c in chunks]
    kh = [k2[csl[c]] * tail[c] for c in chunks]

    at_s = [jnp.concatenate([only(x, h) for h in range(B_HEADS)], axis=0) for x in at]
    rt_s = [jnp.concatenate([only(x, h) for h in range(B_HEADS)], axis=0) for x in rt]
    ab = _dot3_many([(at_s[c], bt[c]) for c in chunks], nt=True)
    ak = _dot3_many([(at_s[c], kt[c]) for c in chunks], nt=True)
    rb = _dot3_many([(rt_s[c], bt[c]) for c in chunks], nt=True)
    rk = _dot3_many([(rt_s[c], kt[c]) for c in chunks], nt=True)

    pw = [jnp.where(ti > tj, ab[c][hsl[h]], 0.0) for c, h in units]
    tinv = [eye + x for x in pw]
    for _ in range(int(math.log2(L)) - 1):
        pw = _dot3_many([(x, x) for x in pw])
        tinv = [t + d for t, d in zip(tinv, _dot3_many(list(zip(tinv, pw))))]
    ak_v = _dot3_many([(jnp.where(ti > tj, ak[c][hsl[h]], 0.0), v_c[c]) for c, h in units])
    w_u = _dot3_many([(tinv[i], only(at[c], h)) for i, (c, h) in enumerate(units)])
    u0_u = [only(x, h) for x, (c, h) in zip(_dot3_many(list(zip(tinv, ak_v))), units)]
    rb_l = [jnp.where(ti >= tj, rb[c][hsl[h]], 0.0) for c, h in units]
    rk_l = [jnp.where(ti >= tj, rk[c][hsl[h]], 0.0) for c, h in units]
    qm_u = _dot3_many(list(zip(rb_l, w_u)))
    y0_u = [only(p + q, h) for p, q, (c, h) in zip(
        _dot3_many(list(zip(rb_l, u0_u))),
        _dot3_many([(rk_l[i], v_c[c]) for i, (c, h) in enumerate(units)]), units)]

    def chunk_sum(xs, c):
        return functools.reduce(lambda p, q: p + q, xs[c * B_HEADS:(c + 1) * B_HEADS])

    w_sum = [chunk_sum(w_u, c) for c in chunks]
    u0 = [chunk_sum(u0_u, c) for c in chunks]
    qm = [rt[c] + chunk_sum(qm_u, c) for c in chunks]
    y0 = [chunk_sum(y0_u, c) for c in chunks]
    bw = _dot3_many([(bh[c].T, w_sum[c]) for c in chunks])
    g_mat = [jnp.where(same_head, bw[c], 0.0) + jnp.where(diag_w, jnp.exp(cum_l[c]).T, 0.0)
             for c in chunks]
    c_mat = [jnp.where(same_head, x, 0.0) for x in _dot3_many(
        [(jnp.concatenate([bh[c], kh[c]], axis=0).T, jnp.concatenate([u0[c], v_c[c]], axis=0))
         for c in chunks])]

    hst = h_ref[...]
    ys = []
    for c in chunks:
        ys.append(_dot3(qm[c], hst) + y0[c])
        hst = _dot3(g_mat[c], hst) + c_mat[c]
    h_ref[...] = hst
    y = jnp.concatenate(ys, axis=0)

    mean = _dot_x2(y, head_ones) * (1.0 / HEAD_DIM)
    yc = y - mean
    var = _dot_x2(yc * yc, head_ones) * (1.0 / HEAD_DIM)
    yn = yc * lax.rsqrt(var + GN_EPS) * lng_ref[...] + lnb_ref[...]
    o_ref[...] = ((yn + bonus) * g).astype(o_ref.dtype)


def _rwkv_call(pb, mu, w0, wd_pad, a0, wa_pad, wg, k_k, k_a, r_k, lnx_g, lnx_b):
    bsz, s, _ = pb.shape
    vec = lambda x: x.reshape(1, -1).astype(F32)
    small = [vec(mu), vec(w0), wd_pad, vec(a0), wa_pad, wg, vec(k_k), vec(k_a), vec(r_k),
             vec(lnx_g), vec(lnx_b)]
    return pl.pallas_call(
        _rwkv_kernel,
        grid=(bsz, s // RWKV_TS),
        in_specs=[pl.BlockSpec((None, RWKV_TS, B_COLS), lambda b, t: (b, t, 0))]
        + [pl.BlockSpec(x.shape, lambda b, t: (0, 0)) for x in small],
        out_specs=pl.BlockSpec((None, RWKV_TS, B_WIDTH), lambda b, t: (b, t, 0)),
        out_shape=jax.ShapeDtypeStruct((bsz, s, B_WIDTH), BF16),
        scratch_shapes=[pltpu.VMEM((1, B_COLS), F32), pltpu.VMEM((B_WIDTH, B_WIDTH), F32)],
        compiler_params=_params("parallel", "arbitrary"),
        name="rwkv7",
    )(pb, *small)


def _softplus2(z):
    return jnp.maximum(z, 0.0) + jnp.log2(1.0 + jnp.exp2(-jnp.abs(z)))


def _sb_kernel(q_ref, k_ref, vt_ref, o_ref, z_sc, lb_sc, e_sc, w_sc, acc_sc):
    it = pl.program_id(2)
    q2 = q_ref[...]
    lane = lax.broadcasted_iota(jnp.int32, (SB_TQ, LANES), 1)
    zero = jnp.zeros_like(q2)
    qh = (jnp.where(lane < HEAD_DIM, q2, zero), jnp.where(lane >= HEAD_DIM, q2, zero))
    ur = lax.broadcasted_iota(jnp.int32, (BLOCK, BLOCK), 0)
    uc = lax.broadcasted_iota(jnp.int32, (BLOCK, BLOCK), 1)
    upper = jnp.where(uc > ur, 1.0, 0.0).astype(BF16)
    nsub = SB_TQ // BLOCK
    first = it * nsub + nsub - 1
    nblk = first + 1

    def key_off(n):
        return pl.multiple_of(jnp.clip(first - n, 0, first) * BLOCK, BLOCK)

    def logits(n):
        kblk = k_ref[pl.ds(key_off(n), BLOCK), :]
        return jnp.concatenate([_dot_nt(kblk, qh[0]), _dot_nt(kblk, qh[1])], axis=1)

    def softplus_stage(zt):
        sp = _softplus2(zt)
        return zt - sp, sp.astype(BF16), jnp.sum(sp, axis=0, keepdims=True)

    def value_stage(n):
        vt2 = vt_ref[:, pl.ds(key_off(n), BLOCK)]
        w = w_sc[...]
        for h in range(2):
            cols = slice(h * SB_TQ, (h + 1) * SB_TQ)
            acc_sc[:, cols] += _dot(vt2[h * HEAD_DIM:(h + 1) * HEAD_DIM], w[:, cols])

    def step(n, carry):
        car, colsum = carry
        value_stage(n - 1)
        z_new = logits(n + 2)
        lb_new, sp16, colsum_new = softplus_stage(z_sc[...])
        e_new = _dot(upper, sp16)
        w_sc[...] = jnp.exp2(lb_sc[...] - e_sc[...] - car).astype(BF16)
        z_sc[...] = z_new
        lb_sc[...] = lb_new
        e_sc[...] = e_new
        return car + colsum, colsum_new

    def lane_off(n):
        return max(0, nsub - 1 - n) * BLOCK

    def near_logits(n):
        kblk = k_ref[pl.ds(key_off(n), BLOCK), :]
        return [_dot_nt(kblk, qh[h][lane_off(n):]) for h in range(2)]

    def near_softplus(zt, n):
        width = SB_TQ - lane_off(n)
        sp = _softplus2(zt)
        lb = zt - sp
        if n < nsub:
            kr = lax.broadcasted_iota(jnp.int32, (BLOCK, width), 0)
            qc = lax.broadcasted_iota(jnp.int32, (BLOCK, width), 1)
            before = kr < qc
            sp = jnp.where(before, sp, 0.0)
            lb = jnp.where(before, lb, MASKED)
        colsum = jnp.sum(sp, axis=0, keepdims=True)
        if lane_off(n):
            colsum = jnp.concatenate([jnp.zeros((1, lane_off(n)), F32), colsum], axis=1)
        return lb, sp.astype(BF16), colsum

    def near_values(n, ws):
        vt2 = vt_ref[:, pl.ds(key_off(n), BLOCK)]
        if n >= nsub:
            vt2 = jnp.where(n < nblk, vt2, jnp.zeros_like(vt2))
        for h in range(2):
            cols = slice(h * SB_TQ + lane_off(n), (h + 1) * SB_TQ)
            acc_sc[:, cols] += _dot(vt2[h * HEAD_DIM:(h + 1) * HEAD_DIM], ws[h])

    acc_sc[...] = jnp.zeros_like(acc_sc)
    zs = {0: near_logits(0), 1: near_logits(1)}
    soft = {0: [near_softplus(zs[0][h], 0) for h in range(2)]}
    excl = {0: [_dot(upper, soft[0][h][1]) for h in range(2)]}
    cars = [jnp.zeros((1, SB_TQ), F32) for _ in range(2)]
    ws = None
    for n in range(SB_NEAR):
        if n >= 1:
            near_values(n - 1, ws)
        if n + 2 < SB_NEAR:
            zs[n + 2] = near_logits(n + 2)
        if n + 1 < SB_NEAR:
            soft[n + 1] = [near_softplus(zs[n + 1][h], n + 1) for h in range(2)]
            excl[n + 1] = [_dot(upper, soft[n + 1][h][1]) for h in range(2)]
        ws = [jnp.exp2(soft[n][h][0] - excl[n][h] - cars[h][:, lane_off(n):]).astype(BF16)
              for h in range(2)]
        cars = [cars[h] + soft[n][h][2] for h in range(2)]
    near_values(SB_NEAR - 1, ws)
    car = jnp.concatenate(cars, axis=1)

    @pl.when(jnp.logical_and(nblk > SB_NEAR, jnp.min(car) < SB_DEAD))
    def _():
        lb0, sp160, colsum0 = softplus_stage(logits(SB_NEAR))
        lb_sc[...] = lb0
        e_sc[...] = _dot(upper, sp160)
        z_sc[...] = logits(SB_NEAR + 1)
        w_sc[...] = jnp.zeros_like(w_sc)

        def live(c):
            n, _, _, alive = c
            return jnp.logical_and(n < nblk, alive > 0)

        def visit(c):
            n, car, colsum, _ = c
            alive = (jnp.min(car) < SB_DEAD).astype(jnp.int32)
            car, colsum = step(n, (car, colsum))
            return n + jnp.int32(1), car, colsum, alive

        n_stop = lax.while_loop(live, visit, (jnp.int32(SB_NEAR), car, colsum0, jnp.int32(1)))[0]
        value_stage(n_stop - 1)

    acc = acc_sc[...]
    out_t = jnp.concatenate([acc[:, :SB_TQ], acc[:, SB_TQ:]], axis=0)
    o_ref[...] = out_t.T.astype(o_ref.dtype)


def _sb_call(qc, kc, vt):
    bsz, s, _ = qc.shape
    return pl.pallas_call(
        _sb_kernel,
        grid=(bsz, C_WIDTH // LANES, s // SB_TQ),
        in_specs=[pl.BlockSpec((None, SB_TQ, LANES), lambda b, hp, i: (b, i, hp)),
                  pl.BlockSpec((None, s, LANES), lambda b, hp, i: (b, 0, hp)),
                  pl.BlockSpec((None, LANES, s), lambda b, hp, i: (b, hp, 0))],
        out_specs=pl.BlockSpec((None, SB_TQ, LANES), lambda b, hp, i: (b, i, hp)),
        out_shape=jax.ShapeDtypeStruct((bsz, s, C_WIDTH), BF16),
        scratch_shapes=[pltpu.VMEM((BLOCK, 2 * SB_TQ), F32),
                        pltpu.VMEM((BLOCK, 2 * SB_TQ), F32),
                        pltpu.VMEM((BLOCK, 2 * SB_TQ), F32),
                        pltpu.VMEM((BLOCK, 2 * SB_TQ), BF16),
                        pltpu.VMEM((HEAD_DIM, 2 * SB_TQ), F32)],
        compiler_params=_params("parallel", "parallel", "parallel"),
        name="stickbreak_attn",
    )(qc, kc, vt)


def _outproj_kernel(h_ref, oa_ref, ob_ref, oc_ref, wa_ref, wb_ref, wc_ref, g_ref, b_ref,
                    rw_ref, rb_ref, o_ref, gate_ref):
    m = (_dot(oa_ref[...], wa_ref[...]) + _dot(ob_ref[...], wb_ref[...])
         + _dot(oc_ref[...], wc_ref[...]))
    h1 = _layer_norm(ALPHA * h_ref[...] + m, g_ref[...], b_ref[...])
    o_ref[...] = h1
    logits = _dot3(h1, rw_ref[...])
    gate_ref[...] = _route(logits.T[:rb_ref.shape[0]], rb_ref[...])


def _outproj_call(h, oa, ob, oc, wa, wb, wc, g, b, rw, rb_col, tm=1024):
    bsz, s, d = h.shape
    nt = s // tm
    rows = rb_col.shape[0]
    row = lambda c: pl.BlockSpec((None, tm, c), lambda bb, t: (bb, t, 0))
    full = lambda w: pl.BlockSpec(w.shape, lambda bb, t: (0, 0))
    g2, b2 = g.reshape(1, d), b.reshape(1, d)
    return pl.pallas_call(
        _outproj_kernel,
        grid=(bsz, nt),
        in_specs=[row(d), row(A_WIDTH), row(B_WIDTH), row(C_WIDTH),
                  full(wa), full(wb), full(wc), full(g2), full(b2), full(rw), full(rb_col)],
        out_specs=[row(d), pl.BlockSpec((rows, tm), lambda bb, t: (0, bb * nt + t))],
        out_shape=[jax.ShapeDtypeStruct((bsz, s, d), F32),
                   jax.ShapeDtypeStruct((rows, bsz * s), F32)],
        compiler_params=_params("parallel", "parallel"),
        name="out_proj_ln_router",
    )(h, oa, ob, oc, wa, wb, wc, g2, b2, rw, rb_col)


def _route(logits, rb):
    scores = _sigmoid(logits)
    sel = scores + rb
    R = ROUTER_ROWS
    s = [sel[m * R:(m + 1) * R] for m in range(EXPERTS_PER_GROUP)]
    sc = [scores[m * R:(m + 1) * R] for m in range(EXPERTS_PER_GROUP)]
    hi01, lo01 = jnp.maximum(s[0], s[1]), jnp.minimum(s[0], s[1])
    hi23, lo23 = jnp.maximum(s[2], s[3]), jnp.minimum(s[2], s[3])
    top1 = jnp.maximum(hi01, hi23)
    top2 = jnp.maximum(jnp.minimum(hi01, hi23), jnp.maximum(lo01, lo23))
    gscore = top1 + top2
    gi = lax.broadcasted_iota(jnp.int32, gscore.shape, 0)
    gmax = jnp.max(gscore, axis=0, keepdims=True)
    best = jnp.min(jnp.where(gscore == gmax, gi, R), axis=0, keepdims=True)
    in_group = gi == best
    picked = []
    for m in range(EXPERTS_PER_GROUP):
        rank = jnp.zeros(gscore.shape, jnp.int32)
        for j in range(EXPERTS_PER_GROUP):
            if j == m:
                continue
            ahead = (s[j] >= s[m]) if j < m else (s[j] > s[m])
            rank = rank + jnp.where(ahead, 1, 0)
        picked.append(jnp.where(jnp.logical_and(in_group, rank < 2), sc[m], 0.0))
    denom = jnp.sum(picked[0] + picked[1] + picked[2] + picked[3], axis=0, keepdims=True)
    return jnp.concatenate([x / denom for x in picked], axis=0)


def _moe_kernel(x_ref, gate_ref, wgu_ref, wd_ref, g_ref, b_ref, o_ref, xb_ref, acc_ref):
    j = pl.program_id(1)

    @pl.when(j == 0)
    def _():
        xb_ref[...] = x_ref[...].astype(BF16)
        acc_ref[...] = jnp.zeros_like(acc_ref)

    xb = xb_ref[...]
    gates = gate_ref[...]
    lane = lax.broadcasted_iota(jnp.int32, gates.shape, 1)
    f = wd_ref.shape[1]
    y = None
    for m in range(MOE_EPS):
        h2 = _dot(xb, wgu_ref[m])
        hg, hu = h2[:, :f], h2[:, f:]
        gcol = jnp.sum(jnp.where(lane == j * MOE_EPS + m, gates, 0.0), axis=-1, keepdims=True)
        act = (hg * _sigmoid(hg)) * hu * gcol
        part = _dot(act.astype(BF16), wd_ref[m])
        y = part if y is None else y + part
    acc_ref[...] += y

    @pl.when(j == pl.num_programs(1) - 1)
    def _():
        o_ref[...] = _layer_norm(ALPHA * x_ref[...] + acc_ref[...], g_ref[...], b_ref[...])


def _moe_call(x, gates, wgu, wd, g, b, tm=1024):
    n, d = x.shape
    f2 = wgu.shape[-1]
    g2, b2 = g.reshape(1, d), b.reshape(1, d)
    return pl.pallas_call(
        _moe_kernel,
        grid=(n // tm, N_EXPERTS // MOE_EPS),
        in_specs=[pl.BlockSpec((tm, d), lambda i, j: (i, 0)),
                  pl.BlockSpec((tm, N_EXPERTS), lambda i, j: (i, 0)),
                  pl.BlockSpec((MOE_EPS, d, f2), lambda i, j: (j, 0, 0)),
                  pl.BlockSpec((MOE_EPS, f2 // 2, d), lambda i, j: (j, 0, 0)),
                  pl.BlockSpec((1, d), lambda i, j: (0, 0)),
                  pl.BlockSpec((1, d), lambda i, j: (0, 0))],
        out_specs=pl.BlockSpec((tm, d), lambda i, j: (i, 0)),
        out_shape=jax.ShapeDtypeStruct((n, d), F32),
        scratch_shapes=[pltpu.VMEM((tm, d), BF16), pltpu.VMEM((tm, d), F32)],
        compiler_params=_params("parallel", "arbitrary"),
        name="moe_experts_ln",
    )(x, gates, wgu, wd, g2, b2)


def _pair_heads(x, axis):
    shape = x.shape
    x = x.reshape(shape[:axis] + (A_KV_HEADS, A_Q_HEADS // A_KV_HEADS, HEAD_DIM) + shape[axis + 1:])
    x = jnp.swapaxes(x, axis, axis + 1)
    return x.reshape(shape)


def _router_layout(router_w, router_bias):
    d = router_w.shape[0]
    w = router_w.astype(F32).T.reshape(N_GROUPS, EXPERTS_PER_GROUP, d).transpose(1, 0, 2)
    w = jnp.pad(w, ((0, 0), (0, ROUTER_ROWS - N_GROUPS), (0, 0)))
    b = router_bias.astype(F32).reshape(N_GROUPS, EXPERTS_PER_GROUP).T
    b = jnp.pad(b, ((0, 0), (0, ROUTER_ROWS - N_GROUPS)), constant_values=MASKED)
    rows = EXPERTS_PER_GROUP * ROUTER_ROWS
    w = jnp.pad(w.reshape(rows, d).T, ((0, 0), (0, LANES - rows)))
    return w, b.reshape(rows, 1)


def _gates_from_router(gates_t):
    n = gates_t.shape[1]
    g = gates_t.reshape(EXPERTS_PER_GROUP, ROUTER_ROWS, n)[:, :N_GROUPS]
    return g.transpose(2, 1, 0).reshape(n, N_EXPERTS)


def kernel(x, ln0_g, ln0_b, w_in, w_out, sinks, rel_bias, shift_mu, decay_w0, decay_up, iclr_a0,
           iclr_up, gate_up, k_k, k_a, r_k, lnx_g, lnx_b, ln1_g, ln1_b, router_w, router_bias,
           w_gate, w_up, w_down, ln2_g, ln2_b):
    bsz, s, d = x.shape
    n = bsz * s
    bias_pairs = _swa_bias_pairs(rel_bias)
    rw, rb_col = _router_layout(router_w, router_bias)
    zeros_lora = jnp.zeros((ICLR_RANK, B_WIDTH), F32)

    h = x
    for l in range(DEPTH):
        wl = w_in[l]
        wa = jnp.concatenate([_pair_heads(wl[:, :A_WIDTH], 1), wl[:, A_WIDTH:A_COLS]], axis=1)
        wb = wl[:, A_COLS:A_COLS + B_COLS]
        c0 = A_COLS + B_COLS
        wq, wk, wv = (wl[:, c0 + j * C_WIDTH:c0 + (j + 1) * C_WIDTH] for j in range(3))
        weights = [wa.astype(BF16), wb.astype(BF16), wq.astype(BF16), wk.astype(BF16),
                   wv.T.astype(BF16)]
        if l == 0:
            pa, pb, qc, kc, vt, h = _proj_call(h, weights, embed_ln=(ln0_g, ln0_b))
        else:
            pa, pb, qc, kc, vt = _proj_call(h, weights)

        out_a = _swa_call(pa, sinks[l].astype(F32), bias_pairs)
        wd_pad = jnp.concatenate([decay_up[l].astype(F32), zeros_lora], axis=0).astype(BF16)
        wa_pad = jnp.concatenate([zeros_lora, iclr_up[l].astype(F32)], axis=0).astype(BF16)
        out_b = _rwkv_call(pb, shift_mu[l], decay_w0[l], wd_pad, iclr_a0[l], wa_pad,
                           gate_up[l].astype(BF16), k_k[l], k_a[l], r_k[l], lnx_g[l], lnx_b[l])
        out_c = _sb_call(qc, kc, vt)

        wo = w_out[l]
        h, gates_t = _outproj_call(h, out_a, out_b, out_c,
                                   _pair_heads(wo[:A_WIDTH], 0).astype(BF16),
                                   wo[A_WIDTH:A_WIDTH + B_WIDTH].astype(BF16),
                                   wo[A_WIDTH + B_WIDTH:].astype(BF16), ln1_g[l], ln1_b[l],
                                   rw, rb_col)

        wgu = jnp.concatenate([w_gate[l], w_up[l]], axis=-1).astype(BF16)
        hf = _moe_call(h.reshape(n, d), _gates_from_router(gates_t), wgu,
                       w_down[l].astype(BF16), ln2_g[l], ln2_b[l])
        h = hf.reshape(bsz, s, d)
    return h
```

```python
import functools
import math

import jax
import jax.numpy as jnp
from jax import lax
from jax.experimental import pallas as pl
from jax.experimental.pallas import tpu as pltpu

F32 = jnp.float32
BF16 = jnp.bfloat16
HI = lax.Precision.HIGHEST

DEPTH = 2
HEAD_DIM = 64
BLOCK = 128
LANES = 128
A_Q_HEADS = 6
A_KV_HEADS = 2
WINDOW = 128
A_WIDTH = A_Q_HEADS * HEAD_DIM
A_KV_WIDTH = A_KV_HEADS * HEAD_DIM
B_HEADS = 4
B_WIDTH = B_HEADS * HEAD_DIM
DECAY_RANK = 64
ICLR_RANK = 64
GATE_RANK = 128
GN_EPS = 64e-5
C_HEADS = 6
C_WIDTH = C_HEADS * HEAD_DIM
A_COLS = A_WIDTH + 2 * A_KV_WIDTH
B_COLS = 3 * B_WIDTH + DECAY_RANK + ICLR_RANK + GATE_RANK
NUM_BUCKETS = 32
MAX_EXACT = NUM_BUCKETS // 2
MAX_DISTANCE = 128
N_EXPERTS = 16
N_GROUPS = 4
EXPERTS_PER_GROUP = N_EXPERTS // N_GROUPS
D_FF_EXPERT = 256
LN_EPS = 1e-5
ALPHA = (2 * DEPTH) ** 0.25
SCALE = HEAD_DIM ** -0.5
MASKED = -1e30
SWA_QB = 2
CHUNK = 64
INV_BASE = 8
RWKV_TS = 256
SB_TQ = 512
SB_NEAR = 6
SB_DEAD = 151.0
LOG2E = 1.4426950408889634
ROUTER_ROWS = 8
MOE_EPS = 4

VMEM_LIMIT = 48 * 1024 * 1024


def _dot(a, b, prec=None):
    return jnp.dot(a, b, preferred_element_type=F32, precision=prec)


def _dot_nt(a, b, prec=None):
    return lax.dot_general(a, b, (((1,), (1,)), ((), ())),
                           preferred_element_type=F32, precision=prec)


def _sigmoid(x):
    return 1.0 / (1.0 + jnp.exp(-x))


def _softplus(x):
    return jnp.maximum(x, 0.0) + jnp.log(1.0 + jnp.exp(-jnp.abs(x)))


def _layer_norm(x, g, b):
    mu = jnp.mean(x, axis=-1, keepdims=True)
    xc = x - mu
    var = jnp.mean(xc * xc, axis=-1, keepdims=True)
    return xc * lax.rsqrt(var + LN_EPS) * g + b


def _params(*sem):
    return pltpu.CompilerParams(dimension_semantics=sem, vmem_limit_bytes=VMEM_LIMIT)


def _proj_body(h, wa_ref, wb_ref, wq_ref, wk_ref, wvt_ref, pa_ref, pb_ref, qc_ref, kc_ref, vt_ref):
    hb = h.astype(BF16)
    pa_ref[...] = _dot(hb, wa_ref[...]).astype(BF16)
    pb_ref[...] = _dot(hb, wb_ref[...])
    qc_ref[...] = (_dot(hb, wq_ref[...]) * (SCALE * LOG2E)).astype(BF16)
    kc_ref[...] = _dot(hb, wk_ref[...]).astype(BF16)
    vt_ref[...] = _dot_nt(wvt_ref[...], hb).astype(BF16)


def _proj_kernel(h_ref, *refs):
    _proj_body(h_ref[...], *refs)


def _embed_proj_kernel(x_ref, g_ref, b_ref, *refs):
    h = _layer_norm(x_ref[...], g_ref[...], b_ref[...])
    refs[-1][...] = h
    _proj_body(h, *refs[:-1])


def _proj_call(h, weights, embed_ln=None, tm=512):
    bsz, s, d = h.shape
    full = lambda w: pl.BlockSpec(w.shape, lambda b, t: (0, 0))
    row = lambda c: pl.BlockSpec((None, tm, c), lambda b, t: (b, t, 0))
    in_specs = [row(d)] + [full(w) for w in weights]
    out_specs = [row(A_COLS), row(B_COLS), row(C_WIDTH), row(C_WIDTH),
                 pl.BlockSpec((None, C_WIDTH, tm), lambda b, t: (b, 0, t))]
    out_shape = [jax.ShapeDtypeStruct((bsz, s, A_COLS), BF16),
                 jax.ShapeDtypeStruct((bsz, s, B_COLS), F32),
                 jax.ShapeDtypeStruct((bsz, s, C_WIDTH), BF16),
                 jax.ShapeDtypeStruct((bsz, s, C_WIDTH), BF16),
                 jax.ShapeDtypeStruct((bsz, C_WIDTH, s), BF16)]
    args = [h] + list(weights)
    if embed_ln is not None:
        vecs = [v.reshape(1, d) for v in embed_ln]
        in_specs[1:1] = [full(v) for v in vecs]
        args[1:1] = vecs
        out_specs.append(row(d))
        out_shape.append(jax.ShapeDtypeStruct((bsz, s, d), F32))
    return pl.pallas_call(
        _proj_kernel if embed_ln is None else _embed_proj_kernel,
        grid=(bsz, s // tm),
        in_specs=in_specs,
        out_specs=out_specs,
        out_shape=out_shape,
        compiler_params=_params("parallel", "parallel"),
        name="in_proj" if embed_ln is None else "embed_ln_in_proj",
    )(*args)


def _swa_kernel(sink_ref, q_ref, kp_ref, kc_ref, vp_ref, vc_ref, bias_ref, o_ref):
    n = pl.program_id(1)
    kall = jnp.concatenate([kp_ref[...], kc_ref[...]], axis=0)
    vall = jnp.concatenate([vp_ref[...], vc_ref[...]], axis=0)
    lane = lax.broadcasted_iota(jnp.int32, (BLOCK, LANES), 1)
    row2 = lax.broadcasted_iota(jnp.int32, (2 * BLOCK, 1), 0)
    col2 = lax.broadcasted_iota(jnp.int32, (1, 2 * BLOCK), 1)
    pad = jnp.where(jnp.logical_and(n == 0, col2 < BLOCK), MASKED, 0.0)
    units = [(j, c) for j in range(SWA_QB) for c in range(A_Q_HEADS // 2)]
    logits = []
    for j, c in units:
        q2 = q_ref[j * BLOCK:(j + 1) * BLOCK, c * LANES:(c + 1) * LANES]
        zero = jnp.zeros_like(q2)
        qs = jnp.concatenate([jnp.where(lane < HEAD_DIM, q2, zero),
                              jnp.where(lane >= HEAD_DIM, q2, zero)], axis=0)
        x = _dot_nt(qs, kall[j * BLOCK:(j + 2) * BLOCK]) * SCALE + bias_ref[c]
        logits.append(x + pad if j == 0 else x)
    probs, denoms = [], []
    for (j, c), x in zip(units, logits):
        sink = jnp.where(row2 < BLOCK, sink_ref[c], sink_ref[c + 3])
        m = jnp.maximum(jnp.max(x, axis=-1, keepdims=True), sink)
        p = jnp.exp(x - m)
        denoms.append(jnp.sum(p, axis=-1, keepdims=True) + jnp.exp(sink - m))
        probs.append(p.astype(BF16))
    for (j, c), p, denom in zip(units, probs, denoms):
        o = _dot(p, vall[j * BLOCK:(j + 2) * BLOCK]) / denom
        o_ref[j * BLOCK:(j + 1) * BLOCK, c * LANES:(c + 1) * LANES] = jnp.where(
            lane < HEAD_DIM, o[:BLOCK], o[BLOCK:]).astype(BF16)


def _swa_call(pa, sinks, bias_pairs):
    bsz, s, _ = pa.shape
    tq = SWA_QB * BLOCK
    kcol = A_WIDTH // LANES
    vcol = kcol + 1
    prev = lambda n: jnp.maximum(n * SWA_QB - 1, 0)
    return pl.pallas_call(
        _swa_kernel,
        grid=(bsz, s // tq),
        in_specs=[pl.BlockSpec(memory_space=pltpu.SMEM),
                  pl.BlockSpec((None, tq, A_WIDTH), lambda b, n: (b, n, 0)),
                  pl.BlockSpec((None, BLOCK, LANES), lambda b, n: (b, prev(n), kcol)),
                  pl.BlockSpec((None, tq, LANES), lambda b, n: (b, n, kcol)),
                  pl.BlockSpec((None, BLOCK, LANES), lambda b, n: (b, prev(n), vcol)),
                  pl.BlockSpec((None, tq, LANES), lambda b, n: (b, n, vcol)),
                  pl.BlockSpec(bias_pairs.shape, lambda b, n: (0, 0, 0))],
        out_specs=pl.BlockSpec((None, tq, A_WIDTH), lambda b, n: (b, n, 0)),
        out_shape=jax.ShapeDtypeStruct((bsz, s, A_WIDTH), BF16),
        compiler_params=_params("parallel", "parallel"),
        name="swa_attn",
    )(sinks, pa, pa, pa, pa, pa, bias_pairs)


def _t5_causal_bucket(dist):
    dist = jnp.maximum(dist, 0)
    d = jnp.maximum(dist, 1).astype(F32)
    large = MAX_EXACT + (jnp.log(d / MAX_EXACT) / math.log(MAX_DISTANCE / MAX_EXACT)
                         * (NUM_BUCKETS - MAX_EXACT)).astype(jnp.int32)
    large = jnp.minimum(large, NUM_BUCKETS - 1)
    return jnp.where(dist < MAX_EXACT, dist, large)


def _swa_bias_pairs(rel_bias):
    qi = jnp.arange(BLOCK)[:, None]
    ki = jnp.arange(2 * BLOCK)[None, :]
    dist = qi + BLOCK - ki
    in_window = (dist >= 0) & (dist < WINDOW)
    onehot = jax.nn.one_hot(_t5_causal_bucket(dist), NUM_BUCKETS, dtype=F32)
    bias = jnp.einsum("qkb,bh->qkh", onehot, rel_bias.astype(F32), precision=HI)
    bias = jnp.where(in_window[..., None], bias, MASKED).transpose(2, 0, 1)
    return jnp.stack([jnp.concatenate([bias[c], bias[c + 3]], axis=0)
                      for c in range(A_Q_HEADS // 2)])


def _split2(x):
    hi = x.astype(BF16)
    return hi, (x - hi.astype(F32)).astype(BF16)


def _dot3(a, b, nt=False):
    ah, al = _split2(a)
    bh, bl = _split2(b)
    d = _dot_nt if nt else _dot
    return d(ah, bh) + d(ah, bl) + d(al, bh)


def _dot1(a, b, nt=False):
    return (_dot_nt if nt else _dot)(a.astype(BF16), b.astype(BF16))


def _dot1_many(pairs, nt=False):
    parts = [(a.astype(BF16), b.astype(BF16)) for a, b in pairs]
    d = _dot_nt if nt else _dot
    return [d(a, b) for a, b in parts]


def _dot_x2(a, b_exact):
    ah, al = _split2(a)
    return _dot(ah, b_exact) + _dot(al, b_exact)


def _dot_2x(a_exact, b):
    bh, bl = _split2(b)
    return _dot(a_exact, bh) + _dot(a_exact, bl)


def _rwkv_kernel(pb_ref, mu_ref, w0_ref, wd_ref, a0_ref, wa_ref, wg_ref, kk_ref, ka_ref,
                 rk_ref, lng_ref, lnb_ref, o_ref, prev_ref, h_ref):
    t = pl.program_id(1)

    @pl.when(t == 0)
    def _():
        prev_ref[...] = jnp.zeros_like(prev_ref)
        h_ref[...] = jnp.zeros_like(h_ref)

    L = CHUNK
    W = B_WIDTH
    TS = RWKV_TS
    p = pb_ref[...]
    rows = lax.broadcasted_iota(jnp.int32, (TS, 1), 0)
    shifted = jnp.where(rows == 0, prev_ref[...], pltpu.roll(p, 1, axis=0))
    prev_ref[...] = p[TS - 1:TS, :]
    pm = p + (shifted - p) * mu_ref[...]
    r = pm[:, 0:W]
    k = pm[:, W:2 * W]
    v = pm[:, 2 * W:3 * W]
    xwa = pm[:, 3 * W:3 * W + DECAY_RANK + ICLR_RANK]
    xg = pm[:, 3 * W + DECAY_RANK + ICLR_RANK:]

    dw = w0_ref[...] + _dot(jnp.tanh(xwa).astype(BF16), wd_ref[...])
    lw = -jnp.exp(-_softplus(-dw) - 0.5)
    a = _sigmoid(a0_ref[...] + _dot(xwa.astype(BF16), wa_ref[...]))
    g = _dot(_sigmoid(xg).astype(BF16), wg_ref[...])

    hr = lax.broadcasted_iota(jnp.int32, (W, W), 0) // HEAD_DIM
    hc = lax.broadcasted_iota(jnp.int32, (W, W), 1) // HEAD_DIM
    same_head = hr == hc
    diag_w = (lax.broadcasted_iota(jnp.int32, (W, W), 0)
              == lax.broadcasted_iota(jnp.int32, (W, W), 1))
    head_ones = jnp.where(same_head, 1.0, 0.0).astype(BF16)
    kk = k * kk_ref[...]
    kk = kk * lax.rsqrt(jnp.maximum(_dot_x2(kk * kk, head_ones), 1e-24))
    k2 = k * (1.0 + (a - 1.0) * ka_ref[...])
    bonus = _dot_x2(r * k2 * rk_ref[...], head_ones) * v
    aa = -kk
    bb = kk * a

    ti = lax.broadcasted_iota(jnp.int32, (L, L), 0)
    tj = lax.broadcasted_iota(jnp.int32, (L, L), 1)
    lower = jnp.where(ti >= tj, 1.0, 0.0).astype(BF16)
    eye = jnp.where(ti == tj, 1.0, 0.0)
    lane_head = lax.broadcasted_iota(jnp.int32, (L, W), 1) // HEAD_DIM

    def only(x, h):
        return jnp.where(lane_head == h, x, 0.0)

    nch = TS // L
    chunks = range(nch)
    units = [(c, h) for c in chunks for h in range(B_HEADS)]
    hsl = [slice(h * L, (h + 1) * L) for h in range(B_HEADS)]
    csl = [slice(c * L, (c + 1) * L) for c in chunks]
    v_c = [v[s] for s in csl]
    cum = [_dot_2x(lower, lw[s]) for s in csl]
    cum_l = [x[L - 1:L, :] for x in cum]
    at = [aa[csl[c]] * jnp.exp(cum[c] - lw[csl[c]]) for c in chunks]
    rt = [r[csl[c]] * jnp.exp(cum[c]) for c in chunks]
    inv = [jnp.exp(-x) for x in cum]
    bt = [bb[csl[c]] * inv[c] for c in chunks]
    kt = [k2[csl[c]] * inv[c] for c in chunks]
    tail = [jnp.exp(cum_l[c] - cum[c]) for c in chunks]
    bh = [bb[csl[c]] * tail[c] for c in chunks]
    kh = [k2[csl[c]] * tail[c] for c in chunks]

    at_s = [jnp.concatenate([only(x, h) for h in range(B_HEADS)], axis=0) for x in at]
    rt_s = [jnp.concatenate([only(x, h) for h in range(B_HEADS)], axis=0) for x in rt]
    ab = _dot1_many([(at_s[c], bt[c]) for c in chunks], nt=True)
    ak = _dot1_many([(at_s[c], kt[c]) for c in chunks], nt=True)
    rb = _dot1_many([(rt_s[c], bt[c]) for c in chunks], nt=True)
    rk = _dot1_many([(rt_s[c], kt[c]) for c in chunks], nt=True)

    a_low = [jnp.where(ti > tj, ab[c][hsl[h]], 0.0) for c, h in units]
    pw = [jnp.where(ti // INV_BASE == tj // INV_BASE, x, 0.0) for x in a_low]
    tinv = [eye + x for x in pw]
    for _ in range(int(math.log2(INV_BASE)) - 1):
        pw = _dot1_many([(x, x) for x in pw])
        tinv = [t + d for t, d in zip(tinv, _dot1_many(list(zip(tinv, pw))))]
    blk = INV_BASE
    while blk < L:
        pair = jnp.logical_and(ti // (2 * blk) == tj // (2 * blk), ti // blk != tj // blk)
        right = _dot1_many([(jnp.where(pair, x, 0.0), t) for x, t in zip(a_low, tinv)])
        tinv = [t + d for t, d in zip(tinv, _dot1_many(list(zip(tinv, right))))]
        blk *= 2
    ak_v = _dot1_many([(jnp.where(ti > tj, ak[c][hsl[h]], 0.0), v_c[c]) for c, h in units])
    w_u = _dot1_many([(tinv[i], only(at[c], h)) for i, (c, h) in enumerate(units)])
    u0_u = [only(x, h) for x, (c, h) in zip(_dot1_many(list(zip(tinv, ak_v))), units)]
    rb_l = [jnp.where(ti >= tj, rb[c][hsl[h]], 0.0) for c, h in units]
    rk_l = [jnp.where(ti >= tj, rk[c][hsl[h]], 0.0) for c, h in units]
    qm_u = _dot1_many(list(zip(rb_l, w_u)))
    y0_u = [only(p + q, h) for p, q, (c, h) in zip(
        _dot1_many(list(zip(rb_l, u0_u))),
        _dot1_many([(rk_l[i], v_c[c]) for i, (c, h) in enumerate(units)]), units)]

    def chunk_sum(xs, c):
        return functools.reduce(lambda p, q: p + q, xs[c * B_HEADS:(c + 1) * B_HEADS])

    w_sum = [chunk_sum(w_u, c) for c in chunks]
    u0 = [chunk_sum(u0_u, c) for c in chunks]
    qm = [rt[c] + chunk_sum(qm_u, c) for c in chunks]
    y0 = [chunk_sum(y0_u, c) for c in chunks]
    bw = _dot1_many([(bh[c].T, w_sum[c]) for c in chunks])
    g_mat = [jnp.where(same_head, bw[c], 0.0) + jnp.where(diag_w, jnp.exp(cum_l[c]).T, 0.0)
             for c in chunks]
    c_mat = [jnp.where(same_head, x, 0.0) for x in _dot1_many(
        [(jnp.concatenate([bh[c], kh[c]], axis=0).T, jnp.concatenate([u0[c], v_c[c]], axis=0))
         for c in chunks])]

    hst = h_ref[...]
    ys = []
    for c in chunks:
        ys.append(_dot1(qm[c], hst) + y0[c])
        hst = _dot1(g_mat[c], hst) + c_mat[c]
    h_ref[...] = hst
    y = jnp.concatenate(ys, axis=0)

    mean = _dot_x2(y, head_ones) * (1.0 / HEAD_DIM)
    yc = y - mean
    var = _dot_x2(yc * yc, head_ones) * (1.0 / HEAD_DIM)
    yn = yc * lax.rsqrt(var + GN_EPS) * lng_ref[...] + lnb_ref[...]
    o_ref[...] = ((yn + bonus) * g).astype(o_ref.dtype)


def _rwkv_call(pb, mu, w0, wd_pad, a0, wa_pad, wg, k_k, k_a, r_k, lnx_g, lnx_b):
    bsz, s, _ = pb.shape
    vec = lambda x: x.reshape(1, -1).astype(F32)
    small = [vec(mu), vec(w0), wd_pad, vec(a0), wa_pad, wg, vec(k_k), vec(k_a), vec(r_k),
             vec(lnx_g), vec(lnx_b)]
    return pl.pallas_call(
        _rwkv_kernel,
        grid=(bsz, s // RWKV_TS),
        in_specs=[pl.BlockSpec((None, RWKV_TS, B_COLS), lambda b, t: (b, t, 0))]
        + [pl.BlockSpec(x.shape, lambda b, t: (0, 0)) for x in small],
        out_specs=pl.BlockSpec((None, RWKV_TS, B_WIDTH), lambda b, t: (b, t, 0)),
        out_shape=jax.ShapeDtypeStruct((bsz, s, B_WIDTH), BF16),
        scratch_shapes=[pltpu.VMEM((1, B_COLS), F32), pltpu.VMEM((B_WIDTH, B_WIDTH), F32)],
        compiler_params=_params("parallel", "arbitrary"),
        name="rwkv7",
    )(pb, *small)


def _softplus2(z):
    return jnp.maximum(z, 0.0) + jnp.log2(1.0 + jnp.exp2(-jnp.abs(z)))


def _sb_kernel(q_ref, k_ref, vt_ref, o_ref, z_sc, lb_sc, e_sc, w_sc, acc_sc):
    it = pl.program_id(2)
    q2 = q_ref[...]
    lane = lax.broadcasted_iota(jnp.int32, (SB_TQ, LANES), 1)
    zero = jnp.zeros_like(q2)
    qh = (jnp.where(lane < HEAD_DIM, q2, zero), jnp.where(lane >= HEAD_DIM, q2, zero))
    ur = lax.broadcasted_iota(jnp.int32, (BLOCK, BLOCK), 0)
    uc = lax.broadcasted_iota(jnp.int32, (BLOCK, BLOCK), 1)
    upper = jnp.where(uc > ur, 1.0, 0.0).astype(BF16)
    nsub = SB_TQ // BLOCK
    first = it * nsub + nsub - 1
    nblk = first + 1

    def key_off(n):
        return pl.multiple_of(jnp.clip(first - n, 0, first) * BLOCK, BLOCK)

    def logits(n):
        kblk = k_ref[pl.ds(key_off(n), BLOCK), :]
        return jnp.concatenate([_dot_nt(kblk, qh[0]), _dot_nt(kblk, qh[1])], axis=1)

    def softplus_stage(zt):
        sp = _softplus2(zt)
        return zt - sp, sp.astype(BF16), jnp.sum(sp, axis=0, keepdims=True)

    def value_stage(n):
        vt2 = vt_ref[:, pl.ds(key_off(n), BLOCK)]
        w = w_sc[...]
        for h in range(2):
            cols = slice(h * SB_TQ, (h + 1) * SB_TQ)
            acc_sc[:, cols] += _dot(vt2[h * HEAD_DIM:(h + 1) * HEAD_DIM], w[:, cols])

    def step(n, carry):
        car, colsum = carry
        value_stage(n - 1)
        z_new = logits(n + 2)
        lb_new, sp16, colsum_new = softplus_stage(z_sc[...])
        e_new = _dot(upper, sp16)
        w_sc[...] = jnp.exp2(lb_sc[...] - e_sc[...] - car).astype(BF16)
        z_sc[...] = z_new
        lb_sc[...] = lb_new
        e_sc[...] = e_new
        return car + colsum, colsum_new

    def lane_off(n):
        return max(0, nsub - 1 - n) * BLOCK

    def near_logits(n):
        kblk = k_ref[pl.ds(key_off(n), BLOCK), :]
        return [_dot_nt(kblk, qh[h][lane_off(n):]) for h in range(2)]

    def near_softplus(zt, n):
        width = SB_TQ - lane_off(n)
        sp = _softplus2(zt)
        lb = zt - sp
        if n < nsub:
            kr = lax.broadcasted_iota(jnp.int32, (BLOCK, width), 0)
            qc = lax.broadcasted_iota(jnp.int32, (BLOCK, width), 1)
            before = kr < qc
            sp = jnp.where(before, sp, 0.0)
            lb = jnp.where(before, lb, MASKED)
        colsum = jnp.sum(sp, axis=0, keepdims=True)
        if lane_off(n):
            colsum = jnp.concatenate([jnp.zeros((1, lane_off(n)), F32), colsum], axis=1)
        return lb, sp.astype(BF16), colsum

    def near_values(n, ws):
        vt2 = vt_ref[:, pl.ds(key_off(n), BLOCK)]
        if n >= nsub:
            vt2 = jnp.where(n < nblk, vt2, jnp.zeros_like(vt2))
        for h in range(2):
            cols = slice(h * SB_TQ + lane_off(n), (h + 1) * SB_TQ)
            acc_sc[:, cols] += _dot(vt2[h * HEAD_DIM:(h + 1) * HEAD_DIM], ws[h])

    acc_sc[...] = jnp.zeros_like(acc_sc)
    zs = {0: near_logits(0), 1: near_logits(1)}
    soft = {0: [near_softplus(zs[0][h], 0) for h in range(2)]}
    excl = {0: [_dot(upper, soft[0][h][1]) for h in range(2)]}
    cars = [jnp.zeros((1, SB_TQ), F32) for _ in range(2)]
    ws = None
    for n in range(SB_NEAR):
        if n >= 1:
            near_values(n - 1, ws)
        if n + 2 < SB_NEAR:
            zs[n + 2] = near_logits(n + 2)
        if n + 1 < SB_NEAR:
            soft[n + 1] = [near_softplus(zs[n + 1][h], n + 1) for h in range(2)]
            excl[n + 1] = [_dot(upper, soft[n + 1][h][1]) for h in range(2)]
        ws = [jnp.exp2(soft[n][h][0] - excl[n][h] - cars[h][:, lane_off(n):]).astype(BF16)
              for h in range(2)]
        cars = [cars[h] + soft[n][h][2] for h in range(2)]
    near_values(SB_NEAR - 1, ws)
    car = jnp.concatenate(cars, axis=1)

    @pl.when(jnp.logical_and(nblk > SB_NEAR, jnp.min(car) < SB_DEAD))
    def _():
        lb0, sp160, colsum0 = softplus_stage(logits(SB_NEAR))
        lb_sc[...] = lb0
        e_sc[...] = _dot(upper, sp160)
        z_sc[...] = logits(SB_NEAR + 1)
        w_sc[...] = jnp.zeros_like(w_sc)

        def live(c):
            n, _, _, alive = c
            return jnp.logical_and(n < nblk, alive > 0)

        def visit(c):
            n, car, colsum, _ = c
            alive = (jnp.min(car) < SB_DEAD).astype(jnp.int32)
            car, colsum = step(n, (car, colsum))
            return n + jnp.int32(1), car, colsum, alive

        n_stop = lax.while_loop(live, visit, (jnp.int32(SB_NEAR), car, colsum0, jnp.int32(1)))[0]
        value_stage(n_stop - 1)

    acc = acc_sc[...]
    out_t = jnp.concatenate([acc[:, :SB_TQ], acc[:, SB_TQ:]], axis=0)
    o_ref[...] = out_t.T.astype(o_ref.dtype)


def _sb_call(qc, kc, vt):
    bsz, s, _ = qc.shape
    return pl.pallas_call(
        _sb_kernel,
        grid=(bsz, C_WIDTH // LANES, s // SB_TQ),
        in_specs=[pl.BlockSpec((None, SB_TQ, LANES), lambda b, hp, i: (b, i, hp)),
                  pl.BlockSpec((None, s, LANES), lambda b, hp, i: (b, 0, hp)),
                  pl.BlockSpec((None, LANES, s), lambda b, hp, i: (b, hp, 0))],
        out_specs=pl.BlockSpec((None, SB_TQ, LANES), lambda b, hp, i: (b, i, hp)),
        out_shape=jax.ShapeDtypeStruct((bsz, s, C_WIDTH), BF16),
        scratch_shapes=[pltpu.VMEM((BLOCK, 2 * SB_TQ), F32),
                        pltpu.VMEM((BLOCK, 2 * SB_TQ), F32),
                        pltpu.VMEM((BLOCK, 2 * SB_TQ), F32),
                        pltpu.VMEM((BLOCK, 2 * SB_TQ), BF16),
                        pltpu.VMEM((HEAD_DIM, 2 * SB_TQ), F32)],
        compiler_params=_params("parallel", "parallel", "parallel"),
        name="stickbreak_attn",
    )(qc, kc, vt)


def _outproj_kernel(h_ref, oa_ref, ob_ref, oc_ref, wa_ref, wb_ref, wc_ref, g_ref, b_ref,
                    rw_ref, rb_ref, o_ref, gate_ref):
    m = (_dot(oa_ref[...], wa_ref[...]) + _dot(ob_ref[...], wb_ref[...])
         + _dot(oc_ref[...], wc_ref[...]))
    h1 = _layer_norm(ALPHA * h_ref[...] + m, g_ref[...], b_ref[...])
    o_ref[...] = h1
    logits = _dot3(h1, rw_ref[...])
    gate_ref[...] = _route(logits.T[:rb_ref.shape[0]], rb_ref[...])


def _outproj_call(h, oa, ob, oc, wa, wb, wc, g, b, rw, rb_col, tm=1024):
    bsz, s, d = h.shape
    nt = s // tm
    rows = rb_col.shape[0]
    row = lambda c: pl.BlockSpec((None, tm, c), lambda bb, t: (bb, t, 0))
    full = lambda w: pl.BlockSpec(w.shape, lambda bb, t: (0, 0))
    g2, b2 = g.reshape(1, d), b.reshape(1, d)
    return pl.pallas_call(
        _outproj_kernel,
        grid=(bsz, nt),
        in_specs=[row(d), row(A_WIDTH), row(B_WIDTH), row(C_WIDTH),
                  full(wa), full(wb), full(wc), full(g2), full(b2), full(rw), full(rb_col)],
        out_specs=[row(d), pl.BlockSpec((rows, tm), lambda bb, t: (0, bb * nt + t))],
        out_shape=[jax.ShapeDtypeStruct((bsz, s, d), F32),
                   jax.ShapeDtypeStruct((rows, bsz * s), F32)],
        compiler_params=_params("parallel", "parallel"),
        name="out_proj_ln_router",
    )(h, oa, ob, oc, wa, wb, wc, g2, b2, rw, rb_col)


def _route(logits, rb):
    scores = _sigmoid(logits)
    sel = scores + rb
    R = ROUTER_ROWS
    s = [sel[m * R:(m + 1) * R] for m in range(EXPERTS_PER_GROUP)]
    sc = [scores[m * R:(m + 1) * R] for m in range(EXPERTS_PER_GROUP)]
    hi01, lo01 = jnp.maximum(s[0], s[1]), jnp.minimum(s[0], s[1])
    hi23, lo23 = jnp.maximum(s[2], s[3]), jnp.minimum(s[2], s[3])
    top1 = jnp.maximum(hi01, hi23)
    top2 = jnp.maximum(jnp.minimum(hi01, hi23), jnp.maximum(lo01, lo23))
    gscore = top1 + top2
    gi = lax.broadcasted_iota(jnp.int32, gscore.shape, 0)
    gmax = jnp.max(gscore, axis=0, keepdims=True)
    best = jnp.min(jnp.where(gscore == gmax, gi, R), axis=0, keepdims=True)
    in_group = gi == best
    picked = []
    for m in range(EXPERTS_PER_GROUP):
        rank = jnp.zeros(gscore.shape, jnp.int32)
        for j in range(EXPERTS_PER_GROUP):
            if j == m:
                continue
            ahead = (s[j] >= s[m]) if j < m else (s[j] > s[m])
            rank = rank + jnp.where(ahead, 1, 0)
        picked.append(jnp.where(jnp.logical_and(in_group, rank < 2), sc[m], 0.0))
    denom = jnp.sum(picked[0] + picked[1] + picked[2] + picked[3], axis=0, keepdims=True)
    return jnp.concatenate([x / denom for x in picked], axis=0)


def _moe_kernel(x_ref, gate_ref, wgu_ref, wd_ref, g_ref, b_ref, o_ref, xb_ref, acc_ref):
    j = pl.program_id(1)

    @pl.when(j == 0)
    def _():
        xb_ref[...] = x_ref[...].astype(BF16)
        acc_ref[...] = jnp.zeros_like(acc_ref)

    xb = xb_ref[...]
    gates = gate_ref[...]
    lane = lax.broadcasted_iota(jnp.int32, gates.shape, 1)
    f = wd_ref.shape[1]
    y = None
    for m in range(MOE_EPS):
        h2 = _dot(xb, wgu_ref[m])
        hg, hu = h2[:, :f], h2[:, f:]
        gcol = jnp.sum(jnp.where(lane == j * MOE_EPS + m, gates, 0.0), axis=-1, keepdims=True)
        act = (hg * _sigmoid(hg)) * hu * gcol
        part = _dot(act.astype(BF16), wd_ref[m])
        y = part if y is None else y + part
    acc_ref[...] += y

    @pl.when(j == pl.num_programs(1) - 1)
    def _():
        o_ref[...] = _layer_norm(ALPHA * x_ref[...] + acc_ref[...], g_ref[...], b_ref[...])


def _moe_call(x, gates, wgu, wd, g, b, tm=1024):
    n, d = x.shape
    f2 = wgu.shape[-1]
    g2, b2 = g.reshape(1, d), b.reshape(1, d)
    return pl.pallas_call(
        _moe_kernel,
        grid=(n // tm, N_EXPERTS // MOE_EPS),
        in_specs=[pl.BlockSpec((tm, d), lambda i, j: (i, 0)),
                  pl.BlockSpec((tm, N_EXPERTS), lambda i, j: (i, 0)),
                  pl.BlockSpec((MOE_EPS, d, f2), lambda i, j: (j, 0, 0)),
                  pl.BlockSpec((MOE_EPS, f2 // 2, d), lambda i, j: (j, 0, 0)),
                  pl.BlockSpec((1, d), lambda i, j: (0, 0)),
                  pl.BlockSpec((1, d), lambda i, j: (0, 0))],
        out_specs=pl.BlockSpec((tm, d), lambda i, j: (i, 0)),
        out_shape=jax.ShapeDtypeStruct((n, d), F32),
        scratch_shapes=[pltpu.VMEM((tm, d), BF16), pltpu.VMEM((tm, d), F32)],
        compiler_params=_params("parallel", "arbitrary"),
        name="moe_experts_ln",
    )(x, gates, wgu, wd, g2, b2)


def _pair_heads(x, axis):
    shape = x.shape
    x = x.reshape(shape[:axis] + (A_KV_HEADS, A_Q_HEADS // A_KV_HEADS, HEAD_DIM) + shape[axis + 1:])
    x = jnp.swapaxes(x, axis, axis + 1)
    return x.reshape(shape)


def _router_layout(router_w, router_bias):
    d = router_w.shape[0]
    w = router_w.astype(F32).T.reshape(N_GROUPS, EXPERTS_PER_GROUP, d).transpose(1, 0, 2)
    w = jnp.pad(w, ((0, 0), (0, ROUTER_ROWS - N_GROUPS), (0, 0)))
    b = router_bias.astype(F32).reshape(N_GROUPS, EXPERTS_PER_GROUP).T
    b = jnp.pad(b, ((0, 0), (0, ROUTER_ROWS - N_GROUPS)), constant_values=MASKED)
    rows = EXPERTS_PER_GROUP * ROUTER_ROWS
    w = jnp.pad(w.reshape(rows, d).T, ((0, 0), (0, LANES - rows)))
    return w, b.reshape(rows, 1)


def _gates_from_router(gates_t):
    n = gates_t.shape[1]
    g = gates_t.reshape(EXPERTS_PER_GROUP, ROUTER_ROWS, n)[:, :N_GROUPS]
    return g.transpose(2, 1, 0).reshape(n, N_EXPERTS)


def kernel(x, ln0_g, ln0_b, w_in, w_out, sinks, rel_bias, shift_mu, decay_w0, decay_up, iclr_a0,
           iclr_up, gate_up, k_k, k_a, r_k, lnx_g, lnx_b, ln1_g, ln1_b, router_w, router_bias,
           w_gate, w_up, w_down, ln2_g, ln2_b):
    bsz, s, d = x.shape
    n = bsz * s
    bias_pairs = _swa_bias_pairs(rel_bias)
    rw, rb_col = _router_layout(router_w, router_bias)
    zeros_lora = jnp.zeros((ICLR_RANK, B_WIDTH), F32)

    h = x
    for l in range(DEPTH):
        wl = w_in[l]
        wa = jnp.concatenate([_pair_heads(wl[:, :A_WIDTH], 1), wl[:, A_WIDTH:A_COLS]], axis=1)
        wb = wl[:, A_COLS:A_COLS + B_COLS]
        c0 = A_COLS + B_COLS
        wq, wk, wv = (wl[:, c0 + j * C_WIDTH:c0 + (j + 1) * C_WIDTH] for j in range(3))
        weights = [wa.astype(BF16), wb.astype(BF16), wq.astype(BF16), wk.astype(BF16),
                   wv.T.astype(BF16)]
        if l == 0:
            pa, pb, qc, kc, vt, h = _proj_call(h, weights, embed_ln=(ln0_g, ln0_b))
        else:
            pa, pb, qc, kc, vt = _proj_call(h, weights)

        out_a = _swa_call(pa, sinks[l].astype(F32), bias_pairs)
        wd_pad = jnp.concatenate([decay_up[l].astype(F32), zeros_lora], axis=0).astype(BF16)
        wa_pad = jnp.concatenate([zeros_lora, iclr_up[l].astype(F32)], axis=0).astype(BF16)
        out_b = _rwkv_call(pb, shift_mu[l], decay_w0[l], wd_pad, iclr_a0[l], wa_pad,
                           gate_up[l].astype(BF16), k_k[l], k_a[l], r_k[l], lnx_g[l], lnx_b[l])
        out_c = _sb_call(qc, kc, vt)

        wo = w_out[l]
        h, gates_t = _outproj_call(h, out_a, out_b, out_c,
                                   _pair_heads(wo[:A_WIDTH], 0).astype(BF16),
                                   wo[A_WIDTH:A_WIDTH + B_WIDTH].astype(BF16),
                                   wo[A_WIDTH + B_WIDTH:].astype(BF16), ln1_g[l], ln1_b[l],
                                   rw, rb_col)

        wgu = jnp.concatenate([w_gate[l], w_up[l]], axis=-1).astype(BF16)
        hf = _moe_call(h.reshape(n, d), _gates_from_router(gates_t), wgu,
                       w_down[l].astype(BF16), ln2_g[l], ln2_b[l])
        h = hf.reshape(bsz, s, d)
    return h
```

```python
import functools
import math

import jax
import jax.numpy as jnp
from jax import lax
from jax.experimental import pallas as pl
from jax.experimental.pallas import tpu as pltpu

F32 = jnp.float32
BF16 = jnp.bfloat16
HI = lax.Precision.HIGHEST

DEPTH = 2
HEAD_DIM = 64
BLOCK = 128
LANES = 128
A_Q_HEADS = 6
A_KV_HEADS = 2
WINDOW = 128
A_WIDTH = A_Q_HEADS * HEAD_DIM
A_KV_WIDTH = A_KV_HEADS * HEAD_DIM
B_HEADS = 4
B_WIDTH = B_HEADS * HEAD_DIM
DECAY_RANK = 64
ICLR_RANK = 64
GATE_RANK = 128
GN_EPS = 64e-5
C_HEADS = 6
C_WIDTH = C_HEADS * HEAD_DIM
A_COLS = A_WIDTH + 2 * A_KV_WIDTH
B_COLS = 3 * B_WIDTH + DECAY_RANK + ICLR_RANK + GATE_RANK
NUM_BUCKETS = 32
MAX_EXACT = NUM_BUCKETS // 2
MAX_DISTANCE = 128
N_EXPERTS = 16
N_GROUPS = 4
EXPERTS_PER_GROUP = N_EXPERTS // N_GROUPS
D_FF_EXPERT = 256
LN_EPS = 1e-5
ALPHA = (2 * DEPTH) ** 0.25
SCALE = HEAD_DIM ** -0.5
MASKED = -1e30
SWA_QB = 4
CHUNK = 64
INV_BASE = 8
RWKV_TS = 512
SB_TQ = 512
SB_NEAR = 6
SB_DEAD = 151.0
LOG2E = 1.4426950408889634
ROUTER_ROWS = 8
MOE_EPS = 4

VMEM_LIMIT = 48 * 1024 * 1024


def _dot(a, b, prec=None):
    return jnp.dot(a, b, preferred_element_type=F32, precision=prec)


def _dot_nt(a, b, prec=None):
    return lax.dot_general(a, b, (((1,), (1,)), ((), ())),
                           preferred_element_type=F32, precision=prec)


def _sigmoid(x):
    return 1.0 / (1.0 + jnp.exp(-x))


def _softplus(x):
    return jnp.maximum(x, 0.0) + jnp.log(1.0 + jnp.exp(-jnp.abs(x)))


def _layer_norm(x, g, b):
    mu = jnp.mean(x, axis=-1, keepdims=True)
    xc = x - mu
    var = jnp.mean(xc * xc, axis=-1, keepdims=True)
    return xc * lax.rsqrt(var + LN_EPS) * g + b


def _params(*sem):
    return pltpu.CompilerParams(dimension_semantics=sem, vmem_limit_bytes=VMEM_LIMIT)


def _proj_body(h, wa_ref, wb_ref, wq_ref, wk_ref, wvt_ref, pa_ref, pb_ref, qc_ref, kc_ref, vt_ref):
    hb = h.astype(BF16)
    pa_ref[...] = _dot(hb, wa_ref[...]).astype(BF16)
    pb_ref[...] = _dot(hb, wb_ref[...])
    qc_ref[...] = (_dot(hb, wq_ref[...]) * (SCALE * LOG2E)).astype(BF16)
    kc_ref[...] = _dot(hb, wk_ref[...]).astype(BF16)
    vt_ref[...] = _dot_nt(wvt_ref[...], hb).astype(BF16)


def _proj_kernel(h_ref, *refs):
    _proj_body(h_ref[...], *refs)


def _embed_proj_kernel(x_ref, g_ref, b_ref, *refs):
    h = _layer_norm(x_ref[...], g_ref[...], b_ref[...])
    refs[-1][...] = h
    _proj_body(h, *refs[:-1])


def _proj_call(h, weights, embed_ln=None, tm=512):
    bsz, s, d = h.shape
    full = lambda w: pl.BlockSpec(w.shape, lambda b, t: (0, 0))
    row = lambda c: pl.BlockSpec((None, tm, c), lambda b, t: (b, t, 0))
    in_specs = [row(d)] + [full(w) for w in weights]
    out_specs = [row(A_COLS), row(B_COLS), row(C_WIDTH), row(C_WIDTH),
                 pl.BlockSpec((None, C_WIDTH, tm), lambda b, t: (b, 0, t))]
    out_shape = [jax.ShapeDtypeStruct((bsz, s, A_COLS), BF16),
                 jax.ShapeDtypeStruct((bsz, s, B_COLS), F32),
                 jax.ShapeDtypeStruct((bsz, s, C_WIDTH), BF16),
                 jax.ShapeDtypeStruct((bsz, s, C_WIDTH), BF16),
                 jax.ShapeDtypeStruct((bsz, C_WIDTH, s), BF16)]
    args = [h] + list(weights)
    if embed_ln is not None:
        vecs = [v.reshape(1, d) for v in embed_ln]
        in_specs[1:1] = [full(v) for v in vecs]
        args[1:1] = vecs
        out_specs.append(row(d))
        out_shape.append(jax.ShapeDtypeStruct((bsz, s, d), F32))
    return pl.pallas_call(
        _proj_kernel if embed_ln is None else _embed_proj_kernel,
        grid=(bsz, s // tm),
        in_specs=in_specs,
        out_specs=out_specs,
        out_shape=out_shape,
        compiler_params=_params("parallel", "parallel"),
        name="in_proj" if embed_ln is None else "embed_ln_in_proj",
    )(*args)


def _swa_kernel(sink_ref, q_ref, kp_ref, kc_ref, vp_ref, vc_ref, bias_ref, o_ref):
    n = pl.program_id(1)
    kall = jnp.concatenate([kp_ref[...], kc_ref[...]], axis=0)
    vall = jnp.concatenate([vp_ref[...], vc_ref[...]], axis=0)
    lane = lax.broadcasted_iota(jnp.int32, (BLOCK, LANES), 1)
    row2 = lax.broadcasted_iota(jnp.int32, (2 * BLOCK, 1), 0)
    col2 = lax.broadcasted_iota(jnp.int32, (1, 2 * BLOCK), 1)
    pad = jnp.where(jnp.logical_and(n == 0, col2 < BLOCK), MASKED, 0.0)
    units = [(j, c) for j in range(SWA_QB) for c in range(A_Q_HEADS // 2)]
    logits = []
    for j, c in units:
        q2 = q_ref[j * BLOCK:(j + 1) * BLOCK, c * LANES:(c + 1) * LANES]
        zero = jnp.zeros_like(q2)
        qs = jnp.concatenate([jnp.where(lane < HEAD_DIM, q2, zero),
                              jnp.where(lane >= HEAD_DIM, q2, zero)], axis=0)
        x = _dot_nt(qs, kall[j * BLOCK:(j + 2) * BLOCK]) * SCALE + bias_ref[c]
        logits.append(x + pad if j == 0 else x)
    probs, denoms = [], []
    for (j, c), x in zip(units, logits):
        sink = jnp.where(row2 < BLOCK, sink_ref[c], sink_ref[c + 3])
        m = jnp.maximum(jnp.max(x, axis=-1, keepdims=True), sink)
        p = jnp.exp(x - m)
        denoms.append(jnp.sum(p, axis=-1, keepdims=True) + jnp.exp(sink - m))
        probs.append(p.astype(BF16))
    for (j, c), p, denom in zip(units, probs, denoms):
        o = _dot(p, vall[j * BLOCK:(j + 2) * BLOCK]) / denom
        o_ref[j * BLOCK:(j + 1) * BLOCK, c * LANES:(c + 1) * LANES] = jnp.where(
            lane < HEAD_DIM, o[:BLOCK], o[BLOCK:]).astype(BF16)


def _swa_call(pa, sinks, bias_pairs):
    bsz, s, _ = pa.shape
    tq = SWA_QB * BLOCK
    kcol = A_WIDTH // LANES
    vcol = kcol + 1
    prev = lambda n: jnp.maximum(n * SWA_QB - 1, 0)
    return pl.pallas_call(
        _swa_kernel,
        grid=(bsz, s // tq),
        in_specs=[pl.BlockSpec(memory_space=pltpu.SMEM),
                  pl.BlockSpec((None, tq, A_WIDTH), lambda b, n: (b, n, 0)),
                  pl.BlockSpec((None, BLOCK, LANES), lambda b, n: (b, prev(n), kcol)),
                  pl.BlockSpec((None, tq, LANES), lambda b, n: (b, n, kcol)),
                  pl.BlockSpec((None, BLOCK, LANES), lambda b, n: (b, prev(n), vcol)),
                  pl.BlockSpec((None, tq, LANES), lambda b, n: (b, n, vcol)),
                  pl.BlockSpec(bias_pairs.shape, lambda b, n: (0, 0, 0))],
        out_specs=pl.BlockSpec((None, tq, A_WIDTH), lambda b, n: (b, n, 0)),
        out_shape=jax.ShapeDtypeStruct((bsz, s, A_WIDTH), BF16),
        compiler_params=_params("parallel", "parallel"),
        name="swa_attn",
    )(sinks, pa, pa, pa, pa, pa, bias_pairs)


def _t5_causal_bucket(dist):
    dist = jnp.maximum(dist, 0)
    d = jnp.maximum(dist, 1).astype(F32)
    large = MAX_EXACT + (jnp.log(d / MAX_EXACT) / math.log(MAX_DISTANCE / MAX_EXACT)
                         * (NUM_BUCKETS - MAX_EXACT)).astype(jnp.int32)
    large = jnp.minimum(large, NUM_BUCKETS - 1)
    return jnp.where(dist < MAX_EXACT, dist, large)


def _swa_bias_pairs(rel_bias):
    qi = jnp.arange(BLOCK)[:, None]
    ki = jnp.arange(2 * BLOCK)[None, :]
    dist = qi + BLOCK - ki
    in_window = (dist >= 0) & (dist < WINDOW)
    onehot = jax.nn.one_hot(_t5_causal_bucket(dist), NUM_BUCKETS, dtype=F32)
    bias = jnp.einsum("qkb,bh->qkh", onehot, rel_bias.astype(F32), precision=HI)
    bias = jnp.where(in_window[..., None], bias, MASKED).transpose(2, 0, 1)
    return jnp.stack([jnp.concatenate([bias[c], bias[c + 3]], axis=0)
                      for c in range(A_Q_HEADS // 2)])


def _split2(x):
    hi = x.astype(BF16)
    return hi, (x - hi.astype(F32)).astype(BF16)


def _dot3(a, b, nt=False):
    ah, al = _split2(a)
    bh, bl = _split2(b)
    d = _dot_nt if nt else _dot
    return d(ah, bh) + d(ah, bl) + d(al, bh)


def _dot1(a, b, nt=False):
    return (_dot_nt if nt else _dot)(a.astype(BF16), b.astype(BF16))


def _dot1_many(pairs, nt=False):
    parts = [(a.astype(BF16), b.astype(BF16)) for a, b in pairs]
    d = _dot_nt if nt else _dot
    return [d(a, b) for a, b in parts]


def _dot_x2(a, b_exact):
    ah, al = _split2(a)
    return _dot(ah, b_exact) + _dot(al, b_exact)


def _dot_2x(a_exact, b):
    bh, bl = _split2(b)
    return _dot(a_exact, bh) + _dot(a_exact, bl)


def _rwkv_kernel(pb_ref, mu_ref, w0_ref, wd_ref, a0_ref, wa_ref, wg_ref, kk_ref, ka_ref,
                 rk_ref, lng_ref, lnb_ref, o_ref, prev_ref, h_ref):
    t = pl.program_id(1)

    @pl.when(t == 0)
    def _():
        prev_ref[...] = jnp.zeros_like(prev_ref)
        h_ref[...] = jnp.zeros_like(h_ref)

    L = CHUNK
    W = B_WIDTH
    TS = RWKV_TS
    p = pb_ref[...]
    rows = lax.broadcasted_iota(jnp.int32, (TS, 1), 0)
    shifted = jnp.where(rows == 0, prev_ref[...], pltpu.roll(p, 1, axis=0))
    prev_ref[...] = p[TS - 1:TS, :]
    pm = p + (shifted - p) * mu_ref[...]
    r = pm[:, 0:W]
    k = pm[:, W:2 * W]
    v = pm[:, 2 * W:3 * W]
    xwa = pm[:, 3 * W:3 * W + DECAY_RANK + ICLR_RANK]
    xg = pm[:, 3 * W + DECAY_RANK + ICLR_RANK:]

    dw = w0_ref[...] + _dot(jnp.tanh(xwa).astype(BF16), wd_ref[...])
    lw = -jnp.exp(-_softplus(-dw) - 0.5)
    a = _sigmoid(a0_ref[...] + _dot(xwa.astype(BF16), wa_ref[...]))
    g = _dot(_sigmoid(xg).astype(BF16), wg_ref[...])

    hr = lax.broadcasted_iota(jnp.int32, (W, W), 0) // HEAD_DIM
    hc = lax.broadcasted_iota(jnp.int32, (W, W), 1) // HEAD_DIM
    same_head = hr == hc
    diag_w = (lax.broadcasted_iota(jnp.int32, (W, W), 0)
              == lax.broadcasted_iota(jnp.int32, (W, W), 1))
    head_ones = jnp.where(same_head, 1.0, 0.0).astype(BF16)
    kk = k * kk_ref[...]
    kk = kk * lax.rsqrt(jnp.maximum(_dot_x2(kk * kk, head_ones), 1e-24))
    k2 = k * (1.0 + (a - 1.0) * ka_ref[...])
    bonus = _dot_x2(r * k2 * rk_ref[...], head_ones) * v
    aa = -kk
    bb = kk * a

    ti = lax.broadcasted_iota(jnp.int32, (L, L), 0)
    tj = lax.broadcasted_iota(jnp.int32, (L, L), 1)
    lower = jnp.where(ti >= tj, 1.0, 0.0).astype(BF16)
    eye = jnp.where(ti == tj, 1.0, 0.0)
    lane_head = lax.broadcasted_iota(jnp.int32, (L, W), 1) // HEAD_DIM

    def only(x, h):
        return jnp.where(lane_head == h, x, 0.0)

    nch = TS // L
    chunks = range(nch)
    units = [(c, h) for c in chunks for h in range(B_HEADS)]
    hsl = [slice(h * L, (h + 1) * L) for h in range(B_HEADS)]
    csl = [slice(c * L, (c + 1) * L) for c in chunks]
    v_c = [v[s] for s in csl]
    cum = [_dot_2x(lower, lw[s]) for s in csl]
    cum_l = [x[L - 1:L, :] for x in cum]
    at = [aa[csl[c]] * jnp.exp(cum[c] - lw[csl[c]]) for c in chunks]
    rt = [r[csl[c]] * jnp.exp(cum[c]) for c in chunks]
    inv = [jnp.exp(-x) for x in cum]
    bt = [bb[csl[c]] * inv[c] for c in chunks]
    kt = [k2[csl[c]] * inv[c] for c in chunks]
    tail = [jnp.exp(cum_l[c] - cum[c]) for c in chunks]
    bh = [bb[csl[c]] * tail[c] for c in chunks]
    kh = [k2[csl[c]] * tail[c] for c in chunks]

    at_s = [jnp.concatenate([only(x, h) for h in range(B_HEADS)], axis=0) for x in at]
    rt_s = [jnp.concatenate([only(x, h) for h in range(B_HEADS)], axis=0) for x in rt]
    ab = _dot1_many([(at_s[c], bt[c]) for c in chunks], nt=True)
    ak = _dot1_many([(at_s[c], kt[c]) for c in chunks], nt=True)
    rb = _dot1_many([(rt_s[c], bt[c]) for c in chunks], nt=True)
    rk = _dot1_many([(rt_s[c], kt[c]) for c in chunks], nt=True)

    a_low = [jnp.where(ti > tj, ab[c][hsl[h]], 0.0) for c, h in units]
    pw = [jnp.where(ti // INV_BASE == tj // INV_BASE, x, 0.0) for x in a_low]
    tinv = [eye + x for x in pw]
    for _ in range(int(math.log2(INV_BASE)) - 1):
        pw = _dot1_many([(x, x) for x in pw])
        tinv = [t + d for t, d in zip(tinv, _dot1_many(list(zip(tinv, pw))))]
    blk = INV_BASE
    while blk < L:
        pair = jnp.logical_and(ti // (2 * blk) == tj // (2 * blk), ti // blk != tj // blk)
        right = _dot1_many([(jnp.where(pair, x, 0.0), t) for x, t in zip(a_low, tinv)])
        tinv = [t + d for t, d in zip(tinv, _dot1_many(list(zip(tinv, right))))]
        blk *= 2
    ak_v = _dot1_many([(jnp.where(ti > tj, ak[c][hsl[h]], 0.0), v_c[c]) for c, h in units])
    w_u = _dot1_many([(tinv[i], only(at[c], h)) for i, (c, h) in enumerate(units)])
    u0_u = [only(x, h) for x, (c, h) in zip(_dot1_many(list(zip(tinv, ak_v))), units)]
    rb_l = [jnp.where(ti >= tj, rb[c][hsl[h]], 0.0) for c, h in units]
    rk_l = [jnp.where(ti >= tj, rk[c][hsl[h]], 0.0) for c, h in units]
    qm_u = _dot1_many(list(zip(rb_l, w_u)))
    y0_u = [only(p + q, h) for p, q, (c, h) in zip(
        _dot1_many(list(zip(rb_l, u0_u))),
        _dot1_many([(rk_l[i], v_c[c]) for i, (c, h) in enumerate(units)]), units)]

    def chunk_sum(xs, c):
        return functools.reduce(lambda p, q: p + q, xs[c * B_HEADS:(c + 1) * B_HEADS])

    w_sum = [chunk_sum(w_u, c) for c in chunks]
    u0 = [chunk_sum(u0_u, c) for c in chunks]
    qm = [rt[c] + chunk_sum(qm_u, c) for c in chunks]
    y0 = [chunk_sum(y0_u, c) for c in chunks]
    bw = _dot1_many([(bh[c].T, w_sum[c]) for c in chunks])
    g_mat = [jnp.where(same_head, bw[c], 0.0) + jnp.where(diag_w, jnp.exp(cum_l[c]).T, 0.0)
             for c in chunks]
    c_mat = [jnp.where(same_head, x, 0.0) for x in _dot1_many(
        [(jnp.concatenate([bh[c], kh[c]], axis=0).T, jnp.concatenate([u0[c], v_c[c]], axis=0))
         for c in chunks])]

    hst = h_ref[...]
    ys = []
    for c in chunks:
        ys.append(_dot1(qm[c], hst) + y0[c])
        hst = _dot1(g_mat[c], hst) + c_mat[c]
    h_ref[...] = hst
    y = jnp.concatenate(ys, axis=0)

    mean = _dot_x2(y, head_ones) * (1.0 / HEAD_DIM)
    yc = y - mean
    var = _dot_x2(yc * yc, head_ones) * (1.0 / HEAD_DIM)
    yn = yc * lax.rsqrt(var + GN_EPS) * lng_ref[...] + lnb_ref[...]
    o_ref[...] = ((yn + bonus) * g).astype(o_ref.dtype)


def _rwkv_call(pb, mu, w0, wd_pad, a0, wa_pad, wg, k_k, k_a, r_k, lnx_g, lnx_b):
    bsz, s, _ = pb.shape
    vec = lambda x: x.reshape(1, -1).astype(F32)
    small = [vec(mu), vec(w0), wd_pad, vec(a0), wa_pad, wg, vec(k_k), vec(k_a), vec(r_k),
             vec(lnx_g), vec(lnx_b)]
    return pl.pallas_call(
        _rwkv_kernel,
        grid=(bsz, s // RWKV_TS),
        in_specs=[pl.BlockSpec((None, RWKV_TS, B_COLS), lambda b, t: (b, t, 0))]
        + [pl.BlockSpec(x.shape, lambda b, t: (0, 0)) for x in small],
        out_specs=pl.BlockSpec((None, RWKV_TS, B_WIDTH), lambda b, t: (b, t, 0)),
        out_shape=jax.ShapeDtypeStruct((bsz, s, B_WIDTH), BF16),
        scratch_shapes=[pltpu.VMEM((1, B_COLS), F32), pltpu.VMEM((B_WIDTH, B_WIDTH), F32)],
        compiler_params=_params("parallel", "arbitrary"),
        name="rwkv7",
    )(pb, *small)


def _softplus2(z):
    return jnp.maximum(z, 0.0) + jnp.log2(1.0 + jnp.exp2(-jnp.abs(z)))


def _sb_kernel(q_ref, k_ref, vt_ref, o_ref, z_sc, lb_sc, e_sc, w_sc, acc_sc):
    it = pl.program_id(2)
    q2 = q_ref[...]
    lane = lax.broadcasted_iota(jnp.int32, (SB_TQ, LANES), 1)
    zero = jnp.zeros_like(q2)
    qh = (jnp.where(lane < HEAD_DIM, q2, zero), jnp.where(lane >= HEAD_DIM, q2, zero))
    ur = lax.broadcasted_iota(jnp.int32, (BLOCK, BLOCK), 0)
    uc = lax.broadcasted_iota(jnp.int32, (BLOCK, BLOCK), 1)
    upper = jnp.where(uc > ur, 1.0, 0.0).astype(BF16)
    nsub = SB_TQ // BLOCK
    first = it * nsub + nsub - 1
    nblk = first + 1

    def key_off(n):
        return pl.multiple_of(jnp.clip(first - n, 0, first) * BLOCK, BLOCK)

    def logits(n):
        kblk = k_ref[pl.ds(key_off(n), BLOCK), :]
        return jnp.concatenate([_dot_nt(kblk, qh[0]), _dot_nt(kblk, qh[1])], axis=1)

    def softplus_stage(zt):
        sp = _softplus2(zt)
        return zt - sp, sp.astype(BF16), jnp.sum(sp, axis=0, keepdims=True)

    def value_stage(n):
        vt2 = vt_ref[:, pl.ds(key_off(n), BLOCK)]
        w = w_sc[...]
        for h in range(2):
            cols = slice(h * SB_TQ, (h + 1) * SB_TQ)
            acc_sc[:, cols] += _dot(vt2[h * HEAD_DIM:(h + 1) * HEAD_DIM], w[:, cols])

    def step(n, carry):
        car, colsum = carry
        value_stage(n - 1)
        z_new = logits(n + 2)
        lb_new, sp16, colsum_new = softplus_stage(z_sc[...])
        e_new = _dot(upper, sp16)
        w_sc[...] = jnp.exp2(lb_sc[...] - e_sc[...] - car).astype(BF16)
        z_sc[...] = z_new
        lb_sc[...] = lb_new
        e_sc[...] = e_new
        return car + colsum, colsum_new

    def lane_off(n):
        return max(0, nsub - 1 - n) * BLOCK

    def near_logits(n):
        kblk = k_ref[pl.ds(key_off(n), BLOCK), :]
        return [_dot_nt(kblk, qh[h][lane_off(n):]) for h in range(2)]

    def near_softplus(zt, n):
        width = SB_TQ - lane_off(n)
        sp = _softplus2(zt)
        lb = zt - sp
        if n < nsub:
            kr = lax.broadcasted_iota(jnp.int32, (BLOCK, width), 0)
            qc = lax.broadcasted_iota(jnp.int32, (BLOCK, width), 1)
            before = kr < qc
            sp = jnp.where(before, sp, 0.0)
            lb = jnp.where(before, lb, MASKED)
        colsum = jnp.sum(sp, axis=0, keepdims=True)
        if lane_off(n):
            colsum = jnp.concatenate([jnp.zeros((1, lane_off(n)), F32), colsum], axis=1)
        return lb, sp.astype(BF16), colsum

    def near_values(n, ws):
        vt2 = vt_ref[:, pl.ds(key_off(n), BLOCK)]
        if n >= nsub:
            vt2 = jnp.where(n < nblk, vt2, jnp.zeros_like(vt2))
        for h in range(2):
            cols = slice(h * SB_TQ + lane_off(n), (h + 1) * SB_TQ)
            acc_sc[:, cols] += _dot(vt2[h * HEAD_DIM:(h + 1) * HEAD_DIM], ws[h])

    acc_sc[...] = jnp.zeros_like(acc_sc)
    zs = {0: near_logits(0), 1: near_logits(1)}
    soft = {0: [near_softplus(zs[0][h], 0) for h in range(2)]}
    excl = {0: [_dot(upper, soft[0][h][1]) for h in range(2)]}
    cars = [jnp.zeros((1, SB_TQ), F32) for _ in range(2)]
    ws = None
    for n in range(SB_NEAR):
        if n >= 1:
            near_values(n - 1, ws)
        if n + 2 < SB_NEAR:
            zs[n + 2] = near_logits(n + 2)
        if n + 1 < SB_NEAR:
            soft[n + 1] = [near_softplus(zs[n + 1][h], n + 1) for h in range(2)]
            excl[n + 1] = [_dot(upper, soft[n + 1][h][1]) for h in range(2)]
        ws = [jnp.exp2(soft[n][h][0] - excl[n][h] - cars[h][:, lane_off(n):]).astype(BF16)
              for h in range(2)]
        cars = [cars[h] + soft[n][h][2] for h in range(2)]
    near_values(SB_NEAR - 1, ws)
    car = jnp.concatenate(cars, axis=1)

    @pl.when(jnp.logical_and(nblk > SB_NEAR, jnp.min(car) < SB_DEAD))
    def _():
        lb0, sp160, colsum0 = softplus_stage(logits(SB_NEAR))
        lb_sc[...] = lb0
        e_sc[...] = _dot(upper, sp160)
        z_sc[...] = logits(SB_NEAR + 1)
        w_sc[...] = jnp.zeros_like(w_sc)

        def live(c):
            n, _, _, alive = c
            return jnp.logical_and(n < nblk, alive > 0)

        def visit(c):
            n, car, colsum, _ = c
            alive = (jnp.min(car) < SB_DEAD).astype(jnp.int32)
            car, colsum = step(n, (car, colsum))
            return n + jnp.int32(1), car, colsum, alive

        n_stop = lax.while_loop(live, visit, (jnp.int32(SB_NEAR), car, colsum0, jnp.int32(1)))[0]
        value_stage(n_stop - 1)

    acc = acc_sc[...]
    out_t = jnp.concatenate([acc[:, :SB_TQ], acc[:, SB_TQ:]], axis=0)
    o_ref[...] = out_t.T.astype(o_ref.dtype)


def _sb_call(qc, kc, vt):
    bsz, s, _ = qc.shape
    return pl.pallas_call(
        _sb_kernel,
        grid=(bsz, C_WIDTH // LANES, s // SB_TQ),
        in_specs=[pl.BlockSpec((None, SB_TQ, LANES), lambda b, hp, i: (b, i, hp)),
                  pl.BlockSpec((None, s, LANES), lambda b, hp, i: (b, 0, hp)),
                  pl.BlockSpec((None, LANES, s), lambda b, hp, i: (b, hp, 0))],
        out_specs=pl.BlockSpec((None, SB_TQ, LANES), lambda b, hp, i: (b, i, hp)),
        out_shape=jax.ShapeDtypeStruct((bsz, s, C_WIDTH), BF16),
        scratch_shapes=[pltpu.VMEM((BLOCK, 2 * SB_TQ), F32),
                        pltpu.VMEM((BLOCK, 2 * SB_TQ), F32),
                        pltpu.VMEM((BLOCK, 2 * SB_TQ), F32),
                        pltpu.VMEM((BLOCK, 2 * SB_TQ), BF16),
                        pltpu.VMEM((HEAD_DIM, 2 * SB_TQ), F32)],
        compiler_params=_params("parallel", "parallel", "parallel"),
        name="stickbreak_attn",
    )(qc, kc, vt)


def _outproj_kernel(h_ref, oa_ref, ob_ref, oc_ref, wa_ref, wb_ref, wc_ref, g_ref, b_ref,
                    rw_ref, rb_ref, o_ref, gate_ref):
    m = (_dot(oa_ref[...], wa_ref[...]) + _dot(ob_ref[...], wb_ref[...])
         + _dot(oc_ref[...], wc_ref[...]))
    h1 = _layer_norm(ALPHA * h_ref[...] + m, g_ref[...], b_ref[...])
    o_ref[...] = h1
    logits = _dot3(h1, rw_ref[...])
    gate_ref[...] = _route(logits.T[:rb_ref.shape[0]], rb_ref[...])


def _outproj_call(h, oa, ob, oc, wa, wb, wc, g, b, rw, rb_col, tm=1024):
    bsz, s, d = h.shape
    nt = s // tm
    rows = rb_col.shape[0]
    row = lambda c: pl.BlockSpec((None, tm, c), lambda bb, t: (bb, t, 0))
    full = lambda w: pl.BlockSpec(w.shape, lambda bb, t: (0, 0))
    g2, b2 = g.reshape(1, d), b.reshape(1, d)
    return pl.pallas_call(
        _outproj_kernel,
        grid=(bsz, nt),
        in_specs=[row(d), row(A_WIDTH), row(B_WIDTH), row(C_WIDTH),
                  full(wa), full(wb), full(wc), full(g2), full(b2), full(rw), full(rb_col)],
        out_specs=[row(d), pl.BlockSpec((rows, tm), lambda bb, t: (0, bb * nt + t))],
        out_shape=[jax.ShapeDtypeStruct((bsz, s, d), F32),
                   jax.ShapeDtypeStruct((rows, bsz * s), F32)],
        compiler_params=_params("parallel", "parallel"),
        name="out_proj_ln_router",
    )(h, oa, ob, oc, wa, wb, wc, g2, b2, rw, rb_col)


def _route(logits, rb):
    scores = _sigmoid(logits)
    sel = scores + rb
    R = ROUTER_ROWS
    s = [sel[m * R:(m + 1) * R] for m in range(EXPERTS_PER_GROUP)]
    sc = [scores[m * R:(m + 1) * R] for m in range(EXPERTS_PER_GROUP)]
    hi01, lo01 = jnp.maximum(s[0], s[1]), jnp.minimum(s[0], s[1])
    hi23, lo23 = jnp.maximum(s[2], s[3]), jnp.minimum(s[2], s[3])
    top1 = jnp.maximum(hi01, hi23)
    top2 = jnp.maximum(jnp.minimum(hi01, hi23), jnp.maximum(lo01, lo23))
    gscore = top1 + top2
    gi = lax.broadcasted_iota(jnp.int32, gscore.shape, 0)
    gmax = jnp.max(gscore, axis=0, keepdims=True)
    best = jnp.min(jnp.where(gscore == gmax, gi, R), axis=0, keepdims=True)
    in_group = gi == best
    picked = []
    for m in range(EXPERTS_PER_GROUP):
        rank = jnp.zeros(gscore.shape, jnp.int32)
        for j in range(EXPERTS_PER_GROUP):
            if j == m:
                continue
            ahead = (s[j] >= s[m]) if j < m else (s[j] > s[m])
            rank = rank + jnp.where(ahead, 1, 0)
        picked.append(jnp.where(jnp.logical_and(in_group, rank < 2), sc[m], 0.0))
    denom = jnp.sum(picked[0] + picked[1] + picked[2] + picked[3], axis=0, keepdims=True)
    return jnp.concatenate([x / denom for x in picked], axis=0)


def _moe_kernel(x_ref, gate_ref, wgu_ref, wd_ref, g_ref, b_ref, o_ref, xb_ref, acc_ref):
    j = pl.program_id(1)

    @pl.when(j == 0)
    def _():
        xb_ref[...] = x_ref[...].astype(BF16)
        acc_ref[...] = jnp.zeros_like(acc_ref)

    xb = xb_ref[...]
    gates = gate_ref[...]
    lane = lax.broadcasted_iota(jnp.int32, gates.shape, 1)
    f = wd_ref.shape[1]
    y = None
    for m in range(MOE_EPS):
        h2 = _dot(xb, wgu_ref[m])
        hg, hu = h2[:, :f], h2[:, f:]
        gcol = jnp.sum(jnp.where(lane == j * MOE_EPS + m, gates, 0.0), axis=-1, keepdims=True)
        act = (hg * _sigmoid(hg)) * hu * gcol
        part = _dot(act.astype(BF16), wd_ref[m])
        y = part if y is None else y + part
    acc_ref[...] += y

    @pl.when(j == pl.num_programs(1) - 1)
    def _():
        o_ref[...] = _layer_norm(ALPHA * x_ref[...] + acc_ref[...], g_ref[...], b_ref[...])


def _moe_call(x, gates, wgu, wd, g, b, tm=1024):
    n, d = x.shape
    f2 = wgu.shape[-1]
    g2, b2 = g.reshape(1, d), b.reshape(1, d)
    return pl.pallas_call(
        _moe_kernel,
        grid=(n // tm, N_EXPERTS // MOE_EPS),
        in_specs=[pl.BlockSpec((tm, d), lambda i, j: (i, 0)),
                  pl.BlockSpec((tm, N_EXPERTS), lambda i, j: (i, 0)),
                  pl.BlockSpec((MOE_EPS, d, f2), lambda i, j: (j, 0, 0)),
                  pl.BlockSpec((MOE_EPS, f2 // 2, d), lambda i, j: (j, 0, 0)),
                  pl.BlockSpec((1, d), lambda i, j: (0, 0)),
                  pl.BlockSpec((1, d), lambda i, j: (0, 0))],
        out_specs=pl.BlockSpec((tm, d), lambda i, j: (i, 0)),
        out_shape=jax.ShapeDtypeStruct((n, d), F32),
        scratch_shapes=[pltpu.VMEM((tm, d), BF16), pltpu.VMEM((tm, d), F32)],
        compiler_params=_params("parallel", "arbitrary"),
        name="moe_experts_ln",
    )(x, gates, wgu, wd, g2, b2)


def _pair_heads(x, axis):
    shape = x.shape
    x = x.reshape(shape[:axis] + (A_KV_HEADS, A_Q_HEADS // A_KV_HEADS, HEAD_DIM) + shape[axis + 1:])
    x = jnp.swapaxes(x, axis, axis + 1)
    return x.reshape(shape)


def _router_layout(router_w, router_bias):
    d = router_w.shape[0]
    w = router_w.astype(F32).T.reshape(N_GROUPS, EXPERTS_PER_GROUP, d).transpose(1, 0, 2)
    w = jnp.pad(w, ((0, 0), (0, ROUTER_ROWS - N_GROUPS), (0, 0)))
    b = router_bias.astype(F32).reshape(N_GROUPS, EXPERTS_PER_GROUP).T
    b = jnp.pad(b, ((0, 0), (0, ROUTER_ROWS - N_GROUPS)), constant_values=MASKED)
    rows = EXPERTS_PER_GROUP * ROUTER_ROWS
    w = jnp.pad(w.reshape(rows, d).T, ((0, 0), (0, LANES - rows)))
    return w, b.reshape(rows, 1)


def _gates_from_router(gates_t):
    n = gates_t.shape[1]
    g = gates_t.reshape(EXPERTS_PER_GROUP, ROUTER_ROWS, n)[:, :N_GROUPS]
    return g.transpose(2, 1, 0).reshape(n, N_EXPERTS)


def kernel(x, ln0_g, ln0_b, w_in, w_out, sinks, rel_bias, shift_mu, decay_w0, decay_up, iclr_a0,
           iclr_up, gate_up, k_k, k_a, r_k, lnx_g, lnx_b, ln1_g, ln1_b, router_w, router_bias,
           w_gate, w_up, w_down, ln2_g, ln2_b):
    bsz, s, d = x.shape
    n = bsz * s
    bias_pairs = _swa_bias_pairs(rel_bias)
    rw, rb_col = _router_layout(router_w, router_bias)
    zeros_lora = jnp.zeros((ICLR_RANK, B_WIDTH), F32)

    h = x
    for l in range(DEPTH):
        wl = w_in[l]
        wa = jnp.concatenate([_pair_heads(wl[:, :A_WIDTH], 1), wl[:, A_WIDTH:A_COLS]], axis=1)
        wb = wl[:, A_COLS:A_COLS + B_COLS]
        c0 = A_COLS + B_COLS
        wq, wk, wv = (wl[:, c0 + j * C_WIDTH:c0 + (j + 1) * C_WIDTH] for j in range(3))
        weights = [wa.astype(BF16), wb.astype(BF16), wq.astype(BF16), wk.astype(BF16),
                   wv.T.astype(BF16)]
        if l == 0:
            pa, pb, qc, kc, vt, h = _proj_call(h, weights, embed_ln=(ln0_g, ln0_b))
        else:
            pa, pb, qc, kc, vt = _proj_call(h, weights)

        out_a = _swa_call(pa, sinks[l].astype(F32), bias_pairs)
        wd_pad = jnp.concatenate([decay_up[l].astype(F32), zeros_lora], axis=0).astype(BF16)
        wa_pad = jnp.concatenate([zeros_lora, iclr_up[l].astype(F32)], axis=0).astype(BF16)
        out_b = _rwkv_call(pb, shift_mu[l], decay_w0[l], wd_pad, iclr_a0[l], wa_pad,
                           gate_up[l].astype(BF16), k_k[l], k_a[l], r_k[l], lnx_g[l], lnx_b[l])
        out_c = _sb_call(qc, kc, vt)

        wo = w_out[l]
        h, gates_t = _outproj_call(h, out_a, out_b, out_c,
                                   _pair_heads(wo[:A_WIDTH], 0).astype(BF16),
                                   wo[A_WIDTH:A_WIDTH + B_WIDTH].astype(BF16),
                                   wo[A_WIDTH + B_WIDTH:].astype(BF16), ln1_g[l], ln1_b[l],
                                   rw, rb_col)

        wgu = jnp.concatenate([w_gate[l], w_up[l]], axis=-1).astype(BF16)
        hf = _moe_call(h.reshape(n, d), _gates_from_router(gates_t), wgu,
                       w_down[l].astype(BF16), ln2_g[l], ln2_b[l])
        h = hf.reshape(bsz, s, d)
    return h
```

```python
import functools
import math

import jax
import jax.numpy as jnp
from jax import lax
from jax.experimental import pallas as pl
from jax.experimental.pallas import tpu as pltpu

F32 = jnp.float32
BF16 = jnp.bfloat16
HI = lax.Precision.HIGHEST

DEPTH = 2
HEAD_DIM = 64
BLOCK = 128
LANES = 128
A_Q_HEADS = 6
A_KV_HEADS = 2
WINDOW = 128
A_WIDTH = A_Q_HEADS * HEAD_DIM
A_KV_WIDTH = A_KV_HEADS * HEAD_DIM
B_HEADS = 4
B_WIDTH = B_HEADS * HEAD_DIM
DECAY_RANK = 64
ICLR_RANK = 64
GATE_RANK = 128
GN_EPS = 64e-5
C_HEADS = 6
C_WIDTH = C_HEADS * HEAD_DIM
A_COLS = A_WIDTH + 2 * A_KV_WIDTH
B_COLS = 3 * B_WIDTH + DECAY_RANK + ICLR_RANK + GATE_RANK
NUM_BUCKETS = 32
MAX_EXACT = NUM_BUCKETS // 2
MAX_DISTANCE = 128
N_EXPERTS = 16
N_GROUPS = 4
EXPERTS_PER_GROUP = N_EXPERTS // N_GROUPS
D_FF_EXPERT = 256
LN_EPS = 1e-5
ALPHA = (2 * DEPTH) ** 0.25
SCALE = HEAD_DIM ** -0.5
MASKED = -1e30
SWA_QB = 4
CHUNK = 64
INV_BASE = 8
RWKV_TS = 512
SB_TQ = 512
SB_NEAR = 6
SB_DEAD = 151.0
LOG2E = 1.4426950408889634
ROUTER_ROWS = 8
MOE_EPS = 4
OUTPROJ_PARTS = 4

VMEM_LIMIT = 48 * 1024 * 1024


def _dot(a, b, prec=None):
    return jnp.dot(a, b, preferred_element_type=F32, precision=prec)


def _dot_nt(a, b, prec=None):
    return lax.dot_general(a, b, (((1,), (1,)), ((), ())),
                           preferred_element_type=F32, precision=prec)


def _sigmoid(x):
    return 1.0 / (1.0 + jnp.exp(-x))


def _softplus(x):
    return jnp.maximum(x, 0.0) + jnp.log(1.0 + jnp.exp(-jnp.abs(x)))


def _layer_norm(x, g, b):
    mu = jnp.mean(x, axis=-1, keepdims=True)
    xc = x - mu
    var = jnp.mean(xc * xc, axis=-1, keepdims=True)
    return xc * lax.rsqrt(var + LN_EPS) * g + b


def _params(*sem):
    return pltpu.CompilerParams(dimension_semantics=sem, vmem_limit_bytes=VMEM_LIMIT)


def _proj_body(h, wa_ref, wb_ref, wq_ref, wk_ref, wvt_ref, pa_ref, pb_ref, qc_ref, kc_ref, vt_ref):
    hb = h.astype(BF16)
    pa_ref[...] = _dot(hb, wa_ref[...]).astype(BF16)
    pb_ref[...] = _dot(hb, wb_ref[...])
    qc_ref[...] = (_dot(hb, wq_ref[...]) * (SCALE * LOG2E)).astype(BF16)
    kc_ref[...] = _dot(hb, wk_ref[...]).astype(BF16)
    vt_ref[...] = _dot_nt(wvt_ref[...], hb).astype(BF16)


def _proj_kernel(h_ref, *refs):
    _proj_body(h_ref[...], *refs)


def _embed_proj_kernel(x_ref, g_ref, b_ref, *refs):
    h = _layer_norm(x_ref[...], g_ref[...], b_ref[...])
    refs[-1][...] = h
    _proj_body(h, *refs[:-1])


def _proj_call(h, weights, embed_ln=None, tm=512):
    bsz, s, d = h.shape
    full = lambda w: pl.BlockSpec(w.shape, lambda b, t: (0, 0))
    row = lambda c: pl.BlockSpec((None, tm, c), lambda b, t: (b, t, 0))
    in_specs = [row(d)] + [full(w) for w in weights]
    out_specs = [row(A_COLS), row(B_COLS), row(C_WIDTH), row(C_WIDTH),
                 pl.BlockSpec((None, C_WIDTH, tm), lambda b, t: (b, 0, t))]
    out_shape = [jax.ShapeDtypeStruct((bsz, s, A_COLS), BF16),
                 jax.ShapeDtypeStruct((bsz, s, B_COLS), F32),
                 jax.ShapeDtypeStruct((bsz, s, C_WIDTH), BF16),
                 jax.ShapeDtypeStruct((bsz, s, C_WIDTH), BF16),
                 jax.ShapeDtypeStruct((bsz, C_WIDTH, s), BF16)]
    args = [h] + list(weights)
    if embed_ln is not None:
        vecs = [v.reshape(1, d) for v in embed_ln]
        in_specs[1:1] = [full(v) for v in vecs]
        args[1:1] = vecs
        out_specs.append(row(d))
        out_shape.append(jax.ShapeDtypeStruct((bsz, s, d), F32))
    return pl.pallas_call(
        _proj_kernel if embed_ln is None else _embed_proj_kernel,
        grid=(bsz, s // tm),
        in_specs=in_specs,
        out_specs=out_specs,
        out_shape=out_shape,
        compiler_params=_params("parallel", "parallel"),
        name="in_proj" if embed_ln is None else "embed_ln_in_proj",
    )(*args)


def _swa_kernel(sink_ref, q_ref, kp_ref, kc_ref, vp_ref, vc_ref, bias_ref, o_ref):
    n = pl.program_id(1)
    kall = jnp.concatenate([kp_ref[...], kc_ref[...]], axis=0)
    vall = jnp.concatenate([vp_ref[...], vc_ref[...]], axis=0)
    lane = lax.broadcasted_iota(jnp.int32, (BLOCK, LANES), 1)
    row2 = lax.broadcasted_iota(jnp.int32, (2 * BLOCK, 1), 0)
    col2 = lax.broadcasted_iota(jnp.int32, (1, 2 * BLOCK), 1)
    pad = jnp.where(jnp.logical_and(n == 0, col2 < BLOCK), MASKED, 0.0)
    units = [(j, c) for j in range(SWA_QB) for c in range(A_Q_HEADS // 2)]
    logits = []
    for j, c in units:
        q2 = q_ref[j * BLOCK:(j + 1) * BLOCK, c * LANES:(c + 1) * LANES]
        zero = jnp.zeros_like(q2)
        qs = jnp.concatenate([jnp.where(lane < HEAD_DIM, q2, zero),
                              jnp.where(lane >= HEAD_DIM, q2, zero)], axis=0)
        x = _dot_nt(qs, kall[j * BLOCK:(j + 2) * BLOCK]) * SCALE + bias_ref[c]
        logits.append(x + pad if j == 0 else x)
    probs, denoms = [], []
    for (j, c), x in zip(units, logits):
        sink = jnp.where(row2 < BLOCK, sink_ref[c], sink_ref[c + 3])
        m = jnp.maximum(jnp.max(x, axis=-1, keepdims=True), sink)
        p = jnp.exp(x - m)
        denoms.append(jnp.sum(p, axis=-1, keepdims=True) + jnp.exp(sink - m))
        probs.append(p.astype(BF16))
    for (j, c), p, denom in zip(units, probs, denoms):
        o = _dot(p, vall[j * BLOCK:(j + 2) * BLOCK]) / denom
        o_ref[j * BLOCK:(j + 1) * BLOCK, c * LANES:(c + 1) * LANES] = jnp.where(
            lane < HEAD_DIM, o[:BLOCK], o[BLOCK:]).astype(BF16)


def _swa_call(pa, sinks, bias_pairs):
    bsz, s, _ = pa.shape
    tq = SWA_QB * BLOCK
    kcol = A_WIDTH // LANES
    vcol = kcol + 1
    prev = lambda n: jnp.maximum(n * SWA_QB - 1, 0)
    return pl.pallas_call(
        _swa_kernel,
        grid=(bsz, s // tq),
        in_specs=[pl.BlockSpec(memory_space=pltpu.SMEM),
                  pl.BlockSpec((None, tq, A_WIDTH), lambda b, n: (b, n, 0)),
                  pl.BlockSpec((None, BLOCK, LANES), lambda b, n: (b, prev(n), kcol)),
                  pl.BlockSpec((None, tq, LANES), lambda b, n: (b, n, kcol)),
                  pl.BlockSpec((None, BLOCK, LANES), lambda b, n: (b, prev(n), vcol)),
                  pl.BlockSpec((None, tq, LANES), lambda b, n: (b, n, vcol)),
                  pl.BlockSpec(bias_pairs.shape, lambda b, n: (0, 0, 0))],
        out_specs=pl.BlockSpec((None, tq, A_WIDTH), lambda b, n: (b, n, 0)),
        out_shape=jax.ShapeDtypeStruct((bsz, s, A_WIDTH), BF16),
        compiler_params=_params("parallel", "parallel"),
        name="swa_attn",
    )(sinks, pa, pa, pa, pa, pa, bias_pairs)


def _t5_causal_bucket(dist):
    dist = jnp.maximum(dist, 0)
    d = jnp.maximum(dist, 1).astype(F32)
    large = MAX_EXACT + (jnp.log(d / MAX_EXACT) / math.log(MAX_DISTANCE / MAX_EXACT)
                         * (NUM_BUCKETS - MAX_EXACT)).astype(jnp.int32)
    large = jnp.minimum(large, NUM_BUCKETS - 1)
    return jnp.where(dist < MAX_EXACT, dist, large)


def _swa_bias_pairs(rel_bias):
    qi = jnp.arange(BLOCK)[:, None]
    ki = jnp.arange(2 * BLOCK)[None, :]
    dist = qi + BLOCK - ki
    in_window = (dist >= 0) & (dist < WINDOW)
    onehot = jax.nn.one_hot(_t5_causal_bucket(dist), NUM_BUCKETS, dtype=F32)
    bias = jnp.einsum("qkb,bh->qkh", onehot, rel_bias.astype(F32), precision=HI)
    bias = jnp.where(in_window[..., None], bias, MASKED).transpose(2, 0, 1)
    return jnp.stack([jnp.concatenate([bias[c], bias[c + 3]], axis=0)
                      for c in range(A_Q_HEADS // 2)])


def _split2(x):
    hi = x.astype(BF16)
    return hi, (x - hi.astype(F32)).astype(BF16)


def _dot3(a, b, nt=False):
    ah, al = _split2(a)
    bh, bl = _split2(b)
    d = _dot_nt if nt else _dot
    return d(ah, bh) + d(ah, bl) + d(al, bh)


def _dot1(a, b, nt=False):
    return (_dot_nt if nt else _dot)(a.astype(BF16), b.astype(BF16))


def _dot1_many(pairs, nt=False):
    parts = [(a.astype(BF16), b.astype(BF16)) for a, b in pairs]
    d = _dot_nt if nt else _dot
    return [d(a, b) for a, b in parts]


def _dot_x2(a, b_exact):
    ah, al = _split2(a)
    return _dot(ah, b_exact) + _dot(al, b_exact)


def _dot_2x(a_exact, b):
    bh, bl = _split2(b)
    return _dot(a_exact, bh) + _dot(a_exact, bl)


def _rwkv_kernel(pb_ref, mu_ref, w0_ref, wd_ref, a0_ref, wa_ref, wg_ref, kk_ref, ka_ref,
                 rk_ref, lng_ref, lnb_ref, o_ref, prev_ref, h_ref):
    t = pl.program_id(1)

    @pl.when(t == 0)
    def _():
        prev_ref[...] = jnp.zeros_like(prev_ref)
        h_ref[...] = jnp.zeros_like(h_ref)

    L = CHUNK
    W = B_WIDTH
    TS = RWKV_TS
    p = pb_ref[...]
    rows = lax.broadcasted_iota(jnp.int32, (TS, 1), 0)
    shifted = jnp.where(rows == 0, prev_ref[...], pltpu.roll(p, 1, axis=0))
    prev_ref[...] = p[TS - 1:TS, :]
    pm = p + (shifted - p) * mu_ref[...]
    r = pm[:, 0:W]
    k = pm[:, W:2 * W]
    v = pm[:, 2 * W:3 * W]
    xwa = pm[:, 3 * W:3 * W + DECAY_RANK + ICLR_RANK]
    xg = pm[:, 3 * W + DECAY_RANK + ICLR_RANK:]

    dw = w0_ref[...] + _dot(jnp.tanh(xwa).astype(BF16), wd_ref[...])
    lw = -jnp.exp(-_softplus(-dw) - 0.5)
    a = _sigmoid(a0_ref[...] + _dot(xwa.astype(BF16), wa_ref[...]))
    g = _dot(_sigmoid(xg).astype(BF16), wg_ref[...])

    hr = lax.broadcasted_iota(jnp.int32, (W, W), 0) // HEAD_DIM
    hc = lax.broadcasted_iota(jnp.int32, (W, W), 1) // HEAD_DIM
    same_head = hr == hc
    diag_w = (lax.broadcasted_iota(jnp.int32, (W, W), 0)
              == lax.broadcasted_iota(jnp.int32, (W, W), 1))
    head_ones = jnp.where(same_head, 1.0, 0.0).astype(BF16)
    kk = k * kk_ref[...]
    kk = kk * lax.rsqrt(jnp.maximum(_dot_x2(kk * kk, head_ones), 1e-24))
    k2 = k * (1.0 + (a - 1.0) * ka_ref[...])
    bonus = _dot_x2(r * k2 * rk_ref[...], head_ones) * v
    aa = -kk
    bb = kk * a

    ti = lax.broadcasted_iota(jnp.int32, (L, L), 0)
    tj = lax.broadcasted_iota(jnp.int32, (L, L), 1)
    lower = jnp.where(ti >= tj, 1.0, 0.0).astype(BF16)
    eye = jnp.where(ti == tj, 1.0, 0.0)
    lane_head = lax.broadcasted_iota(jnp.int32, (L, W), 1) // HEAD_DIM

    def only(x, h):
        return jnp.where(lane_head == h, x, 0.0)

    nch = TS // L
    chunks = range(nch)
    units = [(c, h) for c in chunks for h in range(B_HEADS)]
    hsl = [slice(h * L, (h + 1) * L) for h in range(B_HEADS)]
    csl = [slice(c * L, (c + 1) * L) for c in chunks]
    v_c = [v[s] for s in csl]
    cum = [_dot_2x(lower, lw[s]) for s in csl]
    cum_l = [x[L - 1:L, :] for x in cum]
    at = [aa[csl[c]] * jnp.exp(cum[c] - lw[csl[c]]) for c in chunks]
    rt = [r[csl[c]] * jnp.exp(cum[c]) for c in chunks]
    inv = [jnp.exp(-x) for x in cum]
    bt = [bb[csl[c]] * inv[c] for c in chunks]
    kt = [k2[csl[c]] * inv[c] for c in chunks]
    tail = [jnp.exp(cum_l[c] - cum[c]) for c in chunks]
    bh = [bb[csl[c]] * tail[c] for c in chunks]
    kh = [k2[csl[c]] * tail[c] for c in chunks]

    at_s = [jnp.concatenate([only(x, h) for h in range(B_HEADS)], axis=0) for x in at]
    rt_s = [jnp.concatenate([only(x, h) for h in range(B_HEADS)], axis=0) for x in rt]
    ab = _dot1_many([(at_s[c], bt[c]) for c in chunks], nt=True)
    ak = _dot1_many([(at_s[c], kt[c]) for c in chunks], nt=True)
    rb = _dot1_many([(rt_s[c], bt[c]) for c in chunks], nt=True)
    rk = _dot1_many([(rt_s[c], kt[c]) for c in chunks], nt=True)

    a_low = [jnp.where(ti > tj, ab[c][hsl[h]], 0.0) for c, h in units]
    pw = [jnp.where(ti // INV_BASE == tj // INV_BASE, x, 0.0) for x in a_low]
    tinv = [eye + x for x in pw]
    for _ in range(int(math.log2(INV_BASE)) - 1):
        pw = _dot1_many([(x, x) for x in pw])
        tinv = [t + d for t, d in zip(tinv, _dot1_many(list(zip(tinv, pw))))]
    blk = INV_BASE
    while blk < L:
        pair = jnp.logical_and(ti // (2 * blk) == tj // (2 * blk), ti // blk != tj // blk)
        right = _dot1_many([(jnp.where(pair, x, 0.0), t) for x, t in zip(a_low, tinv)])
        tinv = [t + d for t, d in zip(tinv, _dot1_many(list(zip(tinv, right))))]
        blk *= 2
    ak_v = _dot1_many([(jnp.where(ti > tj, ak[c][hsl[h]], 0.0), v_c[c]) for c, h in units])
    w_u = _dot1_many([(tinv[i], only(at[c], h)) for i, (c, h) in enumerate(units)])
    u0_u = [only(x, h) for x, (c, h) in zip(_dot1_many(list(zip(tinv, ak_v))), units)]
    rb_l = [jnp.where(ti >= tj, rb[c][hsl[h]], 0.0) for c, h in units]
    rk_l = [jnp.where(ti >= tj, rk[c][hsl[h]], 0.0) for c, h in units]
    qm_u = _dot1_many(list(zip(rb_l, w_u)))
    y0_u = [only(p + q, h) for p, q, (c, h) in zip(
        _dot1_many(list(zip(rb_l, u0_u))),
        _dot1_many([(rk_l[i], v_c[c]) for i, (c, h) in enumerate(units)]), units)]

    def chunk_sum(xs, c):
        return functools.reduce(lambda p, q: p + q, xs[c * B_HEADS:(c + 1) * B_HEADS])

    w_sum = [chunk_sum(w_u, c) for c in chunks]
    u0 = [chunk_sum(u0_u, c) for c in chunks]
    qm = [rt[c] + chunk_sum(qm_u, c) for c in chunks]
    y0 = [chunk_sum(y0_u, c) for c in chunks]
    bw = _dot1_many([(bh[c].T, w_sum[c]) for c in chunks])
    g_mat = [jnp.where(same_head, bw[c], 0.0) + jnp.where(diag_w, jnp.exp(cum_l[c]).T, 0.0)
             for c in chunks]
    c_mat = [jnp.where(same_head, x, 0.0) for x in _dot1_many(
        [(jnp.concatenate([bh[c], kh[c]], axis=0).T, jnp.concatenate([u0[c], v_c[c]], axis=0))
         for c in chunks])]

    hst = h_ref[...]
    ys = []
    for c in chunks:
        ys.append(_dot1(qm[c], hst) + y0[c])
        hst = _dot1(g_mat[c], hst) + c_mat[c]
    h_ref[...] = hst
    y = jnp.concatenate(ys, axis=0)

    mean = _dot_x2(y, head_ones) * (1.0 / HEAD_DIM)
    yc = y - mean
    var = _dot_x2(yc * yc, head_ones) * (1.0 / HEAD_DIM)
    yn = yc * lax.rsqrt(var + GN_EPS) * lng_ref[...] + lnb_ref[...]
    o_ref[...] = ((yn + bonus) * g).astype(o_ref.dtype)


def _rwkv_call(pb, mu, w0, wd_pad, a0, wa_pad, wg, k_k, k_a, r_k, lnx_g, lnx_b):
    bsz, s, _ = pb.shape
    vec = lambda x: x.reshape(1, -1).astype(F32)
    small = [vec(mu), vec(w0), wd_pad, vec(a0), wa_pad, wg, vec(k_k), vec(k_a), vec(r_k),
             vec(lnx_g), vec(lnx_b)]
    return pl.pallas_call(
        _rwkv_kernel,
        grid=(bsz, s // RWKV_TS),
        in_specs=[pl.BlockSpec((None, RWKV_TS, B_COLS), lambda b, t: (b, t, 0))]
        + [pl.BlockSpec(x.shape, lambda b, t: (0, 0)) for x in small],
        out_specs=pl.BlockSpec((None, RWKV_TS, B_WIDTH), lambda b, t: (b, t, 0)),
        out_shape=jax.ShapeDtypeStruct((bsz, s, B_WIDTH), BF16),
        scratch_shapes=[pltpu.VMEM((1, B_COLS), F32), pltpu.VMEM((B_WIDTH, B_WIDTH), F32)],
        compiler_params=_params("parallel", "arbitrary"),
        name="rwkv7",
    )(pb, *small)


def _softplus2(z):
    return jnp.maximum(z, 0.0) + jnp.log2(1.0 + jnp.exp2(-jnp.abs(z)))


def _sb_kernel(q_ref, k_ref, vt_ref, o_ref, z_sc, lb_sc, e_sc, w_sc, acc_sc):
    it = pl.program_id(2)
    q2 = q_ref[...]
    lane = lax.broadcasted_iota(jnp.int32, (SB_TQ, LANES), 1)
    zero = jnp.zeros_like(q2)
    qh = (jnp.where(lane < HEAD_DIM, q2, zero), jnp.where(lane >= HEAD_DIM, q2, zero))
    ur = lax.broadcasted_iota(jnp.int32, (BLOCK, BLOCK), 0)
    uc = lax.broadcasted_iota(jnp.int32, (BLOCK, BLOCK), 1)
    upper = jnp.where(uc > ur, 1.0, 0.0).astype(BF16)
    nsub = SB_TQ // BLOCK
    first = it * nsub + nsub - 1
    nblk = first + 1

    def key_off(n):
        return pl.multiple_of(jnp.clip(first - n, 0, first) * BLOCK, BLOCK)

    def logits(n):
        kblk = k_ref[pl.ds(key_off(n), BLOCK), :]
        return jnp.concatenate([_dot_nt(kblk, qh[0]), _dot_nt(kblk, qh[1])], axis=1)

    def softplus_stage(zt):
        sp = _softplus2(zt)
        return zt - sp, sp.astype(BF16), jnp.sum(sp, axis=0, keepdims=True)

    def value_stage(n):
        vt2 = vt_ref[:, pl.ds(key_off(n), BLOCK)]
        w = w_sc[...]
        for h in range(2):
            cols = slice(h * SB_TQ, (h + 1) * SB_TQ)
            acc_sc[:, cols] += _dot(vt2[h * HEAD_DIM:(h + 1) * HEAD_DIM], w[:, cols])

    def step(n, carry):
        car, colsum = carry
        value_stage(n - 1)
        z_new = logits(n + 2)
        lb_new, sp16, colsum_new = softplus_stage(z_sc[...])
        e_new = _dot(upper, sp16)
        w_sc[...] = jnp.exp2(lb_sc[...] - e_sc[...] - car).astype(BF16)
        z_sc[...] = z_new
        lb_sc[...] = lb_new
        e_sc[...] = e_new
        return car + colsum, colsum_new

    def lane_off(n):
        return max(0, nsub - 1 - n) * BLOCK

    def near_logits(n):
        kblk = k_ref[pl.ds(key_off(n), BLOCK), :]
        return [_dot_nt(kblk, qh[h][lane_off(n):]) for h in range(2)]

    def near_softplus(zt, n):
        width = SB_TQ - lane_off(n)
        sp = _softplus2(zt)
        lb = zt - sp
        if n < nsub:
            kr = lax.broadcasted_iota(jnp.int32, (BLOCK, width), 0)
            qc = lax.broadcasted_iota(jnp.int32, (BLOCK, width), 1)
            before = kr < qc
            sp = jnp.where(before, sp, 0.0)
            lb = jnp.where(before, lb, MASKED)
        colsum = jnp.sum(sp, axis=0, keepdims=True)
        if lane_off(n):
            colsum = jnp.concatenate([jnp.zeros((1, lane_off(n)), F32), colsum], axis=1)
        return lb, sp.astype(BF16), colsum

    def near_values(n, ws):
        vt2 = vt_ref[:, pl.ds(key_off(n), BLOCK)]
        if n >= nsub:
            vt2 = jnp.where(n < nblk, vt2, jnp.zeros_like(vt2))
        for h in range(2):
            cols = slice(h * SB_TQ + lane_off(n), (h + 1) * SB_TQ)
            acc_sc[:, cols] += _dot(vt2[h * HEAD_DIM:(h + 1) * HEAD_DIM], ws[h])

    acc_sc[...] = jnp.zeros_like(acc_sc)
    zs = {0: near_logits(0), 1: near_logits(1)}
    soft = {0: [near_softplus(zs[0][h], 0) for h in range(2)]}
    excl = {0: [_dot(upper, soft[0][h][1]) for h in range(2)]}
    cars = [jnp.zeros((1, SB_TQ), F32) for _ in range(2)]
    ws = None
    for n in range(SB_NEAR):
        if n >= 1:
            near_values(n - 1, ws)
        if n + 2 < SB_NEAR:
            zs[n + 2] = near_logits(n + 2)
        if n + 1 < SB_NEAR:
            soft[n + 1] = [near_softplus(zs[n + 1][h], n + 1) for h in range(2)]
            excl[n + 1] = [_dot(upper, soft[n + 1][h][1]) for h in range(2)]
        ws = [jnp.exp2(soft[n][h][0] - excl[n][h] - cars[h][:, lane_off(n):]).astype(BF16)
              for h in range(2)]
        cars = [cars[h] + soft[n][h][2] for h in range(2)]
    near_values(SB_NEAR - 1, ws)
    car = jnp.concatenate(cars, axis=1)

    @pl.when(jnp.logical_and(nblk > SB_NEAR, jnp.min(car) < SB_DEAD))
    def _():
        lb0, sp160, colsum0 = softplus_stage(logits(SB_NEAR))
        lb_sc[...] = lb0
        e_sc[...] = _dot(upper, sp160)
        z_sc[...] = logits(SB_NEAR + 1)
        w_sc[...] = jnp.zeros_like(w_sc)

        def live(c):
            n, _, _, alive = c
            return jnp.logical_and(n < nblk, alive > 0)

        def visit(c):
            n, car, colsum, _ = c
            alive = (jnp.min(car) < SB_DEAD).astype(jnp.int32)
            car, colsum = step(n, (car, colsum))
            return n + jnp.int32(1), car, colsum, alive

        n_stop = lax.while_loop(live, visit, (jnp.int32(SB_NEAR), car, colsum0, jnp.int32(1)))[0]
        value_stage(n_stop - 1)

    acc = acc_sc[...]
    out_t = jnp.concatenate([acc[:, :SB_TQ], acc[:, SB_TQ:]], axis=0)
    o_ref[...] = out_t.T.astype(o_ref.dtype)


def _sb_call(qc, kc, vt):
    bsz, s, _ = qc.shape
    return pl.pallas_call(
        _sb_kernel,
        grid=(bsz, C_WIDTH // LANES, s // SB_TQ),
        in_specs=[pl.BlockSpec((None, SB_TQ, LANES), lambda b, hp, i: (b, i, hp)),
                  pl.BlockSpec((None, s, LANES), lambda b, hp, i: (b, 0, hp)),
                  pl.BlockSpec((None, LANES, s), lambda b, hp, i: (b, hp, 0))],
        out_specs=pl.BlockSpec((None, SB_TQ, LANES), lambda b, hp, i: (b, i, hp)),
        out_shape=jax.ShapeDtypeStruct((bsz, s, C_WIDTH), BF16),
        scratch_shapes=[pltpu.VMEM((BLOCK, 2 * SB_TQ), F32),
                        pltpu.VMEM((BLOCK, 2 * SB_TQ), F32),
                        pltpu.VMEM((BLOCK, 2 * SB_TQ), F32),
                        pltpu.VMEM((BLOCK, 2 * SB_TQ), BF16),
                        pltpu.VMEM((HEAD_DIM, 2 * SB_TQ), F32)],
        compiler_params=_params("parallel", "parallel", "parallel"),
        name="stickbreak_attn",
    )(qc, kc, vt)


def _outproj_kernel(h_ref, oa_ref, ob_ref, oc_ref, wa_ref, wb_ref, wc_ref, g_ref, b_ref,
                    rw_ref, rb_ref, o_ref, gate_ref):
    tm = h_ref.shape[0]
    parts = [slice(i * tm // OUTPROJ_PARTS, (i + 1) * tm // OUTPROJ_PARTS)
             for i in range(OUTPROJ_PARTS)]
    ms = [_dot(oa_ref[s], wa_ref[...]) + _dot(ob_ref[s], wb_ref[...]) + _dot(oc_ref[s], wc_ref[...])
          for s in parts]
    h1s = [_layer_norm(ALPHA * h_ref[s] + m, g_ref[...], b_ref[...]) for s, m in zip(parts, ms)]
    for s, h1 in zip(parts, h1s):
        o_ref[s] = h1
    logits = [_dot3(h1, rw_ref[...]) for h1 in h1s]
    for s, x in zip(parts, logits):
        gate_ref[:, s] = _route(x.T[:rb_ref.shape[0]], rb_ref[...])


def _outproj_call(h, oa, ob, oc, wa, wb, wc, g, b, rw, rb_col, tm=1024):
    bsz, s, d = h.shape
    nt = s // tm
    rows = rb_col.shape[0]
    row = lambda c: pl.BlockSpec((None, tm, c), lambda bb, t: (bb, t, 0))
    full = lambda w: pl.BlockSpec(w.shape, lambda bb, t: (0, 0))
    g2, b2 = g.reshape(1, d), b.reshape(1, d)
    return pl.pallas_call(
        _outproj_kernel,
        grid=(bsz, nt),
        in_specs=[row(d), row(A_WIDTH), row(B_WIDTH), row(C_WIDTH),
                  full(wa), full(wb), full(wc), full(g2), full(b2), full(rw), full(rb_col)],
        out_specs=[row(d), pl.BlockSpec((rows, tm), lambda bb, t: (0, bb * nt + t))],
        out_shape=[jax.ShapeDtypeStruct((bsz, s, d), F32),
                   jax.ShapeDtypeStruct((rows, bsz * s), F32)],
        compiler_params=_params("parallel", "parallel"),
        name="out_proj_ln_router",
    )(h, oa, ob, oc, wa, wb, wc, g2, b2, rw, rb_col)


def _route(logits, rb):
    scores = _sigmoid(logits)
    sel = scores + rb
    R = ROUTER_ROWS
    s = [sel[m * R:(m + 1) * R] for m in range(EXPERTS_PER_GROUP)]
    sc = [scores[m * R:(m + 1) * R] for m in range(EXPERTS_PER_GROUP)]
    hi01, lo01 = jnp.maximum(s[0], s[1]), jnp.minimum(s[0], s[1])
    hi23, lo23 = jnp.maximum(s[2], s[3]), jnp.minimum(s[2], s[3])
    top1 = jnp.maximum(hi01, hi23)
    top2 = jnp.maximum(jnp.minimum(hi01, hi23), jnp.maximum(lo01, lo23))
    gscore = top1 + top2
    gi = lax.broadcasted_iota(jnp.int32, gscore.shape, 0)
    gmax = jnp.max(gscore, axis=0, keepdims=True)
    best = jnp.min(jnp.where(gscore == gmax, gi, R), axis=0, keepdims=True)
    in_group = gi == best
    picked = []
    for m in range(EXPERTS_PER_GROUP):
        rank = jnp.zeros(gscore.shape, jnp.int32)
        for j in range(EXPERTS_PER_GROUP):
            if j == m:
                continue
            ahead = (s[j] >= s[m]) if j < m else (s[j] > s[m])
            rank = rank + jnp.where(ahead, 1, 0)
        picked.append(jnp.where(jnp.logical_and(in_group, rank < 2), sc[m], 0.0))
    denom = jnp.sum(picked[0] + picked[1] + picked[2] + picked[3], axis=0, keepdims=True)
    return jnp.concatenate([x / denom for x in picked], axis=0)


def _moe_kernel(x_ref, gate_ref, wgu_ref, wd_ref, g_ref, b_ref, o_ref, xb_ref, acc_ref):
    j = pl.program_id(1)

    @pl.when(j == 0)
    def _():
        xb_ref[...] = x_ref[...].astype(BF16)
        acc_ref[...] = jnp.zeros_like(acc_ref)

    xb = xb_ref[...]
    gates = gate_ref[...]
    lane = lax.broadcasted_iota(jnp.int32, gates.shape, 1)
    f = wd_ref.shape[1]
    y = None
    for m in range(MOE_EPS):
        h2 = _dot(xb, wgu_ref[m])
        hg, hu = h2[:, :f], h2[:, f:]
        gcol = jnp.sum(jnp.where(lane == j * MOE_EPS + m, gates, 0.0), axis=-1, keepdims=True)
        act = (hg * _sigmoid(hg)) * hu * gcol
        part = _dot(act.astype(BF16), wd_ref[m])
        y = part if y is None else y + part
    acc_ref[...] += y

    @pl.when(j == pl.num_programs(1) - 1)
    def _():
        o_ref[...] = _layer_norm(ALPHA * x_ref[...] + acc_ref[...], g_ref[...], b_ref[...])


def _moe_call(x, gates, wgu, wd, g, b, tm=1024):
    n, d = x.shape
    f2 = wgu.shape[-1]
    g2, b2 = g.reshape(1, d), b.reshape(1, d)
    return pl.pallas_call(
        _moe_kernel,
        grid=(n // tm, N_EXPERTS // MOE_EPS),
        in_specs=[pl.BlockSpec((tm, d), lambda i, j: (i, 0)),
                  pl.BlockSpec((tm, N_EXPERTS), lambda i, j: (i, 0)),
                  pl.BlockSpec((MOE_EPS, d, f2), lambda i, j: (j, 0, 0)),
                  pl.BlockSpec((MOE_EPS, f2 // 2, d), lambda i, j: (j, 0, 0)),
                  pl.BlockSpec((1, d), lambda i, j: (0, 0)),
                  pl.BlockSpec((1, d), lambda i, j: (0, 0))],
        out_specs=pl.BlockSpec((tm, d), lambda i, j: (i, 0)),
        out_shape=jax.ShapeDtypeStruct((n, d), F32),
        scratch_shapes=[pltpu.VMEM((tm, d), BF16), pltpu.VMEM((tm, d), F32)],
        compiler_params=_params("parallel", "arbitrary"),
        name="moe_experts_ln",
    )(x, gates, wgu, wd, g2, b2)


def _pair_heads(x, axis):
    shape = x.shape
    x = x.reshape(shape[:axis] + (A_KV_HEADS, A_Q_HEADS // A_KV_HEADS, HEAD_DIM) + shape[axis + 1:])
    x = jnp.swapaxes(x, axis, axis + 1)
    return x.reshape(shape)


def _router_layout(router_w, router_bias):
    d = router_w.shape[0]
    w = router_w.astype(F32).T.reshape(N_GROUPS, EXPERTS_PER_GROUP, d).transpose(1, 0, 2)
    w = jnp.pad(w, ((0, 0), (0, ROUTER_ROWS - N_GROUPS), (0, 0)))
    b = router_bias.astype(F32).reshape(N_GROUPS, EXPERTS_PER_GROUP).T
    b = jnp.pad(b, ((0, 0), (0, ROUTER_ROWS - N_GROUPS)), constant_values=MASKED)
    rows = EXPERTS_PER_GROUP * ROUTER_ROWS
    w = jnp.pad(w.reshape(rows, d).T, ((0, 0), (0, LANES - rows)))
    return w, b.reshape(rows, 1)


def _gates_from_router(gates_t):
    n = gates_t.shape[1]
    g = gates_t.reshape(EXPERTS_PER_GROUP, ROUTER_ROWS, n)[:, :N_GROUPS]
    return g.transpose(2, 1, 0).reshape(n, N_EXPERTS)


def kernel(x, ln0_g, ln0_b, w_in, w_out, sinks, rel_bias, shift_mu, decay_w0, decay_up, iclr_a0,
           iclr_up, gate_up, k_k, k_a, r_k, lnx_g, lnx_b, ln1_g, ln1_b, router_w, router_bias,
           w_gate, w_up, w_down, ln2_g, ln2_b):
    bsz, s, d = x.shape
    n = bsz * s
    bias_pairs = _swa_bias_pairs(rel_bias)
    rw, rb_col = _router_layout(router_w, router_bias)
    zeros_lora = jnp.zeros((ICLR_RANK, B_WIDTH), F32)

    h = x
    for l in range(DEPTH):
        wl = w_in[l]
        wa = jnp.concatenate([_pair_heads(wl[:, :A_WIDTH], 1), wl[:, A_WIDTH:A_COLS]], axis=1)
        wb = wl[:, A_COLS:A_COLS + B_COLS]
        c0 = A_COLS + B_COLS
        wq, wk, wv = (wl[:, c0 + j * C_WIDTH:c0 + (j + 1) * C_WIDTH] for j in range(3))
        weights = [wa.astype(BF16), wb.astype(BF16), wq.astype(BF16), wk.astype(BF16),
                   wv.T.astype(BF16)]
        if l == 0:
            pa, pb, qc, kc, vt, h = _proj_call(h, weights, embed_ln=(ln0_g, ln0_b))
        else:
            pa, pb, qc, kc, vt = _proj_call(h, weights)

        out_a = _swa_call(pa, sinks[l].astype(F32), bias_pairs)
        wd_pad = jnp.concatenate([decay_up[l].astype(F32), zeros_lora], axis=0).astype(BF16)
        wa_pad = jnp.concatenate([zeros_lora, iclr_up[l].astype(F32)], axis=0).astype(BF16)
        out_b = _rwkv_call(pb, shift_mu[l], decay_w0[l], wd_pad, iclr_a0[l], wa_pad,
                           gate_up[l].astype(BF16), k_k[l], k_a[l], r_k[l], lnx_g[l], lnx_b[l])
        out_c = _sb_call(qc, kc, vt)

        wo = w_out[l]
        h, gates_t = _outproj_call(h, out_a, out_b, out_c,
                                   _pair_heads(wo[:A_WIDTH], 0).astype(BF16),
                                   wo[A_WIDTH:A_WIDTH + B_WIDTH].astype(BF16),
                                   wo[A_WIDTH + B_WIDTH:].astype(BF16), ln1_g[l], ln1_b[l],
                                   rw, rb_col)

        wgu = jnp.concatenate([w_gate[l], w_up[l]], axis=-1).astype(BF16)
        hf = _moe_call(h.reshape(n, d), _gates_from_router(gates_t), wgu,
                       w_down[l].astype(BF16), ln2_g[l], ln2_b[l])
        h = hf.reshape(bsz, s, d)
    return h
```

```python
import functools
import math

import jax
import jax.numpy as jnp
from jax import lax
from jax.experimental import pallas as pl
from jax.experimental.pallas import tpu as pltpu

F32 = jnp.float32
BF16 = jnp.bfloat16
HI = lax.Precision.HIGHEST

DEPTH = 2
HEAD_DIM = 64
BLOCK = 128
LANES = 128
A_Q_HEADS = 6
A_KV_HEADS = 2
WINDOW = 128
A_WIDTH = A_Q_HEADS * HEAD_DIM
A_KV_WIDTH = A_KV_HEADS * HEAD_DIM
B_HEADS = 4
B_WIDTH = B_HEADS * HEAD_DIM
DECAY_RANK = 64
ICLR_RANK = 64
GATE_RANK = 128
GN_EPS = 64e-5
C_HEADS = 6
C_WIDTH = C_HEADS * HEAD_DIM
A_COLS = A_WIDTH + 2 * A_KV_WIDTH
B_COLS = 3 * B_WIDTH + DECAY_RANK + ICLR_RANK + GATE_RANK
NUM_BUCKETS = 32
MAX_EXACT = NUM_BUCKETS // 2
MAX_DISTANCE = 128
N_EXPERTS = 16
N_GROUPS = 4
EXPERTS_PER_GROUP = N_EXPERTS // N_GROUPS
D_FF_EXPERT = 256
LN_EPS = 1e-5
ALPHA = (2 * DEPTH) ** 0.25
SCALE = HEAD_DIM ** -0.5
MASKED = -1e30
SWA_QB = 4
CHUNK = 64
INV_BASE = 8
RWKV_TS = 512
SB_TQ = 512
SB_NEAR = 6
SB_DEAD = 151.0
LOG2E = 1.4426950408889634
ROUTER_ROWS = 8
MOE_EPS = 4
OUTPROJ_PARTS = 4

VMEM_LIMIT = 48 * 1024 * 1024


def _dot(a, b, prec=None):
    return jnp.dot(a, b, preferred_element_type=F32, precision=prec)


def _dot_nt(a, b, prec=None):
    return lax.dot_general(a, b, (((1,), (1,)), ((), ())),
                           preferred_element_type=F32, precision=prec)


def _sigmoid(x):
    return 1.0 / (1.0 + jnp.exp(-x))


def _softplus(x):
    return jnp.maximum(x, 0.0) + jnp.log(1.0 + jnp.exp(-jnp.abs(x)))


def _layer_norm(x, g, b):
    mu = jnp.mean(x, axis=-1, keepdims=True)
    xc = x - mu
    var = jnp.mean(xc * xc, axis=-1, keepdims=True)
    return xc * lax.rsqrt(var + LN_EPS) * g + b


def _params(*sem):
    return pltpu.CompilerParams(dimension_semantics=sem, vmem_limit_bytes=VMEM_LIMIT)


def _proj_body(h, wa_ref, wb_ref, wq_ref, wk_ref, wvt_ref, pa_ref, pb_ref, qc_ref, kc_ref, vt_ref):
    hb = h.astype(BF16)
    pa_ref[...] = _dot(hb, wa_ref[...]).astype(BF16)
    pb_ref[...] = _dot(hb, wb_ref[...])
    qc_ref[...] = (_dot(hb, wq_ref[...]) * (SCALE * LOG2E)).astype(BF16)
    kc_ref[...] = _dot(hb, wk_ref[...]).astype(BF16)
    vt_ref[...] = _dot_nt(wvt_ref[...], hb).astype(BF16)


def _proj_kernel(h_ref, *refs):
    _proj_body(h_ref[...], *refs)


def _embed_proj_kernel(x_ref, g_ref, b_ref, *refs):
    h = _layer_norm(x_ref[...], g_ref[...], b_ref[...])
    refs[-1][...] = h
    _proj_body(h, *refs[:-1])


def _proj_call(h, weights, embed_ln=None, tm=512):
    bsz, s, d = h.shape
    full = lambda w: pl.BlockSpec(w.shape, lambda b, t: (0, 0))
    row = lambda c: pl.BlockSpec((None, tm, c), lambda b, t: (b, t, 0))
    in_specs = [row(d)] + [full(w) for w in weights]
    out_specs = [row(A_COLS), row(B_COLS), row(C_WIDTH), row(C_WIDTH),
                 pl.BlockSpec((None, C_WIDTH, tm), lambda b, t: (b, 0, t))]
    out_shape = [jax.ShapeDtypeStruct((bsz, s, A_COLS), BF16),
                 jax.ShapeDtypeStruct((bsz, s, B_COLS), F32),
                 jax.ShapeDtypeStruct((bsz, s, C_WIDTH), BF16),
                 jax.ShapeDtypeStruct((bsz, s, C_WIDTH), BF16),
                 jax.ShapeDtypeStruct((bsz, C_WIDTH, s), BF16)]
    args = [h] + list(weights)
    if embed_ln is not None:
        vecs = [v.reshape(1, d) for v in embed_ln]
        in_specs[1:1] = [full(v) for v in vecs]
        args[1:1] = vecs
        out_specs.append(row(d))
        out_shape.append(jax.ShapeDtypeStruct((bsz, s, d), F32))
    return pl.pallas_call(
        _proj_kernel if embed_ln is None else _embed_proj_kernel,
        grid=(bsz, s // tm),
        in_specs=in_specs,
        out_specs=out_specs,
        out_shape=out_shape,
        compiler_params=_params("parallel", "parallel"),
        name="in_proj" if embed_ln is None else "embed_ln_in_proj",
    )(*args)


def _swa_kernel(sink_ref, q_ref, kp_ref, kc_ref, vp_ref, vc_ref, bias_ref, o_ref):
    n = pl.program_id(1)
    kall = jnp.concatenate([kp_ref[...], kc_ref[...]], axis=0)
    vall = jnp.concatenate([vp_ref[...], vc_ref[...]], axis=0)
    lane = lax.broadcasted_iota(jnp.int32, (BLOCK, LANES), 1)
    row2 = lax.broadcasted_iota(jnp.int32, (2 * BLOCK, 1), 0)
    col2 = lax.broadcasted_iota(jnp.int32, (1, 2 * BLOCK), 1)
    pad = jnp.where(jnp.logical_and(n == 0, col2 < BLOCK), MASKED, 0.0)
    units = [(j, c) for j in range(SWA_QB) for c in range(A_Q_HEADS // 2)]
    logits = []
    for j, c in units:
        q2 = q_ref[j * BLOCK:(j + 1) * BLOCK, c * LANES:(c + 1) * LANES]
        zero = jnp.zeros_like(q2)
        qs = jnp.concatenate([jnp.where(lane < HEAD_DIM, q2, zero),
                              jnp.where(lane >= HEAD_DIM, q2, zero)], axis=0)
        x = _dot_nt(qs, kall[j * BLOCK:(j + 2) * BLOCK]) * SCALE + bias_ref[c]
        logits.append(x + pad if j == 0 else x)
    probs, denoms = [], []
    for (j, c), x in zip(units, logits):
        sink = jnp.where(row2 < BLOCK, sink_ref[c], sink_ref[c + 3])
        m = jnp.maximum(jnp.max(x, axis=-1, keepdims=True), sink)
        p = jnp.exp(x - m)
        denoms.append(jnp.sum(p, axis=-1, keepdims=True) + jnp.exp(sink - m))
        probs.append(p.astype(BF16))
    for (j, c), p, denom in zip(units, probs, denoms):
        o = _dot(p, vall[j * BLOCK:(j + 2) * BLOCK]) / denom
        o_ref[j * BLOCK:(j + 1) * BLOCK, c * LANES:(c + 1) * LANES] = jnp.where(
            lane < HEAD_DIM, o[:BLOCK], o[BLOCK:]).astype(BF16)


def _swa_call(pa, sinks, bias_pairs):
    bsz, s, _ = pa.shape
    tq = SWA_QB * BLOCK
    kcol = A_WIDTH // LANES
    vcol = kcol + 1
    prev = lambda n: jnp.maximum(n * SWA_QB - 1, 0)
    return pl.pallas_call(
        _swa_kernel,
        grid=(bsz, s // tq),
        in_specs=[pl.BlockSpec(memory_space=pltpu.SMEM),
                  pl.BlockSpec((None, tq, A_WIDTH), lambda b, n: (b, n, 0)),
                  pl.BlockSpec((None, BLOCK, LANES), lambda b, n: (b, prev(n), kcol)),
                  pl.BlockSpec((None, tq, LANES), lambda b, n: (b, n, kcol)),
                  pl.BlockSpec((None, BLOCK, LANES), lambda b, n: (b, prev(n), vcol)),
                  pl.BlockSpec((None, tq, LANES), lambda b, n: (b, n, vcol)),
                  pl.BlockSpec(bias_pairs.shape, lambda b, n: (0, 0, 0))],
        out_specs=pl.BlockSpec((None, tq, A_WIDTH), lambda b, n: (b, n, 0)),
        out_shape=jax.ShapeDtypeStruct((bsz, s, A_WIDTH), BF16),
        compiler_params=_params("parallel", "parallel"),
        name="swa_attn",
    )(sinks, pa, pa, pa, pa, pa, bias_pairs)


def _t5_causal_bucket(dist):
    dist = jnp.maximum(dist, 0)
    d = jnp.maximum(dist, 1).astype(F32)
    large = MAX_EXACT + (jnp.log(d / MAX_EXACT) / math.log(MAX_DISTANCE / MAX_EXACT)
                         * (NUM_BUCKETS - MAX_EXACT)).astype(jnp.int32)
    large = jnp.minimum(large, NUM_BUCKETS - 1)
    return jnp.where(dist < MAX_EXACT, dist, large)


def _swa_bias_pairs(rel_bias):
    nd = 3 * BLOCK - 1
    dist = (2 * BLOCK - 1) - jnp.arange(nd)
    onehot = jax.nn.one_hot(_t5_causal_bucket(dist), NUM_BUCKETS, dtype=F32)
    per_dist = jnp.einsum("db,bh->hd", onehot, rel_bias.astype(F32), precision=HI)
    per_dist = jnp.where((dist >= 0) & (dist < WINDOW), per_dist, MASKED)
    period = nd + 2
    tiled = jnp.tile(jnp.pad(per_dist, ((0, 0), (0, period - nd))), (1, BLOCK))
    rows = tiled[:, :BLOCK * (period - 1)].reshape(A_Q_HEADS, BLOCK, period - 1)
    bias = rows[:, :, BLOCK - 1:3 * BLOCK - 1]
    return jnp.stack([jnp.concatenate([bias[c], bias[c + 3]], axis=0)
                      for c in range(A_Q_HEADS // 2)])


def _split2(x):
    hi = x.astype(BF16)
    return hi, (x - hi.astype(F32)).astype(BF16)


def _dot3(a, b, nt=False):
    ah, al = _split2(a)
    bh, bl = _split2(b)
    d = _dot_nt if nt else _dot
    return d(ah, bh) + d(ah, bl) + d(al, bh)


def _dot1(a, b, nt=False):
    return (_dot_nt if nt else _dot)(a.astype(BF16), b.astype(BF16))


def _dot1_many(pairs, nt=False):
    parts = [(a.astype(BF16), b.astype(BF16)) for a, b in pairs]
    d = _dot_nt if nt else _dot
    return [d(a, b) for a, b in parts]


def _dot_x2(a, b_exact):
    ah, al = _split2(a)
    return _dot(ah, b_exact) + _dot(al, b_exact)


def _dot_2x(a_exact, b):
    bh, bl = _split2(b)
    return _dot(a_exact, bh) + _dot(a_exact, bl)


def _rwkv_kernel(pb_ref, mu_ref, w0_ref, wd_ref, a0_ref, wa_ref, wg_ref, kk_ref, ka_ref,
                 rk_ref, lng_ref, lnb_ref, o_ref, prev_ref, h_ref):
    t = pl.program_id(1)

    @pl.when(t == 0)
    def _():
        prev_ref[...] = jnp.zeros_like(prev_ref)
        h_ref[...] = jnp.zeros_like(h_ref)

    L = CHUNK
    W = B_WIDTH
    TS = RWKV_TS
    p = pb_ref[...]
    rows = lax.broadcasted_iota(jnp.int32, (TS, 1), 0)
    shifted = jnp.where(rows == 0, prev_ref[...], pltpu.roll(p, 1, axis=0))
    prev_ref[...] = p[TS - 1:TS, :]
    pm = p + (shifted - p) * mu_ref[...]
    r = pm[:, 0:W]
    k = pm[:, W:2 * W]
    v = pm[:, 2 * W:3 * W]
    xwa = pm[:, 3 * W:3 * W + DECAY_RANK + ICLR_RANK]
    xg = pm[:, 3 * W + DECAY_RANK + ICLR_RANK:]

    dw = w0_ref[...] + _dot(jnp.tanh(xwa).astype(BF16), wd_ref[...])
    lw = -jnp.exp(-_softplus(-dw) - 0.5)
    a = _sigmoid(a0_ref[...] + _dot(xwa.astype(BF16), wa_ref[...]))
    g = _dot(_sigmoid(xg).astype(BF16), wg_ref[...])

    hr = lax.broadcasted_iota(jnp.int32, (W, W), 0) // HEAD_DIM
    hc = lax.broadcasted_iota(jnp.int32, (W, W), 1) // HEAD_DIM
    same_head = hr == hc
    diag_w = (lax.broadcasted_iota(jnp.int32, (W, W), 0)
              == lax.broadcasted_iota(jnp.int32, (W, W), 1))
    head_ones = jnp.where(same_head, 1.0, 0.0).astype(BF16)
    kk = k * kk_ref[...]
    kk = kk * lax.rsqrt(jnp.maximum(_dot_x2(kk * kk, head_ones), 1e-24))
    k2 = k * (1.0 + (a - 1.0) * ka_ref[...])
    bonus = _dot_x2(r * k2 * rk_ref[...], head_ones) * v
    aa = -kk
    bb = kk * a

    ti = lax.broadcasted_iota(jnp.int32, (L, L), 0)
    tj = lax.broadcasted_iota(jnp.int32, (L, L), 1)
    lower = jnp.where(ti >= tj, 1.0, 0.0).astype(BF16)
    eye = jnp.where(ti == tj, 1.0, 0.0)
    lane_head = lax.broadcasted_iota(jnp.int32, (L, W), 1) // HEAD_DIM

    def only(x, h):
        return jnp.where(lane_head == h, x, 0.0)

    nch = TS // L
    chunks = range(nch)
    units = [(c, h) for c in chunks for h in range(B_HEADS)]
    hsl = [slice(h * L, (h + 1) * L) for h in range(B_HEADS)]
    csl = [slice(c * L, (c + 1) * L) for c in chunks]
    v_c = [v[s] for s in csl]
    cum = [_dot_2x(lower, lw[s]) for s in csl]
    cum_l = [x[L - 1:L, :] for x in cum]
    at = [aa[csl[c]] * jnp.exp(cum[c] - lw[csl[c]]) for c in chunks]
    rt = [r[csl[c]] * jnp.exp(cum[c]) for c in chunks]
    inv = [jnp.exp(-x) for x in cum]
    bt = [bb[csl[c]] * inv[c] for c in chunks]
    kt = [k2[csl[c]] * inv[c] for c in chunks]
    tail = [jnp.exp(cum_l[c] - cum[c]) for c in chunks]
    bh = [bb[csl[c]] * tail[c] for c in chunks]
    kh = [k2[csl[c]] * tail[c] for c in chunks]

    at_s = [jnp.concatenate([only(x, h) for h in range(B_HEADS)], axis=0) for x in at]
    rt_s = [jnp.concatenate([only(x, h) for h in range(B_HEADS)], axis=0) for x in rt]
    ab = _dot1_many([(at_s[c], bt[c]) for c in chunks], nt=True)
    ak = _dot1_many([(at_s[c], kt[c]) for c in chunks], nt=True)
    rb = _dot1_many([(rt_s[c], bt[c]) for c in chunks], nt=True)
    rk = _dot1_many([(rt_s[c], kt[c]) for c in chunks], nt=True)

    a_low = [jnp.where(ti > tj, ab[c][hsl[h]], 0.0) for c, h in units]
    pw = [jnp.where(ti // INV_BASE == tj // INV_BASE, x, 0.0) for x in a_low]
    tinv = [eye + x for x in pw]
    for _ in range(int(math.log2(INV_BASE)) - 1):
        pw = _dot1_many([(x, x) for x in pw])
        tinv = [t + d for t, d in zip(tinv, _dot1_many(list(zip(tinv, pw))))]
    blk = INV_BASE
    while blk < L:
        pair = jnp.logical_and(ti // (2 * blk) == tj // (2 * blk), ti // blk != tj // blk)
        right = _dot1_many([(jnp.where(pair, x, 0.0), t) for x, t in zip(a_low, tinv)])
        tinv = [t + d for t, d in zip(tinv, _dot1_many(list(zip(tinv, right))))]
        blk *= 2
    ak_v = _dot1_many([(jnp.where(ti > tj, ak[c][hsl[h]], 0.0), v_c[c]) for c, h in units])
    w_u = _dot1_many([(tinv[i], only(at[c], h)) for i, (c, h) in enumerate(units)])
    u0_u = [only(x, h) for x, (c, h) in zip(_dot1_many(list(zip(tinv, ak_v))), units)]
    rb_l = [jnp.where(ti >= tj, rb[c][hsl[h]], 0.0) for c, h in units]
    rk_l = [jnp.where(ti >= tj, rk[c][hsl[h]], 0.0) for c, h in units]
    qm_u = _dot1_many(list(zip(rb_l, w_u)))
    y0_u = [only(p + q, h) for p, q, (c, h) in zip(
        _dot1_many(list(zip(rb_l, u0_u))),
        _dot1_many([(rk_l[i], v_c[c]) for i, (c, h) in enumerate(units)]), units)]

    def chunk_sum(xs, c):
        return functools.reduce(lambda p, q: p + q, xs[c * B_HEADS:(c + 1) * B_HEADS])

    w_sum = [chunk_sum(w_u, c) for c in chunks]
    u0 = [chunk_sum(u0_u, c) for c in chunks]
    qm = [rt[c] + chunk_sum(qm_u, c) for c in chunks]
    y0 = [chunk_sum(y0_u, c) for c in chunks]
    bw = _dot1_many([(bh[c].T, w_sum[c]) for c in chunks])
    g_mat = [jnp.where(same_head, bw[c], 0.0) + jnp.where(diag_w, jnp.exp(cum_l[c]).T, 0.0)
             for c in chunks]
    c_mat = [jnp.where(same_head, x, 0.0) for x in _dot1_many(
        [(jnp.concatenate([bh[c], kh[c]], axis=0).T, jnp.concatenate([u0[c], v_c[c]], axis=0))
         for c in chunks])]

    hst = h_ref[...]
    ys = []
    for c in chunks:
        ys.append(_dot1(qm[c], hst) + y0[c])
        hst = _dot1(g_mat[c], hst) + c_mat[c]
    h_ref[...] = hst
    y = jnp.concatenate(ys, axis=0)

    mean = _dot_x2(y, head_ones) * (1.0 / HEAD_DIM)
    yc = y - mean
    var = _dot_x2(yc * yc, head_ones) * (1.0 / HEAD_DIM)
    yn = yc * lax.rsqrt(var + GN_EPS) * lng_ref[...] + lnb_ref[...]
    o_ref[...] = ((yn + bonus) * g).astype(o_ref.dtype)


def _rwkv_call(pb, mu, w0, wd_pad, a0, wa_pad, wg, k_k, k_a, r_k, lnx_g, lnx_b):
    bsz, s, _ = pb.shape
    vec = lambda x: x.reshape(1, -1).astype(F32)
    small = [vec(mu), vec(w0), wd_pad, vec(a0), wa_pad, wg, vec(k_k), vec(k_a), vec(r_k),
             vec(lnx_g), vec(lnx_b)]
    return pl.pallas_call(
        _rwkv_kernel,
        grid=(bsz, s // RWKV_TS),
        in_specs=[pl.BlockSpec((None, RWKV_TS, B_COLS), lambda b, t: (b, t, 0))]
        + [pl.BlockSpec(x.shape, lambda b, t: (0, 0)) for x in small],
        out_specs=pl.BlockSpec((None, RWKV_TS, B_WIDTH), lambda b, t: (b, t, 0)),
        out_shape=jax.ShapeDtypeStruct((bsz, s, B_WIDTH), BF16),
        scratch_shapes=[pltpu.VMEM((1, B_COLS), F32), pltpu.VMEM((B_WIDTH, B_WIDTH), F32)],
        compiler_params=_params("parallel", "arbitrary"),
        name="rwkv7",
    )(pb, *small)


def _softplus2(z):
    return jnp.maximum(z, 0.0) + jnp.log2(1.0 + jnp.exp2(-jnp.abs(z)))


def _sb_kernel(q_ref, k_ref, vt_ref, o_ref, z_sc, lb_sc, e_sc, w_sc, acc_sc):
    it = pl.program_id(2)
    q2 = q_ref[...]
    lane = lax.broadcasted_iota(jnp.int32, (SB_TQ, LANES), 1)
    zero = jnp.zeros_like(q2)
    qh = (jnp.where(lane < HEAD_DIM, q2, zero), jnp.where(lane >= HEAD_DIM, q2, zero))
    ur = lax.broadcasted_iota(jnp.int32, (BLOCK, BLOCK), 0)
    uc = lax.broadcasted_iota(jnp.int32, (BLOCK, BLOCK), 1)
    upper = jnp.where(uc > ur, 1.0, 0.0).astype(BF16)
    nsub = SB_TQ // BLOCK
    first = it * nsub + nsub - 1
    nblk = first + 1

    def key_off(n):
        return pl.multiple_of(jnp.clip(first - n, 0, first) * BLOCK, BLOCK)

    def logits(n):
        kblk = k_ref[pl.ds(key_off(n), BLOCK), :]
        return jnp.concatenate([_dot_nt(kblk, qh[0]), _dot_nt(kblk, qh[1])], axis=1)

    def softplus_stage(zt):
        sp = _softplus2(zt)
        return zt - sp, sp.astype(BF16), jnp.sum(sp, axis=0, keepdims=True)

    def value_stage(n):
        vt2 = vt_ref[:, pl.ds(key_off(n), BLOCK)]
        w = w_sc[...]
        for h in range(2):
            cols = slice(h * SB_TQ, (h + 1) * SB_TQ)
            acc_sc[:, cols] += _dot(vt2[h * HEAD_DIM:(h + 1) * HEAD_DIM], w[:, cols])

    def step(n, carry):
        car, colsum = carry
        value_stage(n - 1)
        z_new = logits(n + 2)
        lb_new, sp16, colsum_new = softplus_stage(z_sc[...])
        e_new = _dot(upper, sp16)
        w_sc[...] = jnp.exp2(lb_sc[...] - e_sc[...] - car).astype(BF16)
        z_sc[...] = z_new
        lb_sc[...] = lb_new
        e_sc[...] = e_new
        return car + colsum, colsum_new

    def lane_off(n):
        return max(0, nsub - 1 - n) * BLOCK

    def near_logits(n):
        kblk = k_ref[pl.ds(key_off(n), BLOCK), :]
        return [_dot_nt(kblk, qh[h][lane_off(n):]) for h in range(2)]

    def near_softplus(zt, n):
        width = SB_TQ - lane_off(n)
        sp = _softplus2(zt)
        lb = zt - sp
        if n < nsub:
            kr = lax.broadcasted_iota(jnp.int32, (BLOCK, width), 0)
            qc = lax.broadcasted_iota(jnp.int32, (BLOCK, width), 1)
            before = kr < qc
            sp = jnp.where(before, sp, 0.0)
            lb = jnp.where(before, lb, MASKED)
        colsum = jnp.sum(sp, axis=0, keepdims=True)
        if lane_off(n):
            colsum = jnp.concatenate([jnp.zeros((1, lane_off(n)), F32), colsum], axis=1)
        return lb, sp.astype(BF16), colsum

    def near_values(n, ws):
        vt2 = vt_ref[:, pl.ds(key_off(n), BLOCK)]
        if n >= nsub:
            vt2 = jnp.where(n < nblk, vt2, jnp.zeros_like(vt2))
        for h in range(2):
            cols = slice(h * SB_TQ + lane_off(n), (h + 1) * SB_TQ)
            acc_sc[:, cols] += _dot(vt2[h * HEAD_DIM:(h + 1) * HEAD_DIM], ws[h])

    acc_sc[...] = jnp.zeros_like(acc_sc)
    zs = {0: near_logits(0), 1: near_logits(1)}
    soft = {0: [near_softplus(zs[0][h], 0) for h in range(2)]}
    excl = {0: [_dot(upper, soft[0][h][1]) for h in range(2)]}
    cars = [jnp.zeros((1, SB_TQ), F32) for _ in range(2)]
    ws = None
    for n in range(SB_NEAR):
        if n >= 1:
            near_values(n - 1, ws)
        if n + 2 < SB_NEAR:
            zs[n + 2] = near_logits(n + 2)
        if n + 1 < SB_NEAR:
            soft[n + 1] = [near_softplus(zs[n + 1][h], n + 1) for h in range(2)]
            excl[n + 1] = [_dot(upper, soft[n + 1][h][1]) for h in range(2)]
        ws = [jnp.exp2(soft[n][h][0] - excl[n][h] - cars[h][:, lane_off(n):]).astype(BF16)
              for h in range(2)]
        cars = [cars[h] + soft[n][h][2] for h in range(2)]
    near_values(SB_NEAR - 1, ws)
    car = jnp.concatenate(cars, axis=1)

    @pl.when(jnp.logical_and(nblk > SB_NEAR, jnp.min(car) < SB_DEAD))
    def _():
        lb0, sp160, colsum0 = softplus_stage(logits(SB_NEAR))
        lb_sc[...] = lb0
        e_sc[...] = _dot(upper, sp160)
        z_sc[...] = logits(SB_NEAR + 1)
        w_sc[...] = jnp.zeros_like(w_sc)

        def live(c):
            n, _, _, alive = c
            return jnp.logical_and(n < nblk, alive > 0)

        def visit(c):
            n, car, colsum, _ = c
            alive = (jnp.min(car) < SB_DEAD).astype(jnp.int32)
            car, colsum = step(n, (car, colsum))
            return n + jnp.int32(1), car, colsum, alive

        n_stop = lax.while_loop(live, visit, (jnp.int32(SB_NEAR), car, colsum0, jnp.int32(1)))[0]
        value_stage(n_stop - 1)

    acc = acc_sc[...]
    out_t = jnp.concatenate([acc[:, :SB_TQ], acc[:, SB_TQ:]], axis=0)
    o_ref[...] = out_t.T.astype(o_ref.dtype)


def _sb_call(qc, kc, vt):
    bsz, s, _ = qc.shape
    return pl.pallas_call(
        _sb_kernel,
        grid=(bsz, C_WIDTH // LANES, s // SB_TQ),
        in_specs=[pl.BlockSpec((None, SB_TQ, LANES), lambda b, hp, i: (b, i, hp)),
                  pl.BlockSpec((None, s, LANES), lambda b, hp, i: (b, 0, hp)),
                  pl.BlockSpec((None, LANES, s), lambda b, hp, i: (b, hp, 0))],
        out_specs=pl.BlockSpec((None, SB_TQ, LANES), lambda b, hp, i: (b, i, hp)),
        out_shape=jax.ShapeDtypeStruct((bsz, s, C_WIDTH), BF16),
        scratch_shapes=[pltpu.VMEM((BLOCK, 2 * SB_TQ), F32),
                        pltpu.VMEM((BLOCK, 2 * SB_TQ), F32),
                        pltpu.VMEM((BLOCK, 2 * SB_TQ), F32),
                        pltpu.VMEM((BLOCK, 2 * SB_TQ), BF16),
                        pltpu.VMEM((HEAD_DIM, 2 * SB_TQ), F32)],
        compiler_params=_params("parallel", "parallel", "parallel"),
        name="stickbreak_attn",
    )(qc, kc, vt)


def _outproj_kernel(h_ref, oa_ref, ob_ref, oc_ref, wa_ref, wb_ref, wc_ref, g_ref, b_ref,
                    rw_ref, rb_ref, o_ref, gate_ref):
    tm = h_ref.shape[0]
    parts = [slice(i * tm // OUTPROJ_PARTS, (i + 1) * tm // OUTPROJ_PARTS)
             for i in range(OUTPROJ_PARTS)]
    ms = [_dot(oa_ref[s], wa_ref[...]) + _dot(ob_ref[s], wb_ref[...]) + _dot(oc_ref[s], wc_ref[...])
          for s in parts]
    h1s = [_layer_norm(ALPHA * h_ref[s] + m, g_ref[...], b_ref[...]) for s, m in zip(parts, ms)]
    for s, h1 in zip(parts, h1s):
        o_ref[s] = h1
    logits = [_dot3(h1, rw_ref[...]) for h1 in h1s]
    for s, x in zip(parts, logits):
        gate_ref[:, s] = _route(x.T[:rb_ref.shape[0]], rb_ref[...])


def _outproj_call(h, oa, ob, oc, wa, wb, wc, g, b, rw, rb_col, tm=1024):
    bsz, s, d = h.shape
    nt = s // tm
    rows = rb_col.shape[0]
    row = lambda c: pl.BlockSpec((None, tm, c), lambda bb, t: (bb, t, 0))
    full = lambda w: pl.BlockSpec(w.shape, lambda bb, t: (0, 0))
    g2, b2 = g.reshape(1, d), b.reshape(1, d)
    return pl.pallas_call(
        _outproj_kernel,
        grid=(bsz, nt),
        in_specs=[row(d), row(A_WIDTH), row(B_WIDTH), row(C_WIDTH),
                  full(wa), full(wb), full(wc), full(g2), full(b2), full(rw), full(rb_col)],
        out_specs=[row(d), pl.BlockSpec((rows, tm), lambda bb, t: (0, bb * nt + t))],
        out_shape=[jax.ShapeDtypeStruct((bsz, s, d), F32),
                   jax.ShapeDtypeStruct((rows, bsz * s), F32)],
        compiler_params=_params("parallel", "parallel"),
        name="out_proj_ln_router",
    )(h, oa, ob, oc, wa, wb, wc, g2, b2, rw, rb_col)


def _route(logits, rb):
    scores = _sigmoid(logits)
    sel = scores + rb
    R = ROUTER_ROWS
    s = [sel[m * R:(m + 1) * R] for m in range(EXPERTS_PER_GROUP)]
    sc = [scores[m * R:(m + 1) * R] for m in range(EXPERTS_PER_GROUP)]
    hi01, lo01 = jnp.maximum(s[0], s[1]), jnp.minimum(s[0], s[1])
    hi23, lo23 = jnp.maximum(s[2], s[3]), jnp.minimum(s[2], s[3])
    top1 = jnp.maximum(hi01, hi23)
    top2 = jnp.maximum(jnp.minimum(hi01, hi23), jnp.maximum(lo01, lo23))
    gscore = top1 + top2
    gi = lax.broadcasted_iota(jnp.int32, gscore.shape, 0)
    gmax = jnp.max(gscore, axis=0, keepdims=True)
    best = jnp.min(jnp.where(gscore == gmax, gi, R), axis=0, keepdims=True)
    in_group = gi == best
    picked = []
    for m in range(EXPERTS_PER_GROUP):
        rank = jnp.zeros(gscore.shape, jnp.int32)
        for j in range(EXPERTS_PER_GROUP):
            if j == m:
                continue
            ahead = (s[j] >= s[m]) if j < m else (s[j] > s[m])
            rank = rank + jnp.where(ahead, 1, 0)
        picked.append(jnp.where(jnp.logical_and(in_group, rank < 2), sc[m], 0.0))
    denom = jnp.sum(picked[0] + picked[1] + picked[2] + picked[3], axis=0, keepdims=True)
    return jnp.concatenate([x / denom for x in picked], axis=0)


def _moe_kernel(x_ref, gate_ref, wgu_ref, wd_ref, g_ref, b_ref, o_ref, xb_ref, acc_ref):
    j = pl.program_id(1)

    @pl.when(j == 0)
    def _():
        xb_ref[...] = x_ref[...].astype(BF16)
        acc_ref[...] = jnp.zeros_like(acc_ref)

    xb = xb_ref[...]
    gates = gate_ref[...]
    lane = lax.broadcasted_iota(jnp.int32, gates.shape, 1)
    f = wd_ref.shape[1]
    y = None
    for m in range(MOE_EPS):
        h2 = _dot(xb, wgu_ref[m])
        hg, hu = h2[:, :f], h2[:, f:]
        gcol = jnp.sum(jnp.where(lane == j * MOE_EPS + m, gates, 0.0), axis=-1, keepdims=True)
        act = (hg * _sigmoid(hg)) * hu * gcol
        part = _dot(act.astype(BF16), wd_ref[m])
        y = part if y is None else y + part
    acc_ref[...] += y

    @pl.when(j == pl.num_programs(1) - 1)
    def _():
        o_ref[...] = _layer_norm(ALPHA * x_ref[...] + acc_ref[...], g_ref[...], b_ref[...])


def _moe_call(x, gates, wgu, wd, g, b, tm=1024):
    n, d = x.shape
    f2 = wgu.shape[-1]
    g2, b2 = g.reshape(1, d), b.reshape(1, d)
    return pl.pallas_call(
        _moe_kernel,
        grid=(n // tm, N_EXPERTS // MOE_EPS),
        in_specs=[pl.BlockSpec((tm, d), lambda i, j: (i, 0)),
                  pl.BlockSpec((tm, N_EXPERTS), lambda i, j: (i, 0)),
                  pl.BlockSpec((MOE_EPS, d, f2), lambda i, j: (j, 0, 0)),
                  pl.BlockSpec((MOE_EPS, f2 // 2, d), lambda i, j: (j, 0, 0)),
                  pl.BlockSpec((1, d), lambda i, j: (0, 0)),
                  pl.BlockSpec((1, d), lambda i, j: (0, 0))],
        out_specs=pl.BlockSpec((tm, d), lambda i, j: (i, 0)),
        out_shape=jax.ShapeDtypeStruct((n, d), F32),
        scratch_shapes=[pltpu.VMEM((tm, d), BF16), pltpu.VMEM((tm, d), F32)],
        compiler_params=_params("parallel", "arbitrary"),
        name="moe_experts_ln",
    )(x, gates, wgu, wd, g2, b2)


def _pair_heads(x, axis):
    shape = x.shape
    x = x.reshape(shape[:axis] + (A_KV_HEADS, A_Q_HEADS // A_KV_HEADS, HEAD_DIM) + shape[axis + 1:])
    x = jnp.swapaxes(x, axis, axis + 1)
    return x.reshape(shape)


def _router_layout(router_w, router_bias):
    d = router_w.shape[0]
    w = router_w.astype(F32).T.reshape(N_GROUPS, EXPERTS_PER_GROUP, d).transpose(1, 0, 2)
    w = jnp.pad(w, ((0, 0), (0, ROUTER_ROWS - N_GROUPS), (0, 0)))
    b = router_bias.astype(F32).reshape(N_GROUPS, EXPERTS_PER_GROUP).T
    b = jnp.pad(b, ((0, 0), (0, ROUTER_ROWS - N_GROUPS)), constant_values=MASKED)
    rows = EXPERTS_PER_GROUP * ROUTER_ROWS
    w = jnp.pad(w.reshape(rows, d).T, ((0, 0), (0, LANES - rows)))
    return w, b.reshape(rows, 1)


def _gates_from_router(gates_t):
    n = gates_t.shape[1]
    g = gates_t.reshape(EXPERTS_PER_GROUP, ROUTER_ROWS, n)[:, :N_GROUPS]
    return g.transpose(2, 1, 0).reshape(n, N_EXPERTS)


def kernel(x, ln0_g, ln0_b, w_in, w_out, sinks, rel_bias, shift_mu, decay_w0, decay_up, iclr_a0,
           iclr_up, gate_up, k_k, k_a, r_k, lnx_g, lnx_b, ln1_g, ln1_b, router_w, router_bias,
           w_gate, w_up, w_down, ln2_g, ln2_b):
    bsz, s, d = x.shape
    n = bsz * s
    bias_pairs = _swa_bias_pairs(rel_bias)
    rw, rb_col = _router_layout(router_w, router_bias)
    zeros_lora = jnp.zeros((ICLR_RANK, B_WIDTH), F32)

    h = x
    for l in range(DEPTH):
        wl = w_in[l]
        wa = jnp.concatenate([_pair_heads(wl[:, :A_WIDTH], 1), wl[:, A_WIDTH:A_COLS]], axis=1)
        wb = wl[:, A_COLS:A_COLS + B_COLS]
        c0 = A_COLS + B_COLS
        wq, wk, wv = (wl[:, c0 + j * C_WIDTH:c0 + (j + 1) * C_WIDTH] for j in range(3))
        weights = [wa.astype(BF16), wb.astype(BF16), wq.astype(BF16), wk.astype(BF16),
                   wv.T.astype(BF16)]
        if l == 0:
            pa, pb, qc, kc, vt, h = _proj_call(h, weights, embed_ln=(ln0_g, ln0_b))
        else:
            pa, pb, qc, kc, vt = _proj_call(h, weights)

        out_a = _swa_call(pa, sinks[l].astype(F32), bias_pairs)
        wd_pad = jnp.concatenate([decay_up[l].astype(F32), zeros_lora], axis=0).astype(BF16)
        wa_pad = jnp.concatenate([zeros_lora, iclr_up[l].astype(F32)], axis=0).astype(BF16)
        out_b = _rwkv_call(pb, shift_mu[l], decay_w0[l], wd_pad, iclr_a0[l], wa_pad,
                           gate_up[l].astype(BF16), k_k[l], k_a[l], r_k[l], lnx_g[l], lnx_b[l])
        out_c = _sb_call(qc, kc, vt)

        wo = w_out[l]
        h, gates_t = _outproj_call(h, out_a, out_b, out_c,
                                   _pair_heads(wo[:A_WIDTH], 0).astype(BF16),
                                   wo[A_WIDTH:A_WIDTH + B_WIDTH].astype(BF16),
                                   wo[A_WIDTH + B_WIDTH:].astype(BF16), ln1_g[l], ln1_b[l],
                                   rw, rb_col)

        wgu = jnp.concatenate([w_gate[l], w_up[l]], axis=-1).astype(BF16)
        hf = _moe_call(h.reshape(n, d), _gates_from_router(gates_t), wgu,
                       w_down[l].astype(BF16), ln2_g[l], ln2_b[l])
        h = hf.reshape(bsz, s, d)
    return h
```

```python
import functools
import math

import jax
import jax.numpy as jnp
from jax import lax
from jax.experimental import pallas as pl
from jax.experimental.pallas import tpu as pltpu

F32 = jnp.float32
BF16 = jnp.bfloat16
HI = lax.Precision.HIGHEST

DEPTH = 2
HEAD_DIM = 64
BLOCK = 128
LANES = 128
A_Q_HEADS = 6
A_KV_HEADS = 2
WINDOW = 128
A_WIDTH = A_Q_HEADS * HEAD_DIM
A_KV_WIDTH = A_KV_HEADS * HEAD_DIM
B_HEADS = 4
B_WIDTH = B_HEADS * HEAD_DIM
DECAY_RANK = 64
ICLR_RANK = 64
GATE_RANK = 128
GN_EPS = 64e-5
C_HEADS = 6
C_WIDTH = C_HEADS * HEAD_DIM
A_COLS = A_WIDTH + 2 * A_KV_WIDTH
B_COLS = 3 * B_WIDTH + DECAY_RANK + ICLR_RANK + GATE_RANK
NUM_BUCKETS = 32
MAX_EXACT = NUM_BUCKETS // 2
MAX_DISTANCE = 128
N_EXPERTS = 16
N_GROUPS = 4
EXPERTS_PER_GROUP = N_EXPERTS // N_GROUPS
D_FF_EXPERT = 256
LN_EPS = 1e-5
ALPHA = (2 * DEPTH) ** 0.25
SCALE = HEAD_DIM ** -0.5
MASKED = -1e30
SWA_QB = 4
CHUNK = 64
INV_BASE = 8
RWKV_TS = 512
SB_TQ = 512
SB_NEAR = 6
SB_DEAD = 151.0
LOG2E = 1.4426950408889634
ROUTER_ROWS = 8
MOE_EPS = 4
OUTPROJ_PARTS = 4

VMEM_LIMIT = 48 * 1024 * 1024


def _dot(a, b, prec=None):
    return jnp.dot(a, b, preferred_element_type=F32, precision=prec)


def _dot_nt(a, b, prec=None):
    return lax.dot_general(a, b, (((1,), (1,)), ((), ())),
                           preferred_element_type=F32, precision=prec)


def _sigmoid(x):
    return 1.0 / (1.0 + jnp.exp(-x))


def _softplus(x):
    return jnp.maximum(x, 0.0) + jnp.log(1.0 + jnp.exp(-jnp.abs(x)))


def _layer_norm(x, g, b):
    mu = jnp.mean(x, axis=-1, keepdims=True)
    xc = x - mu
    var = jnp.mean(xc * xc, axis=-1, keepdims=True)
    return xc * lax.rsqrt(var + LN_EPS) * g + b


def _params(*sem):
    return pltpu.CompilerParams(dimension_semantics=sem, vmem_limit_bytes=VMEM_LIMIT)


def _proj_body(h, wa_ref, wb_ref, wq_ref, wk_ref, wvt_ref, pa_ref, pb_ref, qc_ref, kc_ref, vt_ref):
    hb = h.astype(BF16)
    pa_ref[...] = _dot(hb, wa_ref[...]).astype(BF16)
    pb_ref[...] = _dot(hb, wb_ref[...])
    qc_ref[...] = (_dot(hb, wq_ref[...]) * (SCALE * LOG2E)).astype(BF16)
    kc_ref[...] = _dot(hb, wk_ref[...]).astype(BF16)
    vt_ref[...] = _dot_nt(wvt_ref[...], hb).astype(BF16)


def _proj_kernel(h_ref, *refs):
    _proj_body(h_ref[...], *refs)


def _embed_proj_kernel(x_ref, g_ref, b_ref, *refs):
    h = _layer_norm(x_ref[...], g_ref[...], b_ref[...])
    refs[-1][...] = h
    _proj_body(h, *refs[:-1])


def _proj_call(h, weights, embed_ln=None, tm=512):
    bsz, s, d = h.shape
    full = lambda w: pl.BlockSpec(w.shape, lambda b, t: (0, 0))
    row = lambda c: pl.BlockSpec((None, tm, c), lambda b, t: (b, t, 0))
    in_specs = [row(d)] + [full(w) for w in weights]
    out_specs = [row(A_COLS), row(B_COLS), row(C_WIDTH), row(C_WIDTH),
                 pl.BlockSpec((None, C_WIDTH, tm), lambda b, t: (b, 0, t))]
    out_shape = [jax.ShapeDtypeStruct((bsz, s, A_COLS), BF16),
                 jax.ShapeDtypeStruct((bsz, s, B_COLS), F32),
                 jax.ShapeDtypeStruct((bsz, s, C_WIDTH), BF16),
                 jax.ShapeDtypeStruct((bsz, s, C_WIDTH), BF16),
                 jax.ShapeDtypeStruct((bsz, C_WIDTH, s), BF16)]
    args = [h] + list(weights)
    if embed_ln is not None:
        vecs = [v.reshape(1, d) for v in embed_ln]
        in_specs[1:1] = [full(v) for v in vecs]
        args[1:1] = vecs
        out_specs.append(row(d))
        out_shape.append(jax.ShapeDtypeStruct((bsz, s, d), F32))
    return pl.pallas_call(
        _proj_kernel if embed_ln is None else _embed_proj_kernel,
        grid=(bsz, s // tm),
        in_specs=in_specs,
        out_specs=out_specs,
        out_shape=out_shape,
        compiler_params=_params("parallel", "parallel"),
        name="in_proj" if embed_ln is None else "embed_ln_in_proj",
    )(*args)


def _swa_kernel(sink_ref, q_ref, kp_ref, kc_ref, vp_ref, vc_ref, bias_ref, o_ref):
    n = pl.program_id(1)
    kall = jnp.concatenate([kp_ref[...], kc_ref[...]], axis=0)
    vall = jnp.concatenate([vp_ref[...], vc_ref[...]], axis=0)
    lane = lax.broadcasted_iota(jnp.int32, (BLOCK, LANES), 1)
    row2 = lax.broadcasted_iota(jnp.int32, (2 * BLOCK, 1), 0)
    col2 = lax.broadcasted_iota(jnp.int32, (1, 2 * BLOCK), 1)
    pad = jnp.where(jnp.logical_and(n == 0, col2 < BLOCK), MASKED, 0.0)
    units = [(j, c) for j in range(SWA_QB) for c in range(A_Q_HEADS // 2)]
    logits = []
    for j, c in units:
        q2 = q_ref[j * BLOCK:(j + 1) * BLOCK, c * LANES:(c + 1) * LANES]
        zero = jnp.zeros_like(q2)
        qs = jnp.concatenate([jnp.where(lane < HEAD_DIM, q2, zero),
                              jnp.where(lane >= HEAD_DIM, q2, zero)], axis=0)
        x = _dot_nt(qs, kall[j * BLOCK:(j + 2) * BLOCK]) * SCALE + bias_ref[c]
        logits.append(x + pad if j == 0 else x)
    probs, denoms = [], []
    for (j, c), x in zip(units, logits):
        sink = jnp.where(row2 < BLOCK, sink_ref[c], sink_ref[c + 3])
        m = jnp.maximum(jnp.max(x, axis=-1, keepdims=True), sink)
        p = jnp.exp(x - m)
        denoms.append(jnp.sum(p, axis=-1, keepdims=True) + jnp.exp(sink - m))
        probs.append(p.astype(BF16))
    for (j, c), p, denom in zip(units, probs, denoms):
        o = _dot(p, vall[j * BLOCK:(j + 2) * BLOCK]) / denom
        o_ref[j * BLOCK:(j + 1) * BLOCK, c * LANES:(c + 1) * LANES] = jnp.where(
            lane < HEAD_DIM, o[:BLOCK], o[BLOCK:]).astype(BF16)


def _swa_call(pa, sinks, bias_pairs):
    bsz, s, _ = pa.shape
    tq = SWA_QB * BLOCK
    kcol = A_WIDTH // LANES
    vcol = kcol + 1
    prev = lambda n: jnp.maximum(n * SWA_QB - 1, 0)
    return pl.pallas_call(
        _swa_kernel,
        grid=(bsz, s // tq),
        in_specs=[pl.BlockSpec(memory_space=pltpu.SMEM),
                  pl.BlockSpec((None, tq, A_WIDTH), lambda b, n: (b, n, 0)),
                  pl.BlockSpec((None, BLOCK, LANES), lambda b, n: (b, prev(n), kcol)),
                  pl.BlockSpec((None, tq, LANES), lambda b, n: (b, n, kcol)),
                  pl.BlockSpec((None, BLOCK, LANES), lambda b, n: (b, prev(n), vcol)),
                  pl.BlockSpec((None, tq, LANES), lambda b, n: (b, n, vcol)),
                  pl.BlockSpec(bias_pairs.shape, lambda b, n: (0, 0, 0))],
        out_specs=pl.BlockSpec((None, tq, A_WIDTH), lambda b, n: (b, n, 0)),
        out_shape=jax.ShapeDtypeStruct((bsz, s, A_WIDTH), BF16),
        compiler_params=_params("parallel", "parallel"),
        name="swa_attn",
    )(sinks, pa, pa, pa, pa, pa, bias_pairs)


def _t5_causal_bucket(dist):
    dist = jnp.maximum(dist, 0)
    d = jnp.maximum(dist, 1).astype(F32)
    large = MAX_EXACT + (jnp.log(d / MAX_EXACT) / math.log(MAX_DISTANCE / MAX_EXACT)
                         * (NUM_BUCKETS - MAX_EXACT)).astype(jnp.int32)
    large = jnp.minimum(large, NUM_BUCKETS - 1)
    return jnp.where(dist < MAX_EXACT, dist, large)


def _swa_bias_pairs(rel_bias):
    nd = 3 * BLOCK - 1
    dist = (2 * BLOCK - 1) - jnp.arange(nd)
    onehot = jax.nn.one_hot(_t5_causal_bucket(dist), NUM_BUCKETS, dtype=F32)
    per_dist = jnp.einsum("db,bh->hd", onehot, rel_bias.astype(F32), precision=HI)
    per_dist = jnp.where((dist >= 0) & (dist < WINDOW), per_dist, MASKED)
    period = nd + 2
    tiled = jnp.tile(jnp.pad(per_dist, ((0, 0), (0, period - nd))), (1, BLOCK))
    rows = tiled[:, :BLOCK * (period - 1)].reshape(A_Q_HEADS, BLOCK, period - 1)
    bias = rows[:, :, BLOCK - 1:3 * BLOCK - 1]
    return jnp.stack([jnp.concatenate([bias[c], bias[c + 3]], axis=0)
                      for c in range(A_Q_HEADS // 2)])


def _split2(x):
    hi = x.astype(BF16)
    return hi, (x - hi.astype(F32)).astype(BF16)


def _dot3(a, b, nt=False):
    ah, al = _split2(a)
    bh, bl = _split2(b)
    d = _dot_nt if nt else _dot
    return d(ah, bh) + d(ah, bl) + d(al, bh)


def _dot1(a, b, nt=False):
    return (_dot_nt if nt else _dot)(a.astype(BF16), b.astype(BF16))


def _dot1_many(pairs, nt=False):
    parts = [(a.astype(BF16), b.astype(BF16)) for a, b in pairs]
    d = _dot_nt if nt else _dot
    return [d(a, b) for a, b in parts]


def _dot_x2(a, b_exact):
    ah, al = _split2(a)
    return _dot(ah, b_exact) + _dot(al, b_exact)


def _dot_2x(a_exact, b):
    bh, bl = _split2(b)
    return _dot(a_exact, bh) + _dot(a_exact, bl)


def _rwkv_kernel(pb_ref, mu_ref, w0_ref, wd_ref, a0_ref, wa_ref, wg_ref, kk_ref, ka_ref,
                 rk_ref, lng_ref, lnb_ref, o_ref, prev_ref, h_ref):
    t = pl.program_id(1)

    @pl.when(t == 0)
    def _():
        prev_ref[...] = jnp.zeros_like(prev_ref)
        h_ref[...] = jnp.zeros_like(h_ref)

    L = CHUNK
    W = B_WIDTH
    TS = RWKV_TS
    p = pb_ref[...]
    rows = lax.broadcasted_iota(jnp.int32, (TS, 1), 0)
    shifted = jnp.where(rows == 0, prev_ref[...], pltpu.roll(p, 1, axis=0))
    prev_ref[...] = p[TS - 1:TS, :]
    pm = p + (shifted - p) * mu_ref[...]
    r = pm[:, 0:W]
    k = pm[:, W:2 * W]
    v = pm[:, 2 * W:3 * W]
    xwa = pm[:, 3 * W:3 * W + DECAY_RANK + ICLR_RANK]
    xg = pm[:, 3 * W + DECAY_RANK + ICLR_RANK:]

    dw = w0_ref[...] + _dot(jnp.tanh(xwa).astype(BF16), wd_ref[...])
    lw = -jnp.exp(-_softplus(-dw) - 0.5)
    a = _sigmoid(a0_ref[...] + _dot(xwa.astype(BF16), wa_ref[...]))
    g = _dot(_sigmoid(xg).astype(BF16), wg_ref[...])

    hr = lax.broadcasted_iota(jnp.int32, (W, W), 0) // HEAD_DIM
    hc = lax.broadcasted_iota(jnp.int32, (W, W), 1) // HEAD_DIM
    same_head = hr == hc
    diag_w = (lax.broadcasted_iota(jnp.int32, (W, W), 0)
              == lax.broadcasted_iota(jnp.int32, (W, W), 1))
    head_ones = jnp.where(same_head, 1.0, 0.0).astype(BF16)
    kk = k * kk_ref[...]
    kk = kk * lax.rsqrt(jnp.maximum(_dot_x2(kk * kk, head_ones), 1e-24))
    k2 = k * (1.0 + (a - 1.0) * ka_ref[...])
    bonus = _dot_x2(r * k2 * rk_ref[...], head_ones) * v
    aa = -kk
    bb = kk * a

    ti = lax.broadcasted_iota(jnp.int32, (L, L), 0)
    tj = lax.broadcasted_iota(jnp.int32, (L, L), 1)
    lower = jnp.where(ti >= tj, 1.0, 0.0).astype(BF16)
    eye = jnp.where(ti == tj, 1.0, 0.0)
    lane_head = lax.broadcasted_iota(jnp.int32, (L, W), 1) // HEAD_DIM

    def only(x, h):
        return jnp.where(lane_head == h, x, 0.0)

    nch = TS // L
    chunks = range(nch)
    units = [(c, h) for c in chunks for h in range(B_HEADS)]
    hsl = [slice(h * L, (h + 1) * L) for h in range(B_HEADS)]
    csl = [slice(c * L, (c + 1) * L) for c in chunks]
    v_c = [v[s] for s in csl]
    cum = [_dot_2x(lower, lw[s]) for s in csl]
    cum_l = [x[L - 1:L, :] for x in cum]
    at = [aa[csl[c]] * jnp.exp(cum[c] - lw[csl[c]]) for c in chunks]
    rt = [r[csl[c]] * jnp.exp(cum[c]) for c in chunks]
    inv = [jnp.exp(-x) for x in cum]
    bt = [bb[csl[c]] * inv[c] for c in chunks]
    kt = [k2[csl[c]] * inv[c] for c in chunks]
    tail = [jnp.exp(cum_l[c] - cum[c]) for c in chunks]
    bh = [bb[csl[c]] * tail[c] for c in chunks]
    kh = [k2[csl[c]] * tail[c] for c in chunks]

    at_s = [jnp.concatenate([only(x, h) for h in range(B_HEADS)], axis=0) for x in at]
    rt_s = [jnp.concatenate([only(x, h) for h in range(B_HEADS)], axis=0) for x in rt]
    ab = _dot1_many([(at_s[c], bt[c]) for c in chunks], nt=True)
    ak = _dot1_many([(at_s[c], kt[c]) for c in chunks], nt=True)
    rb = _dot1_many([(rt_s[c], bt[c]) for c in chunks], nt=True)
    rk = _dot1_many([(rt_s[c], kt[c]) for c in chunks], nt=True)

    a_low = [jnp.where(ti > tj, ab[c][hsl[h]], 0.0) for c, h in units]
    pw = [jnp.where(ti // INV_BASE == tj // INV_BASE, x, 0.0) for x in a_low]
    tinv = [eye + x for x in pw]
    for _ in range(int(math.log2(INV_BASE)) - 1):
        pw = _dot1_many([(x, x) for x in pw])
        tinv = [t + d for t, d in zip(tinv, _dot1_many(list(zip(tinv, pw))))]
    blk = INV_BASE
    while blk < L:
        pair = jnp.logical_and(ti // (2 * blk) == tj // (2 * blk), ti // blk != tj // blk)
        right = _dot1_many([(jnp.where(pair, x, 0.0), t) for x, t in zip(a_low, tinv)])
        tinv = [t + d for t, d in zip(tinv, _dot1_many(list(zip(tinv, right))))]
        blk *= 2
    ak_v = _dot1_many([(jnp.where(ti > tj, ak[c][hsl[h]], 0.0), v_c[c]) for c, h in units])
    w_u = _dot1_many([(tinv[i], only(at[c], h)) for i, (c, h) in enumerate(units)])
    u0_u = [only(x, h) for x, (c, h) in zip(_dot1_many(list(zip(tinv, ak_v))), units)]
    rb_l = [jnp.where(ti >= tj, rb[c][hsl[h]], 0.0) for c, h in units]
    rk_l = [jnp.where(ti >= tj, rk[c][hsl[h]], 0.0) for c, h in units]
    qm_u = _dot1_many(list(zip(rb_l, w_u)))
    y0_u = [only(p + q, h) for p, q, (c, h) in zip(
        _dot1_many(list(zip(rb_l, u0_u))),
        _dot1_many([(rk_l[i], v_c[c]) for i, (c, h) in enumerate(units)]), units)]

    def chunk_sum(xs, c):
        return functools.reduce(lambda p, q: p + q, xs[c * B_HEADS:(c + 1) * B_HEADS])

    w_sum = [chunk_sum(w_u, c) for c in chunks]
    u0 = [chunk_sum(u0_u, c) for c in chunks]
    qm = [rt[c] + chunk_sum(qm_u, c) for c in chunks]
    y0 = [chunk_sum(y0_u, c) for c in chunks]
    bw = _dot1_many([(bh[c].T, w_sum[c]) for c in chunks])
    g_mat = [jnp.where(same_head, bw[c], 0.0) + jnp.where(diag_w, jnp.exp(cum_l[c]).T, 0.0)
             for c in chunks]
    c_mat = [jnp.where(same_head, x, 0.0) for x in _dot1_many(
        [(jnp.concatenate([bh[c], kh[c]], axis=0).T, jnp.concatenate([u0[c], v_c[c]], axis=0))
         for c in chunks])]

    hst = h_ref[...]
    ys = []
    for c in chunks:
        ys.append(_dot1(qm[c], hst) + y0[c])
        hst = _dot1(g_mat[c], hst) + c_mat[c]
    h_ref[...] = hst
    y = jnp.concatenate(ys, axis=0)

    mean = _dot_x2(y, head_ones) * (1.0 / HEAD_DIM)
    yc = y - mean
    var = _dot_x2(yc * yc, head_ones) * (1.0 / HEAD_DIM)
    yn = yc * lax.rsqrt(var + GN_EPS) * lng_ref[...] + lnb_ref[...]
    o_ref[...] = ((yn + bonus) * g).astype(o_ref.dtype)


def _rwkv_call(pb, mu, w0, wd_pad, a0, wa_pad, wg, k_k, k_a, r_k, lnx_g, lnx_b):
    bsz, s, _ = pb.shape
    vec = lambda x: x.reshape(1, -1).astype(F32)
    small = [vec(mu), vec(w0), wd_pad, vec(a0), wa_pad, wg, vec(k_k), vec(k_a), vec(r_k),
             vec(lnx_g), vec(lnx_b)]
    return pl.pallas_call(
        _rwkv_kernel,
        grid=(bsz, s // RWKV_TS),
        in_specs=[pl.BlockSpec((None, RWKV_TS, B_COLS), lambda b, t: (b, t, 0))]
        + [pl.BlockSpec(x.shape, lambda b, t: (0, 0)) for x in small],
        out_specs=pl.BlockSpec((None, RWKV_TS, B_WIDTH), lambda b, t: (b, t, 0)),
        out_shape=jax.ShapeDtypeStruct((bsz, s, B_WIDTH), BF16),
        scratch_shapes=[pltpu.VMEM((1, B_COLS), F32), pltpu.VMEM((B_WIDTH, B_WIDTH), F32)],
        compiler_params=_params("parallel", "arbitrary"),
        name="rwkv7",
    )(pb, *small)


def _softplus2(z):
    return jnp.maximum(z, 0.0) + jnp.log2(1.0 + jnp.exp2(-jnp.abs(z)))


def _sb_kernel(q_ref, k_ref, vt_ref, o_ref, z_sc, lb_sc, e_sc, w_sc, acc_sc, car_sc):
    it = pl.program_id(2)
    q2 = q_ref[...]
    lane = lax.broadcasted_iota(jnp.int32, (SB_TQ, LANES), 1)
    zero = jnp.zeros_like(q2)
    qh = (jnp.where(lane < HEAD_DIM, q2, zero), jnp.where(lane >= HEAD_DIM, q2, zero))
    ur = lax.broadcasted_iota(jnp.int32, (BLOCK, BLOCK), 0)
    uc = lax.broadcasted_iota(jnp.int32, (BLOCK, BLOCK), 1)
    upper = jnp.where(uc > ur, 1.0, 0.0).astype(BF16)
    nsub = SB_TQ // BLOCK
    first = it * nsub + nsub - 1
    nblk = first + 1

    def key_off(n):
        return pl.multiple_of(jnp.clip(first - n, 0, first) * BLOCK, BLOCK)

    def logits(n):
        kblk = k_ref[pl.ds(key_off(n), BLOCK), :]
        return jnp.concatenate([_dot_nt(kblk, qh[0]), _dot_nt(kblk, qh[1])], axis=1)

    def softplus_stage(zt):
        sp = _softplus2(zt)
        return zt - sp, sp.astype(BF16), jnp.sum(sp, axis=0, keepdims=True)

    def value_stage(n):
        vt2 = vt_ref[:, pl.ds(key_off(n), BLOCK)]
        w = w_sc[...]
        for h in range(2):
            cols = slice(h * SB_TQ, (h + 1) * SB_TQ)
            acc_sc[:, cols] += _dot(vt2[h * HEAD_DIM:(h + 1) * HEAD_DIM], w[:, cols])

    def step(n, carry):
        car, colsum = carry
        value_stage(n - 1)
        z_new = logits(n + 2)
        lb_new, sp16, colsum_new = softplus_stage(z_sc[...])
        e_new = _dot(upper, sp16)
        w_sc[...] = jnp.exp2(lb_sc[...] - e_sc[...] - car).astype(BF16)
        z_sc[...] = z_new
        lb_sc[...] = lb_new
        e_sc[...] = e_new
        return car + colsum, colsum_new

    half = SB_TQ // 2

    def lanes(n):
        return (max(0, nsub - 1 - n) * BLOCK, SB_TQ) if n < nsub else (0, half)

    def values_block(n):
        vt2 = vt_ref[:, pl.ds(key_off(n), BLOCK)]
        if n >= nsub:
            vt2 = jnp.where(n < nblk, vt2, jnp.zeros_like(vt2))
        return vt2

    def near_logits(n):
        lo, hi = lanes(n)
        kblk = k_ref[pl.ds(key_off(n), BLOCK), :]
        return [_dot_nt(kblk, qh[h][lo:hi]) for h in range(2)]

    def near_softplus(zt, n):
        lo, hi = lanes(n)
        sp = _softplus2(zt)
        lb = zt - sp
        if n < nsub:
            kr = lax.broadcasted_iota(jnp.int32, (BLOCK, hi - lo), 0)
            qc = lax.broadcasted_iota(jnp.int32, (BLOCK, hi - lo), 1)
            before = kr < qc
            sp = jnp.where(before, sp, 0.0)
            lb = jnp.where(before, lb, MASKED)
        pieces = [jnp.zeros((1, lo), F32)] if lo else []
        pieces.append(jnp.sum(sp, axis=0, keepdims=True))
        if hi < SB_TQ:
            pieces.append(jnp.zeros((1, SB_TQ - hi), F32))
        return lb, sp.astype(BF16), jnp.concatenate(pieces, axis=1)

    def near_values(n, ws):
        lo, hi = lanes(n)
        vt2 = values_block(n)
        for h in range(2):
            cols = slice(h * SB_TQ + lo, h * SB_TQ + hi)
            acc_sc[:, cols] += _dot(vt2[h * HEAD_DIM:(h + 1) * HEAD_DIM], ws[h])

    acc_sc[...] = jnp.zeros_like(acc_sc)
    zs = {0: near_logits(0), 1: near_logits(1)}
    soft = {0: [near_softplus(zs[0][h], 0) for h in range(2)]}
    excl = {0: [_dot(upper, soft[0][h][1]) for h in range(2)]}
    cars = [jnp.zeros((1, SB_TQ), F32) for _ in range(2)]
    ws = None
    for n in range(SB_NEAR):
        if n >= 1:
            near_values(n - 1, ws)
        if n + 2 < SB_NEAR:
            zs[n + 2] = near_logits(n + 2)
        if n + 1 < SB_NEAR:
            soft[n + 1] = [near_softplus(zs[n + 1][h], n + 1) for h in range(2)]
            excl[n + 1] = [_dot(upper, soft[n + 1][h][1]) for h in range(2)]
        lo, hi = lanes(n)
        ws = [jnp.exp2(soft[n][h][0] - excl[n][h] - cars[h][:, lo:hi]).astype(BF16)
              for h in range(2)]
        cars = [cars[h] + soft[n][h][2] for h in range(2)]
    near_values(SB_NEAR - 1, ws)
    car_sc[...] = jnp.concatenate(cars, axis=1)

    late_live = jnp.minimum(jnp.min(cars[0][:, half:]), jnp.min(cars[1][:, half:])) < SB_DEAD

    @pl.when(jnp.logical_and(nblk > nsub, late_live))
    def _():
        for n in range(nsub, SB_NEAR):
            kblk = k_ref[pl.ds(key_off(n), BLOCK), :]
            vt2 = values_block(n)
            for h in range(2):
                cols = slice(h * SB_TQ + half, (h + 1) * SB_TQ)
                zt = _dot_nt(kblk, qh[h][half:])
                sp = _softplus2(zt)
                w = jnp.exp2(zt - sp - _dot(upper, sp.astype(BF16)) - car_sc[:, cols])
                acc_sc[:, cols] += _dot(vt2[h * HEAD_DIM:(h + 1) * HEAD_DIM], w.astype(BF16))
                car_sc[:, cols] += jnp.sum(sp, axis=0, keepdims=True)

    car = car_sc[...]

    @pl.when(jnp.logical_and(nblk > SB_NEAR, jnp.min(car) < SB_DEAD))
    def _():
        lb0, sp160, colsum0 = softplus_stage(logits(SB_NEAR))
        lb_sc[...] = lb0
        e_sc[...] = _dot(upper, sp160)
        z_sc[...] = logits(SB_NEAR + 1)
        w_sc[...] = jnp.zeros_like(w_sc)

        def live(c):
            n, _, _, alive = c
            return jnp.logical_and(n < nblk, alive > 0)

        def visit(c):
            n, car, colsum, _ = c
            alive = (jnp.min(car) < SB_DEAD).astype(jnp.int32)
            car, colsum = step(n, (car, colsum))
            return n + jnp.int32(1), car, colsum, alive

        n_stop = lax.while_loop(live, visit, (jnp.int32(SB_NEAR), car, colsum0, jnp.int32(1)))[0]
        value_stage(n_stop - 1)

    acc = acc_sc[...]
    out_t = jnp.concatenate([acc[:, :SB_TQ], acc[:, SB_TQ:]], axis=0)
    o_ref[...] = out_t.T.astype(o_ref.dtype)


def _sb_call(qc, kc, vt):
    bsz, s, _ = qc.shape
    return pl.pallas_call(
        _sb_kernel,
        grid=(bsz, C_WIDTH // LANES, s // SB_TQ),
        in_specs=[pl.BlockSpec((None, SB_TQ, LANES), lambda b, hp, i: (b, i, hp)),
                  pl.BlockSpec((None, s, LANES), lambda b, hp, i: (b, 0, hp)),
                  pl.BlockSpec((None, LANES, s), lambda b, hp, i: (b, hp, 0))],
        out_specs=pl.BlockSpec((None, SB_TQ, LANES), lambda b, hp, i: (b, i, hp)),
        out_shape=jax.ShapeDtypeStruct((bsz, s, C_WIDTH), BF16),
        scratch_shapes=[pltpu.VMEM((BLOCK, 2 * SB_TQ), F32),
                        pltpu.VMEM((BLOCK, 2 * SB_TQ), F32),
                        pltpu.VMEM((BLOCK, 2 * SB_TQ), F32),
                        pltpu.VMEM((BLOCK, 2 * SB_TQ), BF16),
                        pltpu.VMEM((HEAD_DIM, 2 * SB_TQ), F32),
                        pltpu.VMEM((1, 2 * SB_TQ), F32)],
        compiler_params=_params("parallel", "parallel", "parallel"),
        name="stickbreak_attn",
    )(qc, kc, vt)


def _outproj_kernel(h_ref, oa_ref, ob_ref, oc_ref, wa_ref, wb_ref, wc_ref, g_ref, b_ref,
                    rw_ref, rb_ref, o_ref, gate_ref):
    tm = h_ref.shape[0]
    parts = [slice(i * tm // OUTPROJ_PARTS, (i + 1) * tm // OUTPROJ_PARTS)
             for i in range(OUTPROJ_PARTS)]
    ms = [_dot(oa_ref[s], wa_ref[...]) + _dot(ob_ref[s], wb_ref[...]) + _dot(oc_ref[s], wc_ref[...])
          for s in parts]
    h1s = [_layer_norm(ALPHA * h_ref[s] + m, g_ref[...], b_ref[...]) for s, m in zip(parts, ms)]
    for s, h1 in zip(parts, h1s):
        o_ref[s] = h1
    logits = [_dot3(h1, rw_ref[...]) for h1 in h1s]
    for s, x in zip(parts, logits):
        gate_ref[:, s] = _route(x.T[:rb_ref.shape[0]], rb_ref[...])


def _outproj_call(h, oa, ob, oc, wa, wb, wc, g, b, rw, rb_col, tm=1024):
    bsz, s, d = h.shape
    nt = s // tm
    rows = rb_col.shape[0]
    row = lambda c: pl.BlockSpec((None, tm, c), lambda bb, t: (bb, t, 0))
    full = lambda w: pl.BlockSpec(w.shape, lambda bb, t: (0, 0))
    g2, b2 = g.reshape(1, d), b.reshape(1, d)
    return pl.pallas_call(
        _outproj_kernel,
        grid=(bsz, nt),
        in_specs=[row(d), row(A_WIDTH), row(B_WIDTH), row(C_WIDTH),
                  full(wa), full(wb), full(wc), full(g2), full(b2), full(rw), full(rb_col)],
        out_specs=[row(d), pl.BlockSpec((rows, tm), lambda bb, t: (0, bb * nt + t))],
        out_shape=[jax.ShapeDtypeStruct((bsz, s, d), F32),
                   jax.ShapeDtypeStruct((rows, bsz * s), F32)],
        compiler_params=_params("parallel", "parallel"),
        name="out_proj_ln_router",
    )(h, oa, ob, oc, wa, wb, wc, g2, b2, rw, rb_col)


def _route(logits, rb):
    scores = _sigmoid(logits)
    sel = scores + rb
    R = ROUTER_ROWS
    s = [sel[m * R:(m + 1) * R] for m in range(EXPERTS_PER_GROUP)]
    sc = [scores[m * R:(m + 1) * R] for m in range(EXPERTS_PER_GROUP)]
    hi01, lo01 = jnp.maximum(s[0], s[1]), jnp.minimum(s[0], s[1])
    hi23, lo23 = jnp.maximum(s[2], s[3]), jnp.minimum(s[2], s[3])
    top1 = jnp.maximum(hi01, hi23)
    top2 = jnp.maximum(jnp.minimum(hi01, hi23), jnp.maximum(lo01, lo23))
    gscore = top1 + top2
    gi = lax.broadcasted_iota(jnp.int32, gscore.shape, 0)
    gmax = jnp.max(gscore, axis=0, keepdims=True)
    best = jnp.min(jnp.where(gscore == gmax, gi, R), axis=0, keepdims=True)
    in_group = gi == best
    picked = []
    for m in range(EXPERTS_PER_GROUP):
        rank = jnp.zeros(gscore.shape, jnp.int32)
        for j in range(EXPERTS_PER_GROUP):
            if j == m:
                continue
            ahead = (s[j] >= s[m]) if j < m else (s[j] > s[m])
            rank = rank + jnp.where(ahead, 1, 0)
        picked.append(jnp.where(jnp.logical_and(in_group, rank < 2), sc[m], 0.0))
    denom = jnp.sum(picked[0] + picked[1] + picked[2] + picked[3], axis=0, keepdims=True)
    return jnp.concatenate([x / denom for x in picked], axis=0)


def _moe_kernel(x_ref, gate_ref, wgu_ref, wd_ref, g_ref, b_ref, o_ref, xb_ref, acc_ref):
    j = pl.program_id(1)

    @pl.when(j == 0)
    def _():
        xb_ref[...] = x_ref[...].astype(BF16)
        acc_ref[...] = jnp.zeros_like(acc_ref)

    xb = xb_ref[...]
    gates = gate_ref[...]
    lane = lax.broadcasted_iota(jnp.int32, gates.shape, 1)
    f = wd_ref.shape[1]
    y = None
    for m in range(MOE_EPS):
        h2 = _dot(xb, wgu_ref[m])
        hg, hu = h2[:, :f], h2[:, f:]
        gcol = jnp.sum(jnp.where(lane == j * MOE_EPS + m, gates, 0.0), axis=-1, keepdims=True)
        act = (hg * _sigmoid(hg)) * hu * gcol
        part = _dot(act.astype(BF16), wd_ref[m])
        y = part if y is None else y + part
    acc_ref[...] += y

    @pl.when(j == pl.num_programs(1) - 1)
    def _():
        o_ref[...] = _layer_norm(ALPHA * x_ref[...] + acc_ref[...], g_ref[...], b_ref[...])


def _moe_call(x, gates, wgu, wd, g, b, tm=1024):
    n, d = x.shape
    f2 = wgu.shape[-1]
    g2, b2 = g.reshape(1, d), b.reshape(1, d)
    return pl.pallas_call(
        _moe_kernel,
        grid=(n // tm, N_EXPERTS // MOE_EPS),
        in_specs=[pl.BlockSpec((tm, d), lambda i, j: (i, 0)),
                  pl.BlockSpec((tm, N_EXPERTS), lambda i, j: (i, 0)),
                  pl.BlockSpec((MOE_EPS, d, f2), lambda i, j: (j, 0, 0)),
                  pl.BlockSpec((MOE_EPS, f2 // 2, d), lambda i, j: (j, 0, 0)),
                  pl.BlockSpec((1, d), lambda i, j: (0, 0)),
                  pl.BlockSpec((1, d), lambda i, j: (0, 0))],
        out_specs=pl.BlockSpec((tm, d), lambda i, j: (i, 0)),
        out_shape=jax.ShapeDtypeStruct((n, d), F32),
        scratch_shapes=[pltpu.VMEM((tm, d), BF16), pltpu.VMEM((tm, d), F32)],
        compiler_params=_params("parallel", "arbitrary"),
        name="moe_experts_ln",
    )(x, gates, wgu, wd, g2, b2)


def _pair_heads(x, axis):
    shape = x.shape
    x = x.reshape(shape[:axis] + (A_KV_HEADS, A_Q_HEADS // A_KV_HEADS, HEAD_DIM) + shape[axis + 1:])
    x = jnp.swapaxes(x, axis, axis + 1)
    return x.reshape(shape)


def _router_layout(router_w, router_bias):
    d = router_w.shape[0]
    w = router_w.astype(F32).T.reshape(N_GROUPS, EXPERTS_PER_GROUP, d).transpose(1, 0, 2)
    w = jnp.pad(w, ((0, 0), (0, ROUTER_ROWS - N_GROUPS), (0, 0)))
    b = router_bias.astype(F32).reshape(N_GROUPS, EXPERTS_PER_GROUP).T
    b = jnp.pad(b, ((0, 0), (0, ROUTER_ROWS - N_GROUPS)), constant_values=MASKED)
    rows = EXPERTS_PER_GROUP * ROUTER_ROWS
    w = jnp.pad(w.reshape(rows, d).T, ((0, 0), (0, LANES - rows)))
    return w, b.reshape(rows, 1)


def _gates_from_router(gates_t):
    n = gates_t.shape[1]
    g = gates_t.reshape(EXPERTS_PER_GROUP, ROUTER_ROWS, n)[:, :N_GROUPS]
    return g.transpose(2, 1, 0).reshape(n, N_EXPERTS)


def kernel(x, ln0_g, ln0_b, w_in, w_out, sinks, rel_bias, shift_mu, decay_w0, decay_up, iclr_a0,
           iclr_up, gate_up, k_k, k_a, r_k, lnx_g, lnx_b, ln1_g, ln1_b, router_w, router_bias,
           w_gate, w_up, w_down, ln2_g, ln2_b):
    bsz, s, d = x.shape
    n = bsz * s
    bias_pairs = _swa_bias_pairs(rel_bias)
    rw, rb_col = _router_layout(router_w, router_bias)
    zeros_lora = jnp.zeros((ICLR_RANK, B_WIDTH), F32)

    h = x
    for l in range(DEPTH):
        wl = w_in[l]
        wa = jnp.concatenate([_pair_heads(wl[:, :A_WIDTH], 1), wl[:, A_WIDTH:A_COLS]], axis=1)
        wb = wl[:, A_COLS:A_COLS + B_COLS]
        c0 = A_COLS + B_COLS
        wq, wk, wv = (wl[:, c0 + j * C_WIDTH:c0 + (j + 1) * C_WIDTH] for j in range(3))
        weights = [wa.astype(BF16), wb.astype(BF16), wq.astype(BF16), wk.astype(BF16),
                   wv.T.astype(BF16)]
        if l == 0:
            pa, pb, qc, kc, vt, h = _proj_call(h, weights, embed_ln=(ln0_g, ln0_b))
        else:
            pa, pb, qc, kc, vt = _proj_call(h, weights)

        out_a = _swa_call(pa, sinks[l].astype(F32), bias_pairs)
        wd_pad = jnp.concatenate([decay_up[l].astype(F32), zeros_lora], axis=0).astype(BF16)
        wa_pad = jnp.concatenate([zeros_lora, iclr_up[l].astype(F32)], axis=0).astype(BF16)
        out_b = _rwkv_call(pb, shift_mu[l], decay_w0[l], wd_pad, iclr_a0[l], wa_pad,
                           gate_up[l].astype(BF16), k_k[l], k_a[l], r_k[l], lnx_g[l], lnx_b[l])
        out_c = _sb_call(qc, kc, vt)

        wo = w_out[l]
        h, gates_t = _outproj_call(h, out_a, out_b, out_c,
                                   _pair_heads(wo[:A_WIDTH], 0).astype(BF16),
                                   wo[A_WIDTH:A_WIDTH + B_WIDTH].astype(BF16),
                                   wo[A_WIDTH + B_WIDTH:].astype(BF16), ln1_g[l], ln1_b[l],
                                   rw, rb_col)

        wgu = jnp.concatenate([w_gate[l], w_up[l]], axis=-1).astype(BF16)
        hf = _moe_call(h.reshape(n, d), _gates_from_router(gates_t), wgu,
                       w_down[l].astype(BF16), ln2_g[l], ln2_b[l])
        h = hf.reshape(bsz, s, d)
    return h
```

```python
import functools
import math

import jax
import jax.numpy as jnp
from jax import lax
from jax.experimental import pallas as pl
from jax.experimental.pallas import tpu as pltpu

F32 = jnp.float32
BF16 = jnp.bfloat16
HI = lax.Precision.HIGHEST

DEPTH = 2
HEAD_DIM = 64
BLOCK = 128
LANES = 128
A_Q_HEADS = 6
A_KV_HEADS = 2
WINDOW = 128
A_WIDTH = A_Q_HEADS * HEAD_DIM
A_KV_WIDTH = A_KV_HEADS * HEAD_DIM
B_HEADS = 4
B_WIDTH = B_HEADS * HEAD_DIM
DECAY_RANK = 64
ICLR_RANK = 64
GATE_RANK = 128
GN_EPS = 64e-5
C_HEADS = 6
C_WIDTH = C_HEADS * HEAD_DIM
A_COLS = A_WIDTH + 2 * A_KV_WIDTH
B_COLS = 3 * B_WIDTH + DECAY_RANK + ICLR_RANK + GATE_RANK
NUM_BUCKETS = 32
MAX_EXACT = NUM_BUCKETS // 2
MAX_DISTANCE = 128
N_EXPERTS = 16
N_GROUPS = 4
EXPERTS_PER_GROUP = N_EXPERTS // N_GROUPS
D_FF_EXPERT = 256
LN_EPS = 1e-5
ALPHA = (2 * DEPTH) ** 0.25
SCALE = HEAD_DIM ** -0.5
MASKED = -1e30
SWA_QB = 4
CHUNK = 64
INV_BASE = 8
RWKV_TS = 512
SB_TQ = 512
SB_NEAR = 6
SB_DEAD = 151.0
LOG2E = 1.4426950408889634
ROUTER_ROWS = 8
MOE_EPS = 4
OUTPROJ_PARTS = 4

VMEM_LIMIT = 48 * 1024 * 1024


def _dot(a, b, prec=None):
    return jnp.dot(a, b, preferred_element_type=F32, precision=prec)


def _dot_nt(a, b, prec=None):
    return lax.dot_general(a, b, (((1,), (1,)), ((), ())),
                           preferred_element_type=F32, precision=prec)


def _sigmoid(x):
    return 1.0 / (1.0 + jnp.exp(-x))


def _softplus(x):
    return jnp.maximum(x, 0.0) + jnp.log(1.0 + jnp.exp(-jnp.abs(x)))


def _layer_norm(x, g, b):
    mu = jnp.mean(x, axis=-1, keepdims=True)
    xc = x - mu
    var = jnp.mean(xc * xc, axis=-1, keepdims=True)
    return xc * lax.rsqrt(var + LN_EPS) * g + b


def _params(*sem):
    return pltpu.CompilerParams(dimension_semantics=sem, vmem_limit_bytes=VMEM_LIMIT)


def _proj_body(h, wa_ref, wb_ref, wq_ref, wk_ref, wvt_ref, pa_ref, pb_ref, qc_ref, kc_ref, vt_ref):
    hb = h.astype(BF16)
    pa_ref[...] = _dot(hb, wa_ref[...]).astype(BF16)
    pb_ref[...] = _dot(hb, wb_ref[...])
    qc_ref[...] = (_dot(hb, wq_ref[...]) * (SCALE * LOG2E)).astype(BF16)
    kc_ref[...] = _dot(hb, wk_ref[...]).astype(BF16)
    vt_ref[...] = _dot_nt(wvt_ref[...], hb).astype(BF16)


def _proj_kernel(h_ref, *refs):
    _proj_body(h_ref[...], *refs)


def _embed_proj_kernel(x_ref, g_ref, b_ref, *refs):
    h = _layer_norm(x_ref[...], g_ref[...], b_ref[...])
    refs[-1][...] = h
    _proj_body(h, *refs[:-1])


def _proj_call(h, weights, embed_ln=None, tm=512):
    bsz, s, d = h.shape
    full = lambda w: pl.BlockSpec(w.shape, lambda b, t: (0, 0))
    row = lambda c: pl.BlockSpec((None, tm, c), lambda b, t: (b, t, 0))
    in_specs = [row(d)] + [full(w) for w in weights]
    out_specs = [row(A_COLS), row(B_COLS), row(C_WIDTH), row(C_WIDTH),
                 pl.BlockSpec((None, C_WIDTH, tm), lambda b, t: (b, 0, t))]
    out_shape = [jax.ShapeDtypeStruct((bsz, s, A_COLS), BF16),
                 jax.ShapeDtypeStruct((bsz, s, B_COLS), F32),
                 jax.ShapeDtypeStruct((bsz, s, C_WIDTH), BF16),
                 jax.ShapeDtypeStruct((bsz, s, C_WIDTH), BF16),
                 jax.ShapeDtypeStruct((bsz, C_WIDTH, s), BF16)]
    args = [h] + list(weights)
    if embed_ln is not None:
        vecs = [v.reshape(1, d) for v in embed_ln]
        in_specs[1:1] = [full(v) for v in vecs]
        args[1:1] = vecs
        out_specs.append(row(d))
        out_shape.append(jax.ShapeDtypeStruct((bsz, s, d), F32))
    return pl.pallas_call(
        _proj_kernel if embed_ln is None else _embed_proj_kernel,
        grid=(bsz, s // tm),
        in_specs=in_specs,
        out_specs=out_specs,
        out_shape=out_shape,
        compiler_params=_params("parallel", "parallel"),
        name="in_proj" if embed_ln is None else "embed_ln_in_proj",
    )(*args)


def _swa_kernel(sink_ref, q_ref, kp_ref, kc_ref, vp_ref, vc_ref, bias_ref, o_ref):
    n = pl.program_id(1)
    kall = jnp.concatenate([kp_ref[...], kc_ref[...]], axis=0)
    vall = jnp.concatenate([vp_ref[...], vc_ref[...]], axis=0)
    lane = lax.broadcasted_iota(jnp.int32, (BLOCK, LANES), 1)
    row2 = lax.broadcasted_iota(jnp.int32, (2 * BLOCK, 1), 0)
    col2 = lax.broadcasted_iota(jnp.int32, (1, 2 * BLOCK), 1)
    pad = jnp.where(jnp.logical_and(n == 0, col2 < BLOCK), MASKED, 0.0)
    units = [(j, c) for j in range(SWA_QB) for c in range(A_Q_HEADS // 2)]
    logits = []
    for j, c in units:
        q2 = q_ref[j * BLOCK:(j + 1) * BLOCK, c * LANES:(c + 1) * LANES]
        zero = jnp.zeros_like(q2)
        qs = jnp.concatenate([jnp.where(lane < HEAD_DIM, q2, zero),
                              jnp.where(lane >= HEAD_DIM, q2, zero)], axis=0)
        x = _dot_nt(qs, kall[j * BLOCK:(j + 2) * BLOCK]) * SCALE + bias_ref[c]
        logits.append(x + pad if j == 0 else x)
    probs, denoms = [], []
    for (j, c), x in zip(units, logits):
        sink = jnp.where(row2 < BLOCK, sink_ref[c], sink_ref[c + 3])
        m = jnp.maximum(jnp.max(x, axis=-1, keepdims=True), sink)
        p = jnp.exp(x - m)
        denoms.append(jnp.sum(p, axis=-1, keepdims=True) + jnp.exp(sink - m))
        probs.append(p.astype(BF16))
    for (j, c), p, denom in zip(units, probs, denoms):
        o = _dot(p, vall[j * BLOCK:(j + 2) * BLOCK]) / denom
        o_ref[j * BLOCK:(j + 1) * BLOCK, c * LANES:(c + 1) * LANES] = jnp.where(
            lane < HEAD_DIM, o[:BLOCK], o[BLOCK:]).astype(BF16)


def _swa_call(pa, sinks, bias_pairs):
    bsz, s, _ = pa.shape
    tq = SWA_QB * BLOCK
    kcol = A_WIDTH // LANES
    vcol = kcol + 1
    prev = lambda n: jnp.maximum(n * SWA_QB - 1, 0)
    return pl.pallas_call(
        _swa_kernel,
        grid=(bsz, s // tq),
        in_specs=[pl.BlockSpec(memory_space=pltpu.SMEM),
                  pl.BlockSpec((None, tq, A_WIDTH), lambda b, n: (b, n, 0)),
                  pl.BlockSpec((None, BLOCK, LANES), lambda b, n: (b, prev(n), kcol)),
                  pl.BlockSpec((None, tq, LANES), lambda b, n: (b, n, kcol)),
                  pl.BlockSpec((None, BLOCK, LANES), lambda b, n: (b, prev(n), vcol)),
                  pl.BlockSpec((None, tq, LANES), lambda b, n: (b, n, vcol)),
                  pl.BlockSpec(bias_pairs.shape, lambda b, n: (0, 0, 0))],
        out_specs=pl.BlockSpec((None, tq, A_WIDTH), lambda b, n: (b, n, 0)),
        out_shape=jax.ShapeDtypeStruct((bsz, s, A_WIDTH), BF16),
        compiler_params=_params("parallel", "parallel"),
        name="swa_attn",
    )(sinks, pa, pa, pa, pa, pa, bias_pairs)


def _t5_causal_bucket(dist):
    dist = jnp.maximum(dist, 0)
    d = jnp.maximum(dist, 1).astype(F32)
    large = MAX_EXACT + (jnp.log(d / MAX_EXACT) / math.log(MAX_DISTANCE / MAX_EXACT)
                         * (NUM_BUCKETS - MAX_EXACT)).astype(jnp.int32)
    large = jnp.minimum(large, NUM_BUCKETS - 1)
    return jnp.where(dist < MAX_EXACT, dist, large)


def _swa_bias_pairs(rel_bias):
    nd = 3 * BLOCK - 1
    dist = (2 * BLOCK - 1) - jnp.arange(nd)
    onehot = jax.nn.one_hot(_t5_causal_bucket(dist), NUM_BUCKETS, dtype=F32)
    per_dist = jnp.einsum("db,bh->hd", onehot, rel_bias.astype(F32), precision=HI)
    per_dist = jnp.where((dist >= 0) & (dist < WINDOW), per_dist, MASKED)
    period = nd + 2
    tiled = jnp.tile(jnp.pad(per_dist, ((0, 0), (0, period - nd))), (1, BLOCK))
    rows = tiled[:, :BLOCK * (period - 1)].reshape(A_Q_HEADS, BLOCK, period - 1)
    bias = rows[:, :, BLOCK - 1:3 * BLOCK - 1]
    return jnp.stack([jnp.concatenate([bias[c], bias[c + 3]], axis=0)
                      for c in range(A_Q_HEADS // 2)])


def _split2(x):
    hi = x.astype(BF16)
    return hi, (x - hi.astype(F32)).astype(BF16)


def _dot3(a, b, nt=False):
    ah, al = _split2(a)
    bh, bl = _split2(b)
    d = _dot_nt if nt else _dot
    return d(ah, bh) + d(ah, bl) + d(al, bh)


def _dot1(a, b, nt=False):
    return (_dot_nt if nt else _dot)(a.astype(BF16), b.astype(BF16))


def _dot1_many(pairs, nt=False):
    parts = [(a.astype(BF16), b.astype(BF16)) for a, b in pairs]
    d = _dot_nt if nt else _dot
    return [d(a, b) for a, b in parts]


def _dot_x2(a, b_exact):
    ah, al = _split2(a)
    return _dot(ah, b_exact) + _dot(al, b_exact)


def _dot_2x(a_exact, b):
    bh, bl = _split2(b)
    return _dot(a_exact, bh) + _dot(a_exact, bl)


def _rwkv_kernel(pb_ref, mu_ref, w0_ref, wd_ref, a0_ref, wa_ref, wg_ref, kk_ref, ka_ref,
                 rk_ref, lng_ref, lnb_ref, o_ref, prev_ref, h_ref):
    t = pl.program_id(1)

    @pl.when(t == 0)
    def _():
        prev_ref[...] = jnp.zeros_like(prev_ref)
        h_ref[...] = jnp.zeros_like(h_ref)

    L = CHUNK
    W = B_WIDTH
    TS = RWKV_TS
    p = pb_ref[...]
    rows = lax.broadcasted_iota(jnp.int32, (TS, 1), 0)
    shifted = jnp.where(rows == 0, prev_ref[...], pltpu.roll(p, 1, axis=0))
    prev_ref[...] = p[TS - 1:TS, :]
    pm = p + (shifted - p) * mu_ref[...]
    r = pm[:, 0:W]
    k = pm[:, W:2 * W]
    v = pm[:, 2 * W:3 * W]
    xwa = pm[:, 3 * W:3 * W + DECAY_RANK + ICLR_RANK]
    xg = pm[:, 3 * W + DECAY_RANK + ICLR_RANK:]

    dw = w0_ref[...] + _dot(jnp.tanh(xwa).astype(BF16), wd_ref[...])
    lw = -jnp.exp(-_softplus(-dw) - 0.5)
    a = _sigmoid(a0_ref[...] + _dot(xwa.astype(BF16), wa_ref[...]))
    g = _dot(_sigmoid(xg).astype(BF16), wg_ref[...])

    hr = lax.broadcasted_iota(jnp.int32, (W, W), 0) // HEAD_DIM
    hc = lax.broadcasted_iota(jnp.int32, (W, W), 1) // HEAD_DIM
    same_head = hr == hc
    diag_w = (lax.broadcasted_iota(jnp.int32, (W, W), 0)
              == lax.broadcasted_iota(jnp.int32, (W, W), 1))
    head_ones = jnp.where(same_head, 1.0, 0.0).astype(BF16)
    kk = k * kk_ref[...]
    kk = kk * lax.rsqrt(jnp.maximum(_dot_x2(kk * kk, head_ones), 1e-24))
    k2 = k * (1.0 + (a - 1.0) * ka_ref[...])
    bonus = _dot_x2(r * k2 * rk_ref[...], head_ones) * v
    aa = -kk
    bb = kk * a

    ti = lax.broadcasted_iota(jnp.int32, (L, L), 0)
    tj = lax.broadcasted_iota(jnp.int32, (L, L), 1)
    lower = jnp.where(ti >= tj, 1.0, 0.0).astype(BF16)
    eye = jnp.where(ti == tj, 1.0, 0.0)
    lane_head = lax.broadcasted_iota(jnp.int32, (L, W), 1) // HEAD_DIM

    def only(x, h):
        return jnp.where(lane_head == h, x, 0.0)

    nch = TS // L
    chunks = range(nch)
    units = [(c, h) for c in chunks for h in range(B_HEADS)]
    hsl = [slice(h * L, (h + 1) * L) for h in range(B_HEADS)]
    csl = [slice(c * L, (c + 1) * L) for c in chunks]
    v_c = [v[s] for s in csl]
    cum = [_dot_2x(lower, lw[s]) for s in csl]
    cum_l = [x[L - 1:L, :] for x in cum]
    at = [aa[csl[c]] * jnp.exp(cum[c] - lw[csl[c]]) for c in chunks]
    rt = [r[csl[c]] * jnp.exp(cum[c]) for c in chunks]
    inv = [jnp.exp(-x) for x in cum]
    bt = [bb[csl[c]] * inv[c] for c in chunks]
    kt = [k2[csl[c]] * inv[c] for c in chunks]
    tail = [jnp.exp(cum_l[c] - cum[c]) for c in chunks]
    bh = [bb[csl[c]] * tail[c] for c in chunks]
    kh = [k2[csl[c]] * tail[c] for c in chunks]

    at_s = [jnp.concatenate([only(x, h) for h in range(B_HEADS)], axis=0) for x in at]
    rt_s = [jnp.concatenate([only(x, h) for h in range(B_HEADS)], axis=0) for x in rt]
    ab = _dot1_many([(at_s[c], bt[c]) for c in chunks], nt=True)
    ak = _dot1_many([(at_s[c], kt[c]) for c in chunks], nt=True)
    rb = _dot1_many([(rt_s[c], bt[c]) for c in chunks], nt=True)
    rk = _dot1_many([(rt_s[c], kt[c]) for c in chunks], nt=True)

    a_low = [jnp.where(ti > tj, ab[c][hsl[h]], 0.0) for c, h in units]
    pw = [jnp.where(ti // INV_BASE == tj // INV_BASE, x, 0.0) for x in a_low]
    tinv = [eye + x for x in pw]
    for _ in range(int(math.log2(INV_BASE)) - 1):
        pw = _dot1_many([(x, x) for x in pw])
        tinv = [t + d for t, d in zip(tinv, _dot1_many(list(zip(tinv, pw))))]
    blk = INV_BASE
    while blk < L:
        pair = jnp.logical_and(ti // (2 * blk) == tj // (2 * blk), ti // blk != tj // blk)
        right = _dot1_many([(jnp.where(pair, x, 0.0), t) for x, t in zip(a_low, tinv)])
        tinv = [t + d for t, d in zip(tinv, _dot1_many(list(zip(tinv, right))))]
        blk *= 2
    ak_v = _dot1_many([(jnp.where(ti > tj, ak[c][hsl[h]], 0.0), v_c[c]) for c, h in units])
    w_u = _dot1_many([(tinv[i], only(at[c], h)) for i, (c, h) in enumerate(units)])
    u0_u = [only(x, h) for x, (c, h) in zip(_dot1_many(list(zip(tinv, ak_v))), units)]
    rb_l = [jnp.where(ti >= tj, rb[c][hsl[h]], 0.0) for c, h in units]
    rk_l = [jnp.where(ti >= tj, rk[c][hsl[h]], 0.0) for c, h in units]
    qm_u = _dot1_many(list(zip(rb_l, w_u)))
    y0_u = [only(p + q, h) for p, q, (c, h) in zip(
        _dot1_many(list(zip(rb_l, u0_u))),
        _dot1_many([(rk_l[i], v_c[c]) for i, (c, h) in enumerate(units)]), units)]

    def chunk_sum(xs, c):
        return functools.reduce(lambda p, q: p + q, xs[c * B_HEADS:(c + 1) * B_HEADS])

    w_sum = [chunk_sum(w_u, c) for c in chunks]
    u0 = [chunk_sum(u0_u, c) for c in chunks]
    qm = [rt[c] + chunk_sum(qm_u, c) for c in chunks]
    y0 = [chunk_sum(y0_u, c) for c in chunks]
    bw = _dot1_many([(bh[c].T, w_sum[c]) for c in chunks])
    g_mat = [jnp.where(same_head, bw[c], 0.0) + jnp.where(diag_w, jnp.exp(cum_l[c]).T, 0.0)
             for c in chunks]
    c_mat = [jnp.where(same_head, x, 0.0) for x in _dot1_many(
        [(jnp.concatenate([bh[c], kh[c]], axis=0).T, jnp.concatenate([u0[c], v_c[c]], axis=0))
         for c in chunks])]

    hst = h_ref[...]
    ys = []
    for c in chunks:
        ys.append(_dot1(qm[c], hst) + y0[c])
        hst = _dot1(g_mat[c], hst) + c_mat[c]
    h_ref[...] = hst
    y = jnp.concatenate(ys, axis=0)

    mean = _dot_x2(y, head_ones) * (1.0 / HEAD_DIM)
    yc = y - mean
    var = _dot_x2(yc * yc, head_ones) * (1.0 / HEAD_DIM)
    yn = yc * lax.rsqrt(var + GN_EPS) * lng_ref[...] + lnb_ref[...]
    o_ref[...] = ((yn + bonus) * g).astype(o_ref.dtype)


def _rwkv_call(pb, mu, w0, wd_pad, a0, wa_pad, wg, k_k, k_a, r_k, lnx_g, lnx_b):
    bsz, s, _ = pb.shape
    vec = lambda x: x.reshape(1, -1).astype(F32)
    small = [vec(mu), vec(w0), wd_pad, vec(a0), wa_pad, wg, vec(k_k), vec(k_a), vec(r_k),
             vec(lnx_g), vec(lnx_b)]
    return pl.pallas_call(
        _rwkv_kernel,
        grid=(bsz, s // RWKV_TS),
        in_specs=[pl.BlockSpec((None, RWKV_TS, B_COLS), lambda b, t: (b, t, 0))]
        + [pl.BlockSpec(x.shape, lambda b, t: (0, 0)) for x in small],
        out_specs=pl.BlockSpec((None, RWKV_TS, B_WIDTH), lambda b, t: (b, t, 0)),
        out_shape=jax.ShapeDtypeStruct((bsz, s, B_WIDTH), BF16),
        scratch_shapes=[pltpu.VMEM((1, B_COLS), F32), pltpu.VMEM((B_WIDTH, B_WIDTH), F32)],
        compiler_params=_params("parallel", "arbitrary"),
        name="rwkv7",
    )(pb, *small)


def _softplus2(z):
    return jnp.maximum(z, 0.0) + jnp.log2(1.0 + jnp.exp2(-jnp.abs(z)))


def _sb_kernel(q_ref, k_ref, vt_ref, o_ref, z_sc, lb_sc, e_sc, w_sc, acc_sc, car_sc):
    it = pl.program_id(2)
    q2 = q_ref[...]
    lane = lax.broadcasted_iota(jnp.int32, (SB_TQ, LANES), 1)
    zero = jnp.zeros_like(q2)
    qh = (jnp.where(lane < HEAD_DIM, q2, zero), jnp.where(lane >= HEAD_DIM, q2, zero))
    ur = lax.broadcasted_iota(jnp.int32, (BLOCK, BLOCK), 0)
    uc = lax.broadcasted_iota(jnp.int32, (BLOCK, BLOCK), 1)
    upper = jnp.where(uc > ur, 1.0, 0.0).astype(BF16)
    nsub = SB_TQ // BLOCK
    first = it * nsub + nsub - 1
    nblk = first + 1

    def key_off(n):
        return pl.multiple_of(jnp.clip(first - n, 0, first) * BLOCK, BLOCK)

    def logits(n):
        kblk = k_ref[pl.ds(key_off(n), BLOCK), :]
        return jnp.concatenate([_dot_nt(kblk, qh[0]), _dot_nt(kblk, qh[1])], axis=1)

    def softplus_stage(zt):
        sp = _softplus2(zt)
        return zt - sp, sp.astype(BF16), jnp.sum(sp, axis=0, keepdims=True)

    def value_stage(n):
        vt2 = vt_ref[:, pl.ds(key_off(n), BLOCK)]
        w = w_sc[...]
        for h in range(2):
            cols = slice(h * SB_TQ, (h + 1) * SB_TQ)
            acc_sc[:, cols] += _dot(vt2[h * HEAD_DIM:(h + 1) * HEAD_DIM], w[:, cols])

    def step(n, carry):
        car, colsum = carry
        value_stage(n - 1)
        z_new = logits(n + 2)
        lb_new, sp16, colsum_new = softplus_stage(z_sc[...])
        e_new = _dot(upper, sp16)
        w_sc[...] = jnp.exp2(lb_sc[...] - e_sc[...] - car).astype(BF16)
        z_sc[...] = z_new
        lb_sc[...] = lb_new
        e_sc[...] = e_new
        return car + colsum, colsum_new

    half = SB_TQ // 2

    def lanes(n):
        return (max(0, nsub - 1 - n) * BLOCK, SB_TQ) if n < nsub else (0, half)

    def values_block(n):
        vt2 = vt_ref[:, pl.ds(key_off(n), BLOCK)]
        if n >= nsub:
            vt2 = jnp.where(n < nblk, vt2, jnp.zeros_like(vt2))
        return vt2

    def near_logits(n):
        lo, hi = lanes(n)
        kblk = k_ref[pl.ds(key_off(n), BLOCK), :]
        return [_dot_nt(kblk, qh[h][lo:hi]) for h in range(2)]

    def near_softplus(zt, n):
        lo, hi = lanes(n)
        sp = _softplus2(zt)
        lb = zt - sp
        if n < nsub:
            kr = lax.broadcasted_iota(jnp.int32, (BLOCK, hi - lo), 0)
            qc = lax.broadcasted_iota(jnp.int32, (BLOCK, hi - lo), 1)
            before = kr < qc
            sp = jnp.where(before, sp, 0.0)
            lb = jnp.where(before, lb, MASKED)
        pieces = [jnp.zeros((1, lo), F32)] if lo else []
        pieces.append(jnp.sum(sp, axis=0, keepdims=True))
        if hi < SB_TQ:
            pieces.append(jnp.zeros((1, SB_TQ - hi), F32))
        return lb, sp.astype(BF16), jnp.concatenate(pieces, axis=1)

    def near_values(n, ws):
        lo, hi = lanes(n)
        vt2 = values_block(n)
        for h in range(2):
            cols = slice(h * SB_TQ + lo, h * SB_TQ + hi)
            acc_sc[:, cols] += _dot(vt2[h * HEAD_DIM:(h + 1) * HEAD_DIM], ws[h])

    acc_sc[...] = jnp.zeros_like(acc_sc)
    near = range(SB_NEAR)
    zs = [near_logits(n) for n in near]
    soft = [[near_softplus(zs[n][h], n) for h in range(2)] for n in near]
    excl = [[_dot(upper, soft[n][h][1]) for h in range(2)] for n in near]
    cars = [jnp.zeros((1, SB_TQ), F32) for _ in range(2)]
    ws = []
    for n in near:
        lo, hi = lanes(n)
        ws.append([jnp.exp2(soft[n][h][0] - excl[n][h] - cars[h][:, lo:hi]).astype(BF16)
                   for h in range(2)])
        cars = [cars[h] + soft[n][h][2] for h in range(2)]
    for n in near:
        near_values(n, ws[n])
    car_sc[...] = jnp.concatenate(cars, axis=1)

    late_live = jnp.minimum(jnp.min(cars[0][:, half:]), jnp.min(cars[1][:, half:])) < SB_DEAD

    @pl.when(jnp.logical_and(nblk > nsub, late_live))
    def _():
        for n in range(nsub, SB_NEAR):
            kblk = k_ref[pl.ds(key_off(n), BLOCK), :]
            vt2 = values_block(n)
            for h in range(2):
                cols = slice(h * SB_TQ + half, (h + 1) * SB_TQ)
                zt = _dot_nt(kblk, qh[h][half:])
                sp = _softplus2(zt)
                w = jnp.exp2(zt - sp - _dot(upper, sp.astype(BF16)) - car_sc[:, cols])
                acc_sc[:, cols] += _dot(vt2[h * HEAD_DIM:(h + 1) * HEAD_DIM], w.astype(BF16))
                car_sc[:, cols] += jnp.sum(sp, axis=0, keepdims=True)

    car = car_sc[...]

    @pl.when(jnp.logical_and(nblk > SB_NEAR, jnp.min(car) < SB_DEAD))
    def _():
        lb0, sp160, colsum0 = softplus_stage(logits(SB_NEAR))
        lb_sc[...] = lb0
        e_sc[...] = _dot(upper, sp160)
        z_sc[...] = logits(SB_NEAR + 1)
        w_sc[...] = jnp.zeros_like(w_sc)

        def live(c):
            n, _, _, alive = c
            return jnp.logical_and(n < nblk, alive > 0)

        def visit(c):
            n, car, colsum, _ = c
            alive = (jnp.min(car) < SB_DEAD).astype(jnp.int32)
            car, colsum = step(n, (car, colsum))
            return n + jnp.int32(1), car, colsum, alive

        n_stop = lax.while_loop(live, visit, (jnp.int32(SB_NEAR), car, colsum0, jnp.int32(1)))[0]
        value_stage(n_stop - 1)

    acc = acc_sc[...]
    out_t = jnp.concatenate([acc[:, :SB_TQ], acc[:, SB_TQ:]], axis=0)
    o_ref[...] = out_t.T.astype(o_ref.dtype)


def _sb_call(qc, kc, vt):
    bsz, s, _ = qc.shape
    return pl.pallas_call(
        _sb_kernel,
        grid=(bsz, C_WIDTH // LANES, s // SB_TQ),
        in_specs=[pl.BlockSpec((None, SB_TQ, LANES), lambda b, hp, i: (b, i, hp)),
                  pl.BlockSpec((None, s, LANES), lambda b, hp, i: (b, 0, hp)),
                  pl.BlockSpec((None, LANES, s), lambda b, hp, i: (b, hp, 0))],
        out_specs=pl.BlockSpec((None, SB_TQ, LANES), lambda b, hp, i: (b, i, hp)),
        out_shape=jax.ShapeDtypeStruct((bsz, s, C_WIDTH), BF16),
        scratch_shapes=[pltpu.VMEM((BLOCK, 2 * SB_TQ), F32),
                        pltpu.VMEM((BLOCK, 2 * SB_TQ), F32),
                        pltpu.VMEM((BLOCK, 2 * SB_TQ), F32),
                        pltpu.VMEM((BLOCK, 2 * SB_TQ), BF16),
                        pltpu.VMEM((HEAD_DIM, 2 * SB_TQ), F32),
                        pltpu.VMEM((1, 2 * SB_TQ), F32)],
        compiler_params=_params("parallel", "parallel", "parallel"),
        name="stickbreak_attn",
    )(qc, kc, vt)


def _outproj_kernel(h_ref, oa_ref, ob_ref, oc_ref, wa_ref, wb_ref, wc_ref, g_ref, b_ref,
                    rw_ref, rb_ref, o_ref, gate_ref):
    tm = h_ref.shape[0]
    parts = [slice(i * tm // OUTPROJ_PARTS, (i + 1) * tm // OUTPROJ_PARTS)
             for i in range(OUTPROJ_PARTS)]
    ms = [_dot(oa_ref[s], wa_ref[...]) + _dot(ob_ref[s], wb_ref[...]) + _dot(oc_ref[s], wc_ref[...])
          for s in parts]
    h1s = [_layer_norm(ALPHA * h_ref[s] + m, g_ref[...], b_ref[...]) for s, m in zip(parts, ms)]
    for s, h1 in zip(parts, h1s):
        o_ref[s] = h1
    logits = [_dot3(h1, rw_ref[...]) for h1 in h1s]
    for s, x in zip(parts, logits):
        gate_ref[:, s] = _route(x.T[:rb_ref.shape[0]], rb_ref[...])


def _outproj_call(h, oa, ob, oc, wa, wb, wc, g, b, rw, rb_col, tm=1024):
    bsz, s, d = h.shape
    nt = s // tm
    rows = rb_col.shape[0]
    row = lambda c: pl.BlockSpec((None, tm, c), lambda bb, t: (bb, t, 0))
    full = lambda w: pl.BlockSpec(w.shape, lambda bb, t: (0, 0))
    g2, b2 = g.reshape(1, d), b.reshape(1, d)
    return pl.pallas_call(
        _outproj_kernel,
        grid=(bsz, nt),
        in_specs=[row(d), row(A_WIDTH), row(B_WIDTH), row(C_WIDTH),
                  full(wa), full(wb), full(wc), full(g2), full(b2), full(rw), full(rb_col)],
        out_specs=[row(d), pl.BlockSpec((rows, tm), lambda bb, t: (0, bb * nt + t))],
        out_shape=[jax.ShapeDtypeStruct((bsz, s, d), F32),
                   jax.ShapeDtypeStruct((rows, bsz * s), F32)],
        compiler_params=_params("parallel", "parallel"),
        name="out_proj_ln_router",
    )(h, oa, ob, oc, wa, wb, wc, g2, b2, rw, rb_col)


def _route(logits, rb):
    scores = _sigmoid(logits)
    sel = scores + rb
    R = ROUTER_ROWS
    s = [sel[m * R:(m + 1) * R] for m in range(EXPERTS_PER_GROUP)]
    sc = [scores[m * R:(m + 1) * R] for m in range(EXPERTS_PER_GROUP)]
    hi01, lo01 = jnp.maximum(s[0], s[1]), jnp.minimum(s[0], s[1])
    hi23, lo23 = jnp.maximum(s[2], s[3]), jnp.minimum(s[2], s[3])
    top1 = jnp.maximum(hi01, hi23)
    top2 = jnp.maximum(jnp.minimum(hi01, hi23), jnp.maximum(lo01, lo23))
    gscore = top1 + top2
    gi = lax.broadcasted_iota(jnp.int32, gscore.shape, 0)
    gmax = jnp.max(gscore, axis=0, keepdims=True)
    best = jnp.min(jnp.where(gscore == gmax, gi, R), axis=0, keepdims=True)
    in_group = gi == best
    picked = []
    for m in range(EXPERTS_PER_GROUP):
        rank = jnp.zeros(gscore.shape, jnp.int32)
        for j in range(EXPERTS_PER_GROUP):
            if j == m:
                continue
            ahead = (s[j] >= s[m]) if j < m else (s[j] > s[m])
            rank = rank + jnp.where(ahead, 1, 0)
        picked.append(jnp.where(jnp.logical_and(in_group, rank < 2), sc[m], 0.0))
    denom = jnp.sum(picked[0] + picked[1] + picked[2] + picked[3], axis=0, keepdims=True)
    return jnp.concatenate([x / denom for x in picked], axis=0)


def _moe_kernel(x_ref, gate_ref, wgu_ref, wd_ref, g_ref, b_ref, o_ref, xb_ref, acc_ref):
    j = pl.program_id(1)

    @pl.when(j == 0)
    def _():
        xb_ref[...] = x_ref[...].astype(BF16)
        acc_ref[...] = jnp.zeros_like(acc_ref)

    xb = xb_ref[...]
    gates = gate_ref[...]
    lane = lax.broadcasted_iota(jnp.int32, gates.shape, 1)
    f = wd_ref.shape[1]
    y = None
    for m in range(MOE_EPS):
        h2 = _dot(xb, wgu_ref[m])
        hg, hu = h2[:, :f], h2[:, f:]
        gcol = jnp.sum(jnp.where(lane == j * MOE_EPS + m, gates, 0.0), axis=-1, keepdims=True)
        act = (hg * _sigmoid(hg)) * hu * gcol
        part = _dot(act.astype(BF16), wd_ref[m])
        y = part if y is None else y + part
    acc_ref[...] += y

    @pl.when(j == pl.num_programs(1) - 1)
    def _():
        o_ref[...] = _layer_norm(ALPHA * x_ref[...] + acc_ref[...], g_ref[...], b_ref[...])


def _moe_call(x, gates, wgu, wd, g, b, tm=1024):
    n, d = x.shape
    f2 = wgu.shape[-1]
    g2, b2 = g.reshape(1, d), b.reshape(1, d)
    return pl.pallas_call(
        _moe_kernel,
        grid=(n // tm, N_EXPERTS // MOE_EPS),
        in_specs=[pl.BlockSpec((tm, d), lambda i, j: (i, 0)),
                  pl.BlockSpec((tm, N_EXPERTS), lambda i, j: (i, 0)),
                  pl.BlockSpec((MOE_EPS, d, f2), lambda i, j: (j, 0, 0)),
                  pl.BlockSpec((MOE_EPS, f2 // 2, d), lambda i, j: (j, 0, 0)),
                  pl.BlockSpec((1, d), lambda i, j: (0, 0)),
                  pl.BlockSpec((1, d), lambda i, j: (0, 0))],
        out_specs=pl.BlockSpec((tm, d), lambda i, j: (i, 0)),
        out_shape=jax.ShapeDtypeStruct((n, d), F32),
        scratch_shapes=[pltpu.VMEM((tm, d), BF16), pltpu.VMEM((tm, d), F32)],
        compiler_params=_params("parallel", "arbitrary"),
        name="moe_experts_ln",
    )(x, gates, wgu, wd, g2, b2)


def _pair_heads(x, axis):
    shape = x.shape
    x = x.reshape(shape[:axis] + (A_KV_HEADS, A_Q_HEADS // A_KV_HEADS, HEAD_DIM) + shape[axis + 1:])
    x = jnp.swapaxes(x, axis, axis + 1)
    return x.reshape(shape)


def _router_layout(router_w, router_bias):
    d = router_w.shape[0]
    w = router_w.astype(F32).T.reshape(N_GROUPS, EXPERTS_PER_GROUP, d).transpose(1, 0, 2)
    w = jnp.pad(w, ((0, 0), (0, ROUTER_ROWS - N_GROUPS), (0, 0)))
    b = router_bias.astype(F32).reshape(N_GROUPS, EXPERTS_PER_GROUP).T
    b = jnp.pad(b, ((0, 0), (0, ROUTER_ROWS - N_GROUPS)), constant_values=MASKED)
    rows = EXPERTS_PER_GROUP * ROUTER_ROWS
    w = jnp.pad(w.reshape(rows, d).T, ((0, 0), (0, LANES - rows)))
    return w, b.reshape(rows, 1)


def _gates_from_router(gates_t):
    n = gates_t.shape[1]
    g = gates_t.reshape(EXPERTS_PER_GROUP, ROUTER_ROWS, n)[:, :N_GROUPS]
    return g.transpose(2, 1, 0).reshape(n, N_EXPERTS)


def kernel(x, ln0_g, ln0_b, w_in, w_out, sinks, rel_bias, shift_mu, decay_w0, decay_up, iclr_a0,
           iclr_up, gate_up, k_k, k_a, r_k, lnx_g, lnx_b, ln1_g, ln1_b, router_w, router_bias,
           w_gate, w_up, w_down, ln2_g, ln2_b):
    bsz, s, d = x.shape
    n = bsz * s
    bias_pairs = _swa_bias_pairs(rel_bias)
    rw, rb_col = _router_layout(router_w, router_bias)
    zeros_lora = jnp.zeros((ICLR_RANK, B_WIDTH), F32)

    h = x
    for l in range(DEPTH):
        wl = w_in[l]
        wa = jnp.concatenate([_pair_heads(wl[:, :A_WIDTH], 1), wl[:, A_WIDTH:A_COLS]], axis=1)
        wb = wl[:, A_COLS:A_COLS + B_COLS]
        c0 = A_COLS + B_COLS
        wq, wk, wv = (wl[:, c0 + j * C_WIDTH:c0 + (j + 1) * C_WIDTH] for j in range(3))
        weights = [wa.astype(BF16), wb.astype(BF16), wq.astype(BF16), wk.astype(BF16),
                   wv.T.astype(BF16)]
        if l == 0:
            pa, pb, qc, kc, vt, h = _proj_call(h, weights, embed_ln=(ln0_g, ln0_b))
        else:
            pa, pb, qc, kc, vt = _proj_call(h, weights)

        out_a = _swa_call(pa, sinks[l].astype(F32), bias_pairs)
        wd_pad = jnp.concatenate([decay_up[l].astype(F32), zeros_lora], axis=0).astype(BF16)
        wa_pad = jnp.concatenate([zeros_lora, iclr_up[l].astype(F32)], axis=0).astype(BF16)
        out_b = _rwkv_call(pb, shift_mu[l], decay_w0[l], wd_pad, iclr_a0[l], wa_pad,
                           gate_up[l].astype(BF16), k_k[l], k_a[l], r_k[l], lnx_g[l], lnx_b[l])
        out_c = _sb_call(qc, kc, vt)

        wo = w_out[l]
        h, gates_t = _outproj_call(h, out_a, out_b, out_c,
                                   _pair_heads(wo[:A_WIDTH], 0).astype(BF16),
                                   wo[A_WIDTH:A_WIDTH + B_WIDTH].astype(BF16),
                                   wo[A_WIDTH + B_WIDTH:].astype(BF16), ln1_g[l], ln1_b[l],
                                   rw, rb_col)

        wgu = jnp.concatenate([w_gate[l], w_up[l]], axis=-1).astype(BF16)
        hf = _moe_call(h.reshape(n, d), _gates_from_router(gates_t), wgu,
                       w_down[l].astype(BF16), ln2_g[l], ln2_b[l])
        h = hf.reshape(bsz, s, d)
    return h
```

```python
import functools
import math

import jax
import jax.numpy as jnp
from jax import lax
from jax.experimental import pallas as pl
from jax.experimental.pallas import tpu as pltpu

F32 = jnp.float32
BF16 = jnp.bfloat16
HI = lax.Precision.HIGHEST

DEPTH = 2
HEAD_DIM = 64
BLOCK = 128
LANES = 128
A_Q_HEADS = 6
A_KV_HEADS = 2
WINDOW = 128
A_WIDTH = A_Q_HEADS * HEAD_DIM
A_KV_WIDTH = A_KV_HEADS * HEAD_DIM
B_HEADS = 4
B_WIDTH = B_HEADS * HEAD_DIM
DECAY_RANK = 64
ICLR_RANK = 64
GATE_RANK = 128
GN_EPS = 64e-5
C_HEADS = 6
C_WIDTH = C_HEADS * HEAD_DIM
A_COLS = A_WIDTH + 2 * A_KV_WIDTH
B_COLS = 3 * B_WIDTH + DECAY_RANK + ICLR_RANK + GATE_RANK
NUM_BUCKETS = 32
MAX_EXACT = NUM_BUCKETS // 2
MAX_DISTANCE = 128
N_EXPERTS = 16
N_GROUPS = 4
EXPERTS_PER_GROUP = N_EXPERTS // N_GROUPS
D_FF_EXPERT = 256
LN_EPS = 1e-5
ALPHA = (2 * DEPTH) ** 0.25
SCALE = HEAD_DIM ** -0.5
MASKED = -1e30
SWA_QB = 4
CHUNK = 64
INV_BASE = 8
RWKV_TS = 512
SB_TQ = 512
SB_NEAR = 6
SB_DEAD = 151.0
LOG2E = 1.4426950408889634
ROUTER_ROWS = 8
MOE_EPS = 4
OUTPROJ_PARTS = 4

VMEM_LIMIT = 48 * 1024 * 1024


def _dot(a, b, prec=None):
    return jnp.dot(a, b, preferred_element_type=F32, precision=prec)


def _dot_nt(a, b, prec=None):
    return lax.dot_general(a, b, (((1,), (1,)), ((), ())),
                           preferred_element_type=F32, precision=prec)


def _sigmoid(x):
    return 1.0 / (1.0 + jnp.exp(-x))


def _softplus(x):
    return jnp.maximum(x, 0.0) + jnp.log(1.0 + jnp.exp(-jnp.abs(x)))


def _layer_norm(x, g, b):
    mu = jnp.mean(x, axis=-1, keepdims=True)
    xc = x - mu
    var = jnp.mean(xc * xc, axis=-1, keepdims=True)
    return xc * lax.rsqrt(var + LN_EPS) * g + b


def _params(*sem):
    return pltpu.CompilerParams(dimension_semantics=sem, vmem_limit_bytes=VMEM_LIMIT)


def _proj_body(h, wa_ref, wb_ref, wq_ref, wk_ref, wvt_ref, pa_ref, pb_ref, qc_ref, kc_ref, vt_ref):
    hb = h.astype(BF16)
    pa_ref[...] = _dot(hb, wa_ref[...]).astype(BF16)
    pb_ref[...] = _dot(hb, wb_ref[...])
    qc_ref[...] = (_dot(hb, wq_ref[...]) * (SCALE * LOG2E)).astype(BF16)
    kc_ref[...] = _dot(hb, wk_ref[...]).astype(BF16)
    vt_ref[...] = _dot_nt(wvt_ref[...], hb).astype(BF16)


def _proj_kernel(h_ref, *refs):
    _proj_body(h_ref[...], *refs)


def _embed_proj_kernel(x_ref, g_ref, b_ref, *refs):
    h = _layer_norm(x_ref[...], g_ref[...], b_ref[...])
    refs[-1][...] = h
    _proj_body(h, *refs[:-1])


def _proj_call(h, weights, embed_ln=None, tm=512):
    bsz, s, d = h.shape
    full = lambda w: pl.BlockSpec(w.shape, lambda b, t: (0, 0))
    row = lambda c: pl.BlockSpec((None, tm, c), lambda b, t: (b, t, 0))
    in_specs = [row(d)] + [full(w) for w in weights]
    out_specs = [row(A_COLS), row(B_COLS), row(C_WIDTH), row(C_WIDTH),
                 pl.BlockSpec((None, C_WIDTH, tm), lambda b, t: (b, 0, t))]
    out_shape = [jax.ShapeDtypeStruct((bsz, s, A_COLS), BF16),
                 jax.ShapeDtypeStruct((bsz, s, B_COLS), F32),
                 jax.ShapeDtypeStruct((bsz, s, C_WIDTH), BF16),
                 jax.ShapeDtypeStruct((bsz, s, C_WIDTH), BF16),
                 jax.ShapeDtypeStruct((bsz, C_WIDTH, s), BF16)]
    args = [h] + list(weights)
    if embed_ln is not None:
        vecs = [v.reshape(1, d) for v in embed_ln]
        in_specs[1:1] = [full(v) for v in vecs]
        args[1:1] = vecs
        out_specs.append(row(d))
        out_shape.append(jax.ShapeDtypeStruct((bsz, s, d), F32))
    return pl.pallas_call(
        _proj_kernel if embed_ln is None else _embed_proj_kernel,
        grid=(bsz, s // tm),
        in_specs=in_specs,
        out_specs=out_specs,
        out_shape=out_shape,
        compiler_params=_params("parallel", "parallel"),
        name="in_proj" if embed_ln is None else "embed_ln_in_proj",
    )(*args)


def _swa_kernel(sink_ref, q_ref, kp_ref, kc_ref, vp_ref, vc_ref, bias_ref, o_ref):
    n = pl.program_id(1)
    kall = jnp.concatenate([kp_ref[...], kc_ref[...]], axis=0)
    vall = jnp.concatenate([vp_ref[...], vc_ref[...]], axis=0)
    lane = lax.broadcasted_iota(jnp.int32, (BLOCK, LANES), 1)
    row2 = lax.broadcasted_iota(jnp.int32, (2 * BLOCK, 1), 0)
    col2 = lax.broadcasted_iota(jnp.int32, (1, 2 * BLOCK), 1)
    pad = jnp.where(jnp.logical_and(n == 0, col2 < BLOCK), MASKED, 0.0)
    units = [(j, c) for j in range(SWA_QB) for c in range(A_Q_HEADS // 2)]
    logits = []
    for j, c in units:
        q2 = q_ref[j * BLOCK:(j + 1) * BLOCK, c * LANES:(c + 1) * LANES]
        zero = jnp.zeros_like(q2)
        qs = jnp.concatenate([jnp.where(lane < HEAD_DIM, q2, zero),
                              jnp.where(lane >= HEAD_DIM, q2, zero)], axis=0)
        x = _dot_nt(qs, kall[j * BLOCK:(j + 2) * BLOCK]) * SCALE + bias_ref[c]
        logits.append(x + pad if j == 0 else x)
    probs, denoms = [], []
    for (j, c), x in zip(units, logits):
        sink = jnp.where(row2 < BLOCK, sink_ref[c], sink_ref[c + 3])
        m = jnp.maximum(jnp.max(x, axis=-1, keepdims=True), sink)
        p = jnp.exp(x - m)
        denoms.append(jnp.sum(p, axis=-1, keepdims=True) + jnp.exp(sink - m))
        probs.append(p.astype(BF16))
    for (j, c), p, denom in zip(units, probs, denoms):
        o = _dot(p, vall[j * BLOCK:(j + 2) * BLOCK]) / denom
        o_ref[j * BLOCK:(j + 1) * BLOCK, c * LANES:(c + 1) * LANES] = jnp.where(
            lane < HEAD_DIM, o[:BLOCK], o[BLOCK:]).astype(BF16)


def _swa_call(pa, sinks, bias_pairs):
    bsz, s, _ = pa.shape
    tq = SWA_QB * BLOCK
    kcol = A_WIDTH // LANES
    vcol = kcol + 1
    prev = lambda n: jnp.maximum(n * SWA_QB - 1, 0)
    return pl.pallas_call(
        _swa_kernel,
        grid=(bsz, s // tq),
        in_specs=[pl.BlockSpec(memory_space=pltpu.SMEM),
                  pl.BlockSpec((None, tq, A_WIDTH), lambda b, n: (b, n, 0)),
                  pl.BlockSpec((None, BLOCK, LANES), lambda b, n: (b, prev(n), kcol)),
                  pl.BlockSpec((None, tq, LANES), lambda b, n: (b, n, kcol)),
                  pl.BlockSpec((None, BLOCK, LANES), lambda b, n: (b, prev(n), vcol)),
                  pl.BlockSpec((None, tq, LANES), lambda b, n: (b, n, vcol)),
                  pl.BlockSpec(bias_pairs.shape, lambda b, n: (0, 0, 0))],
        out_specs=pl.BlockSpec((None, tq, A_WIDTH), lambda b, n: (b, n, 0)),
        out_shape=jax.ShapeDtypeStruct((bsz, s, A_WIDTH), BF16),
        compiler_params=_params("parallel", "parallel"),
        name="swa_attn",
    )(sinks, pa, pa, pa, pa, pa, bias_pairs)


def _t5_causal_bucket(dist):
    dist = jnp.maximum(dist, 0)
    d = jnp.maximum(dist, 1).astype(F32)
    large = MAX_EXACT + (jnp.log(d / MAX_EXACT) / math.log(MAX_DISTANCE / MAX_EXACT)
                         * (NUM_BUCKETS - MAX_EXACT)).astype(jnp.int32)
    large = jnp.minimum(large, NUM_BUCKETS - 1)
    return jnp.where(dist < MAX_EXACT, dist, large)


def _swa_bias_pairs(rel_bias):
    nd = 3 * BLOCK - 1
    dist = (2 * BLOCK - 1) - jnp.arange(nd)
    onehot = jax.nn.one_hot(_t5_causal_bucket(dist), NUM_BUCKETS, dtype=F32)
    per_dist = jnp.einsum("db,bh->hd", onehot, rel_bias.astype(F32), precision=HI)
    per_dist = jnp.where((dist >= 0) & (dist < WINDOW), per_dist, MASKED)
    period = nd + 2
    tiled = jnp.tile(jnp.pad(per_dist, ((0, 0), (0, period - nd))), (1, BLOCK))
    rows = tiled[:, :BLOCK * (period - 1)].reshape(A_Q_HEADS, BLOCK, period - 1)
    bias = rows[:, :, BLOCK - 1:3 * BLOCK - 1]
    return jnp.stack([jnp.concatenate([bias[c], bias[c + 3]], axis=0)
                      for c in range(A_Q_HEADS // 2)])


def _split2(x):
    hi = x.astype(BF16)
    return hi, (x - hi.astype(F32)).astype(BF16)


def _dot3(a, b, nt=False):
    ah, al = _split2(a)
    bh, bl = _split2(b)
    d = _dot_nt if nt else _dot
    return d(ah, bh) + d(ah, bl) + d(al, bh)


def _dot1(a, b, nt=False):
    return (_dot_nt if nt else _dot)(a.astype(BF16), b.astype(BF16))


def _dot1_many(pairs, nt=False):
    parts = [(a.astype(BF16), b.astype(BF16)) for a, b in pairs]
    d = _dot_nt if nt else _dot
    return [d(a, b) for a, b in parts]


def _dot_x2(a, b_exact):
    ah, al = _split2(a)
    return _dot(ah, b_exact) + _dot(al, b_exact)


def _dot_2x(a_exact, b):
    bh, bl = _split2(b)
    return _dot(a_exact, bh) + _dot(a_exact, bl)


def _rwkv_kernel(pb_ref, mu_ref, w0_ref, wd_ref, a0_ref, wa_ref, wg_ref, kk_ref, ka_ref,
                 rk_ref, lng_ref, lnb_ref, o_ref, prev_ref, h_ref):
    t = pl.program_id(1)

    @pl.when(t == 0)
    def _():
        prev_ref[...] = jnp.zeros_like(prev_ref)
        h_ref[...] = jnp.zeros_like(h_ref)

    L = CHUNK
    W = B_WIDTH
    TS = RWKV_TS
    p = pb_ref[...]
    rows = lax.broadcasted_iota(jnp.int32, (TS, 1), 0)
    shifted = jnp.where(rows == 0, prev_ref[...], pltpu.roll(p, 1, axis=0))
    prev_ref[...] = p[TS - 1:TS, :]
    pm = p + (shifted - p) * mu_ref[...]
    r = pm[:, 0:W]
    k = pm[:, W:2 * W]
    v = pm[:, 2 * W:3 * W]
    xwa = pm[:, 3 * W:3 * W + DECAY_RANK + ICLR_RANK]
    xg = pm[:, 3 * W + DECAY_RANK + ICLR_RANK:]

    dw = w0_ref[...] + _dot(jnp.tanh(xwa).astype(BF16), wd_ref[...])
    lw = -jnp.exp(-_softplus(-dw) - 0.5)
    a = _sigmoid(a0_ref[...] + _dot(xwa.astype(BF16), wa_ref[...]))
    g = _dot(_sigmoid(xg).astype(BF16), wg_ref[...])

    hr = lax.broadcasted_iota(jnp.int32, (W, W), 0) // HEAD_DIM
    hc = lax.broadcasted_iota(jnp.int32, (W, W), 1) // HEAD_DIM
    same_head = hr == hc
    diag_w = (lax.broadcasted_iota(jnp.int32, (W, W), 0)
              == lax.broadcasted_iota(jnp.int32, (W, W), 1))
    head_ones = jnp.where(same_head, 1.0, 0.0).astype(BF16)
    kk = k * kk_ref[...]
    kk = kk * lax.rsqrt(jnp.maximum(_dot_x2(kk * kk, head_ones), 1e-24))
    k2 = k * (1.0 + (a - 1.0) * ka_ref[...])
    bonus = _dot_x2(r * k2 * rk_ref[...], head_ones) * v
    aa = -kk
    bb = kk * a

    ti = lax.broadcasted_iota(jnp.int32, (L, L), 0)
    tj = lax.broadcasted_iota(jnp.int32, (L, L), 1)
    lower = jnp.where(ti >= tj, 1.0, 0.0).astype(BF16)
    eye = jnp.where(ti == tj, 1.0, 0.0)
    lane_head = lax.broadcasted_iota(jnp.int32, (L, W), 1) // HEAD_DIM

    def only(x, h):
        return jnp.where(lane_head == h, x, 0.0)

    nch = TS // L
    chunks = range(nch)
    units = [(c, h) for c in chunks for h in range(B_HEADS)]
    hsl = [slice(h * L, (h + 1) * L) for h in range(B_HEADS)]
    csl = [slice(c * L, (c + 1) * L) for c in chunks]
    v_c = [v[s] for s in csl]
    cum = [_dot_2x(lower, lw[s]) for s in csl]
    cum_l = [x[L - 1:L, :] for x in cum]
    at = [aa[csl[c]] * jnp.exp(cum[c] - lw[csl[c]]) for c in chunks]
    rt = [r[csl[c]] * jnp.exp(cum[c]) for c in chunks]
    inv = [jnp.exp(-x) for x in cum]
    bt = [bb[csl[c]] * inv[c] for c in chunks]
    kt = [k2[csl[c]] * inv[c] for c in chunks]
    tail = [jnp.exp(cum_l[c] - cum[c]) for c in chunks]
    bh = [bb[csl[c]] * tail[c] for c in chunks]
    kh = [k2[csl[c]] * tail[c] for c in chunks]

    at_s = [jnp.concatenate([only(x, h) for h in range(B_HEADS)], axis=0) for x in at]
    rt_s = [jnp.concatenate([only(x, h) for h in range(B_HEADS)], axis=0) for x in rt]
    ab = _dot1_many([(at_s[c], bt[c]) for c in chunks], nt=True)
    ak = _dot1_many([(at_s[c], kt[c]) for c in chunks], nt=True)
    rb = _dot1_many([(rt_s[c], bt[c]) for c in chunks], nt=True)
    rk = _dot1_many([(rt_s[c], kt[c]) for c in chunks], nt=True)

    a_low = [jnp.where(ti > tj, ab[c][hsl[h]], 0.0) for c, h in units]
    pw = [jnp.where(ti // INV_BASE == tj // INV_BASE, x, 0.0) for x in a_low]
    tinv = [eye + x for x in pw]
    for _ in range(int(math.log2(INV_BASE)) - 1):
        pw = _dot1_many([(x, x) for x in pw])
        tinv = [t + d for t, d in zip(tinv, _dot1_many(list(zip(tinv, pw))))]
    blk = INV_BASE
    while blk < L:
        pair = jnp.logical_and(ti // (2 * blk) == tj // (2 * blk), ti // blk != tj // blk)
        right = _dot1_many([(jnp.where(pair, x, 0.0), t) for x, t in zip(a_low, tinv)])
        tinv = [t + d for t, d in zip(tinv, _dot1_many(list(zip(tinv, right))))]
        blk *= 2
    ak_v = _dot1_many([(jnp.where(ti > tj, ak[c][hsl[h]], 0.0), v_c[c]) for c, h in units])
    w_u = _dot1_many([(tinv[i], only(at[c], h)) for i, (c, h) in enumerate(units)])
    u0_u = [only(x, h) for x, (c, h) in zip(_dot1_many(list(zip(tinv, ak_v))), units)]
    rb_l = [jnp.where(ti >= tj, rb[c][hsl[h]], 0.0) for c, h in units]
    rk_l = [jnp.where(ti >= tj, rk[c][hsl[h]], 0.0) for c, h in units]
    qm_u = _dot1_many(list(zip(rb_l, w_u)))
    y0_u = [only(p + q, h) for p, q, (c, h) in zip(
        _dot1_many(list(zip(rb_l, u0_u))),
        _dot1_many([(rk_l[i], v_c[c]) for i, (c, h) in enumerate(units)]), units)]

    def chunk_sum(xs, c):
        return functools.reduce(lambda p, q: p + q, xs[c * B_HEADS:(c + 1) * B_HEADS])

    w_sum = [chunk_sum(w_u, c) for c in chunks]
    u0 = [chunk_sum(u0_u, c) for c in chunks]
    qm = [rt[c] + chunk_sum(qm_u, c) for c in chunks]
    y0 = [chunk_sum(y0_u, c) for c in chunks]
    bw = _dot1_many([(bh[c].T, w_sum[c]) for c in chunks])
    g_mat = [jnp.where(same_head, bw[c], 0.0) + jnp.where(diag_w, jnp.exp(cum_l[c]).T, 0.0)
             for c in chunks]
    c_mat = [jnp.where(same_head, x, 0.0) for x in _dot1_many(
        [(jnp.concatenate([bh[c], kh[c]], axis=0).T, jnp.concatenate([u0[c], v_c[c]], axis=0))
         for c in chunks])]

    hst = h_ref[...]
    ys = []
    for c in chunks:
        ys.append(_dot1(qm[c], hst) + y0[c])
        hst = _dot1(g_mat[c], hst) + c_mat[c]
    h_ref[...] = hst
    y = jnp.concatenate(ys, axis=0)

    mean = _dot_x2(y, head_ones) * (1.0 / HEAD_DIM)
    yc = y - mean
    var = _dot_x2(yc * yc, head_ones) * (1.0 / HEAD_DIM)
    yn = yc * lax.rsqrt(var + GN_EPS) * lng_ref[...] + lnb_ref[...]
    o_ref[...] = ((yn + bonus) * g).astype(o_ref.dtype)


def _rwkv_call(pb, mu, w0, wd_pad, a0, wa_pad, wg, k_k, k_a, r_k, lnx_g, lnx_b):
    bsz, s, _ = pb.shape
    vec = lambda x: x.reshape(1, -1).astype(F32)
    small = [vec(mu), vec(w0), wd_pad, vec(a0), wa_pad, wg, vec(k_k), vec(k_a), vec(r_k),
             vec(lnx_g), vec(lnx_b)]
    return pl.pallas_call(
        _rwkv_kernel,
        grid=(bsz, s // RWKV_TS),
        in_specs=[pl.BlockSpec((None, RWKV_TS, B_COLS), lambda b, t: (b, t, 0))]
        + [pl.BlockSpec(x.shape, lambda b, t: (0, 0)) for x in small],
        out_specs=pl.BlockSpec((None, RWKV_TS, B_WIDTH), lambda b, t: (b, t, 0)),
        out_shape=jax.ShapeDtypeStruct((bsz, s, B_WIDTH), BF16),
        scratch_shapes=[pltpu.VMEM((1, B_COLS), F32), pltpu.VMEM((B_WIDTH, B_WIDTH), F32)],
        compiler_params=_params("parallel", "arbitrary"),
        name="rwkv7",
    )(pb, *small)


def _softplus2(z):
    return jnp.maximum(z, 0.0) + jnp.log2(1.0 + jnp.exp2(-jnp.abs(z)))


def _sb_kernel(q_ref, k_ref, vt_ref, o_ref, z_sc, lb_sc, e_sc, w_sc, acc_sc, car_sc):
    it = pl.program_id(2)
    q2 = q_ref[...]
    lane = lax.broadcasted_iota(jnp.int32, (SB_TQ, LANES), 1)
    zero = jnp.zeros_like(q2)
    qh = (jnp.where(lane < HEAD_DIM, q2, zero), jnp.where(lane >= HEAD_DIM, q2, zero))
    ur = lax.broadcasted_iota(jnp.int32, (BLOCK, BLOCK), 0)
    uc = lax.broadcasted_iota(jnp.int32, (BLOCK, BLOCK), 1)
    upper = jnp.where(uc > ur, 1.0, 0.0).astype(BF16)
    nsub = SB_TQ // BLOCK
    first = it * nsub + nsub - 1
    nblk = first + 1

    def key_off(n):
        return pl.multiple_of(jnp.clip(first - n, 0, first) * BLOCK, BLOCK)

    def logits(n):
        kblk = k_ref[pl.ds(key_off(n), BLOCK), :]
        return jnp.concatenate([_dot_nt(kblk, qh[0]), _dot_nt(kblk, qh[1])], axis=1)

    def softplus_stage(zt):
        sp = _softplus2(zt)
        return zt - sp, sp.astype(BF16), jnp.sum(sp, axis=0, keepdims=True)

    def value_stage(n):
        vt2 = vt_ref[:, pl.ds(key_off(n), BLOCK)]
        w = w_sc[...]
        for h in range(2):
            cols = slice(h * SB_TQ, (h + 1) * SB_TQ)
            acc_sc[:, cols] += _dot(vt2[h * HEAD_DIM:(h + 1) * HEAD_DIM], w[:, cols])

    def step(n, carry):
        car, colsum = carry
        value_stage(n - 1)
        z_new = logits(n + 2)
        lb_new, sp16, colsum_new = softplus_stage(z_sc[...])
        e_new = _dot(upper, sp16)
        w_sc[...] = jnp.exp2(lb_sc[...] - e_sc[...] - car).astype(BF16)
        z_sc[...] = z_new
        lb_sc[...] = lb_new
        e_sc[...] = e_new
        return car + colsum, colsum_new

    half = SB_TQ // 2

    def lanes(n):
        return (max(0, nsub - 1 - n) * BLOCK, SB_TQ) if n < nsub else (0, half)

    def values_block(n):
        vt2 = vt_ref[:, pl.ds(key_off(n), BLOCK)]
        if n >= nsub:
            vt2 = jnp.where(n < nblk, vt2, jnp.zeros_like(vt2))
        return vt2

    def near_logits(n):
        lo, hi = lanes(n)
        kblk = k_ref[pl.ds(key_off(n), BLOCK), :]
        return [_dot_nt(kblk, qh[h][lo:hi]) for h in range(2)]

    def near_softplus(zt, n):
        lo, hi = lanes(n)
        sp = _softplus2(zt)
        lb = zt - sp
        if n < nsub:
            kr = lax.broadcasted_iota(jnp.int32, (BLOCK, hi - lo), 0)
            qc = lax.broadcasted_iota(jnp.int32, (BLOCK, hi - lo), 1)
            before = kr < qc
            sp = jnp.where(before, sp, 0.0)
            lb = jnp.where(before, lb, MASKED)
        pieces = [jnp.zeros((1, lo), F32)] if lo else []
        pieces.append(jnp.sum(sp, axis=0, keepdims=True))
        if hi < SB_TQ:
            pieces.append(jnp.zeros((1, SB_TQ - hi), F32))
        return lb, sp.astype(BF16), jnp.concatenate(pieces, axis=1)

    def near_values(n, ws):
        lo, hi = lanes(n)
        vt2 = values_block(n)
        for h in range(2):
            cols = slice(h * SB_TQ + lo, h * SB_TQ + hi)
            acc_sc[:, cols] += _dot(vt2[h * HEAD_DIM:(h + 1) * HEAD_DIM], ws[h])

    acc_sc[...] = jnp.zeros_like(acc_sc)
    near = range(SB_NEAR)
    zs = [near_logits(n) for n in near]
    soft = [[near_softplus(zs[n][h], n) for h in range(2)] for n in near]
    excl = [[_dot(upper, soft[n][h][1]) for h in range(2)] for n in near]
    cars = [jnp.zeros((1, SB_TQ), F32) for _ in range(2)]
    ws = []
    for n in near:
        lo, hi = lanes(n)
        ws.append([jnp.exp2(soft[n][h][0] - excl[n][h] - cars[h][:, lo:hi]).astype(BF16)
                   for h in range(2)])
        cars = [cars[h] + soft[n][h][2] for h in range(2)]
    for n in near:
        near_values(n, ws[n])
    car_sc[...] = jnp.concatenate(cars, axis=1)

    late_live = jnp.minimum(jnp.min(cars[0][:, half:]), jnp.min(cars[1][:, half:])) < SB_DEAD

    @pl.when(jnp.logical_and(nblk > nsub, late_live))
    def _():
        for n in range(nsub, SB_NEAR):
            kblk = k_ref[pl.ds(key_off(n), BLOCK), :]
            vt2 = values_block(n)
            for h in range(2):
                cols = slice(h * SB_TQ + half, (h + 1) * SB_TQ)
                zt = _dot_nt(kblk, qh[h][half:])
                sp = _softplus2(zt)
                w = jnp.exp2(zt - sp - _dot(upper, sp.astype(BF16)) - car_sc[:, cols])
                acc_sc[:, cols] += _dot(vt2[h * HEAD_DIM:(h + 1) * HEAD_DIM], w.astype(BF16))
                car_sc[:, cols] += jnp.sum(sp, axis=0, keepdims=True)

    car = car_sc[...]

    @pl.when(jnp.logical_and(nblk > SB_NEAR, jnp.min(car) < SB_DEAD))
    def _():
        lb0, sp160, colsum0 = softplus_stage(logits(SB_NEAR))
        lb_sc[...] = lb0
        e_sc[...] = _dot(upper, sp160)
        z_sc[...] = logits(SB_NEAR + 1)
        w_sc[...] = jnp.zeros_like(w_sc)

        def live(c):
            n, _, _, alive = c
            return jnp.logical_and(n < nblk, alive > 0)

        def visit(c):
            n, car, colsum, _ = c
            alive = (jnp.min(car) < SB_DEAD).astype(jnp.int32)
            car, colsum = step(n, (car, colsum))
            return n + jnp.int32(1), car, colsum, alive

        n_stop = lax.while_loop(live, visit, (jnp.int32(SB_NEAR), car, colsum0, jnp.int32(1)))[0]
        value_stage(n_stop - 1)

    acc = acc_sc[...]
    out_t = jnp.concatenate([acc[:, :SB_TQ], acc[:, SB_TQ:]], axis=0)
    o_ref[...] = out_t.T.astype(o_ref.dtype)


def _sb_call(qc, kc, vt):
    bsz, s, _ = qc.shape
    return pl.pallas_call(
        _sb_kernel,
        grid=(bsz, C_WIDTH // LANES, s // SB_TQ),
        in_specs=[pl.BlockSpec((None, SB_TQ, LANES), lambda b, hp, i: (b, i, hp)),
                  pl.BlockSpec((None, s, LANES), lambda b, hp, i: (b, 0, hp)),
                  pl.BlockSpec((None, LANES, s), lambda b, hp, i: (b, hp, 0))],
        out_specs=pl.BlockSpec((None, SB_TQ, LANES), lambda b, hp, i: (b, i, hp)),
        out_shape=jax.ShapeDtypeStruct((bsz, s, C_WIDTH), BF16),
        scratch_shapes=[pltpu.VMEM((BLOCK, 2 * SB_TQ), F32),
                        pltpu.VMEM((BLOCK, 2 * SB_TQ), F32),
                        pltpu.VMEM((BLOCK, 2 * SB_TQ), F32),
                        pltpu.VMEM((BLOCK, 2 * SB_TQ), BF16),
                        pltpu.VMEM((HEAD_DIM, 2 * SB_TQ), F32),
                        pltpu.VMEM((1, 2 * SB_TQ), F32)],
        compiler_params=_params("parallel", "parallel", "parallel"),
        name="stickbreak_attn",
    )(qc, kc, vt)


def _outproj_kernel(h_ref, oa_ref, ob_ref, oc_ref, wa_ref, wb_ref, wc_ref, g_ref, b_ref,
                    rw_ref, rb_ref, o_ref, gate_ref, o16_ref):
    tm = h_ref.shape[0]
    parts = [slice(i * tm // OUTPROJ_PARTS, (i + 1) * tm // OUTPROJ_PARTS)
             for i in range(OUTPROJ_PARTS)]
    ms = [_dot(oa_ref[s], wa_ref[...]) + _dot(ob_ref[s], wb_ref[...]) + _dot(oc_ref[s], wc_ref[...])
          for s in parts]
    h1s = [_layer_norm(ALPHA * h_ref[s] + m, g_ref[...], b_ref[...]) for s, m in zip(parts, ms)]
    for s, h1 in zip(parts, h1s):
        o_ref[s] = h1
        o16_ref[s] = h1.astype(BF16)
    logits = [_dot3(h1, rw_ref[...]) for h1 in h1s]
    for s, x in zip(parts, logits):
        gate_ref[:, s] = _route(x.T[:rb_ref.shape[0]], rb_ref[...])


def _outproj_call(h, oa, ob, oc, wa, wb, wc, g, b, rw, rb_col, tm=1024):
    bsz, s, d = h.shape
    nt = s // tm
    rows = rb_col.shape[0]
    row = lambda c: pl.BlockSpec((None, tm, c), lambda bb, t: (bb, t, 0))
    full = lambda w: pl.BlockSpec(w.shape, lambda bb, t: (0, 0))
    g2, b2 = g.reshape(1, d), b.reshape(1, d)
    return pl.pallas_call(
        _outproj_kernel,
        grid=(bsz, nt),
        in_specs=[row(d), row(A_WIDTH), row(B_WIDTH), row(C_WIDTH),
                  full(wa), full(wb), full(wc), full(g2), full(b2), full(rw), full(rb_col)],
        out_specs=[row(d), pl.BlockSpec((rows, tm), lambda bb, t: (0, bb * nt + t)), row(d)],
        out_shape=[jax.ShapeDtypeStruct((bsz, s, d), F32),
                   jax.ShapeDtypeStruct((rows, bsz * s), F32),
                   jax.ShapeDtypeStruct((bsz, s, d), BF16)],
        compiler_params=_params("parallel", "parallel"),
        name="out_proj_ln_router",
    )(h, oa, ob, oc, wa, wb, wc, g2, b2, rw, rb_col)


def _route(logits, rb):
    scores = _sigmoid(logits)
    sel = scores + rb
    R = ROUTER_ROWS
    s = [sel[m * R:(m + 1) * R] for m in range(EXPERTS_PER_GROUP)]
    sc = [scores[m * R:(m + 1) * R] for m in range(EXPERTS_PER_GROUP)]
    hi01, lo01 = jnp.maximum(s[0], s[1]), jnp.minimum(s[0], s[1])
    hi23, lo23 = jnp.maximum(s[2], s[3]), jnp.minimum(s[2], s[3])
    top1 = jnp.maximum(hi01, hi23)
    top2 = jnp.maximum(jnp.minimum(hi01, hi23), jnp.maximum(lo01, lo23))
    gscore = top1 + top2
    gi = lax.broadcasted_iota(jnp.int32, gscore.shape, 0)
    gmax = jnp.max(gscore, axis=0, keepdims=True)
    best = jnp.min(jnp.where(gscore == gmax, gi, R), axis=0, keepdims=True)
    in_group = gi == best
    picked = []
    for m in range(EXPERTS_PER_GROUP):
        rank = jnp.zeros(gscore.shape, jnp.int32)
        for j in range(EXPERTS_PER_GROUP):
            if j == m:
                continue
            ahead = (s[j] >= s[m]) if j < m else (s[j] > s[m])
            rank = rank + jnp.where(ahead, 1, 0)
        picked.append(jnp.where(jnp.logical_and(in_group, rank < 2), sc[m], 0.0))
    denom = jnp.sum(picked[0] + picked[1] + picked[2] + picked[3], axis=0, keepdims=True)
    return jnp.concatenate([x / denom for x in picked], axis=0)


def _moe_kernel(x_ref, xb_ref, gate_ref, wgu_ref, wd_ref, g_ref, b_ref, o_ref, acc_ref):
    j = pl.program_id(1)

    @pl.when(j == 0)
    def _():
        acc_ref[...] = jnp.zeros_like(acc_ref)

    xb = xb_ref[...]
    gates = gate_ref[...]
    lane = lax.broadcasted_iota(jnp.int32, gates.shape, 1)
    f = wd_ref.shape[1]
    y = None
    for m in range(MOE_EPS):
        h2 = _dot(xb, wgu_ref[m])
        hg, hu = h2[:, :f], h2[:, f:]
        gcol = jnp.sum(jnp.where(lane == j * MOE_EPS + m, gates, 0.0), axis=-1, keepdims=True)
        act = (hg * _sigmoid(hg)) * hu * gcol
        part = _dot(act.astype(BF16), wd_ref[m])
        y = part if y is None else y + part
    acc_ref[...] += y

    @pl.when(j == pl.num_programs(1) - 1)
    def _():
        o_ref[...] = _layer_norm(ALPHA * x_ref[...] + acc_ref[...], g_ref[...], b_ref[...])


def _moe_call(x, xb, gates, wgu, wd, g, b, tm=1024):
    n, d = x.shape
    f2 = wgu.shape[-1]
    g2, b2 = g.reshape(1, d), b.reshape(1, d)
    return pl.pallas_call(
        _moe_kernel,
        grid=(n // tm, N_EXPERTS // MOE_EPS),
        in_specs=[pl.BlockSpec((tm, d), lambda i, j: (i, 0)),
                  pl.BlockSpec((tm, d), lambda i, j: (i, 0)),
                  pl.BlockSpec((tm, N_EXPERTS), lambda i, j: (i, 0)),
                  pl.BlockSpec((MOE_EPS, d, f2), lambda i, j: (j, 0, 0)),
                  pl.BlockSpec((MOE_EPS, f2 // 2, d), lambda i, j: (j, 0, 0)),
                  pl.BlockSpec((1, d), lambda i, j: (0, 0)),
                  pl.BlockSpec((1, d), lambda i, j: (0, 0))],
        out_specs=pl.BlockSpec((tm, d), lambda i, j: (i, 0)),
        out_shape=jax.ShapeDtypeStruct((n, d), F32),
        scratch_shapes=[pltpu.VMEM((tm, d), F32)],
        compiler_params=_params("parallel", "arbitrary"),
        name="moe_experts_ln",
    )(x, xb, gates, wgu, wd, g2, b2)


def _pair_heads(x, axis):
    shape = x.shape
    x = x.reshape(shape[:axis] + (A_KV_HEADS, A_Q_HEADS // A_KV_HEADS, HEAD_DIM) + shape[axis + 1:])
    x = jnp.swapaxes(x, axis, axis + 1)
    return x.reshape(shape)


def _router_layout(router_w, router_bias):
    d = router_w.shape[0]
    w = router_w.astype(F32).T.reshape(N_GROUPS, EXPERTS_PER_GROUP, d).transpose(1, 0, 2)
    w = jnp.pad(w, ((0, 0), (0, ROUTER_ROWS - N_GROUPS), (0, 0)))
    b = router_bias.astype(F32).reshape(N_GROUPS, EXPERTS_PER_GROUP).T
    b = jnp.pad(b, ((0, 0), (0, ROUTER_ROWS - N_GROUPS)), constant_values=MASKED)
    rows = EXPERTS_PER_GROUP * ROUTER_ROWS
    w = jnp.pad(w.reshape(rows, d).T, ((0, 0), (0, LANES - rows)))
    return w, b.reshape(rows, 1)


def _gates_from_router(gates_t):
    n = gates_t.shape[1]
    g = gates_t.reshape(EXPERTS_PER_GROUP, ROUTER_ROWS, n)[:, :N_GROUPS]
    return g.transpose(2, 1, 0).reshape(n, N_EXPERTS)


def kernel(x, ln0_g, ln0_b, w_in, w_out, sinks, rel_bias, shift_mu, decay_w0, decay_up, iclr_a0,
           iclr_up, gate_up, k_k, k_a, r_k, lnx_g, lnx_b, ln1_g, ln1_b, router_w, router_bias,
           w_gate, w_up, w_down, ln2_g, ln2_b):
    bsz, s, d = x.shape
    n = bsz * s
    bias_pairs = _swa_bias_pairs(rel_bias)
    rw, rb_col = _router_layout(router_w, router_bias)
    zeros_lora = jnp.zeros((ICLR_RANK, B_WIDTH), F32)

    h = x
    for l in range(DEPTH):
        wl = w_in[l]
        wa = jnp.concatenate([_pair_heads(wl[:, :A_WIDTH], 1), wl[:, A_WIDTH:A_COLS]], axis=1)
        wb = wl[:, A_COLS:A_COLS + B_COLS]
        c0 = A_COLS + B_COLS
        wq, wk, wv = (wl[:, c0 + j * C_WIDTH:c0 + (j + 1) * C_WIDTH] for j in range(3))
        weights = [wa.astype(BF16), wb.astype(BF16), wq.astype(BF16), wk.astype(BF16),
                   wv.T.astype(BF16)]
        if l == 0:
            pa, pb, qc, kc, vt, h = _proj_call(h, weights, embed_ln=(ln0_g, ln0_b))
        else:
            pa, pb, qc, kc, vt = _proj_call(h, weights)

        out_a = _swa_call(pa, sinks[l].astype(F32), bias_pairs)
        wd_pad = jnp.concatenate([decay_up[l].astype(F32), zeros_lora], axis=0).astype(BF16)
        wa_pad = jnp.concatenate([zeros_lora, iclr_up[l].astype(F32)], axis=0).astype(BF16)
        out_b = _rwkv_call(pb, shift_mu[l], decay_w0[l], wd_pad, iclr_a0[l], wa_pad,
                           gate_up[l].astype(BF16), k_k[l], k_a[l], r_k[l], lnx_g[l], lnx_b[l])
        out_c = _sb_call(qc, kc, vt)

        wo = w_out[l]
        h, gates_t, hb = _outproj_call(h, out_a, out_b, out_c,
                                       _pair_heads(wo[:A_WIDTH], 0).astype(BF16),
                                       wo[A_WIDTH:A_WIDTH + B_WIDTH].astype(BF16),
                                       wo[A_WIDTH + B_WIDTH:].astype(BF16), ln1_g[l], ln1_b[l],
                                       rw, rb_col)

        wgu = jnp.concatenate([w_gate[l], w_up[l]], axis=-1).astype(BF16)
        hf = _moe_call(h.reshape(n, d), hb.reshape(n, d), _gates_from_router(gates_t), wgu,
                       w_down[l].astype(BF16), ln2_g[l], ln2_b[l])
        h = hf.reshape(bsz, s, d)
    return h
```
